```python
import math
import jax, jax.numpy as jnp
from jax import lax
import numpy as np


D_MODEL = 1024
BATCH = 16
SEQ = 2048
DEPTH = 2

N_MIXERS = 4
HEADS = 4
HEAD_DIM = D_MODEL // (N_MIXERS * HEADS)
GROUP_WIDTH = HEADS * HEAD_DIM
MIX_WIDTH = N_MIXERS * GROUP_WIDTH
D_FF = 4 * D_MODEL
LN_EPS = 1e-5
NEG = -1e30

M_QK_DIM = HEAD_DIM // 2
M_CHUNK = 64
M_CONV = 4

DIL_PATTERNS = ((128, 1), (512, 4), (2048, 16))

IDX_HEADS = 4
IDX_DIM = 64
DSA_TOPK = 256
DSA_QBLOCK = 128

NSA_CMP_LEN = 32
NSA_CMP_STRIDE = 16
NSA_SEL_LEN = 64
NSA_TOPN = 16
NSA_WINDOW = 512
NSA_CMP_HIDDEN = 256
NSA_QBLOCK = 64
NSA_FORCE = 1e9

NUM_BUCKETS = 32
MAX_DISTANCE = 128
N_BIAS_HEADS = 3 * HEADS

ALPHA = (2 * DEPTH) ** 0.25
BETA = (8 * DEPTH) ** -0.25

IN_SPLITS = (
    ('a_q', HEADS * M_QK_DIM), ('a_k', HEADS * M_QK_DIM), ('a_v', GROUP_WIDTH),
    ('a_i', HEADS), ('a_f', HEADS), ('a_o', GROUP_WIDTH),
    ('b_q', GROUP_WIDTH), ('b_k', GROUP_WIDTH), ('b_v', GROUP_WIDTH),
    ('c_q', GROUP_WIDTH), ('c_k', HEAD_DIM), ('c_v', HEAD_DIM),
    ('c_iq', IDX_HEADS * IDX_DIM), ('c_ik', IDX_DIM), ('c_iw', IDX_HEADS),
    ('d_q', GROUP_WIDTH), ('d_kc', HEAD_DIM), ('d_vc', HEAD_DIM),
    ('d_ks', HEAD_DIM), ('d_vs', HEAD_DIM), ('d_kw', HEAD_DIM), ('d_vw', HEAD_DIM),
    ('d_g', 3 * HEADS),
)
D_IN = sum(w for _, w in IN_SPLITS)

kernel_name = 'hybrid_parallel_mixers_deepnorm'


def _col_range(name):
    off = 0
    for n, w in IN_SPLITS:
        if n == name:
            return off, off + w
        off += w
    raise KeyError(name)


def split_cols(z):
    out = {}
    off = 0
    for n, w in IN_SPLITS:
        out[n] = z[..., off:off + w]
        off += w
    return out


def layer_norm(x, g, b):
    xf = x.astype(jnp.float32)
    mu = jnp.mean(xf, axis=-1, keepdims=True)
    var = jnp.mean(jnp.square(xf - mu), axis=-1, keepdims=True)
    return ((xf - mu) * lax.rsqrt(var + LN_EPS) * g + b).astype(x.dtype)


def t5_bucket(dist):
    n = jnp.maximum(dist, 0)
    max_exact = NUM_BUCKETS // 2
    nf = jnp.maximum(n, max_exact).astype(jnp.float32)
    large = max_exact + (jnp.log(nf / max_exact) / math.log(MAX_DISTANCE / max_exact)
                         * (NUM_BUCKETS - max_exact)).astype(jnp.int32)
    large = jnp.minimum(large, NUM_BUCKETS - 1)
    return jnp.where(n < max_exact, n, large)


def masked_softmax(logits, mask):
    logits = jnp.where(mask, logits.astype(jnp.float32), NEG)
    m = jnp.max(logits, axis=-1, keepdims=True)
    p = jnp.where(mask, jnp.exp(logits - m), 0.0)
    den = jnp.maximum(jnp.sum(p, axis=-1, keepdims=True), 1e-30)
    return p / den, (m + jnp.log(den))[..., 0]


def gather_rows(t, idx):
    return jax.vmap(lambda tt, ii: tt[ii])(t, idx)


def causal_conv(x, w):
    c = x.shape[-1]
    return lax.conv_general_dilated(
        x, w[:, None, :].astype(x.dtype), window_strides=(1,),
        padding=((w.shape[0] - 1, 0),), dimension_numbers=('NWC', 'WIO', 'NWC'),
        feature_group_count=c)


def mlstm_mixer(q, k, v, i_pre, f_pre, o_pre, norm_g):
    B, S, H, DK = q.shape
    DV = v.shape[-1]
    nc = S // M_CHUNK
    f32 = jnp.float32

    def chunks(t):
        t = t.astype(f32).reshape((B, nc, M_CHUNK) + t.shape[2:])
        return jnp.transpose(t, (1, 0, 3, 2) + tuple(range(4, t.ndim)))

    qc = chunks(q)
    kc = chunks(k * (DK ** -0.5))
    vc = chunks(v)
    ic = chunks(i_pre)
    fc = chunks(jax.nn.log_sigmoid(f_pre.astype(f32)))
    tri = jnp.tril(jnp.ones((M_CHUNK, M_CHUNK), dtype=bool))

    def step(carry, xs):
        C, n, m = carry
        qb, kb, vb, ib, fb = xs
        b = jnp.cumsum(fb, axis=-1)
        D = jnp.where(tri, b[..., :, None] - b[..., None, :] + ib[..., None, :], NEG)
        inter = b + m[..., None]
        m_t = jnp.maximum(inter, jnp.max(D, axis=-1))
        sc = jnp.einsum('bhtd,bhsd->bhts', qb, kb) * jnp.exp(D - m_t[..., None])
        wi = jnp.exp(inter - m_t)
        num = jnp.einsum('bhts,bhsv->bhtv', sc, vb) + wi[..., None] * jnp.einsum('bhtd,bhdv->bhtv', qb, C)
        den = jnp.sum(sc, axis=-1) + wi * jnp.einsum('bhtd,bhd->bht', qb, n)
        h = num / jnp.maximum(jnp.abs(den), jnp.exp(-m_t))[..., None]
        bL = b[..., -1]
        g = bL[..., None] - b + ib
        m_new = jnp.maximum(bL + m, jnp.max(g, axis=-1))
        ws = jnp.exp(g - m_new[..., None])
        wc = jnp.exp(bL + m - m_new)
        C = wc[..., None, None] * C + jnp.einsum('bhs,bhsd,bhsv->bhdv', ws, kb, vb)
        n = wc[..., None] * n + jnp.einsum('bhs,bhsd->bhd', ws, kb)
        return (C, n, m_new), h

    init = (jnp.zeros((B, H, DK, DV), f32), jnp.zeros((B, H, DK), f32), jnp.zeros((B, H), f32))
    _, hc = lax.scan(step, init, (qc, kc, vc, ic, fc))
    h = jnp.transpose(hc, (1, 0, 3, 2, 4)).reshape(B, S, H, DV)
    h = jax.nn.sigmoid(o_pre.astype(f32)).reshape(B, S, H, DV) * h
    mu = jnp.mean(h, axis=-1, keepdims=True)
    var = jnp.mean(jnp.square(h - mu), axis=-1, keepdims=True)
    h = (h - mu) * lax.rsqrt(var + LN_EPS)
    return (h.reshape(B, S, H * DV) * norm_g).astype(v.dtype)


def dilated_branch(q, k, v, bias_tab, window, dil):
    B, S, H, hd = q.shape
    W = window // dil
    L = S // dil
    nb = -(-L // W)
    Lp = nb * W

    def residues(t):
        t = jnp.transpose(t.reshape(B, L, dil, H, hd), (0, 2, 1, 3, 4))
        t = jnp.pad(t, ((0, 0), (0, 0), (0, Lp - L), (0, 0), (0, 0)))
        return t.reshape(B, dil, nb, W, H, hd)

    qb, kb, vb = residues(q), residues(k), residues(v)
    shift = lambda t: jnp.pad(t, ((0, 0), (0, 0), (1, 0), (0, 0), (0, 0), (0, 0)))[:, :, :-1]
    kk = jnp.concatenate([shift(kb), kb], axis=3)
    vv = jnp.concatenate([shift(vb), vb], axis=3)
    qi = jnp.arange(W)[:, None]
    ki = jnp.arange(2 * W)[None, :]
    j = W + qi - ki
    band = (j >= 0) & (j <= W)
    first_ok = (jnp.arange(nb)[:, None, None] > 0) | (ki[None] >= W)
    mask = (band[None] & first_ok)[None, None, :, None]
    bias = jnp.transpose(bias_tab[t5_bucket(j * dil)], (2, 0, 1))
    logits = jnp.einsum('brnqhd,brnkhd->brnhqk', qb, kk).astype(jnp.float32) * hd ** -0.5 + bias
    p, lse = masked_softmax(logits, mask)
    o = jnp.einsum('brnhqk,brnkhd->brnqhd', p, vv.astype(jnp.float32))
    o = jnp.transpose(o.reshape(B, dil, Lp, H, hd)[:, :, :L], (0, 2, 1, 3, 4)).reshape(B, S, H, hd)
    lse = jnp.transpose(lse, (0, 1, 2, 4, 3)).reshape(B, dil, Lp, H)[:, :, :L]
    lse = jnp.transpose(lse, (0, 2, 1, 3)).reshape(B, S, H)
    return o, lse


def dilated_mixer(q, k, v, bias_tab):
    B, S, H, hd = q.shape
    outs, lses = [], []
    for window, dil in DIL_PATTERNS:
        o, lse = dilated_branch(q, k, v, bias_tab, window, dil)
        outs.append(o)
        lses.append(lse)
    wts = jax.nn.softmax(jnp.stack(lses, axis=0), axis=0)
    o = jnp.sum(wts[..., None] * jnp.stack(outs, axis=0), axis=0)
    return o.reshape(B, S, H * hd).astype(q.dtype)


def dsa_mixer(q, k, v, iq, ik, iw, bias_tab):
    B, S, H, hd = q.shape
    topk = min(DSA_TOPK, S // 4)
    kpos = jnp.arange(S)
    iw = iw.astype(jnp.float32) * (IDX_HEADS * IDX_DIM) ** -0.5

    def block(i):
        s0 = i * DSA_QBLOCK
        tq = s0 + jnp.arange(DSA_QBLOCK)
        qb = lax.dynamic_slice_in_dim(q, s0, DSA_QBLOCK, axis=1)
        iqb = lax.dynamic_slice_in_dim(iq, s0, DSA_QBLOCK, axis=1)
        iwb = lax.dynamic_slice_in_dim(iw, s0, DSA_QBLOCK, axis=1)
        rel = jax.nn.relu(jnp.einsum('bqhd,bsd->bqhs', iqb, ik).astype(jnp.float32))
        score = jnp.einsum('bqhs,bqh->bqs', rel, iwb)
        score = jnp.where((kpos[None, :] <= tq[:, None])[None], score, NEG)
        _, idx = lax.top_k(score, topk)
        kg = gather_rows(k, idx)
        vg = gather_rows(v, idx)
        dist = tq[None, :, None] - idx
        bias = jnp.transpose(bias_tab[t5_bucket(dist)], (0, 3, 1, 2))
        logits = jnp.einsum('bqhd,bqkd->bhqk', qb, kg).astype(jnp.float32) * hd ** -0.5 + bias
        p, _ = masked_softmax(logits, (dist >= 0)[:, None])
        return jnp.einsum('bhqk,bqkd->bqhd', p, vg.astype(jnp.float32))

    out = lax.map(block, jnp.arange(S // DSA_QBLOCK))
    return jnp.transpose(out, (1, 0, 2, 3, 4)).reshape(B, S, H * hd).astype(q.dtype)


def nsa_compress(t, pos, w1, w2):
    B, S, hd = t.shape
    n_cmp = (S - NSA_CMP_LEN) // NSA_CMP_STRIDE + 1
    idx = jnp.arange(n_cmp)[:, None] * NSA_CMP_STRIDE + jnp.arange(NSA_CMP_LEN)[None, :]
    blocks = (t[:, idx] + pos).reshape(B, n_cmp, NSA_CMP_LEN * hd)
    return jax.nn.silu(blocks @ w1) @ w2


def nsa_mixer(q, kc, vc, ks, vs, kw, vw, gates, cmp_pos, cmp_w1, cmp_w2, bias_tab):
    B, S, H, hd = q.shape
    f32 = jnp.float32
    scale = hd ** -0.5
    kcmp = nsa_compress(kc, cmp_pos[0], cmp_w1[0], cmp_w2[0])
    vcmp = nsa_compress(vc, cmp_pos[1], cmp_w1[1], cmp_w2[1]).astype(f32)
    n_cmp = kcmp.shape[1]
    cmp_start = jnp.arange(n_cmp) * NSA_CMP_STRIDE
    cmp_end = cmp_start + NSA_CMP_LEN - 1
    n_sel = S // NSA_SEL_LEN
    topn = min(NSA_TOPN, n_sel)
    sel_start = jnp.arange(n_sel) * NSA_SEL_LEN
    overlap = jnp.clip(jnp.minimum(cmp_start[:, None] + NSA_CMP_LEN, sel_start[None] + NSA_SEL_LEN)
                       - jnp.maximum(cmp_start[:, None], sel_start[None]), 0).astype(f32) / NSA_CMP_LEN
    kw_pad = jnp.pad(kw, ((0, 0), (NSA_WINDOW, 0), (0, 0)))
    vw_pad = jnp.pad(vw, ((0, 0), (NSA_WINDOW, 0), (0, 0)))
    g = jax.nn.sigmoid(gates.astype(f32)).reshape(B, S, H, 3)
    jj = jnp.arange(n_sel)

    def block(i):
        s0 = i * NSA_QBLOCK
        tq = s0 + jnp.arange(NSA_QBLOCK)
        qb = lax.dynamic_slice_in_dim(q, s0, NSA_QBLOCK, axis=1)
        dist_c = tq[:, None] - cmp_end[None]
        bias_c = jnp.transpose(bias_tab[t5_bucket(dist_c)], (2, 0, 1))[None]
        lc = jnp.einsum('bqhd,bcd->bhqc', qb, kcmp).astype(f32) * scale + bias_c
        pc, _ = masked_softmax(lc, (dist_c >= 0)[None, None])
        o_c = jnp.einsum('bhqc,bcd->bqhd', pc, vcmp)
        imp = jnp.einsum('bhqc,cj->bqj', pc, overlap)
        cur = tq // NSA_SEL_LEN
        forced = (jj[None] == 0) | (jj[None] == cur[:, None]) | (jj[None] == cur[:, None] - 1)
        admissible = sel_start[None] <= tq[:, None]
        imp = jnp.where(forced[None], NSA_FORCE, imp)
        imp = jnp.where(admissible[None], imp, NEG)
        _, sel = lax.top_k(imp, topn)
        tok = (sel[..., None] * NSA_SEL_LEN + jnp.arange(NSA_SEL_LEN)).reshape(B, NSA_QBLOCK, topn * NSA_SEL_LEN)
        ksg = gather_rows(ks, tok)
        vsg = gather_rows(vs, tok)
        dist_s = tq[None, :, None] - tok
        bias_s = jnp.transpose(bias_tab[t5_bucket(dist_s)], (0, 3, 1, 2))
        ls = jnp.einsum('bqhd,bqkd->bhqk', qb, ksg).astype(f32) * scale + bias_s
        ps, _ = masked_softmax(ls, (dist_s >= 0)[:, None])
        o_s = jnp.einsum('bhqk,bqkd->bqhd', ps, vsg.astype(f32))
        kwb = lax.dynamic_slice_in_dim(kw_pad, s0, NSA_WINDOW + NSA_QBLOCK, axis=1)
        vwb = lax.dynamic_slice_in_dim(vw_pad, s0, NSA_WINDOW + NSA_QBLOCK, axis=1)
        kpos = s0 - NSA_WINDOW + jnp.arange(NSA_WINDOW + NSA_QBLOCK)
        dist_w = tq[:, None] - kpos[None]
        valid_w = (dist_w >= 0) & (dist_w < NSA_WINDOW) & (kpos[None] >= 0)
        bias_w = jnp.transpose(bias_tab[t5_bucket(dist_w)], (2, 0, 1))[None]
        lw = jnp.einsum('bqhd,bkd->bhqk', qb, kwb).astype(f32) * scale + bias_w
        pw, _ = masked_softmax(lw, valid_w[None, None])
        o_w = jnp.einsum('bhqk,bkd->bqhd', pw, vwb.astype(f32))
        gb = lax.dynamic_slice_in_dim(g, s0, NSA_QBLOCK, axis=1)
        return gb[..., 0:1] * o_c + gb[..., 1:2] * o_s + gb[..., 2:3] * o_w

    out = lax.map(block, jnp.arange(S // NSA_QBLOCK))
    return jnp.transpose(out, (1, 0, 2, 3, 4)).reshape(B, S, H * hd).astype(q.dtype)


def hybrid_layer(x, w_in, b_in, a_conv, a_norm, d_cmp_pos, d_cmp_w1, d_cmp_w2, w_out, b_out,
                 ln1_g, ln1_b, w_ff1, b_ff1, w_ff2, b_ff2, ln2_g, ln2_b, rel_bias):
    B, S, _ = x.shape
    H, hd = HEADS, HEAD_DIM
    z = split_cols(jnp.einsum('bsd,de->bse', x, w_in) + b_in)
    qk = jax.nn.silu(causal_conv(jnp.concatenate([z['a_q'], z['a_k']], axis=-1), a_conv))
    a_q, a_k = jnp.split(qk, 2, axis=-1)
    out_a = mlstm_mixer(a_q.reshape(B, S, H, M_QK_DIM), a_k.reshape(B, S, H, M_QK_DIM),
                        z['a_v'].reshape(B, S, H, hd), z['a_i'], z['a_f'], z['a_o'], a_norm)
    out_b = dilated_mixer(z['b_q'].reshape(B, S, H, hd), z['b_k'].reshape(B, S, H, hd),
                          z['b_v'].reshape(B, S, H, hd), rel_bias[:, 0:H])
    out_c = dsa_mixer(z['c_q'].reshape(B, S, H, hd), z['c_k'], z['c_v'],
                      z['c_iq'].reshape(B, S, IDX_HEADS, IDX_DIM), z['c_ik'], z['c_iw'],
                      rel_bias[:, H:2 * H])
    out_d = nsa_mixer(z['d_q'].reshape(B, S, H, hd), z['d_kc'], z['d_vc'], z['d_ks'], z['d_vs'],
                      z['d_kw'], z['d_vw'], z['d_g'], d_cmp_pos, d_cmp_w1, d_cmp_w2,
                      rel_bias[:, 2 * H:3 * H])
    mixed = jnp.concatenate([out_a, out_b, out_c, out_d], axis=-1).astype(x.dtype)
    x = layer_norm(ALPHA * x + (mixed @ w_out + b_out), ln1_g, ln1_b)
    ff = jnp.square(jax.nn.relu(x @ w_ff1 + b_ff1)) @ w_ff2 + b_ff2
    return layer_norm(ALPHA * x + ff, ln2_g, ln2_b)


def setup_inputs(seed: int = 0) -> dict:
    key = jax.random.key(seed)
    ks = jax.random.split(key, 20)
    f32 = jnp.float32

    def nrm(k, shape, scale):
        return scale * jax.random.normal(k, shape, f32)

    x = nrm(ks[0], (BATCH, SEQ, D_MODEL), 1.0)
    w_in = nrm(ks[1], (DEPTH, D_MODEL, D_IN), D_MODEL ** -0.5)
    f0, f1 = _col_range('a_f')
    b_in = nrm(ks[2], (DEPTH, D_IN), 0.02)
    b_in = b_in.at[:, f0:f1].set(3.0 + 3.0 * jax.random.uniform(ks[3], (DEPTH, f1 - f0), f32))
    a_conv = nrm(ks[4], (DEPTH, M_CONV, 2 * HEADS * M_QK_DIM), M_CONV ** -0.5)
    a_norm = 1.0 + nrm(ks[5], (DEPTH, GROUP_WIDTH), 0.02)
    d_cmp_pos = nrm(ks[6], (DEPTH, 2, NSA_CMP_LEN, HEAD_DIM), 0.02)
    d_cmp_w1 = nrm(ks[7], (DEPTH, 2, NSA_CMP_LEN * HEAD_DIM, NSA_CMP_HIDDEN), (NSA_CMP_LEN * HEAD_DIM) ** -0.5)
    d_cmp_w2 = nrm(ks[8], (DEPTH, 2, NSA_CMP_HIDDEN, HEAD_DIM), NSA_CMP_HIDDEN ** -0.5)
    w_out = nrm(ks[9], (DEPTH, MIX_WIDTH, D_MODEL), BETA * MIX_WIDTH ** -0.5)
    b_out = nrm(ks[10], (DEPTH, D_MODEL), 0.02)
    ln1_g = 1.0 + nrm(ks[11], (DEPTH, D_MODEL), 0.02)
    ln1_b = nrm(ks[12], (DEPTH, D_MODEL), 0.02)
    w_ff1 = nrm(ks[13], (DEPTH, D_MODEL, D_FF), D_MODEL ** -0.5)
    b_ff1 = nrm(ks[14], (DEPTH, D_FF), 0.02)
    w_ff2 = nrm(ks[15], (DEPTH, D_FF, D_MODEL), BETA * D_FF ** -0.5)
    b_ff2 = nrm(ks[16], (DEPTH, D_MODEL), 0.02)
    ln2_g = 1.0 + nrm(ks[17], (DEPTH, D_MODEL), 0.02)
    ln2_b = nrm(ks[18], (DEPTH, D_MODEL), 0.02)
    rel_bias = nrm(ks[19], (NUM_BUCKETS, N_BIAS_HEADS), 0.2)
    return {'x': x, 'w_in': w_in, 'b_in': b_in, 'a_conv': a_conv, 'a_norm': a_norm,
            'd_cmp_pos': d_cmp_pos, 'd_cmp_w1': d_cmp_w1, 'd_cmp_w2': d_cmp_w2,
            'w_out': w_out, 'b_out': b_out, 'ln1_g': ln1_g, 'ln1_b': ln1_b,
            'w_ff1': w_ff1, 'b_ff1': b_ff1, 'w_ff2': w_ff2, 'b_ff2': b_ff2,
            'ln2_g': ln2_g, 'ln2_b': ln2_b, 'rel_bias': rel_bias}


def reference(x, w_in, b_in, a_conv, a_norm, d_cmp_pos, d_cmp_w1, d_cmp_w2, w_out, b_out,
              ln1_g, ln1_b, w_ff1, b_ff1, w_ff2, b_ff2, ln2_g, ln2_b, rel_bias):
    h = x
    for l in range(DEPTH):
        h = hybrid_layer(h, w_in[l], b_in[l], a_conv[l], a_norm[l], d_cmp_pos[l], d_cmp_w1[l],
                         d_cmp_w2[l], w_out[l], b_out[l], ln1_g[l], ln1_b[l], w_ff1[l], b_ff1[l],
                         w_ff2[l], b_ff2[l], ln2_g[l], ln2_b[l], rel_bias)
    return h
```

```python
import functools
import math

import numpy as np
import jax
import jax.numpy as jnp
from jax import lax
from jax.experimental import pallas as pl
from jax.experimental.pallas import tpu as pltpu

F32 = jnp.float32
BF16 = jnp.bfloat16
I32 = jnp.int32

D_MODEL = 1024
N_MIXERS = 4
HEADS = 4
HEAD_DIM = D_MODEL // (N_MIXERS * HEADS)
GROUP_WIDTH = HEADS * HEAD_DIM
D_FF = 4 * D_MODEL
LN_EPS = 1e-5
NEG = -1e30

M_QK_DIM = HEAD_DIM // 2
M_CHUNK = 64
M_CONV = 4
DIL_PATTERNS = ((128, 1), (512, 4), (2048, 16))
IDX_HEADS = 4
IDX_DIM = 64
DSA_TOPK = 256
NSA_CMP_LEN = 32
NSA_CMP_STRIDE = 16
NSA_SEL_LEN = 64
NSA_TOPN = 16
NSA_WINDOW = 512
NSA_CMP_HIDDEN = 256
NSA_FORCE = 1e9
NUM_BUCKETS = 32
MAX_DISTANCE = 128

LANES = 128
BLK = 128
VMEM_LIMIT_BYTES = 56 * 1024 * 1024

IN_SPLITS = (
    ('a_q', HEADS * M_QK_DIM), ('a_k', HEADS * M_QK_DIM), ('a_v', GROUP_WIDTH),
    ('a_i', HEADS), ('a_f', HEADS), ('a_o', GROUP_WIDTH),
    ('b_q', GROUP_WIDTH), ('b_k', GROUP_WIDTH), ('b_v', GROUP_WIDTH),
    ('c_q', GROUP_WIDTH), ('c_k', HEAD_DIM), ('c_v', HEAD_DIM),
    ('c_iq', IDX_HEADS * IDX_DIM), ('c_ik', IDX_DIM), ('c_iw', IDX_HEADS),
    ('d_q', GROUP_WIDTH), ('d_kc', HEAD_DIM), ('d_vc', HEAD_DIM),
    ('d_ks', HEAD_DIM), ('d_vs', HEAD_DIM), ('d_kw', HEAD_DIM), ('d_vw', HEAD_DIM),
    ('d_g', 3 * HEADS),
)

GROUP_LAYOUT = (
    (('a_q', 'a_k'), ('a_v',), ('a_o',), ('a_i', 'a_f')),
    (('b_q',), ('b_k',), ('b_v',)),
    (('c_q',), ('c_iq',), ('c_k', 'c_v'), ('c_ik', 'c_iw')),
    (('d_q',), ('d_kc', 'd_vc'), ('d_ks', 'd_vs'), ('d_kw', 'd_vw'), ('d_g',)),
)


def _round_up(n, m):
    return -(-n // m) * m


def _projection_layout():
    offs, off = {}, 0
    for name, width in IN_SPLITS:
        offs[name] = (off, width)
        off += width
    perm, group_widths = [], []
    for group in GROUP_LAYOUT:
        start = len(perm)
        for chunk in group:
            cstart = len(perm)
            for name in chunk:
                o, w = offs[name]
                perm.extend(range(o, o + w))
            perm.extend([-1] * (_round_up(len(perm) - cstart, LANES) - (len(perm) - cstart)))
        group_widths.append(len(perm) - start)
    return np.asarray(perm, np.int32), tuple(group_widths)


PROJ_PERM, GROUP_WIDTHS = _projection_layout()
PROJ_WIDTH = int(sum(GROUP_WIDTHS))


def _t5_bucket_np(dist):
    n = np.maximum(dist, 0)
    max_exact = NUM_BUCKETS // 2
    nf = np.maximum(n, max_exact).astype(np.float32)
    large = max_exact + (np.log(nf / max_exact) / math.log(MAX_DISTANCE / max_exact)
                         * (NUM_BUCKETS - max_exact)).astype(np.int32)
    large = np.minimum(large, NUM_BUCKETS - 1)
    return np.where(n < max_exact, n, large).astype(np.int32)


def _nt_dot(a, b, precision=None):
    return lax.dot_general(a, b, (((1,), (1,)), ((), ())), precision=precision,
                           preferred_element_type=F32)


def _dot(a, b, precision=None):
    return jnp.dot(a, b, precision=precision, preferred_element_type=F32)


def _layer_norm_rows(y, g, b):
    mu = jnp.mean(y, axis=-1, keepdims=True)
    var = jnp.mean(jnp.square(y - mu), axis=-1, keepdims=True)
    return (y - mu) * lax.rsqrt(var + LN_EPS) * g + b


def _sigmoid(x):
    return 1.0 / (1.0 + jnp.exp(-x))


def _log_sigmoid(x):
    return -(jnp.maximum(-x, 0.0) + jnp.log1p(jnp.exp(-jnp.abs(x))))


def _sortable_key(x):
    bits = pltpu.bitcast(x, I32)
    return bits ^ ((bits >> 31) & jnp.int32(0x7FFFFFFF))


def _inproj_kernel(x_ref, w_ref, b_ref, *out_refs):
    xb = x_ref[...].astype(BF16)
    off = 0
    for o_ref, width in zip(out_refs, GROUP_WIDTHS):
        o_ref[...] = _dot(xb, w_ref[:, off:off + width]) + b_ref[:, off:off + width]
        off += width


def _inproj(x2d, w, b, tm=512):
    t = x2d.shape[0]
    return pl.pallas_call(
        _inproj_kernel,
        grid=(t // tm,),
        in_specs=[pl.BlockSpec((tm, D_MODEL), lambda i: (i, 0)),
                  pl.BlockSpec((D_MODEL, PROJ_WIDTH), lambda i: (0, 0)),
                  pl.BlockSpec((1, PROJ_WIDTH), lambda i: (0, 0))],
        out_specs=[pl.BlockSpec((tm, gw), lambda i: (i, 0)) for gw in GROUP_WIDTHS],
        out_shape=[jax.ShapeDtypeStruct((t, gw), F32) for gw in GROUP_WIDTHS],
        compiler_params=pltpu.CompilerParams(dimension_semantics=("arbitrary",),
                                             vmem_limit_bytes=VMEM_LIMIT_BYTES),
        name="inproj",
    )(x2d, w, b)


def _mlstm_kernel(z_ref, gt_ref, cw_ref, ng_ref, o_ref, xpad_ref):
    seq = z_ref.shape[1]
    L, DK, DV = M_CHUNK, M_QK_DIM, HEAD_DIM
    nqk = 2 * HEADS * DK
    xpad_ref[0:8, :] = jnp.zeros((8, nqk), F32)
    xpad_ref[8:, :] = z_ref[0, :, 0:nqk]
    row = lax.broadcasted_iota(I32, (L, L), 0)
    col = lax.broadcasted_iota(I32, (L, L), 1)
    tri = row >= col
    tri_l = tri.astype(F32)
    tri_u = (row <= col).astype(F32)
    cw = cw_ref[...]
    ng = ng_ref[...]
    hi = lax.Precision.HIGHEST

    def chunk(c, carry):
        cs, ns, ms = carry
        s0 = pl.multiple_of(c * L, L)
        xw = xpad_ref[pl.ds(s0, L + 8), :]
        y = sum(cw[j:j + 1, :] * xw[5 + j:5 + j + L, :] for j in range(M_CONV))
        qk = y * _sigmoid(y)
        q = qk[:, 0:HEADS * DK]
        k = qk[:, HEADS * DK:] * (DK ** -0.5)
        v = z_ref[0, pl.ds(s0, L), 256:512]
        op = z_ref[0, pl.ds(s0, L), 512:768]
        gc = z_ref[0, pl.ds(s0, L), 768:896]
        gr = gt_ref[0, c]
        bcol = _dot(tri_l, _log_sigmoid(gc), precision=hi)
        brow = _dot(_log_sigmoid(gr), tri_u, precision=hi)
        new_c, new_n, new_m, outs = [], [], [], []
        for h in range(HEADS):
            bc = bcol[:, HEADS + h:HEADS + h + 1]
            br = brow[HEADS + h:HEADS + h + 1, :]
            ir = gr[h:h + 1, :]
            ic = gc[:, h:h + 1]
            m = ms[h]
            dmat = jnp.where(tri, bc - br + ir, NEG)
            inter = bc + m
            m_t = jnp.maximum(inter, jnp.max(dmat, axis=-1, keepdims=True))
            qh = q[:, DK * h:DK * (h + 1)]
            kh = k[:, DK * h:DK * (h + 1)]
            vh = v[:, DV * h:DV * (h + 1)]
            qb, kb, vb = qh.astype(BF16), kh.astype(BF16), vh.astype(BF16)
            sc = _nt_dot(qb, kb) * jnp.exp(dmat - m_t)
            wi = jnp.exp(inter - m_t)
            num = _dot(sc.astype(BF16), vb) + wi * _dot(qb, cs[h].astype(BF16))
            den = jnp.sum(sc, axis=-1, keepdims=True) + wi * jnp.sum(qh * ns[h], axis=-1, keepdims=True)
            hh = num / jnp.maximum(jnp.abs(den), jnp.exp(-m_t))
            bl = bc[L - 1:L, :]
            g = bl - bc + ic
            m_new = jnp.maximum(bl + m, jnp.max(g, axis=0, keepdims=True))
            ws = jnp.exp(g - m_new)
            wc = jnp.exp(bl + m - m_new)
            kw = kh * ws
            upd = lax.dot_general(kw.astype(BF16), vb, (((0,), (0,)), ((), ())),
                                  preferred_element_type=F32)
            new_c.append(wc * cs[h] + upd)
            new_n.append(wc * ns[h] + jnp.sum(kw, axis=0, keepdims=True))
            new_m.append(m_new)
            og = _sigmoid(op[:, DV * h:DV * (h + 1)]) * hh
            mu = jnp.mean(og, axis=-1, keepdims=True)
            var = jnp.mean(jnp.square(og - mu), axis=-1, keepdims=True)
            outs.append((og - mu) * lax.rsqrt(var + LN_EPS) * ng[:, DV * h:DV * (h + 1)])
        o_ref[0, pl.ds(s0, L), :] = jnp.concatenate(outs, axis=-1)
        return tuple(new_c), tuple(new_n), tuple(new_m)

    init = (tuple(jnp.zeros((DK, DV), F32) for _ in range(HEADS)),
            tuple(jnp.zeros((1, DK), F32) for _ in range(HEADS)),
            tuple(jnp.zeros((1, 1), F32) for _ in range(HEADS)))
    lax.fori_loop(0, seq // L, chunk, init)


def _mlstm(za, gates_t, conv_w, norm_g):
    b, s, wa = za.shape
    nc = s // M_CHUNK
    return pl.pallas_call(
        _mlstm_kernel,
        grid=(b,),
        in_specs=[pl.BlockSpec((1, s, wa), lambda i: (i, 0, 0)),
                  pl.BlockSpec((1, nc, 8, M_CHUNK), lambda i: (i, 0, 0, 0)),
                  pl.BlockSpec((M_CONV, 2 * HEADS * M_QK_DIM), lambda i: (0, 0)),
                  pl.BlockSpec((1, GROUP_WIDTH), lambda i: (0, 0))],
        out_specs=pl.BlockSpec((1, s, GROUP_WIDTH), lambda i: (i, 0, 0)),
        out_shape=jax.ShapeDtypeStruct((b, s, GROUP_WIDTH), F32),
        scratch_shapes=[pltpu.VMEM((s + 8, 2 * HEADS * M_QK_DIM), F32)],
        compiler_params=pltpu.CompilerParams(dimension_semantics=("arbitrary",),
                                             vmem_limit_bytes=VMEM_LIMIT_BYTES),
        name="mlstm",
    )(za, gates_t, conv_w, norm_g)


def _dilated_kernel(q0_ref, q1_ref, k0_ref, k1_ref, v0_ref, v1_ref, bias_ref, o_ref, acc_ref, mx_ref, den_ref):
    seq = q0_ref.shape[1]
    W = BLK
    hd = HEAD_DIM
    npair = HEADS // 2
    scale = hd ** -0.5
    qi = lax.broadcasted_iota(I32, (W, 2 * W), 0)
    ki = lax.broadcasted_iota(I32, (W, 2 * W), 1)
    j = W + qi - ki
    band = (j >= 0) & (j <= W)
    q_refs, k_refs, v_refs = (q0_ref, q1_ref), (k0_ref, k1_ref), (v0_ref, v1_ref)

    for br, (window, dil) in enumerate(DIL_PATTERNS):
        assert window // dil == W
        nb = (seq // dil) // W

        def piece(idx, _, br=br, dil=dil, nb=nb):
            n = idx % nb
            if dil == 1:
                rows_q = pl.ds(pl.multiple_of(W * n, W), W)
                rows_p = pl.ds(pl.multiple_of(W * jnp.maximum(n - 1, 0), W), W)
            else:
                r = idx // nb
                rows_q = pl.ds(r + dil * W * n, W, stride=dil)
                rows_p = pl.ds(r + dil * W * jnp.maximum(n - 1, 0), W, stride=dil)
            mask = band & (ki >= jnp.where(n > 0, 0, W))
            for pr in range(npair):
                q = q_refs[pr][0, rows_q, :]
                k2 = jnp.concatenate([k_refs[pr][0, rows_p, :], k_refs[pr][0, rows_q, :]], axis=0)
                v2 = jnp.concatenate([v_refs[pr][0, rows_p, :], v_refs[pr][0, rows_q, :]], axis=0)
                accs, mxs, dens = [], [], []
                for hh in range(2):
                    sl = slice(hd * hh, hd * (hh + 1))
                    lg = _nt_dot(q[:, sl].astype(BF16), k2[:, sl].astype(BF16)) * scale + bias_ref[br, 2 * pr + hh]
                    lg = jnp.where(mask, lg, NEG)
                    m = jnp.max(lg, axis=-1, keepdims=True)
                    p = jnp.where(mask, jnp.exp(lg - m), 0.0)
                    dens.append(jnp.broadcast_to(jnp.sum(p, axis=-1, keepdims=True), (W, hd)))
                    mxs.append(jnp.broadcast_to(m, (W, hd)))
                    accs.append(_dot(p.astype(BF16), v2[:, sl].astype(BF16)))
                acc_ref[br, pr, rows_q, :] = jnp.concatenate(accs, axis=-1)
                mx_ref[br, pr, rows_q, :] = jnp.concatenate(mxs, axis=-1)
                den_ref[br, pr, rows_q, :] = jnp.concatenate(dens, axis=-1)
            return 0

        lax.fori_loop(0, dil * nb, piece, 0)

    def combine(i, _):
        rows = pl.ds(pl.multiple_of(i * W, W), W)
        outs = []
        for pr in range(npair):
            ms = [mx_ref[b, pr, rows, :] for b in range(len(DIL_PATTERNS))]
            top = functools.reduce(jnp.maximum, ms)
            es = [jnp.exp(m - top) for m in ms]
            num = sum(e * acc_ref[b, pr, rows, :] for b, e in enumerate(es))
            den = sum(e * jnp.maximum(den_ref[b, pr, rows, :], 1e-30) for b, e in enumerate(es))
            outs.append(num / den)
        o_ref[0, rows, :] = jnp.concatenate(outs, axis=-1)
        return 0

    lax.fori_loop(0, seq // W, combine, 0)


def _dilated(zb, bias):
    b, s, wb = zb.shape
    nbr = len(DIL_PATTERNS)
    pair_spec = lambda c: pl.BlockSpec((1, s, LANES), lambda i: (i, 0, c))
    return pl.pallas_call(
        _dilated_kernel,
        grid=(b,),
        in_specs=[pair_spec(c) for c in range(wb // LANES)]
                 + [pl.BlockSpec(bias.shape, lambda i: (0, 0, 0, 0))],
        out_specs=pl.BlockSpec((1, s, GROUP_WIDTH), lambda i: (i, 0, 0)),
        out_shape=jax.ShapeDtypeStruct((b, s, GROUP_WIDTH), F32),
        scratch_shapes=[pltpu.VMEM((nbr, HEADS // 2, s, LANES), F32) for _ in range(3)],
        compiler_params=pltpu.CompilerParams(dimension_semantics=("arbitrary",),
                                             vmem_limit_bytes=VMEM_LIMIT_BYTES),
        name="dilated",
    )(*([zb] * (wb // LANES)), bias)


def _stack_heads(q):
    return jnp.concatenate([q[:, HEAD_DIM * h:HEAD_DIM * (h + 1)] for h in range(HEADS)], axis=0)


def _unstack_heads_t(per_head_t):
    halves = []
    for h in range(0, HEADS, 2):
        halves.append(jnp.concatenate([per_head_t[h], per_head_t[h + 1]], axis=0).T)
    return jnp.concatenate(halves, axis=-1)


def _online_step(lg, mask, vt, m, den, acc):
    lg = jnp.where(mask, lg, NEG)
    m_new = jnp.maximum(m, jnp.max(lg, axis=0, keepdims=True))
    p = jnp.where(mask, jnp.exp(lg - m_new), 0.0)
    corr = jnp.exp(m - m_new)
    den = den * corr + jnp.sum(p, axis=0, keepdims=True)
    acc = acc * corr + _dot(vt, p.astype(BF16))
    return m_new, den, acc


def _online_init():
    return (tuple(jnp.full((1, BLK), NEG, F32) for _ in range(HEADS)),
            tuple(jnp.zeros((1, BLK), F32) for _ in range(HEADS)),
            tuple(jnp.zeros((HEAD_DIM, BLK), F32) for _ in range(HEADS)))


def _dsa_kernel(z_ref, btab_ref, o_ref, kvt_ref, key_ref):
    i = pl.program_id(1)
    seq = z_ref.shape[1]
    nkb = seq // BLK
    hd = HEAD_DIM
    topk = min(DSA_TOPK, seq // 4)
    scale = hd ** -0.5
    t0 = pl.multiple_of(i * BLK, BLK)

    @pl.when(i == 0)
    def _():
        for kb in range(nkb):
            kvt_ref[kb] = z_ref[0, kb * BLK:(kb + 1) * BLK, 512:640].T

    zq = z_ref[0, pl.ds(t0, BLK), :]
    cq = zq[:, 0:256]
    ciq = zq[:, 256:512].astype(BF16)
    iw = zq[:, 640:768].T[IDX_DIM:IDX_DIM + IDX_HEADS, :] * ((IDX_HEADS * IDX_DIM) ** -0.5)
    s_loc = lax.broadcasted_iota(I32, (BLK, BLK), 0)
    t_glob = t0 + lax.broadcasted_iota(I32, (BLK, BLK), 1)

    def score_block(kb, _):
        r0 = pl.multiple_of(kb * BLK, BLK)
        ik = z_ref[0, pl.ds(r0, BLK), 640:704].astype(BF16)
        sc = jnp.zeros((BLK, BLK), F32)
        for h in range(IDX_HEADS):
            rel = _nt_dot(ik, ciq[:, IDX_DIM * h:IDX_DIM * (h + 1)])
            sc = sc + jnp.maximum(rel, 0.0) * iw[h:h + 1, :]
        sc = jnp.where(r0 + s_loc <= t_glob, sc, NEG)
        key_ref[kb] = _sortable_key(sc)
        return 0

    lax.fori_loop(0, i + 1, score_block, 0)

    def count(pred):
        def body(kb, acc):
            return acc + pred(kb, key_ref[kb]).astype(I32)
        acc = lax.fori_loop(0, i + 1, body, jnp.zeros((BLK, BLK), I32))
        return jnp.sum(acc, axis=0, keepdims=True)

    int_min = jnp.int32(-2 ** 31)

    def thr_bit(it, thr):
        cand = thr + lax.shift_left(jnp.int32(1), 31 - it)
        c = count(lambda kb, key: key >= cand)
        return jnp.where(c >= topk, cand, thr)

    thr = lax.fori_loop(0, 32, thr_bit, jnp.full((1, BLK), int_min, I32))
    n_gt = count(lambda kb, key: key > thr)
    n_eq = count(lambda kb, key: key == thr)
    need = topk - n_gt

    def tie_search(_):
        def idx_bit(it, jm):
            cand = jm + lax.shift_left(jnp.int32(1), 10 - it)
            c = count(lambda kb, key: (key == thr) & (kb * BLK + s_loc < cand))
            return jnp.where(c < need, cand, jm)
        return lax.fori_loop(0, 11, idx_bit, jnp.zeros((1, BLK), I32))

    assert seq == 2 ** 11
    jmax = lax.cond(jnp.max(n_eq - need) > 0, tie_search,
                    lambda _: jnp.full((1, BLK), seq - 1, I32), 0)

    qs = _stack_heads(cq).astype(BF16)

    def attend(kb, carry):
        ms, dens, accs = carry
        r0 = pl.multiple_of(kb * BLK, BLK)
        kblk = z_ref[0, pl.ds(r0, BLK), 512:576].astype(BF16)
        lg = _nt_dot(kblk, qs) * scale + btab_ref[jnp.minimum(i - kb, 2)]
        key = key_ref[kb]
        s_glob = r0 + s_loc
        mask = ((key > thr) | ((key == thr) & (s_glob <= jmax))) & (s_glob <= t_glob)
        vt = kvt_ref[kb][hd:2 * hd, :].astype(BF16)
        out = [_online_step(lg[:, BLK * h:BLK * (h + 1)], mask, vt, ms[h], dens[h], accs[h])
               for h in range(HEADS)]
        return tuple(o[0] for o in out), tuple(o[1] for o in out), tuple(o[2] for o in out)

    ms, dens, accs = lax.fori_loop(0, i + 1, attend, _online_init())
    o_ref[0] = _unstack_heads_t([accs[h] / jnp.maximum(dens[h], 1e-30) for h in range(HEADS)])


def _dsa(zc, btab):
    b, s, wc = zc.shape
    nq = s // BLK
    return pl.pallas_call(
        _dsa_kernel,
        grid=(b, nq),
        in_specs=[pl.BlockSpec((1, s, wc), lambda bi, i: (bi, 0, 0)),
                  pl.BlockSpec(btab.shape, lambda bi, i: (0, 0, 0))],
        out_specs=pl.BlockSpec((1, BLK, GROUP_WIDTH), lambda bi, i: (bi, i, 0)),
        out_shape=jax.ShapeDtypeStruct((b, s, GROUP_WIDTH), F32),
        scratch_shapes=[pltpu.VMEM((nq, BLK, BLK), F32), pltpu.VMEM((nq, BLK, BLK), I32)],
        compiler_params=pltpu.CompilerParams(dimension_semantics=("arbitrary", "arbitrary"),
                                             vmem_limit_bytes=VMEM_LIMIT_BYTES),
        name="dsa",
    )(zc, btab)


def _nsa_kernel(z_ref, zc_ref, w1_ref, pos_ref, w2_ref, bsel_ref, bcmp_ref, ovt_ref, exp_ref, o_ref,
                kst_ref, kwt_ref, cmp_ref, cmpt_ref):
    i = pl.program_id(1)
    seq = z_ref.shape[1]
    nkb = seq // BLK
    hd = HEAD_DIM
    scale = hd ** -0.5
    n_cmp = (seq - NSA_CMP_LEN) // NSA_CMP_STRIDE + 1
    n_sel = seq // NSA_SEL_LEN
    topn = min(NSA_TOPN, n_sel)
    half = NSA_CMP_LEN // 2
    assert half == NSA_CMP_STRIDE and n_cmp + 1 == seq // NSA_CMP_STRIDE == BLK and n_sel <= BLK
    t0 = pl.multiple_of(i * BLK, BLK)
    hi = lax.Precision.HIGHEST

    @pl.when(i == 0)
    def _():
        for kb in range(nkb):
            rows = slice(kb * BLK, (kb + 1) * BLK)
            kst_ref[kb] = z_ref[0, rows, 384:512].T
            kwt_ref[kb] = z_ref[0, rows, 512:640].T
        first = jnp.zeros((BLK, 2 * NSA_CMP_HIDDEN), F32)
        second = jnp.zeros((BLK, 2 * NSA_CMP_HIDDEN), F32)
        for j in range(half):
            xj = zc_ref[0, pl.ds(j, BLK, stride=NSA_CMP_STRIDE), :]
            first = first + _dot((xj + pos_ref[j:j + 1, :]).astype(BF16), w1_ref[j])
            second = second + _dot((xj + pos_ref[half + j:half + j + 1, :]).astype(BF16), w1_ref[half + j])
        hid = first + pltpu.roll(second, BLK - 1, 0)
        hid = hid * _sigmoid(hid)
        cmp = _dot(hid.astype(BF16), w2_ref[...])
        cmp_ref[...] = cmp
        cmpt_ref[...] = cmp.T

    zq = z_ref[0, pl.ds(t0, BLK), :]
    qs = _stack_heads(zq[:, 0:256]).astype(BF16)
    gates = _sigmoid(zq[:, 640:768].T[0:16, :])
    row = lax.broadcasted_iota(I32, (BLK, BLK), 0)
    t_glob = t0 + lax.broadcasted_iota(I32, (BLK, BLK), 1)

    kcmp = cmp_ref[:, 0:hd].astype(BF16)
    vcmpt = cmpt_ref[hd:2 * hd, :].astype(BF16)
    lgc = _nt_dot(kcmp, qs) * scale + bcmp_ref[0]
    mask_c = (t_glob - (row * NSA_CMP_STRIDE + NSA_CMP_LEN - 1) >= 0) & (row < n_cmp)
    o_cmp, psum = [], jnp.zeros((BLK, BLK), F32)
    for h in range(HEADS):
        lg = jnp.where(mask_c, lgc[:, BLK * h:BLK * (h + 1)], NEG)
        m = jnp.max(lg, axis=0, keepdims=True)
        p = jnp.where(mask_c, jnp.exp(lg - m), 0.0)
        p = p / jnp.maximum(jnp.sum(p, axis=0, keepdims=True), 1e-30)
        o_cmp.append(_dot(vcmpt, p.astype(BF16)))
        psum = psum + p

    imp = _dot(ovt_ref[...], psum, precision=hi)
    cur = t_glob >> int(math.log2(NSA_SEL_LEN))
    forced = (row == 0) | (row == cur) | (row == cur - 1)
    imp = jnp.where(forced, NSA_FORCE, imp)
    imp = jnp.where(row * NSA_SEL_LEN <= t_glob, imp, NEG)
    imp = imp[0:n_sel, :]
    jrow = row[0:n_sel, :]
    rank = jnp.zeros((n_sel, BLK), I32)
    for jp in range(n_sel):
        other = imp[jp:jp + 1, :]
        rank = rank + ((other > imp) | ((other == imp) & (jp < jrow))).astype(I32)
    chosen = jnp.where(rank < topn, 1.0, 0.0)
    chosen = jnp.concatenate([chosen, jnp.zeros((BLK - n_sel, BLK), F32)], axis=0).astype(BF16)

    def attend(kt_ref, lanes, mask_fn):
        def body(kb, carry):
            ms, dens, accs = carry
            r0 = pl.multiple_of(kb * BLK, BLK)
            kblk = z_ref[0, pl.ds(r0, BLK), lanes].astype(BF16)
            lg = _nt_dot(kblk, qs) * scale + bsel_ref[jnp.minimum(i - kb, 2)]
            mask = mask_fn(r0)
            vt = kt_ref[kb][hd:2 * hd, :].astype(BF16)
            out = [_online_step(lg[:, BLK * h:BLK * (h + 1)], mask, vt, ms[h], dens[h], accs[h])
                   for h in range(HEADS)]
            return tuple(o[0] for o in out), tuple(o[1] for o in out), tuple(o[2] for o in out)
        return body

    def mask_sel(r0):
        picked = _dot(exp_ref[pl.ds(r0, BLK), :], chosen) > 0.5
        return picked & (r0 + row <= t_glob)

    def mask_win(r0):
        dist = t_glob - (r0 + row)
        return (dist >= 0) & (dist < NSA_WINDOW)

    _, den_s, acc_s = lax.fori_loop(0, i + 1, attend(kst_ref, slice(384, 448), mask_sel), _online_init())
    first_w = jnp.maximum(i - NSA_WINDOW // BLK, 0)
    _, den_w, acc_w = lax.fori_loop(first_w, i + 1, attend(kwt_ref, slice(512, 576), mask_win), _online_init())

    outs = []
    for h in range(HEADS):
        o_s = acc_s[h] / jnp.maximum(den_s[h], 1e-30)
        o_w = acc_w[h] / jnp.maximum(den_w[h], 1e-30)
        outs.append(gates[3 * h:3 * h + 1, :] * o_cmp[h] + gates[3 * h + 1:3 * h + 2, :] * o_s
                    + gates[3 * h + 2:3 * h + 3, :] * o_w)
    o_ref[0] = _unstack_heads_t(outs)


def _nsa(zd, w1, pos, w2, bsel, bcmp, ovt, expand):
    b, s, wd = zd.shape
    nq = s // BLK
    full = lambda a: pl.BlockSpec(a.shape, lambda bi, i: (0,) * a.ndim)
    return pl.pallas_call(
        _nsa_kernel,
        grid=(b, nq),
        in_specs=[pl.BlockSpec((1, s, wd), lambda bi, i: (bi, 0, 0)),
                  pl.BlockSpec((1, s, LANES), lambda bi, i: (bi, 0, GROUP_WIDTH // LANES)),
                  full(w1), full(pos), full(w2), full(bsel),
                  pl.BlockSpec((1, BLK, HEADS * BLK), lambda bi, i: (i, 0, 0)),
                  full(ovt), full(expand)],
        out_specs=pl.BlockSpec((1, BLK, GROUP_WIDTH), lambda bi, i: (bi, i, 0)),
        out_shape=jax.ShapeDtypeStruct((b, s, GROUP_WIDTH), F32),
        scratch_shapes=[pltpu.VMEM((nq, BLK, BLK), F32), pltpu.VMEM((nq, BLK, BLK), F32),
                        pltpu.VMEM((BLK, BLK), F32), pltpu.VMEM((BLK, BLK), F32)],
        compiler_params=pltpu.CompilerParams(dimension_semantics=("arbitrary", "arbitrary"),
                                             vmem_limit_bytes=VMEM_LIMIT_BYTES),
        name="nsa",
    )(zd, zd, w1, pos, w2, bsel, bcmp, ovt, expand)


def _outproj_kernel(alpha, x_ref, a_ref, b_ref, c_ref, d_ref, w_ref, bo_ref, g_ref, beta_ref, o_ref):
    acc = bo_ref[...] + _dot(a_ref[...].astype(BF16), w_ref[0:GROUP_WIDTH, :])
    for n, m_ref in enumerate((b_ref, c_ref, d_ref), start=1):
        acc = acc + _dot(m_ref[...].astype(BF16), w_ref[n * GROUP_WIDTH:(n + 1) * GROUP_WIDTH, :])
    o_ref[...] = _layer_norm_rows(alpha * x_ref[...] + acc, g_ref[...], beta_ref[...])


def _outproj(alpha, x2d, mixed, w, bo, g, beta, tm=512):
    t = x2d.shape[0]
    row_spec = lambda width: pl.BlockSpec((tm, width), lambda i: (i, 0))
    const = lambda a: pl.BlockSpec(a.shape, lambda i: (0, 0))
    return pl.pallas_call(
        functools.partial(_outproj_kernel, alpha),
        grid=(t // tm,),
        in_specs=[row_spec(D_MODEL)] + [row_spec(GROUP_WIDTH)] * N_MIXERS
                 + [const(w), const(bo), const(g), const(beta)],
        out_specs=row_spec(D_MODEL),
        out_shape=jax.ShapeDtypeStruct((t, D_MODEL), F32),
        compiler_params=pltpu.CompilerParams(dimension_semantics=("arbitrary",),
                                             vmem_limit_bytes=VMEM_LIMIT_BYTES),
        name="outproj_ln",
    )(x2d, *mixed, w, bo, g, beta)


def _ffn_kernel(alpha, x_ref, w1_ref, b1_ref, w2_ref, b2_ref, g_ref, beta_ref, o_ref, xb_ref, acc_ref):
    j = pl.program_id(1)

    @pl.when(j == 0)
    def _():
        xb_ref[...] = x_ref[...].astype(BF16)
        acc_ref[...] = jnp.zeros_like(acc_ref)

    hdn = jnp.maximum(_dot(xb_ref[...], w1_ref[...]) + b1_ref[...], 0.0)
    acc_ref[...] += _dot(jnp.square(hdn).astype(BF16), w2_ref[...])

    @pl.when(j == pl.num_programs(1) - 1)
    def _():
        y = alpha * x_ref[...] + (acc_ref[...] + b2_ref[...])
        o_ref[...] = _layer_norm_rows(y, g_ref[...], beta_ref[...])


def _ffn(alpha, x2d, w1, b1, w2, b2, g, beta, tm=1024, tf=512):
    t = x2d.shape[0]
    return pl.pallas_call(
        functools.partial(_ffn_kernel, alpha),
        grid=(t // tm, D_FF // tf),
        in_specs=[pl.BlockSpec((tm, D_MODEL), lambda i, j: (i, 0)),
                  pl.BlockSpec((D_MODEL, tf), lambda i, j: (0, j)),
                  pl.BlockSpec((1, tf), lambda i, j: (0, j)),
                  pl.BlockSpec((tf, D_MODEL), lambda i, j: (j, 0)),
                  pl.BlockSpec((1, D_MODEL), lambda i, j: (0, 0)),
                  pl.BlockSpec((1, D_MODEL), lambda i, j: (0, 0)),
                  pl.BlockSpec((1, D_MODEL), lambda i, j: (0, 0))],
        out_specs=pl.BlockSpec((tm, D_MODEL), lambda i, j: (i, 0)),
        out_shape=jax.ShapeDtypeStruct((t, D_MODEL), F32),
        scratch_shapes=[pltpu.VMEM((tm, D_MODEL), BF16), pltpu.VMEM((tm, D_MODEL), F32)],
        compiler_params=pltpu.CompilerParams(dimension_semantics=("arbitrary", "arbitrary"),
                                             vmem_limit_bytes=VMEM_LIMIT_BYTES),
        name="ffn_ln",
    )(x2d, w1, b1, w2, b2, g, beta)


def _dilated_bias(rel_bias):
    qi = np.arange(BLK)[:, None]
    ki = np.arange(2 * BLK)[None, :]
    idx = np.stack([_t5_bucket_np((BLK + qi - ki) * dil) for _, dil in DIL_PATTERNS])
    return jnp.transpose(rel_bias[idx][..., 0:HEADS], (0, 3, 1, 2))


def _toeplitz_bias_t(rel_bias_heads):
    c = np.arange(BLK)[:, None]
    a = np.arange(BLK)[None, :]
    idx = np.stack([_t5_bucket_np(BLK * delta + a - c) for delta in range(3)])
    assert (_t5_bucket_np(np.arange(BLK + 1, 64 * BLK)) == NUM_BUCKETS - 1).all()
    tab = rel_bias_heads[idx]
    return jnp.transpose(tab, (0, 1, 3, 2)).reshape(3, BLK, HEADS * BLK)


def _compressed_bias_t(rel_bias_heads, seq):
    nq = seq // BLK
    c = np.arange(BLK)[None, :, None]
    t = (np.arange(nq)[:, None, None] * BLK) + np.arange(BLK)[None, None, :]
    idx = _t5_bucket_np(t - (c * NSA_CMP_STRIDE + NSA_CMP_LEN - 1))
    tab = rel_bias_heads[idx]
    return jnp.transpose(tab, (0, 1, 3, 2)).reshape(nq, BLK, HEADS * BLK)


def _nsa_constants(seq):
    n_cmp = (seq - NSA_CMP_LEN) // NSA_CMP_STRIDE + 1
    n_sel = seq // NSA_SEL_LEN
    cs = np.arange(n_cmp)[:, None] * NSA_CMP_STRIDE
    ss = np.arange(n_sel)[None, :] * NSA_SEL_LEN
    ov = np.clip(np.minimum(cs + NSA_CMP_LEN, ss + NSA_SEL_LEN) - np.maximum(cs, ss), 0, None) / NSA_CMP_LEN
    ovt = np.zeros((BLK, BLK), np.float32)
    ovt[:n_sel, :n_cmp] = ov.T
    expand = np.zeros((seq, BLK), np.float32)
    expand[np.arange(seq), np.arange(seq) // NSA_SEL_LEN] = 1.0
    return jnp.asarray(ovt), jnp.asarray(expand, BF16)


def _nsa_weights(cmp_pos, cmp_w1, cmp_w2):
    hd, hid = HEAD_DIM, NSA_CMP_HIDDEN
    w1 = cmp_w1.reshape(2, NSA_CMP_LEN, hd, hid)
    zeros = jnp.zeros((NSA_CMP_LEN, hd, hid), F32)
    w1 = jnp.concatenate([jnp.concatenate([w1[0], zeros], axis=-1),
                          jnp.concatenate([zeros, w1[1]], axis=-1)], axis=1)
    pos = jnp.concatenate([cmp_pos[0], cmp_pos[1]], axis=-1)
    z2 = jnp.zeros((hid, hd), F32)
    w2 = jnp.concatenate([jnp.concatenate([cmp_w2[0], z2], axis=-1),
                          jnp.concatenate([z2, cmp_w2[1]], axis=-1)], axis=0)
    return w1.astype(BF16), pos, w2.astype(BF16)


def kernel(x, w_in, b_in, a_conv, a_norm, d_cmp_pos, d_cmp_w1, d_cmp_w2, w_out, b_out, ln1_g, ln1_b,
           w_ff1, b_ff1, w_ff2, b_ff2, ln2_g, ln2_b, rel_bias):
    bsz, seq, _ = x.shape
    depth = w_in.shape[0]
    alpha = (2 * depth) ** 0.25
    nc = seq // M_CHUNK
    perm = jnp.asarray(np.maximum(PROJ_PERM, 0))
    keep = jnp.asarray(PROJ_PERM >= 0)

    bias_dil = _dilated_bias(rel_bias)
    btab_dsa = _toeplitz_bias_t(rel_bias[:, HEADS:2 * HEADS])
    btab_nsa = _toeplitz_bias_t(rel_bias[:, 2 * HEADS:3 * HEADS])
    bcmp_nsa = _compressed_bias_t(rel_bias[:, 2 * HEADS:3 * HEADS], seq)
    ovt, expand = _nsa_constants(seq)

    h = x.reshape(bsz * seq, D_MODEL)
    for l in range(depth):
        w_l = jnp.where(keep[None, :], w_in[l][:, perm], 0.0).astype(BF16)
        b_l = jnp.where(keep, b_in[l][perm], 0.0)[None, :]
        za, zb, zc, zd = (z.reshape(bsz, seq, -1) for z in _inproj(h, w_l, b_l))
        gates_t = jnp.swapaxes(za[:, :, 768:776].reshape(bsz, nc, M_CHUNK, 2 * HEADS), 2, 3)
        out_a = _mlstm(za, gates_t, a_conv[l], a_norm[l][None, :])
        out_b = _dilated(zb, bias_dil)
        out_c = _dsa(zc, btab_dsa)
        nsa_w1, nsa_pos, nsa_w2 = _nsa_weights(d_cmp_pos[l], d_cmp_w1[l], d_cmp_w2[l])
        out_d = _nsa(zd, nsa_w1, nsa_pos, nsa_w2, btab_nsa, bcmp_nsa, ovt, expand)
        mixed = [o.reshape(bsz * seq, GROUP_WIDTH) for o in (out_a, out_b, out_c, out_d)]
        h = _outproj(alpha, h, mixed, w_out[l].astype(BF16), b_out[l][None, :],
                     ln1_g[l][None, :], ln1_b[l][None, :])
        h = _ffn(alpha, h, w_ff1[l].astype(BF16), b_ff1[l][None, :], w_ff2[l].astype(BF16),
                 b_ff2[l][None, :], ln2_g[l][None, :], ln2_b[l][None, :])
    return h.reshape(bsz, seq, D_MODEL)
```

```python
import functools
import math

import numpy as np
import jax
import jax.numpy as jnp
from jax import lax
from jax.experimental import pallas as pl
from jax.experimental.pallas import tpu as pltpu

F32 = jnp.float32
BF16 = jnp.bfloat16
I32 = jnp.int32
I16 = jnp.int16

D_MODEL = 1024
N_MIXERS = 4
HEADS = 4
HEAD_DIM = D_MODEL // (N_MIXERS * HEADS)
GROUP_WIDTH = HEADS * HEAD_DIM
D_FF = 4 * D_MODEL
LN_EPS = 1e-5
NEG = -1e30

M_QK_DIM = HEAD_DIM // 2
M_CHUNK = 64
M_CONV = 4
DIL_PATTERNS = ((128, 1), (512, 4), (2048, 16))
IDX_HEADS = 4
IDX_DIM = 64
DSA_TOPK = 256
NSA_CMP_LEN = 32
NSA_CMP_STRIDE = 16
NSA_SEL_LEN = 64
NSA_TOPN = 16
NSA_WINDOW = 512
NSA_CMP_HIDDEN = 256
NSA_FORCE = 1e9
NUM_BUCKETS = 32
MAX_DISTANCE = 128

LANES = 128
BLK = 128
KEYS = 2 * BLK
VMEM_LIMIT_BYTES = 56 * 1024 * 1024

IN_SPLITS = (
    ('a_q', HEADS * M_QK_DIM), ('a_k', HEADS * M_QK_DIM), ('a_v', GROUP_WIDTH),
    ('a_i', HEADS), ('a_f', HEADS), ('a_o', GROUP_WIDTH),
    ('b_q', GROUP_WIDTH), ('b_k', GROUP_WIDTH), ('b_v', GROUP_WIDTH),
    ('c_q', GROUP_WIDTH), ('c_k', HEAD_DIM), ('c_v', HEAD_DIM),
    ('c_iq', IDX_HEADS * IDX_DIM), ('c_ik', IDX_DIM), ('c_iw', IDX_HEADS),
    ('d_q', GROUP_WIDTH), ('d_kc', HEAD_DIM), ('d_vc', HEAD_DIM),
    ('d_ks', HEAD_DIM), ('d_vs', HEAD_DIM), ('d_kw', HEAD_DIM), ('d_vw', HEAD_DIM),
    ('d_g', 3 * HEADS),
)

GROUP_LAYOUT = (
    (('a_q', 'a_k'), ('a_v',), ('a_o',), ('a_i', 'a_f')),
    (('b_q',), ('b_k',), ('b_v',)),
    (('c_q',), ('c_iq',), ('c_k', 'c_v'), ('c_ik', 'c_iw')),
    (('d_q',), ('d_kc', 'd_vc'), ('d_ks', 'd_vs'), ('d_kw', 'd_vw'), ('d_g',)),
)


def _round_up(n, m):
    return -(-n // m) * m


def _projection_layout():
    offs, off = {}, 0
    for name, width in IN_SPLITS:
        offs[name] = (off, width)
        off += width
    perm, group_widths = [], []
    for group in GROUP_LAYOUT:
        start = len(perm)
        for chunk in group:
            cstart = len(perm)
            for name in chunk:
                o, w = offs[name]
                perm.extend(range(o, o + w))
            perm.extend([-1] * (_round_up(len(perm) - cstart, LANES) - (len(perm) - cstart)))
        group_widths.append(len(perm) - start)
    return np.asarray(perm, np.int32), tuple(group_widths)


PROJ_PERM, GROUP_WIDTHS = _projection_layout()
PROJ_WIDTH = int(sum(GROUP_WIDTHS))


def _permute_columns(a):
    parts, start = [], 0
    for end in range(1, PROJ_WIDTH + 1):
        src = int(PROJ_PERM[start])
        run_ends = end == PROJ_WIDTH or (PROJ_PERM[end] != PROJ_PERM[end - 1] + 1 if src >= 0 else PROJ_PERM[end] >= 0)
        if run_ends:
            parts.append(a[..., src:src + end - start] if src >= 0
                         else jnp.zeros(a.shape[:-1] + (end - start,), a.dtype))
            start = end
    return jnp.concatenate(parts, axis=-1)


def _t5_bucket_np(dist):
    n = np.maximum(dist, 0)
    max_exact = NUM_BUCKETS // 2
    nf = np.maximum(n, max_exact).astype(np.float32)
    large = max_exact + (np.log(nf / max_exact) / math.log(MAX_DISTANCE / max_exact)
                         * (NUM_BUCKETS - max_exact)).astype(np.int32)
    large = np.minimum(large, NUM_BUCKETS - 1)
    return np.where(n < max_exact, n, large).astype(np.int32)


def _nt_dot(a, b, precision=None):
    return lax.dot_general(a, b, (((1,), (1,)), ((), ())), precision=precision,
                           preferred_element_type=F32)


def _dot(a, b, precision=None):
    return jnp.dot(a, b, precision=precision, preferred_element_type=F32)


def _layer_norm_rows(y, g, b):
    mu = jnp.mean(y, axis=-1, keepdims=True)
    var = jnp.mean(jnp.square(y - mu), axis=-1, keepdims=True)
    return (y - mu) * lax.rsqrt(var + LN_EPS) * g + b


def _sigmoid(x):
    return 1.0 / (1.0 + jnp.exp(-x))


def _log_sigmoid(x):
    return -(jnp.maximum(-x, 0.0) + jnp.log1p(jnp.exp(-jnp.abs(x))))


def _sortable_key(x):
    bits = pltpu.bitcast(x, I32)
    return bits ^ ((bits >> 31) & jnp.int32(0x7FFFFFFF))


def _inproj_kernel(x_ref, w_ref, b_ref, *out_refs):
    xb = x_ref[...].astype(BF16)
    off = 0
    for o_ref, width in zip(out_refs, GROUP_WIDTHS):
        o_ref[...] = _dot(xb, w_ref[:, off:off + width]) + b_ref[:, off:off + width]
        off += width


def _inproj(x2d, w, b, tm=512):
    t = x2d.shape[0]
    return pl.pallas_call(
        _inproj_kernel,
        grid=(t // tm,),
        in_specs=[pl.BlockSpec((tm, D_MODEL), lambda i: (i, 0)),
                  pl.BlockSpec((D_MODEL, PROJ_WIDTH), lambda i: (0, 0)),
                  pl.BlockSpec((1, PROJ_WIDTH), lambda i: (0, 0))],
        out_specs=[pl.BlockSpec((tm, gw), lambda i: (i, 0)) for gw in GROUP_WIDTHS],
        out_shape=[jax.ShapeDtypeStruct((t, gw), F32) for gw in GROUP_WIDTHS],
        compiler_params=pltpu.CompilerParams(dimension_semantics=("arbitrary",),
                                             vmem_limit_bytes=VMEM_LIMIT_BYTES),
        name="inproj",
    )(x2d, w, b)


def _mlstm_kernel(z_ref, gt_ref, cw_ref, ng_ref, o_ref, xpad_ref):
    seq = z_ref.shape[1]
    L, DK, DV = M_CHUNK, M_QK_DIM, HEAD_DIM
    nqk = 2 * HEADS * DK
    xpad_ref[0:8, :] = jnp.zeros((8, nqk), F32)
    xpad_ref[8:, :] = z_ref[0, :, 0:nqk]
    row = lax.broadcasted_iota(I32, (L, L), 0)
    col = lax.broadcasted_iota(I32, (L, L), 1)
    tri = row >= col
    tri_l = tri.astype(F32)
    tri_u = (row <= col).astype(F32)
    cw = cw_ref[...]
    ng = ng_ref[...]
    hi = lax.Precision.HIGHEST

    def chunk(c, carry):
        cs, ns, ms = carry
        s0 = pl.multiple_of(c * L, L)
        xw = xpad_ref[pl.ds(s0, L + 8), :]
        y = sum(cw[j:j + 1, :] * xw[5 + j:5 + j + L, :] for j in range(M_CONV))
        qk = y * _sigmoid(y)
        q = qk[:, 0:HEADS * DK]
        k = qk[:, HEADS * DK:] * (DK ** -0.5)
        v = z_ref[0, pl.ds(s0, L), 256:512]
        op = z_ref[0, pl.ds(s0, L), 512:768]
        gc = z_ref[0, pl.ds(s0, L), 768:896]
        gr = gt_ref[0, c]
        bcol = _dot(tri_l, _log_sigmoid(gc), precision=hi)
        brow = _dot(_log_sigmoid(gr), tri_u, precision=hi)
        new_c, new_n, new_m, outs = [], [], [], []
        for h in range(HEADS):
            bc = bcol[:, HEADS + h:HEADS + h + 1]
            br = brow[HEADS + h:HEADS + h + 1, :]
            ir = gr[h:h + 1, :]
            ic = gc[:, h:h + 1]
            m = ms[h]
            dmat = jnp.where(tri, bc - br + ir, NEG)
            inter = bc + m
            m_t = jnp.maximum(inter, jnp.max(dmat, axis=-1, keepdims=True))
            qh = q[:, DK * h:DK * (h + 1)]
            kh = k[:, DK * h:DK * (h + 1)]
            vh = v[:, DV * h:DV * (h + 1)]
            qb, kb, vb = qh.astype(BF16), kh.astype(BF16), vh.astype(BF16)
            sc = _nt_dot(qb, kb) * jnp.exp(dmat - m_t)
            wi = jnp.exp(inter - m_t)
            num = _dot(sc.astype(BF16), vb) + wi * _dot(qb, cs[h].astype(BF16))
            den = jnp.sum(sc, axis=-1, keepdims=True) + wi * jnp.sum(qh * ns[h], axis=-1, keepdims=True)
            hh = num / jnp.maximum(jnp.abs(den), jnp.exp(-m_t))
            bl = bc[L - 1:L, :]
            g = bl - bc + ic
            m_new = jnp.maximum(bl + m, jnp.max(g, axis=0, keepdims=True))
            ws = jnp.exp(g - m_new)
            wc = jnp.exp(bl + m - m_new)
            kw = kh * ws
            upd = lax.dot_general(kw.astype(BF16), vb, (((0,), (0,)), ((), ())),
                                  preferred_element_type=F32)
            new_c.append(wc * cs[h] + upd)
            new_n.append(wc * ns[h] + jnp.sum(kw, axis=0, keepdims=True))
            new_m.append(m_new)
            og = _sigmoid(op[:, DV * h:DV * (h + 1)]) * hh
            mu = jnp.mean(og, axis=-1, keepdims=True)
            var = jnp.mean(jnp.square(og - mu), axis=-1, keepdims=True)
            outs.append((og - mu) * lax.rsqrt(var + LN_EPS) * ng[:, DV * h:DV * (h + 1)])
        o_ref[0, pl.ds(s0, L), :] = jnp.concatenate(outs, axis=-1)
        return tuple(new_c), tuple(new_n), tuple(new_m)

    init = (tuple(jnp.zeros((DK, DV), F32) for _ in range(HEADS)),
            tuple(jnp.zeros((1, DK), F32) for _ in range(HEADS)),
            tuple(jnp.zeros((1, 1), F32) for _ in range(HEADS)))
    lax.fori_loop(0, seq // L, chunk, init)


def _mlstm(za, gates_t, conv_w, norm_g):
    b, s, wa = za.shape
    nc = s // M_CHUNK
    return pl.pallas_call(
        _mlstm_kernel,
        grid=(b,),
        in_specs=[pl.BlockSpec((1, s, wa), lambda i: (i, 0, 0)),
                  pl.BlockSpec((1, nc, 8, M_CHUNK), lambda i: (i, 0, 0, 0)),
                  pl.BlockSpec((M_CONV, 2 * HEADS * M_QK_DIM), lambda i: (0, 0)),
                  pl.BlockSpec((1, GROUP_WIDTH), lambda i: (0, 0))],
        out_specs=pl.BlockSpec((1, s, GROUP_WIDTH), lambda i: (i, 0, 0)),
        out_shape=jax.ShapeDtypeStruct((b, s, GROUP_WIDTH), F32),
        scratch_shapes=[pltpu.VMEM((s + 8, 2 * HEADS * M_QK_DIM), F32)],
        compiler_params=pltpu.CompilerParams(dimension_semantics=("arbitrary",),
                                             vmem_limit_bytes=VMEM_LIMIT_BYTES),
        name="mlstm",
    )(za, gates_t, conv_w, norm_g)


def _dilated_kernel(q0_ref, q1_ref, k0_ref, k1_ref, v0_ref, v1_ref, bias_ref, o_ref, acc_ref, mx_ref, den_ref):
    seq = q0_ref.shape[1]
    W = BLK
    hd = HEAD_DIM
    npair = HEADS // 2
    scale = hd ** -0.5
    qi = lax.broadcasted_iota(I32, (W, 2 * W), 0)
    ki = lax.broadcasted_iota(I32, (W, 2 * W), 1)
    j = W + qi - ki
    band = (j >= 0) & (j <= W)
    q_refs, k_refs, v_refs = (q0_ref, q1_ref), (k0_ref, k1_ref), (v0_ref, v1_ref)

    for br, (window, dil) in enumerate(DIL_PATTERNS):
        assert window // dil == W
        nb = (seq // dil) // W

        def piece(idx, _, br=br, dil=dil, nb=nb):
            n = idx % nb
            if dil == 1:
                rows_q = pl.ds(pl.multiple_of(W * n, W), W)
                rows_p = pl.ds(pl.multiple_of(W * jnp.maximum(n - 1, 0), W), W)
            else:
                r = idx // nb
                rows_q = pl.ds(r + dil * W * n, W, stride=dil)
                rows_p = pl.ds(r + dil * W * jnp.maximum(n - 1, 0), W, stride=dil)
            mask = band & (ki >= jnp.where(n > 0, 0, W))
            for pr in range(npair):
                q = q_refs[pr][0, rows_q, :]
                k2 = jnp.concatenate([k_refs[pr][0, rows_p, :], k_refs[pr][0, rows_q, :]], axis=0)
                v2 = jnp.concatenate([v_refs[pr][0, rows_p, :], v_refs[pr][0, rows_q, :]], axis=0)
                accs, mxs, dens = [], [], []
                for hh in range(2):
                    sl = slice(hd * hh, hd * (hh + 1))
                    lg = _nt_dot(q[:, sl].astype(BF16), k2[:, sl].astype(BF16)) * scale + bias_ref[br, 2 * pr + hh]
                    lg = jnp.where(mask, lg, NEG)
                    m = jnp.max(lg, axis=-1, keepdims=True)
                    p = jnp.where(mask, jnp.exp(lg - m), 0.0)
                    dens.append(jnp.broadcast_to(jnp.sum(p, axis=-1, keepdims=True), (W, hd)))
                    mxs.append(jnp.broadcast_to(m, (W, hd)))
                    accs.append(_dot(p.astype(BF16), v2[:, sl].astype(BF16)))
                acc_ref[br, pr, rows_q, :] = jnp.concatenate(accs, axis=-1)
                mx_ref[br, pr, rows_q, :] = jnp.concatenate(mxs, axis=-1)
                den_ref[br, pr, rows_q, :] = jnp.concatenate(dens, axis=-1)
            return 0

        lax.fori_loop(0, dil * nb, piece, 0)

    def combine(i, _):
        rows = pl.ds(pl.multiple_of(i * W, W), W)
        outs = []
        for pr in range(npair):
            ms = [mx_ref[b, pr, rows, :] for b in range(len(DIL_PATTERNS))]
            top = functools.reduce(jnp.maximum, ms)
            es = [jnp.exp(m - top) for m in ms]
            num = sum(e * acc_ref[b, pr, rows, :] for b, e in enumerate(es))
            den = sum(e * jnp.maximum(den_ref[b, pr, rows, :], 1e-30) for b, e in enumerate(es))
            outs.append(num / den)
        o_ref[0, rows, :] = jnp.concatenate(outs, axis=-1)
        return 0

    lax.fori_loop(0, seq // W, combine, 0)


def _dilated(zb, bias):
    b, s, wb = zb.shape
    nbr = len(DIL_PATTERNS)
    pair_spec = lambda c: pl.BlockSpec((1, s, LANES), lambda i: (i, 0, c))
    return pl.pallas_call(
        _dilated_kernel,
        grid=(b,),
        in_specs=[pair_spec(c) for c in range(wb // LANES)]
                 + [pl.BlockSpec(bias.shape, lambda i: (0, 0, 0, 0))],
        out_specs=pl.BlockSpec((1, s, GROUP_WIDTH), lambda i: (i, 0, 0)),
        out_shape=jax.ShapeDtypeStruct((b, s, GROUP_WIDTH), F32),
        scratch_shapes=[pltpu.VMEM((nbr, HEADS // 2, s, LANES), F32) for _ in range(3)],
        compiler_params=pltpu.CompilerParams(dimension_semantics=("arbitrary",),
                                             vmem_limit_bytes=VMEM_LIMIT_BYTES),
        name="dilated",
    )(*([zb] * (wb // LANES)), bias)


def _stack_heads(q):
    return jnp.concatenate([q[:, HEAD_DIM * h:HEAD_DIM * (h + 1)] for h in range(HEADS)], axis=0)


def _unstack_heads_t(per_head_t):
    halves = []
    for h in range(0, HEADS, 2):
        halves.append(jnp.concatenate([per_head_t[h], per_head_t[h + 1]], axis=0).T)
    return jnp.concatenate(halves, axis=-1)


def _online_step(lg, mask, vt, m, den, acc):
    lg = jnp.where(mask, lg, NEG)
    m_new = jnp.maximum(m, jnp.max(lg, axis=0, keepdims=True))
    p = jnp.where(mask, jnp.exp(lg - m_new), 0.0)
    corr = jnp.exp(m - m_new)
    den = den * corr + jnp.sum(p, axis=0, keepdims=True)
    acc = acc * corr + _dot(vt, p.astype(BF16))
    return m_new, den, acc


def _pair_bias(btab_ref, behind):
    return jnp.concatenate([btab_ref[jnp.minimum(behind, 2)], btab_ref[jnp.clip(behind - 1, 0, 2)]], axis=0)


def _online_init():
    return (tuple(jnp.full((1, BLK), NEG, F32) for _ in range(HEADS)),
            tuple(jnp.zeros((1, BLK), F32) for _ in range(HEADS)),
            tuple(jnp.zeros((HEAD_DIM, BLK), F32) for _ in range(HEADS)))


def _dsa_kernel(z_ref, btab_ref, o_ref, vt_ref, key_ref, hi_ref, lo_ref, lom_ref):
    i = pl.program_id(1)
    seq = z_ref.shape[1]
    nkb = seq // BLK
    hd = HEAD_DIM
    topk = min(DSA_TOPK, seq // 4)
    scale = hd ** -0.5
    t0 = pl.multiple_of(i * BLK, BLK)

    npair = i // 2 + 1

    @pl.when(i == 0)
    def _():
        for kb in range(nkb):
            vt = z_ref[0, kb * BLK:(kb + 1) * BLK, 512:640].T[hd:2 * hd, :]
            vt_ref[kb // 2, :, (kb % 2) * BLK:(kb % 2 + 1) * BLK] = vt.astype(BF16)

    zq = z_ref[0, pl.ds(t0, BLK), :]
    cq = zq[:, 0:256]
    ciq = _stack_heads(zq[:, 256:512]).astype(BF16)
    iw = zq[:, 640:768].T[IDX_DIM:IDX_DIM + IDX_HEADS, :] * ((IDX_HEADS * IDX_DIM) ** -0.5)
    s_loc = lax.broadcasted_iota(I32, (KEYS, BLK), 0)
    t_glob = t0 + lax.broadcasted_iota(I32, (KEYS, BLK), 1)

    def score_block(kp, _):
        r0 = pl.multiple_of(kp * KEYS, KEYS)
        ik = z_ref[0, pl.ds(r0, KEYS), 640:704].astype(BF16)
        rel = _nt_dot(ik, ciq)
        sc = jnp.zeros((KEYS, BLK), F32)
        for h in range(IDX_HEADS):
            sc = sc + jnp.maximum(rel[:, BLK * h:BLK * (h + 1)], 0.0) * iw[h:h + 1, :]
        sc = jnp.where(r0 + s_loc <= t_glob, sc, NEG)
        key = _sortable_key(sc)
        key_ref[kp] = key
        hi_ref[kp] = (key >> 16).astype(I16)
        lo_ref[kp] = ((key & 0xFFFF) - 2 ** 15).astype(I16)
        return 0

    lax.fori_loop(0, npair, score_block, 0)

    def count(ref, pred):
        dt = ref.dtype
        rows = 8 * 4 // dt.itemsize
        def body(kp, acc):
            hit = jnp.where(pred(kp, ref[kp]), jnp.ones((), dt), jnp.zeros((), dt))
            hit = hit.reshape(KEYS // rows, rows, BLK)
            parts = [hit[n] for n in range(KEYS // rows)]
            while len(parts) > 1:
                parts = [a + b for a, b in zip(parts[0::2], parts[1::2])]
            return acc + parts[0]
        acc = lax.fori_loop(0, npair, body, jnp.zeros((rows, BLK), dt))
        return jnp.sum(acc.astype(I32), axis=0, keepdims=True)

    i16_min = -2 ** 15

    def half_search(ref, k):
        def bit(it, thr):
            cand = thr + lax.shift_left(jnp.int32(1), 15 - it)
            c = count(ref, lambda kp, half: half >= cand.astype(I16))
            return jnp.where(c >= k, cand, thr)
        return lax.fori_loop(0, 16, bit, jnp.full((1, BLK), i16_min, I32))

    thr_hi = half_search(hi_ref, topk)
    thr_hi16 = thr_hi.astype(I16)
    above_hi = count(hi_ref, lambda kp, half: half > thr_hi16)

    def mask_low_halves(kp, _):
        lom_ref[kp] = jnp.where(hi_ref[kp] == thr_hi16, lo_ref[kp], jnp.int16(i16_min))
        return 0

    lax.fori_loop(0, npair, mask_low_halves, 0)
    thr_lo = half_search(lom_ref, topk - above_hi)
    thr = (thr_hi << 16) | (thr_lo + 2 ** 15)
    n_gt = count(key_ref, lambda kp, key: key > thr)
    n_eq = count(key_ref, lambda kp, key: key == thr)
    need = topk - n_gt

    def tie_search(_):
        def idx_bit(it, jm):
            cand = jm + lax.shift_left(jnp.int32(1), 10 - it)
            c = count(key_ref, lambda kp, key: (key == thr) & (kp * KEYS + s_loc < cand))
            return jnp.where(c < need, cand, jm)
        return lax.fori_loop(0, 11, idx_bit, jnp.zeros((1, BLK), I32))

    assert seq == 2 ** 11
    jmax = lax.cond(jnp.max(n_eq - need) > 0, tie_search,
                    lambda _: jnp.full((1, BLK), seq - 1, I32), 0)

    qs = _stack_heads(cq).astype(BF16)

    def attend(kp, carry):
        ms, dens, accs = carry
        r0 = pl.multiple_of(kp * KEYS, KEYS)
        kblk = z_ref[0, pl.ds(r0, KEYS), 512:576].astype(BF16)
        lg = _nt_dot(kblk, qs) * scale + _pair_bias(btab_ref, i - 2 * kp)
        key = key_ref[kp]
        s_glob = r0 + s_loc
        mask = ((key > thr) | ((key == thr) & (s_glob <= jmax))) & (s_glob <= t_glob)
        vt = vt_ref[kp]
        out = [_online_step(lg[:, BLK * h:BLK * (h + 1)], mask, vt, ms[h], dens[h], accs[h])
               for h in range(HEADS)]
        return tuple(o[0] for o in out), tuple(o[1] for o in out), tuple(o[2] for o in out)

    ms, dens, accs = lax.fori_loop(0, npair, attend, _online_init())
    o_ref[0] = _unstack_heads_t([accs[h] / jnp.maximum(dens[h], 1e-30) for h in range(HEADS)])


def _dsa(zc, btab):
    b, s, wc = zc.shape
    nq = s // BLK
    return pl.pallas_call(
        _dsa_kernel,
        grid=(b, nq),
        in_specs=[pl.BlockSpec((1, s, wc), lambda bi, i: (bi, 0, 0)),
                  pl.BlockSpec(btab.shape, lambda bi, i: (0, 0, 0))],
        out_specs=pl.BlockSpec((1, BLK, GROUP_WIDTH), lambda bi, i: (bi, i, 0)),
        out_shape=jax.ShapeDtypeStruct((b, s, GROUP_WIDTH), F32),
        scratch_shapes=[pltpu.VMEM((s // KEYS, HEAD_DIM, KEYS), BF16), pltpu.VMEM((s // KEYS, KEYS, BLK), I32)]
                       + [pltpu.VMEM((s // KEYS, KEYS, BLK), I16)] * 3,
        compiler_params=pltpu.CompilerParams(dimension_semantics=("arbitrary", "arbitrary"),
                                             vmem_limit_bytes=VMEM_LIMIT_BYTES),
        name="dsa",
    )(zc, btab)


def _nsa_kernel(z_ref, zc_ref, w1_ref, pos_ref, w2_ref, bsel_ref, bcmp_ref, ovt_ref, exp_ref, o_ref,
                vst_ref, vwt_ref, cmp_ref, cmpt_ref):
    i = pl.program_id(1)
    seq = z_ref.shape[1]
    nkb = seq // BLK
    hd = HEAD_DIM
    scale = hd ** -0.5
    n_cmp = (seq - NSA_CMP_LEN) // NSA_CMP_STRIDE + 1
    n_sel = seq // NSA_SEL_LEN
    topn = min(NSA_TOPN, n_sel)
    half = NSA_CMP_LEN // 2
    assert half == NSA_CMP_STRIDE and n_cmp + 1 == seq // NSA_CMP_STRIDE == BLK and n_sel <= BLK
    t0 = pl.multiple_of(i * BLK, BLK)
    hi = lax.Precision.HIGHEST

    @pl.when(i == 0)
    def _():
        for kb in range(nkb):
            rows = slice(kb * BLK, (kb + 1) * BLK)
            cols = slice((kb % 2) * BLK, (kb % 2 + 1) * BLK)
            vst_ref[kb // 2, :, cols] = z_ref[0, rows, 384:512].T[hd:2 * hd, :].astype(BF16)
            vwt_ref[kb // 2, :, cols] = z_ref[0, rows, 512:640].T[hd:2 * hd, :].astype(BF16)
        first = jnp.zeros((BLK, 2 * NSA_CMP_HIDDEN), F32)
        second = jnp.zeros((BLK, 2 * NSA_CMP_HIDDEN), F32)
        for j in range(half):
            xj = zc_ref[0, pl.ds(j, BLK, stride=NSA_CMP_STRIDE), :]
            first = first + _dot((xj + pos_ref[j:j + 1, :]).astype(BF16), w1_ref[j])
            second = second + _dot((xj + pos_ref[half + j:half + j + 1, :]).astype(BF16), w1_ref[half + j])
        hid = first + pltpu.roll(second, BLK - 1, 0)
        hid = hid * _sigmoid(hid)
        cmp = _dot(hid.astype(BF16), w2_ref[...])
        cmp_ref[...] = cmp
        cmpt_ref[...] = cmp.T

    zq = z_ref[0, pl.ds(t0, BLK), :]
    qs = _stack_heads(zq[:, 0:256]).astype(BF16)
    gates = _sigmoid(zq[:, 640:768].T[0:16, :])
    row = lax.broadcasted_iota(I32, (BLK, BLK), 0)
    t_glob = t0 + lax.broadcasted_iota(I32, (BLK, BLK), 1)

    kcmp = cmp_ref[:, 0:hd].astype(BF16)
    vcmpt = cmpt_ref[hd:2 * hd, :].astype(BF16)
    lgc = _nt_dot(kcmp, qs) * scale + bcmp_ref[0]
    mask_c = (t_glob - (row * NSA_CMP_STRIDE + NSA_CMP_LEN - 1) >= 0) & (row < n_cmp)
    o_cmp, psum = [], jnp.zeros((BLK, BLK), F32)
    for h in range(HEADS):
        lg = jnp.where(mask_c, lgc[:, BLK * h:BLK * (h + 1)], NEG)
        m = jnp.max(lg, axis=0, keepdims=True)
        p = jnp.where(mask_c, jnp.exp(lg - m), 0.0)
        p = p / jnp.maximum(jnp.sum(p, axis=0, keepdims=True), 1e-30)
        o_cmp.append(_dot(vcmpt, p.astype(BF16)))
        psum = psum + p

    imp = _dot(ovt_ref[...], psum, precision=hi)
    cur = t_glob >> int(math.log2(NSA_SEL_LEN))
    forced = (row == 0) | (row == cur) | (row == cur - 1)
    imp = jnp.where(forced, NSA_FORCE, imp)
    imp = jnp.where(row * NSA_SEL_LEN <= t_glob, imp, NEG)
    imp = imp[0:n_sel, :]
    jrow = row[0:n_sel, :]
    rank = jnp.zeros((n_sel, BLK), I32)
    for jp in range(n_sel):
        other = imp[jp:jp + 1, :]
        rank = rank + ((other > imp) | ((other == imp) & (jp < jrow))).astype(I32)
    chosen = jnp.where(rank < topn, 1.0, 0.0)
    chosen = jnp.concatenate([chosen, jnp.zeros((BLK - n_sel, BLK), F32)], axis=0).astype(BF16)

    s_loc = lax.broadcasted_iota(I32, (KEYS, BLK), 0)
    t_keys = t0 + lax.broadcasted_iota(I32, (KEYS, BLK), 1)

    def attend(vt_ref, lanes, mask_fn):
        def body(kp, carry):
            ms, dens, accs = carry
            r0 = pl.multiple_of(kp * KEYS, KEYS)
            kblk = z_ref[0, pl.ds(r0, KEYS), lanes].astype(BF16)
            lg = _nt_dot(kblk, qs) * scale + _pair_bias(bsel_ref, i - 2 * kp)
            mask = mask_fn(r0)
            vt = vt_ref[kp]
            out = [_online_step(lg[:, BLK * h:BLK * (h + 1)], mask, vt, ms[h], dens[h], accs[h])
                   for h in range(HEADS)]
            return tuple(o[0] for o in out), tuple(o[1] for o in out), tuple(o[2] for o in out)
        return body

    def mask_sel(r0):
        picked = _dot(exp_ref[pl.ds(r0, KEYS), :], chosen) > 0.5
        return picked & (r0 + s_loc <= t_keys)

    def mask_win(r0):
        dist = t_keys - (r0 + s_loc)
        return (dist >= 0) & (dist < NSA_WINDOW)

    last_pair = i // 2 + 1
    _, den_s, acc_s = lax.fori_loop(0, last_pair, attend(vst_ref, slice(384, 448), mask_sel), _online_init())
    first_w = jnp.maximum(i - NSA_WINDOW // BLK, 0) // 2
    _, den_w, acc_w = lax.fori_loop(first_w, last_pair, attend(vwt_ref, slice(512, 576), mask_win), _online_init())

    outs = []
    for h in range(HEADS):
        o_s = acc_s[h] / jnp.maximum(den_s[h], 1e-30)
        o_w = acc_w[h] / jnp.maximum(den_w[h], 1e-30)
        outs.append(gates[3 * h:3 * h + 1, :] * o_cmp[h] + gates[3 * h + 1:3 * h + 2, :] * o_s
                    + gates[3 * h + 2:3 * h + 3, :] * o_w)
    o_ref[0] = _unstack_heads_t(outs)


def _nsa(zd, w1, pos, w2, bsel, bcmp, ovt, expand):
    b, s, wd = zd.shape
    nq = s // BLK
    full = lambda a: pl.BlockSpec(a.shape, lambda bi, i: (0,) * a.ndim)
    return pl.pallas_call(
        _nsa_kernel,
        grid=(b, nq),
        in_specs=[pl.BlockSpec((1, s, wd), lambda bi, i: (bi, 0, 0)),
                  pl.BlockSpec((1, s, LANES), lambda bi, i: (bi, 0, GROUP_WIDTH // LANES)),
                  full(w1), full(pos), full(w2), full(bsel),
                  pl.BlockSpec((1, BLK, HEADS * BLK), lambda bi, i: (i, 0, 0)),
                  full(ovt), full(expand)],
        out_specs=pl.BlockSpec((1, BLK, GROUP_WIDTH), lambda bi, i: (bi, i, 0)),
        out_shape=jax.ShapeDtypeStruct((b, s, GROUP_WIDTH), F32),
        scratch_shapes=[pltpu.VMEM((s // KEYS, HEAD_DIM, KEYS), BF16), pltpu.VMEM((s // KEYS, HEAD_DIM, KEYS), BF16),
                        pltpu.VMEM((BLK, BLK), F32), pltpu.VMEM((BLK, BLK), F32)],
        compiler_params=pltpu.CompilerParams(dimension_semantics=("arbitrary", "arbitrary"),
                                             vmem_limit_bytes=VMEM_LIMIT_BYTES),
        name="nsa",
    )(zd, zd, w1, pos, w2, bsel, bcmp, ovt, expand)


def _outproj_kernel(alpha, x_ref, a_ref, b_ref, c_ref, d_ref, w_ref, bo_ref, g_ref, beta_ref, o_ref):
    acc = bo_ref[...] + _dot(a_ref[...].astype(BF16), w_ref[0:GROUP_WIDTH, :])
    for n, m_ref in enumerate((b_ref, c_ref, d_ref), start=1):
        acc = acc + _dot(m_ref[...].astype(BF16), w_ref[n * GROUP_WIDTH:(n + 1) * GROUP_WIDTH, :])
    o_ref[...] = _layer_norm_rows(alpha * x_ref[...] + acc, g_ref[...], beta_ref[...])


def _outproj(alpha, x2d, mixed, w, bo, g, beta, tm=512):
    t = x2d.shape[0]
    row_spec = lambda width: pl.BlockSpec((tm, width), lambda i: (i, 0))
    const = lambda a: pl.BlockSpec(a.shape, lambda i: (0, 0))
    return pl.pallas_call(
        functools.partial(_outproj_kernel, alpha),
        grid=(t // tm,),
        in_specs=[row_spec(D_MODEL)] + [row_spec(GROUP_WIDTH)] * N_MIXERS
                 + [const(w), const(bo), const(g), const(beta)],
        out_specs=row_spec(D_MODEL),
        out_shape=jax.ShapeDtypeStruct((t, D_MODEL), F32),
        compiler_params=pltpu.CompilerParams(dimension_semantics=("arbitrary",),
                                             vmem_limit_bytes=VMEM_LIMIT_BYTES),
        name="outproj_ln",
    )(x2d, *mixed, w, bo, g, beta)


def _ffn_kernel(alpha, x_ref, w1_ref, b1_ref, w2_ref, b2_ref, g_ref, beta_ref, o_ref, xb_ref, acc_ref):
    j = pl.program_id(1)

    @pl.when(j == 0)
    def _():
        xb_ref[...] = x_ref[...].astype(BF16)
        acc_ref[...] = jnp.zeros_like(acc_ref)

    hdn = jnp.maximum(_dot(xb_ref[...], w1_ref[...]) + b1_ref[...], 0.0)
    acc_ref[...] += _dot(jnp.square(hdn).astype(BF16), w2_ref[...])

    @pl.when(j == pl.num_programs(1) - 1)
    def _():
        y = alpha * x_ref[...] + (acc_ref[...] + b2_ref[...])
        o_ref[...] = _layer_norm_rows(y, g_ref[...], beta_ref[...])


def _ffn(alpha, x2d, w1, b1, w2, b2, g, beta, tm=1024, tf=512):
    t = x2d.shape[0]
    return pl.pallas_call(
        functools.partial(_ffn_kernel, alpha),
        grid=(t // tm, D_FF // tf),
        in_specs=[pl.BlockSpec((tm, D_MODEL), lambda i, j: (i, 0)),
                  pl.BlockSpec((D_MODEL, tf), lambda i, j: (0, j)),
                  pl.BlockSpec((1, tf), lambda i, j: (0, j)),
                  pl.BlockSpec((tf, D_MODEL), lambda i, j: (j, 0)),
                  pl.BlockSpec((1, D_MODEL), lambda i, j: (0, 0)),
                  pl.BlockSpec((1, D_MODEL), lambda i, j: (0, 0)),
                  pl.BlockSpec((1, D_MODEL), lambda i, j: (0, 0))],
        out_specs=pl.BlockSpec((tm, D_MODEL), lambda i, j: (i, 0)),
        out_shape=jax.ShapeDtypeStruct((t, D_MODEL), F32),
        scratch_shapes=[pltpu.VMEM((tm, D_MODEL), BF16), pltpu.VMEM((tm, D_MODEL), F32)],
        compiler_params=pltpu.CompilerParams(dimension_semantics=("arbitrary", "arbitrary"),
                                             vmem_limit_bytes=VMEM_LIMIT_BYTES),
        name="ffn_ln",
    )(x2d, w1, b1, w2, b2, g, beta)


def _bias_of_distance(rel_bias_heads, dist):
    onehot = np.eye(NUM_BUCKETS, dtype=np.float32)[_t5_bucket_np(np.asarray(dist))]
    return jnp.dot(rel_bias_heads.T, jnp.asarray(onehot.T), precision=lax.Precision.HIGHEST)


def _shifted_rows(v, n_rows, n_cols, step):
    period = v.shape[-1]
    assert n_cols <= period - step
    flat = jnp.tile(v, (1,) * (v.ndim - 1) + (n_rows,))[..., :n_rows * (period - step)]
    return flat.reshape(v.shape[:-1] + (n_rows, period - step))[..., :n_cols]


def _wrapped(period):
    idx = np.arange(period)
    return np.where(idx < period // 2, idx, idx - period)


def _dilated_bias(rel_bias):
    x = _wrapped(4 * BLK)
    tabs = [_shifted_rows(_bias_of_distance(rel_bias[:, 0:HEADS], (BLK - x) * dil), BLK, 2 * BLK, 1)
            for _, dil in DIL_PATTERNS]
    return jnp.stack(tabs)


def _toeplitz_bias_t(rel_bias_heads):
    assert (_t5_bucket_np(np.arange(BLK + 1, 64 * BLK)) == NUM_BUCKETS - 1).all()
    x = _wrapped(2 * BLK)
    tabs = [_shifted_rows(_bias_of_distance(rel_bias_heads, BLK * delta + x), BLK, BLK, 1)
            for delta in range(3)]
    return jnp.transpose(jnp.stack(tabs), (0, 2, 1, 3)).reshape(3, BLK, HEADS * BLK)


def _compressed_bias_t(rel_bias_heads, seq):
    nq = seq // BLK
    x = _wrapped(2 * seq + BLK)
    v = _bias_of_distance(rel_bias_heads, x - (NSA_CMP_LEN - 1))
    tab = _shifted_rows(v, BLK, seq, NSA_CMP_STRIDE)
    tab = tab.reshape(HEADS, BLK, nq, BLK)
    return jnp.transpose(tab, (2, 1, 0, 3)).reshape(nq, BLK, HEADS * BLK)


def _nsa_constants(seq):
    n_cmp = (seq - NSA_CMP_LEN) // NSA_CMP_STRIDE + 1
    n_sel = seq // NSA_SEL_LEN
    cs = np.arange(n_cmp)[:, None] * NSA_CMP_STRIDE
    ss = np.arange(n_sel)[None, :] * NSA_SEL_LEN
    ov = np.clip(np.minimum(cs + NSA_CMP_LEN, ss + NSA_SEL_LEN) - np.maximum(cs, ss), 0, None) / NSA_CMP_LEN
    ovt = np.zeros((BLK, BLK), np.float32)
    ovt[:n_sel, :n_cmp] = ov.T
    expand = np.zeros((seq, BLK), np.float32)
    expand[np.arange(seq), np.arange(seq) // NSA_SEL_LEN] = 1.0
    return jnp.asarray(ovt), jnp.asarray(expand, BF16)


def _nsa_weights(cmp_pos, cmp_w1, cmp_w2):
    hd, hid = HEAD_DIM, NSA_CMP_HIDDEN
    w1 = cmp_w1.reshape(2, NSA_CMP_LEN, hd, hid)
    zeros = jnp.zeros((NSA_CMP_LEN, hd, hid), F32)
    w1 = jnp.concatenate([jnp.concatenate([w1[0], zeros], axis=-1),
                          jnp.concatenate([zeros, w1[1]], axis=-1)], axis=1)
    pos = jnp.concatenate([cmp_pos[0], cmp_pos[1]], axis=-1)
    z2 = jnp.zeros((hid, hd), F32)
    w2 = jnp.concatenate([jnp.concatenate([cmp_w2[0], z2], axis=-1),
                          jnp.concatenate([z2, cmp_w2[1]], axis=-1)], axis=0)
    return w1.astype(BF16), pos, w2.astype(BF16)


def kernel(x, w_in, b_in, a_conv, a_norm, d_cmp_pos, d_cmp_w1, d_cmp_w2, w_out, b_out, ln1_g, ln1_b,
           w_ff1, b_ff1, w_ff2, b_ff2, ln2_g, ln2_b, rel_bias):
    bsz, seq, _ = x.shape
    depth = w_in.shape[0]
    alpha = (2 * depth) ** 0.25
    nc = seq // M_CHUNK
    bias_dil = _dilated_bias(rel_bias)
    btab_dsa = _toeplitz_bias_t(rel_bias[:, HEADS:2 * HEADS])
    btab_nsa = _toeplitz_bias_t(rel_bias[:, 2 * HEADS:3 * HEADS])
    bcmp_nsa = _compressed_bias_t(rel_bias[:, 2 * HEADS:3 * HEADS], seq)
    ovt, expand = _nsa_constants(seq)

    h = x.reshape(bsz * seq, D_MODEL)
    for l in range(depth):
        w_l = _permute_columns(w_in[l]).astype(BF16)
        b_l = _permute_columns(b_in[l])[None, :]
        za, zb, zc, zd = (z.reshape(bsz, seq, -1) for z in _inproj(h, w_l, b_l))
        gates_t = jnp.swapaxes(za[:, :, 768:776].reshape(bsz, nc, M_CHUNK, 2 * HEADS), 2, 3)
        out_a = _mlstm(za, gates_t, a_conv[l], a_norm[l][None, :])
        out_b = _dilated(zb, bias_dil)
        out_c = _dsa(zc, btab_dsa)
        nsa_w1, nsa_pos, nsa_w2 = _nsa_weights(d_cmp_pos[l], d_cmp_w1[l], d_cmp_w2[l])
        out_d = _nsa(zd, nsa_w1, nsa_pos, nsa_w2, btab_nsa, bcmp_nsa, ovt, expand)
        mixed = [o.reshape(bsz * seq, GROUP_WIDTH) for o in (out_a, out_b, out_c, out_d)]
        h = _outproj(alpha, h, mixed, w_out[l].astype(BF16), b_out[l][None, :],
                     ln1_g[l][None, :], ln1_b[l][None, :])
        h = _ffn(alpha, h, w_ff1[l].astype(BF16), b_ff1[l][None, :], w_ff2[l].astype(BF16),
                 b_ff2[l][None, :], ln2_g[l][None, :], ln2_b[l][None, :])
    return h.reshape(bsz, seq, D_MODEL)
```

```python
import functools
import math

import numpy as np
import jax
import jax.numpy as jnp
from jax import lax
from jax.experimental import pallas as pl
from jax.experimental.pallas import tpu as pltpu

F32 = jnp.float32
BF16 = jnp.bfloat16
I32 = jnp.int32
I16 = jnp.int16
I16_MIN = -2 ** 15

D_MODEL = 1024
N_MIXERS = 4
HEADS = 4
HEAD_DIM = D_MODEL // (N_MIXERS * HEADS)
GROUP_WIDTH = HEADS * HEAD_DIM
D_FF = 4 * D_MODEL
LN_EPS = 1e-5
NEG = -1e30

M_QK_DIM = HEAD_DIM // 2
M_CHUNK = 64
M_CONV = 4
DIL_PATTERNS = ((128, 1), (512, 4), (2048, 16))
IDX_HEADS = 4
IDX_DIM = 64
DSA_TOPK = 256
NSA_CMP_LEN = 32
NSA_CMP_STRIDE = 16
NSA_SEL_LEN = 64
NSA_TOPN = 16
NSA_WINDOW = 512
NSA_CMP_HIDDEN = 256
NSA_FORCE = 1e9
NUM_BUCKETS = 32
MAX_DISTANCE = 128

LANES = 128
BLK = 128
KEYS = 2 * BLK
VMEM_LIMIT_BYTES = 56 * 1024 * 1024

IN_SPLITS = (
    ('a_q', HEADS * M_QK_DIM), ('a_k', HEADS * M_QK_DIM), ('a_v', GROUP_WIDTH),
    ('a_i', HEADS), ('a_f', HEADS), ('a_o', GROUP_WIDTH),
    ('b_q', GROUP_WIDTH), ('b_k', GROUP_WIDTH), ('b_v', GROUP_WIDTH),
    ('c_q', GROUP_WIDTH), ('c_k', HEAD_DIM), ('c_v', HEAD_DIM),
    ('c_iq', IDX_HEADS * IDX_DIM), ('c_ik', IDX_DIM), ('c_iw', IDX_HEADS),
    ('d_q', GROUP_WIDTH), ('d_kc', HEAD_DIM), ('d_vc', HEAD_DIM),
    ('d_ks', HEAD_DIM), ('d_vs', HEAD_DIM), ('d_kw', HEAD_DIM), ('d_vw', HEAD_DIM),
    ('d_g', 3 * HEADS),
)

GROUP_LAYOUT = (
    (('a_q', 'a_k'), ('a_v',), ('a_o',), (('a_i', HEAD_DIM),), (('a_i', M_QK_DIM),),
     (('a_f', HEAD_DIM),), (('a_f', M_QK_DIM),)),
    (('b_q',), ('b_k',), ('b_v',)),
    (('c_q',), ('c_iq',), ('c_k', 'c_v'), ('c_ik', 'c_iw')),
    (('d_q',), ('d_kc', 'd_vc'), ('d_ks', 'd_vs'), ('d_kw', 'd_vw'), ('d_g',)),
)


def _round_up(n, m):
    return -(-n // m) * m


def _projection_layout():
    offs, off = {}, 0
    for name, width in IN_SPLITS:
        offs[name] = (off, width)
        off += width
    runs, group_widths = [], []
    for group in GROUP_LAYOUT:
        gwidth = 0
        for chunk in group:
            cwidth = 0
            for entry in chunk:
                name, rep = entry if isinstance(entry, tuple) else (entry, 1)
                o, w = offs[name]
                runs.append((o, w, rep))
                cwidth += w * rep
            pad = _round_up(cwidth, LANES) - cwidth
            if pad:
                runs.append((-1, pad, 1))
            gwidth += cwidth + pad
        group_widths.append(gwidth)
    return tuple(runs), tuple(group_widths)


PROJ_RUNS, GROUP_WIDTHS = _projection_layout()
PROJ_WIDTH = int(sum(GROUP_WIDTHS))
PROJ_PERM = np.concatenate([np.repeat(np.arange(o, o + w), r) if o >= 0 else np.full(w, -1)
                            for o, w, r in PROJ_RUNS]).astype(np.int32)


def _permute_columns(a):
    parts = []
    for o, w, r in PROJ_RUNS:
        if o < 0:
            parts.append(jnp.zeros(a.shape[:-1] + (w,), a.dtype))
        else:
            parts.append(a[..., o:o + w] if r == 1 else jnp.repeat(a[..., o:o + w], r, axis=-1))
    return jnp.concatenate(parts, axis=-1)


def _t5_bucket_np(dist):
    n = np.maximum(dist, 0)
    max_exact = NUM_BUCKETS // 2
    nf = np.maximum(n, max_exact).astype(np.float32)
    large = max_exact + (np.log(nf / max_exact) / math.log(MAX_DISTANCE / max_exact)
                         * (NUM_BUCKETS - max_exact)).astype(np.int32)
    large = np.minimum(large, NUM_BUCKETS - 1)
    return np.where(n < max_exact, n, large).astype(np.int32)


def _nt_dot(a, b, precision=None):
    return lax.dot_general(a, b, (((1,), (1,)), ((), ())), precision=precision,
                           preferred_element_type=F32)


def _dot(a, b, precision=None):
    return jnp.dot(a, b, precision=precision, preferred_element_type=F32)


def _layer_norm_rows(y, g, b):
    mu = jnp.mean(y, axis=-1, keepdims=True)
    var = jnp.mean(jnp.square(y - mu), axis=-1, keepdims=True)
    return (y - mu) * lax.rsqrt(var + LN_EPS) * g + b


def _sigmoid(x):
    return 1.0 / (1.0 + jnp.exp(-x))


def _log_sigmoid(x):
    return -(jnp.maximum(-x, 0.0) + jnp.log1p(jnp.exp(-jnp.abs(x))))


def _sortable_key(x):
    bits = pltpu.bitcast(x, I32)
    return bits ^ ((bits >> 31) & jnp.int32(0x7FFFFFFF))


def _inproj_kernel(x_ref, w_ref, b_ref, *out_refs):
    xb = x_ref[...].astype(BF16)
    off = 0
    for o_ref, width in zip(out_refs, GROUP_WIDTHS):
        o_ref[...] = _dot(xb, w_ref[:, off:off + width]) + b_ref[:, off:off + width]
        off += width


def _inproj(x2d, w, b, tm=512):
    t = x2d.shape[0]
    return pl.pallas_call(
        _inproj_kernel,
        grid=(t // tm,),
        in_specs=[pl.BlockSpec((tm, D_MODEL), lambda i: (i, 0)),
                  pl.BlockSpec((D_MODEL, PROJ_WIDTH), lambda i: (0, 0)),
                  pl.BlockSpec((1, PROJ_WIDTH), lambda i: (0, 0))],
        out_specs=[pl.BlockSpec((tm, gw), lambda i: (i, 0)) for gw in GROUP_WIDTHS],
        out_shape=[jax.ShapeDtypeStruct((t, gw), F32) for gw in GROUP_WIDTHS],
        compiler_params=pltpu.CompilerParams(dimension_semantics=("arbitrary",),
                                             vmem_limit_bytes=VMEM_LIMIT_BYTES),
        name="inproj",
    )(x2d, w, b)


def _split_terms(x, n):
    terms, rest = [], x
    for _ in range(n):
        terms.append(rest.astype(BF16))
        rest = rest - terms[-1].astype(F32)
    return terms


def _iota(shape, dim):
    return lax.broadcasted_iota(I32, shape, dim)


def _mlstm_kernel(z_ref, gt_ref, cw_ref, ng_ref, o_ref, xpad_ref):
    seq = z_ref.shape[1]
    L, DK, DV, H = M_CHUNK, M_QK_DIM, HEAD_DIM, HEADS
    assert L == DV
    wq, wv = H * DK, H * DV
    lg_dk, lg_dv = int(math.log2(DK)), int(math.log2(DV))
    c_v, c_o, c_i64 = 2 * wq, 2 * wq + wv, 2 * wq + 2 * wv
    c_i32, c_f64 = c_i64 + wv, c_i64 + wv + wq
    xpad_ref[0:8, :] = jnp.zeros((8, 2 * wq), F32)
    xpad_ref[8:, :] = z_ref[0, :, 0:2 * wq]

    one_if = lambda cond: jnp.where(cond, 1.0, 0.0).astype(BF16)
    tri_l = one_if(_iota((L, L), 0) >= _iota((L, L), 1))
    trow = _iota((L, wv), 0)
    tri_heads = trow >= (_iota((L, wv), 1) & (L - 1))
    r_vv, c_vv = _iota((wv, wv), 0), _iota((wv, wv), 1)
    same_head = (r_vv >> lg_dv) == (c_vv >> lg_dv)
    ones_bd = one_if(same_head)
    mean_bd = jnp.where(same_head, 1.0 / DV, 0.0).astype(BF16)
    tri_u_bd = one_if(same_head & ((r_vv & (L - 1)) <= (c_vv & (L - 1))))
    state_mask = (_iota((wq, wv), 0) >> lg_dk) == (_iota((wq, wv), 1) >> lg_dv)
    eye_q = one_if(_iota((wq, wq), 0) == _iota((wq, wq), 1))
    head_of_qlane = _iota((L, wq), 1) >> lg_dk
    head_of_vlane = _iota((L, wv), 1) >> lg_dv
    row8 = _iota((8, wv), 0)
    cw = cw_ref[...]
    ng = ng_ref[...]

    def head_mean(x):
        hi_lo = _split_terms(x, 2)
        r = _dot(jnp.concatenate(hi_lo, axis=0), mean_bd)
        return r[0:L] + r[L:2 * L]

    def chunk(c, carry):
        cbd, nbd, m64, m32 = carry
        s0 = pl.multiple_of(c * L, L)
        rows = pl.ds(s0, L)
        xw = xpad_ref[pl.ds(s0, L + 8), :]
        y = sum(cw[j:j + 1, :] * xw[5 + j:5 + j + L, :] for j in range(M_CONV))
        qk = y * _sigmoid(y)
        q = qk[:, 0:wq]
        k = qk[:, wq:] * (DK ** -0.5)
        qb = q.astype(BF16)
        v = z_ref[0, rows, c_v:c_v + wv]
        i64 = z_ref[0, rows, c_i64:c_i64 + wv]
        i32 = z_ref[0, rows, c_i32:c_i32 + wq]
        gr = gt_ref[0, c]

        flog = _log_sigmoid(z_ref[0, rows, c_f64:c_f64 + wv + wq])
        bsum = _dot(tri_l, jnp.concatenate(_split_terms(flog, 3), axis=1))
        w3 = wv + wq
        ball = bsum[:, 0:w3] + bsum[:, w3:2 * w3] + bsum[:, 2 * w3:3 * w3]
        b64, b32 = ball[:, 0:wv], ball[:, wv:w3]
        fterms = [t.astype(F32) for t in _split_terms(_log_sigmoid(gr[1:2, :]), 3)]
        frows = jnp.where(row8 == 0, fterms[0], jnp.where(row8 == 1, fterms[1], jnp.where(row8 == 2, fterms[2], 0.0)))
        bparts = _dot(frows.astype(BF16), tri_u_bd)
        brow = bparts[0:1, :] + bparts[1:2, :] + bparts[2:3, :]

        dall = jnp.where(tri_heads, b64 - brow + gr[0:1, :], NEG)
        cm = i64 - b64
        for sh in (1, 2, 4, 8, 16, 32):
            cm = jnp.where(trow >= sh, jnp.maximum(cm, pltpu.roll(cm, sh, 0)), cm)
        inter = b64 + m64
        m_t = jnp.maximum(inter, b64 + cm)
        kbd = jnp.concatenate([jnp.where(head_of_qlane == h, k, 0.0) for h in range(H)], axis=0).astype(BF16)
        sc = _nt_dot(qb, kbd) * jnp.exp(dall - m_t)
        wi = jnp.exp(inter - m_t)
        vb = v.astype(BF16)
        vbd = jnp.concatenate([jnp.where(head_of_vlane == h, v, 0.0) for h in range(H)], axis=0).astype(BF16)
        pv = _dot(sc.astype(BF16), jnp.concatenate([vbd, ones_bd], axis=1))
        qst = _dot(qb, jnp.concatenate([cbd, nbd], axis=1).astype(BF16))
        num = pv[:, 0:wv] + wi * qst[:, 0:wv]
        den = pv[:, wv:] + wi * qst[:, wv:]
        hh = num / jnp.maximum(jnp.abs(den), jnp.exp(-m_t))

        og = _sigmoid(z_ref[0, rows, c_o:c_o + wv]) * hh
        dev = og - head_mean(og)
        o_ref[0, rows, :] = dev * lax.rsqrt(head_mean(dev * dev) + LN_EPS) * ng

        bl64, bl32 = b64[L - 1:L, :], b32[L - 1:L, :]
        m64_new = jnp.maximum(bl64 + m64, jnp.max(bl64 - b64 + i64, axis=0, keepdims=True))
        g32 = bl32 - b32 + i32
        m32_new = jnp.maximum(bl32 + m32, jnp.max(g32, axis=0, keepdims=True))
        kw = k * jnp.exp(g32 - m32_new)
        wc = jnp.exp(bl64 + m64 - m64_new)
        kwt = _nt_dot(eye_q, kw.astype(BF16)).astype(BF16)
        upd = _dot(kwt, jnp.concatenate([vb, jnp.ones((L, wv), BF16)], axis=1))
        cbd = wc * cbd + jnp.where(state_mask, upd[:, 0:wv], 0.0)
        nbd = wc * nbd + jnp.where(state_mask, upd[:, wv:], 0.0)
        return cbd, nbd, m64_new, m32_new

    init = (jnp.zeros((wq, wv), F32), jnp.zeros((wq, wv), F32), jnp.zeros((1, wv), F32), jnp.zeros((1, wq), F32))
    lax.fori_loop(0, seq // L, chunk, init, unroll=2)


def _mlstm(za, gates_t, conv_w, norm_g):
    b, s, wa = za.shape
    nc = s // M_CHUNK
    return pl.pallas_call(
        _mlstm_kernel,
        grid=(b,),
        in_specs=[pl.BlockSpec((1, s, wa), lambda i: (i, 0, 0)),
                  pl.BlockSpec((1, nc) + gates_t.shape[2:], lambda i: (i, 0, 0, 0)),
                  pl.BlockSpec((M_CONV, 2 * HEADS * M_QK_DIM), lambda i: (0, 0)),
                  pl.BlockSpec((1, GROUP_WIDTH), lambda i: (0, 0))],
        out_specs=pl.BlockSpec((1, s, GROUP_WIDTH), lambda i: (i, 0, 0)),
        out_shape=jax.ShapeDtypeStruct((b, s, GROUP_WIDTH), F32),
        scratch_shapes=[pltpu.VMEM((s + 8, 2 * HEADS * M_QK_DIM), F32)],
        compiler_params=pltpu.CompilerParams(dimension_semantics=("arbitrary",),
                                             vmem_limit_bytes=VMEM_LIMIT_BYTES),
        name="mlstm",
    )(za, gates_t, conv_w, norm_g)


def _mlstm_gate_rows(za):
    b, s, _ = za.shape
    c_i64 = 2 * HEADS * M_QK_DIM + 2 * GROUP_WIDTH
    c_f64 = c_i64 + GROUP_WIDTH + HEADS * M_QK_DIM
    gates = jnp.stack([za[:, :, c:c + GROUP_WIDTH:HEAD_DIM] for c in (c_i64, c_f64)], axis=2)
    gates = gates.reshape(b, s // M_CHUNK, M_CHUNK, 2, HEADS)
    return jnp.transpose(gates, (0, 1, 3, 4, 2)).reshape(b, s // M_CHUNK, 2, HEADS * M_CHUNK)


def _dilated_kernel(q0_ref, q1_ref, k0_ref, k1_ref, v0_ref, v1_ref, bias_ref, o_ref, acc_ref, mx_ref, den_ref):
    seq = q0_ref.shape[1]
    W = BLK
    hd = HEAD_DIM
    npair = HEADS // 2
    scale = hd ** -0.5
    assert math.log2(scale).is_integer()
    nk = 2 * W
    qi = _iota((W, 2 * nk), 0)
    ki = _iota((W, 2 * nk), 1) & (nk - 1)
    j = W + qi - ki
    band = (j >= 0) & (j <= W)
    head_of_lane = _iota((nk, 2 * hd), 1) >> int(math.log2(hd))
    ones_bd = jnp.concatenate([jnp.where(head_of_lane == hh, 1.0, 0.0) for hh in range(2)], axis=0).astype(BF16)
    q_refs, k_refs, v_refs = (q0_ref, q1_ref), (k0_ref, k1_ref), (v0_ref, v1_ref)

    def block_diag(x):
        return jnp.concatenate([jnp.where(head_of_lane == hh, x, 0.0) for hh in range(2)], axis=0).astype(BF16)

    for br, (window, dil) in enumerate(DIL_PATTERNS):
        assert window // dil == W
        nb = (seq // dil) // W

        def piece(idx, _, br=br, dil=dil, nb=nb):
            n = idx % nb
            if dil == 1:
                rows_q = pl.ds(pl.multiple_of(W * n, W), W)
                rows_p = pl.ds(pl.multiple_of(W * jnp.maximum(n - 1, 0), W), W)
            else:
                r = idx // nb
                rows_q = pl.ds(r + dil * W * n, W, stride=dil)
                rows_p = pl.ds(r + dil * W * jnp.maximum(n - 1, 0), W, stride=dil)
            mask = band & (ki >= jnp.where(n > 0, 0, W))
            for pr in range(npair):
                q = (q_refs[pr][0, rows_q, :] * scale).astype(BF16)
                k2 = jnp.concatenate([k_refs[pr][0, rows_p, :], k_refs[pr][0, rows_q, :]], axis=0)
                v2 = jnp.concatenate([v_refs[pr][0, rows_p, :], v_refs[pr][0, rows_q, :]], axis=0)
                lg = jnp.where(mask, _nt_dot(q, block_diag(k2)) + bias_ref[br, pr], NEG)
                ps, mxs = [], []
                for hh in range(2):
                    sl = slice(nk * hh, nk * (hh + 1))
                    m = jnp.max(lg[:, sl], axis=-1, keepdims=True)
                    ps.append(jnp.where(mask[:, sl], jnp.exp(lg[:, sl] - m), 0.0))
                    mxs.append(jnp.broadcast_to(m, (W, hd)))
                p = jnp.concatenate(ps, axis=-1).astype(BF16)
                pv = _dot(p, jnp.concatenate([block_diag(v2), ones_bd], axis=1))
                acc_ref[br, pr, rows_q, :] = pv[:, 0:2 * hd]
                den_ref[br, pr, rows_q, :] = pv[:, 2 * hd:]
                mx_ref[br, pr, rows_q, :] = jnp.concatenate(mxs, axis=-1)
            return 0

        lax.fori_loop(0, dil * nb, piece, 0, unroll=2)

    def combine(i, _):
        rows = pl.ds(pl.multiple_of(i * W, W), W)
        outs = []
        for pr in range(npair):
            ms = [mx_ref[b, pr, rows, :] for b in range(len(DIL_PATTERNS))]
            top = functools.reduce(jnp.maximum, ms)
            es = [jnp.exp(m - top) for m in ms]
            num = sum(e * acc_ref[b, pr, rows, :] for b, e in enumerate(es))
            den = sum(e * jnp.maximum(den_ref[b, pr, rows, :], 1e-30) for b, e in enumerate(es))
            outs.append(num / den)
        o_ref[0, rows, :] = jnp.concatenate(outs, axis=-1)
        return 0

    lax.fori_loop(0, seq // W, combine, 0)


def _dilated(zb, bias):
    b, s, wb = zb.shape
    nbr = len(DIL_PATTERNS)
    pair_spec = lambda c: pl.BlockSpec((1, s, LANES), lambda i: (i, 0, c))
    return pl.pallas_call(
        _dilated_kernel,
        grid=(b,),
        in_specs=[pair_spec(c) for c in range(wb // LANES)]
                 + [pl.BlockSpec(bias.shape, lambda i: (0, 0, 0, 0))],
        out_specs=pl.BlockSpec((1, s, GROUP_WIDTH), lambda i: (i, 0, 0)),
        out_shape=jax.ShapeDtypeStruct((b, s, GROUP_WIDTH), F32),
        scratch_shapes=[pltpu.VMEM((nbr, HEADS // 2, s, LANES), F32) for _ in range(3)],
        compiler_params=pltpu.CompilerParams(dimension_semantics=("arbitrary",),
                                             vmem_limit_bytes=VMEM_LIMIT_BYTES),
        name="dilated",
    )(*([zb] * (wb // LANES)), bias)


def _stack_heads(q):
    return jnp.concatenate([q[:, HEAD_DIM * h:HEAD_DIM * (h + 1)] for h in range(HEADS)], axis=0)


def _scaled_query_stack(q):
    scale = HEAD_DIM ** -0.5
    assert math.log2(scale).is_integer()
    return (_stack_heads(q) * scale).astype(BF16)


def _unstack_heads_t(per_head_t):
    halves = []
    for h in range(0, HEADS, 2):
        halves.append(jnp.concatenate([per_head_t[h], per_head_t[h + 1]], axis=0).T)
    return jnp.concatenate(halves, axis=-1)


def _online_step(lg, mask, vt, m, den, acc):
    lg = jnp.where(mask, lg, NEG)
    m_new = jnp.maximum(m, jnp.max(lg, axis=0, keepdims=True))
    p = jnp.where(mask, jnp.exp(lg - m_new), 0.0)
    corr = jnp.exp(m - m_new)
    den = den * corr + jnp.sum(p, axis=0, keepdims=True)
    acc = acc * corr + _dot(vt, p.astype(BF16))
    return m_new, den, acc


def _pair_bias(btab_ref, behind):
    return jnp.concatenate([btab_ref[jnp.minimum(behind, 2)], btab_ref[jnp.clip(behind - 1, 0, 2)]], axis=0)


def _online_init():
    return (tuple(jnp.full((1, BLK), NEG, F32) for _ in range(HEADS)),
            tuple(jnp.zeros((1, BLK), F32) for _ in range(HEADS)),
            tuple(jnp.zeros((HEAD_DIM, BLK), F32) for _ in range(HEADS)))


def _dsa_kernel(z_ref, btab_ref, o_ref, vt_ref, key_ref, hi_ref, lo_ref, lom_ref):
    i = pl.program_id(1)
    seq = z_ref.shape[1]
    nkb = seq // BLK
    hd = HEAD_DIM
    topk = min(DSA_TOPK, seq // 4)
    scale = hd ** -0.5
    t0 = pl.multiple_of(i * BLK, BLK)

    npair = i // 2 + 1

    @pl.when(i == 0)
    def _():
        for kb in range(nkb):
            vt = z_ref[0, kb * BLK:(kb + 1) * BLK, 512:640].T[hd:2 * hd, :]
            vt_ref[kb // 2, :, (kb % 2) * BLK:(kb % 2 + 1) * BLK] = vt.astype(BF16)
        key_ref[...] = jnp.full(key_ref.shape, -2 ** 31, I32)
        for half_ref in (hi_ref, lo_ref, lom_ref):
            half_ref[...] = jnp.full(half_ref.shape, I16_MIN, I16)

    zq = z_ref[0, pl.ds(t0, BLK), :]
    cq = zq[:, 0:256]
    ciq = _stack_heads(zq[:, 256:512]).astype(BF16)
    iw = zq[:, 640:768].T[IDX_DIM:IDX_DIM + IDX_HEADS, :] * ((IDX_HEADS * IDX_DIM) ** -0.5)
    s_loc = lax.broadcasted_iota(I32, (KEYS, BLK), 0)
    t_glob = t0 + lax.broadcasted_iota(I32, (KEYS, BLK), 1)

    def score_block(kp, _):
        r0 = pl.multiple_of(kp * KEYS, KEYS)
        ik = z_ref[0, pl.ds(r0, KEYS), 640:704].astype(BF16)
        rel = _nt_dot(ik, ciq)
        sc = jnp.zeros((KEYS, BLK), F32)
        for h in range(IDX_HEADS):
            sc = sc + jnp.maximum(rel[:, BLK * h:BLK * (h + 1)], 0.0) * iw[h:h + 1, :]
        sc = jnp.where(r0 + s_loc <= t_glob, sc, NEG)
        key = _sortable_key(sc)
        key_ref[kp] = key
        hi_ref[kp] = (key >> 16).astype(I16)
        lo_ref[kp] = ((key & 0xFFFF) - 2 ** 15).astype(I16)
        return 0

    lax.fori_loop(0, npair, score_block, 0)

    def count(ref, pred, pairs):
        dt = ref.dtype
        rows = 8 * 4 // dt.itemsize
        def body(kp, acc):
            hit = jnp.where(pred(kp, ref[kp]), jnp.ones((), dt), jnp.zeros((), dt))
            hit = hit.reshape(KEYS // rows, rows, BLK)
            parts = [hit[n] for n in range(KEYS // rows)]
            while len(parts) > 1:
                parts = [a + b for a, b in zip(parts[0::2], parts[1::2])]
            return acc + parts[0]
        acc = jnp.zeros((rows, BLK), dt)
        if isinstance(pairs, int):
            for kp in range(pairs):
                acc = body(kp, acc)
        else:
            acc = lax.fori_loop(0, pairs, body, acc)
        return jnp.sum(acc.astype(I32), axis=0, keepdims=True)

    def threshold(pairs):
        def half_search(ref, k):
            def bit(it, thr):
                cand = thr + lax.shift_left(jnp.int32(1), 15 - it)
                c = count(ref, lambda kp, half: half >= cand.astype(I16), pairs)
                return jnp.where(c >= k, cand, thr)
            return lax.fori_loop(0, 16, bit, jnp.full((1, BLK), I16_MIN, I32))

        def run(_):
            thr_hi = half_search(hi_ref, topk)
            thr_hi16 = thr_hi.astype(I16)
            above_hi = count(hi_ref, lambda kp, half: half > thr_hi16, pairs)
            for kp in range(pairs):
                lom_ref[kp] = jnp.where(hi_ref[kp] == thr_hi16, lo_ref[kp], jnp.int16(I16_MIN))
            thr_lo = half_search(lom_ref, topk - above_hi)
            thr = (thr_hi << 16) | (thr_lo + 2 ** 15)
            n_gt = count(key_ref, lambda kp, key: key > thr, pairs)
            n_eq = count(key_ref, lambda kp, key: key == thr, pairs)
            return thr, n_gt, n_eq
        return run

    walks = list(range(2, seq // KEYS + 1, 2))
    thr, n_gt, n_eq = lax.switch((npair - 1) // 2, [threshold(p) for p in walks], 0)
    need = topk - n_gt

    def tie_search(_):
        def idx_bit(it, jm):
            cand = jm + lax.shift_left(jnp.int32(1), 10 - it)
            c = count(key_ref, lambda kp, key: (key == thr) & (kp * KEYS + s_loc < cand), npair)
            return jnp.where(c < need, cand, jm)
        return lax.fori_loop(0, 11, idx_bit, jnp.zeros((1, BLK), I32))

    assert seq == 2 ** 11
    jmax = lax.cond(jnp.max(n_eq - need) > 0, tie_search,
                    lambda _: jnp.full((1, BLK), seq - 1, I32), 0)

    qs = _scaled_query_stack(cq)

    def attend(kp, carry):
        ms, dens, accs = carry
        r0 = pl.multiple_of(kp * KEYS, KEYS)
        kblk = z_ref[0, pl.ds(r0, KEYS), 512:576].astype(BF16)
        lg = _nt_dot(kblk, qs) + _pair_bias(btab_ref, i - 2 * kp)
        key = key_ref[kp]
        s_glob = r0 + s_loc
        mask = ((key > thr) | ((key == thr) & (s_glob <= jmax))) & (s_glob <= t_glob)
        vt = vt_ref[kp]
        out = [_online_step(lg[:, BLK * h:BLK * (h + 1)], mask, vt, ms[h], dens[h], accs[h])
               for h in range(HEADS)]
        return tuple(o[0] for o in out), tuple(o[1] for o in out), tuple(o[2] for o in out)

    ms, dens, accs = lax.fori_loop(0, npair, attend, _online_init())
    o_ref[0] = _unstack_heads_t([accs[h] / jnp.maximum(dens[h], 1e-30) for h in range(HEADS)])


def _dsa(zc, btab):
    b, s, wc = zc.shape
    nq = s // BLK
    return pl.pallas_call(
        _dsa_kernel,
        grid=(b, nq),
        in_specs=[pl.BlockSpec((1, s, wc), lambda bi, i: (bi, 0, 0)),
                  pl.BlockSpec(btab.shape, lambda bi, i: (0, 0, 0))],
        out_specs=pl.BlockSpec((1, BLK, GROUP_WIDTH), lambda bi, i: (bi, i, 0)),
        out_shape=jax.ShapeDtypeStruct((b, s, GROUP_WIDTH), F32),
        scratch_shapes=[pltpu.VMEM((s // KEYS, HEAD_DIM, KEYS), BF16), pltpu.VMEM((s // KEYS, KEYS, BLK), I32)]
                       + [pltpu.VMEM((s // KEYS, KEYS, BLK), I16)] * 3,
        compiler_params=pltpu.CompilerParams(dimension_semantics=("arbitrary", "arbitrary"),
                                             vmem_limit_bytes=VMEM_LIMIT_BYTES),
        name="dsa",
    )(zc, btab)


def _nsa_kernel(z_ref, zc_ref, w1_ref, pos_ref, w2_ref, bsel_ref, bcmp_ref, ovt_ref, exp_ref, o_ref,
                vst_ref, vwt_ref, cmp_ref, cmpt_ref):
    i = pl.program_id(1)
    seq = z_ref.shape[1]
    nkb = seq // BLK
    hd = HEAD_DIM
    scale = hd ** -0.5
    n_cmp = (seq - NSA_CMP_LEN) // NSA_CMP_STRIDE + 1
    n_sel = seq // NSA_SEL_LEN
    topn = min(NSA_TOPN, n_sel)
    half = NSA_CMP_LEN // 2
    assert half == NSA_CMP_STRIDE and n_cmp + 1 == seq // NSA_CMP_STRIDE == BLK and n_sel <= BLK
    t0 = pl.multiple_of(i * BLK, BLK)
    hi = lax.Precision.HIGHEST

    @pl.when(i == 0)
    def _():
        for kb in range(nkb):
            rows = slice(kb * BLK, (kb + 1) * BLK)
            cols = slice((kb % 2) * BLK, (kb % 2 + 1) * BLK)
            vst_ref[kb // 2, :, cols] = z_ref[0, rows, 384:512].T[hd:2 * hd, :].astype(BF16)
            vwt_ref[kb // 2, :, cols] = z_ref[0, rows, 512:640].T[hd:2 * hd, :].astype(BF16)
        first = jnp.zeros((BLK, 2 * NSA_CMP_HIDDEN), F32)
        second = jnp.zeros((BLK, 2 * NSA_CMP_HIDDEN), F32)
        for j in range(half):
            xj = zc_ref[0, pl.ds(j, BLK, stride=NSA_CMP_STRIDE), :]
            first = first + _dot((xj + pos_ref[j:j + 1, :]).astype(BF16), w1_ref[j])
            second = second + _dot((xj + pos_ref[half + j:half + j + 1, :]).astype(BF16), w1_ref[half + j])
        hid = first + pltpu.roll(second, BLK - 1, 0)
        hid = hid * _sigmoid(hid)
        cmp = _dot(hid.astype(BF16), w2_ref[...])
        cmp_ref[...] = cmp
        cmpt_ref[...] = cmp.T

    zq = z_ref[0, pl.ds(t0, BLK), :]
    qs = _scaled_query_stack(zq[:, 0:256])
    gates = _sigmoid(zq[:, 640:768].T[0:16, :])
    row = lax.broadcasted_iota(I32, (BLK, BLK), 0)
    t_glob = t0 + lax.broadcasted_iota(I32, (BLK, BLK), 1)

    kcmp = cmp_ref[:, 0:hd].astype(BF16)
    vcmpt = cmpt_ref[hd:2 * hd, :].astype(BF16)
    lgc = _nt_dot(kcmp, qs) + bcmp_ref[0]
    mask_c = (t_glob - (row * NSA_CMP_STRIDE + NSA_CMP_LEN - 1) >= 0) & (row < n_cmp)
    o_cmp, psum = [], jnp.zeros((BLK, BLK), F32)
    for h in range(HEADS):
        lg = jnp.where(mask_c, lgc[:, BLK * h:BLK * (h + 1)], NEG)
        m = jnp.max(lg, axis=0, keepdims=True)
        p = jnp.where(mask_c, jnp.exp(lg - m), 0.0)
        p = p / jnp.maximum(jnp.sum(p, axis=0, keepdims=True), 1e-30)
        o_cmp.append(_dot(vcmpt, p.astype(BF16)))
        psum = psum + p

    imp = _dot(ovt_ref[...], psum, precision=hi)
    cur = t_glob >> int(math.log2(NSA_SEL_LEN))
    forced = (row == 0) | (row == cur) | (row == cur - 1)
    imp = jnp.where(forced, NSA_FORCE, imp)
    imp = jnp.where(row * NSA_SEL_LEN <= t_glob, imp, NEG)
    imp = imp[0:n_sel, :]
    jrow = row[0:n_sel, :]
    rank = jnp.zeros((n_sel, BLK), I32)
    for jp in range(n_sel):
        other = imp[jp:jp + 1, :]
        rank = rank + ((other > imp) | ((other == imp) & (jp < jrow))).astype(I32)
    chosen = jnp.where(rank < topn, 1.0, 0.0)
    chosen = jnp.concatenate([chosen, jnp.zeros((BLK - n_sel, BLK), F32)], axis=0).astype(BF16)

    s_loc = lax.broadcasted_iota(I32, (KEYS, BLK), 0)
    t_keys = t0 + lax.broadcasted_iota(I32, (KEYS, BLK), 1)

    def attend(vt_ref, lanes, mask_fn):
        def body(kp, carry):
            ms, dens, accs = carry
            r0 = pl.multiple_of(kp * KEYS, KEYS)
            kblk = z_ref[0, pl.ds(r0, KEYS), lanes].astype(BF16)
            lg = _nt_dot(kblk, qs) + _pair_bias(bsel_ref, i - 2 * kp)
            mask = mask_fn(r0)
            vt = vt_ref[kp]
            out = [_online_step(lg[:, BLK * h:BLK * (h + 1)], mask, vt, ms[h], dens[h], accs[h])
                   for h in range(HEADS)]
            return tuple(o[0] for o in out), tuple(o[1] for o in out), tuple(o[2] for o in out)
        return body

    def mask_sel(r0):
        picked = _dot(exp_ref[pl.ds(r0, KEYS), :], chosen) > 0.5
        return picked & (r0 + s_loc <= t_keys)

    def mask_win(r0):
        dist = t_keys - (r0 + s_loc)
        return (dist >= 0) & (dist < NSA_WINDOW)

    last_pair = i // 2 + 1
    _, den_s, acc_s = lax.fori_loop(0, last_pair, attend(vst_ref, slice(384, 448), mask_sel), _online_init())
    first_w = jnp.maximum(i - NSA_WINDOW // BLK, 0) // 2
    _, den_w, acc_w = lax.fori_loop(first_w, last_pair, attend(vwt_ref, slice(512, 576), mask_win), _online_init())

    outs = []
    for h in range(HEADS):
        o_s = acc_s[h] / jnp.maximum(den_s[h], 1e-30)
        o_w = acc_w[h] / jnp.maximum(den_w[h], 1e-30)
        outs.append(gates[3 * h:3 * h + 1, :] * o_cmp[h] + gates[3 * h + 1:3 * h + 2, :] * o_s
                    + gates[3 * h + 2:3 * h + 3, :] * o_w)
    o_ref[0] = _unstack_heads_t(outs)


def _nsa(zd, w1, pos, w2, bsel, bcmp, ovt, expand):
    b, s, wd = zd.shape
    nq = s // BLK
    full = lambda a: pl.BlockSpec(a.shape, lambda bi, i: (0,) * a.ndim)
    return pl.pallas_call(
        _nsa_kernel,
        grid=(b, nq),
        in_specs=[pl.BlockSpec((1, s, wd), lambda bi, i: (bi, 0, 0)),
                  pl.BlockSpec((1, s, LANES), lambda bi, i: (bi, 0, GROUP_WIDTH // LANES)),
                  full(w1), full(pos), full(w2), full(bsel),
                  pl.BlockSpec((1, BLK, HEADS * BLK), lambda bi, i: (i, 0, 0)),
                  full(ovt), full(expand)],
        out_specs=pl.BlockSpec((1, BLK, GROUP_WIDTH), lambda bi, i: (bi, i, 0)),
        out_shape=jax.ShapeDtypeStruct((b, s, GROUP_WIDTH), F32),
        scratch_shapes=[pltpu.VMEM((s // KEYS, HEAD_DIM, KEYS), BF16), pltpu.VMEM((s // KEYS, HEAD_DIM, KEYS), BF16),
                        pltpu.VMEM((BLK, BLK), F32), pltpu.VMEM((BLK, BLK), F32)],
        compiler_params=pltpu.CompilerParams(dimension_semantics=("arbitrary", "arbitrary"),
                                             vmem_limit_bytes=VMEM_LIMIT_BYTES),
        name="nsa",
    )(zd, zd, w1, pos, w2, bsel, bcmp, ovt, expand)


def _outproj_kernel(alpha, x_ref, a_ref, b_ref, c_ref, d_ref, w_ref, bo_ref, g_ref, beta_ref, o_ref):
    acc = bo_ref[...] + _dot(a_ref[...].astype(BF16), w_ref[0:GROUP_WIDTH, :])
    for n, m_ref in enumerate((b_ref, c_ref, d_ref), start=1):
        acc = acc + _dot(m_ref[...].astype(BF16), w_ref[n * GROUP_WIDTH:(n + 1) * GROUP_WIDTH, :])
    o_ref[...] = _layer_norm_rows(alpha * x_ref[...] + acc, g_ref[...], beta_ref[...])


def _outproj(alpha, x2d, mixed, w, bo, g, beta, tm=512):
    t = x2d.shape[0]
    row_spec = lambda width: pl.BlockSpec((tm, width), lambda i: (i, 0))
    const = lambda a: pl.BlockSpec(a.shape, lambda i: (0, 0))
    return pl.pallas_call(
        functools.partial(_outproj_kernel, alpha),
        grid=(t // tm,),
        in_specs=[row_spec(D_MODEL)] + [row_spec(GROUP_WIDTH)] * N_MIXERS
                 + [const(w), const(bo), const(g), const(beta)],
        out_specs=row_spec(D_MODEL),
        out_shape=jax.ShapeDtypeStruct((t, D_MODEL), F32),
        compiler_params=pltpu.CompilerParams(dimension_semantics=("arbitrary",),
                                             vmem_limit_bytes=VMEM_LIMIT_BYTES),
        name="outproj_ln",
    )(x2d, *mixed, w, bo, g, beta)


def _ffn_kernel(alpha, x_ref, w1_ref, b1_ref, w2_ref, b2_ref, g_ref, beta_ref, o_ref, xb_ref, acc_ref):
    j = pl.program_id(1)

    @pl.when(j == 0)
    def _():
        xb_ref[...] = x_ref[...].astype(BF16)
        acc_ref[...] = jnp.zeros_like(acc_ref)

    hdn = jnp.maximum(_dot(xb_ref[...], w1_ref[...]) + b1_ref[...], 0.0)
    acc_ref[...] += _dot(jnp.square(hdn).astype(BF16), w2_ref[...])

    @pl.when(j == pl.num_programs(1) - 1)
    def _():
        y = alpha * x_ref[...] + (acc_ref[...] + b2_ref[...])
        o_ref[...] = _layer_norm_rows(y, g_ref[...], beta_ref[...])


def _ffn(alpha, x2d, w1, b1, w2, b2, g, beta, tm=1024, tf=512):
    t = x2d.shape[0]
    return pl.pallas_call(
        functools.partial(_ffn_kernel, alpha),
        grid=(t // tm, D_FF // tf),
        in_specs=[pl.BlockSpec((tm, D_MODEL), lambda i, j: (i, 0)),
                  pl.BlockSpec((D_MODEL, tf), lambda i, j: (0, j)),
                  pl.BlockSpec((1, tf), lambda i, j: (0, j)),
                  pl.BlockSpec((tf, D_MODEL), lambda i, j: (j, 0)),
                  pl.BlockSpec((1, D_MODEL), lambda i, j: (0, 0)),
                  pl.BlockSpec((1, D_MODEL), lambda i, j: (0, 0)),
                  pl.BlockSpec((1, D_MODEL), lambda i, j: (0, 0))],
        out_specs=pl.BlockSpec((tm, D_MODEL), lambda i, j: (i, 0)),
        out_shape=jax.ShapeDtypeStruct((t, D_MODEL), F32),
        scratch_shapes=[pltpu.VMEM((tm, D_MODEL), BF16), pltpu.VMEM((tm, D_MODEL), F32)],
        compiler_params=pltpu.CompilerParams(dimension_semantics=("arbitrary", "arbitrary"),
                                             vmem_limit_bytes=VMEM_LIMIT_BYTES),
        name="ffn_ln",
    )(x2d, w1, b1, w2, b2, g, beta)


def _bias_of_distance(rel_bias_heads, dist):
    onehot = np.eye(NUM_BUCKETS, dtype=np.float32)[_t5_bucket_np(np.asarray(dist))]
    return jnp.dot(rel_bias_heads.T, jnp.asarray(onehot.T), precision=lax.Precision.HIGHEST)


def _shifted_rows(v, n_rows, n_cols, step):
    period = v.shape[-1]
    assert n_cols <= period - step
    flat = jnp.tile(v, (1,) * (v.ndim - 1) + (n_rows,))[..., :n_rows * (period - step)]
    return flat.reshape(v.shape[:-1] + (n_rows, period - step))[..., :n_cols]


def _wrapped(period):
    idx = np.arange(period)
    return np.where(idx < period // 2, idx, idx - period)


def _dilated_bias(rel_bias):
    x = _wrapped(4 * BLK)
    tabs = [_shifted_rows(_bias_of_distance(rel_bias[:, 0:HEADS], (BLK - x) * dil), BLK, 2 * BLK, 1)
            for _, dil in DIL_PATTERNS]
    tabs = jnp.stack(tabs).reshape(len(DIL_PATTERNS), HEADS // 2, 2, BLK, 2 * BLK)
    return jnp.transpose(tabs, (0, 1, 3, 2, 4)).reshape(len(DIL_PATTERNS), HEADS // 2, BLK, 4 * BLK)


def _toeplitz_bias_t(rel_bias_heads):
    assert (_t5_bucket_np(np.arange(BLK + 1, 64 * BLK)) == NUM_BUCKETS - 1).all()
    x = _wrapped(2 * BLK)
    tabs = [_shifted_rows(_bias_of_distance(rel_bias_heads, BLK * delta + x), BLK, BLK, 1)
            for delta in range(3)]
    return jnp.transpose(jnp.stack(tabs), (0, 2, 1, 3)).reshape(3, BLK, HEADS * BLK)


def _compressed_bias_t(rel_bias_heads, seq):
    nq = seq // BLK
    x = _wrapped(2 * seq + BLK)
    v = _bias_of_distance(rel_bias_heads, x - (NSA_CMP_LEN - 1))
    tab = _shifted_rows(v, BLK, seq, NSA_CMP_STRIDE)
    tab = tab.reshape(HEADS, BLK, nq, BLK)
    return jnp.transpose(tab, (2, 1, 0, 3)).reshape(nq, BLK, HEADS * BLK)


def _nsa_constants(seq):
    n_cmp = (seq - NSA_CMP_LEN) // NSA_CMP_STRIDE + 1
    n_sel = seq // NSA_SEL_LEN
    cs = np.arange(n_cmp)[:, None] * NSA_CMP_STRIDE
    ss = np.arange(n_sel)[None, :] * NSA_SEL_LEN
    ov = np.clip(np.minimum(cs + NSA_CMP_LEN, ss + NSA_SEL_LEN) - np.maximum(cs, ss), 0, None) / NSA_CMP_LEN
    ovt = np.zeros((BLK, BLK), np.float32)
    ovt[:n_sel, :n_cmp] = ov.T
    expand = np.zeros((seq, BLK), np.float32)
    expand[np.arange(seq), np.arange(seq) // NSA_SEL_LEN] = 1.0
    return jnp.asarray(ovt), jnp.asarray(expand, BF16)


def _nsa_weights(cmp_pos, cmp_w1, cmp_w2):
    hd, hid = HEAD_DIM, NSA_CMP_HIDDEN
    w1 = cmp_w1.reshape(2, NSA_CMP_LEN, hd, hid)
    zeros = jnp.zeros((NSA_CMP_LEN, hd, hid), F32)
    w1 = jnp.concatenate([jnp.concatenate([w1[0], zeros], axis=-1),
                          jnp.concatenate([zeros, w1[1]], axis=-1)], axis=1)
    pos = jnp.concatenate([cmp_pos[0], cmp_pos[1]], axis=-1)
    z2 = jnp.zeros((hid, hd), F32)
    w2 = jnp.concatenate([jnp.concatenate([cmp_w2[0], z2], axis=-1),
                          jnp.concatenate([z2, cmp_w2[1]], axis=-1)], axis=0)
    return w1.astype(BF16), pos, w2.astype(BF16)


def kernel(x, w_in, b_in, a_conv, a_norm, d_cmp_pos, d_cmp_w1, d_cmp_w2, w_out, b_out, ln1_g, ln1_b,
           w_ff1, b_ff1, w_ff2, b_ff2, ln2_g, ln2_b, rel_bias):
    bsz, seq, _ = x.shape
    depth = w_in.shape[0]
    alpha = (2 * depth) ** 0.25
    nc = seq // M_CHUNK
    bias_dil = _dilated_bias(rel_bias)
    btab_dsa = _toeplitz_bias_t(rel_bias[:, HEADS:2 * HEADS])
    btab_nsa = _toeplitz_bias_t(rel_bias[:, 2 * HEADS:3 * HEADS])
    bcmp_nsa = _compressed_bias_t(rel_bias[:, 2 * HEADS:3 * HEADS], seq)
    ovt, expand = _nsa_constants(seq)

    h = x.reshape(bsz * seq, D_MODEL)
    for l in range(depth):
        w_l = _permute_columns(w_in[l]).astype(BF16)
        b_l = _permute_columns(b_in[l])[None, :]
        za, zb, zc, zd = (z.reshape(bsz, seq, -1) for z in _inproj(h, w_l, b_l))
        out_a = _mlstm(za, _mlstm_gate_rows(za), a_conv[l], a_norm[l][None, :])
        out_b = _dilated(zb, bias_dil)
        out_c = _dsa(zc, btab_dsa)
        nsa_w1, nsa_pos, nsa_w2 = _nsa_weights(d_cmp_pos[l], d_cmp_w1[l], d_cmp_w2[l])
        out_d = _nsa(zd, nsa_w1, nsa_pos, nsa_w2, btab_nsa, bcmp_nsa, ovt, expand)
        mixed = [o.reshape(bsz * seq, GROUP_WIDTH) for o in (out_a, out_b, out_c, out_d)]
        h = _outproj(alpha, h, mixed, w_out[l].astype(BF16), b_out[l][None, :],
                     ln1_g[l][None, :], ln1_b[l][None, :])
        h = _ffn(alpha, h, w_ff1[l].astype(BF16), b_ff1[l][None, :], w_ff2[l].astype(BF16),
                 b_ff2[l][None, :], ln2_g[l][None, :], ln2_b[l][None, :])
    return h.reshape(bsz, seq, D_MODEL)
```

```python
import functools
import math

import numpy as np
import jax
import jax.numpy as jnp
from jax import lax
from jax.experimental import pallas as pl
from jax.experimental.pallas import tpu as pltpu

F32 = jnp.float32
BF16 = jnp.bfloat16
I32 = jnp.int32
I16 = jnp.int16
I16_MIN = -2 ** 15

D_MODEL = 1024
N_MIXERS = 4
HEADS = 4
HEAD_DIM = D_MODEL // (N_MIXERS * HEADS)
GROUP_WIDTH = HEADS * HEAD_DIM
D_FF = 4 * D_MODEL
LN_EPS = 1e-5
NEG = -1e30

M_QK_DIM = HEAD_DIM // 2
M_CHUNK = 64
M_CONV = 4
DIL_PATTERNS = ((128, 1), (512, 4), (2048, 16))
IDX_HEADS = 4
IDX_DIM = 64
DSA_TOPK = 256
NSA_CMP_LEN = 32
NSA_CMP_STRIDE = 16
NSA_SEL_LEN = 64
NSA_TOPN = 16
NSA_WINDOW = 512
NSA_CMP_HIDDEN = 256
NSA_FORCE = 1e9
NUM_BUCKETS = 32
MAX_DISTANCE = 128

LANES = 128
BLK = 128
KEYS = 2 * BLK
VMEM_LIMIT_BYTES = 56 * 1024 * 1024

IN_SPLITS = (
    ('a_q', HEADS * M_QK_DIM), ('a_k', HEADS * M_QK_DIM), ('a_v', GROUP_WIDTH),
    ('a_i', HEADS), ('a_f', HEADS), ('a_o', GROUP_WIDTH),
    ('b_q', GROUP_WIDTH), ('b_k', GROUP_WIDTH), ('b_v', GROUP_WIDTH),
    ('c_q', GROUP_WIDTH), ('c_k', HEAD_DIM), ('c_v', HEAD_DIM),
    ('c_iq', IDX_HEADS * IDX_DIM), ('c_ik', IDX_DIM), ('c_iw', IDX_HEADS),
    ('d_q', GROUP_WIDTH), ('d_kc', HEAD_DIM), ('d_vc', HEAD_DIM),
    ('d_ks', HEAD_DIM), ('d_vs', HEAD_DIM), ('d_kw', HEAD_DIM), ('d_vw', HEAD_DIM),
    ('d_g', 3 * HEADS),
)

GROUP_LAYOUT = (
    (('a_q', 'a_k'), ('a_v',), ('a_o',), (('a_i', HEAD_DIM),), (('a_i', M_QK_DIM),),
     (('a_f', HEAD_DIM),), (('a_f', M_QK_DIM),)),
    (('b_q',), ('b_k',), ('b_v',)),
    (('c_q',), ('c_iq',), ('c_k', 'c_v'), ('c_ik', 'c_iw')),
    (('d_q',), ('d_kc', 'd_vc'), ('d_ks', 'd_vs'), ('d_kw', 'd_vw'), ('d_g',)),
    (('a_i', 'a_f'),),
)


def _round_up(n, m):
    return -(-n // m) * m


def _projection_layout():
    offs, off = {}, 0
    for name, width in IN_SPLITS:
        offs[name] = (off, width)
        off += width
    runs, group_widths = [], []
    for group in GROUP_LAYOUT:
        gwidth = 0
        for chunk in group:
            cwidth = 0
            for entry in chunk:
                name, rep = entry if isinstance(entry, tuple) else (entry, 1)
                o, w = offs[name]
                runs.append((o, w, rep))
                cwidth += w * rep
            pad = _round_up(cwidth, LANES) - cwidth
            if pad:
                runs.append((-1, pad, 1))
            gwidth += cwidth + pad
        group_widths.append(gwidth)
    return tuple(runs), tuple(group_widths)


PROJ_RUNS, GROUP_WIDTHS = _projection_layout()
PROJ_WIDTH = int(sum(GROUP_WIDTHS))
PROJ_PERM = np.concatenate([np.repeat(np.arange(o, o + w), r) if o >= 0 else np.full(w, -1)
                            for o, w, r in PROJ_RUNS]).astype(np.int32)


def _permute_columns(a):
    parts = []
    for o, w, r in PROJ_RUNS:
        if o < 0:
            parts.append(jnp.zeros(a.shape[:-1] + (w,), a.dtype))
        else:
            parts.append(a[..., o:o + w] if r == 1 else jnp.repeat(a[..., o:o + w], r, axis=-1))
    return jnp.concatenate(parts, axis=-1)


def _t5_bucket_np(dist):
    n = np.maximum(dist, 0)
    max_exact = NUM_BUCKETS // 2
    nf = np.maximum(n, max_exact).astype(np.float32)
    large = max_exact + (np.log(nf / max_exact) / math.log(MAX_DISTANCE / max_exact)
                         * (NUM_BUCKETS - max_exact)).astype(np.int32)
    large = np.minimum(large, NUM_BUCKETS - 1)
    return np.where(n < max_exact, n, large).astype(np.int32)


def _nt_dot(a, b, precision=None):
    return lax.dot_general(a, b, (((1,), (1,)), ((), ())), precision=precision,
                           preferred_element_type=F32)


def _dot(a, b, precision=None):
    return jnp.dot(a, b, precision=precision, preferred_element_type=F32)


def _layer_norm_rows(y, g, b):
    mu = jnp.mean(y, axis=-1, keepdims=True)
    var = jnp.mean(jnp.square(y - mu), axis=-1, keepdims=True)
    return (y - mu) * lax.rsqrt(var + LN_EPS) * g + b


def _sigmoid(x):
    return 1.0 / (1.0 + jnp.exp(-x))


def _log_sigmoid(x):
    return -(jnp.maximum(-x, 0.0) + jnp.log1p(jnp.exp(-jnp.abs(x))))


def _sortable_key(x):
    bits = pltpu.bitcast(x, I32)
    return bits ^ ((bits >> 31) & jnp.int32(0x7FFFFFFF))


def _inproj_kernel(x_ref, w_ref, b_ref, *out_refs):
    xb = x_ref[...].astype(BF16)
    off = 0
    for o_ref, width in zip(out_refs, GROUP_WIDTHS):
        o_ref[...] = _dot(xb, w_ref[:, off:off + width]) + b_ref[:, off:off + width]
        off += width


def _inproj(x2d, w, b, tm=512):
    t = x2d.shape[0]
    return pl.pallas_call(
        _inproj_kernel,
        grid=(t // tm,),
        in_specs=[pl.BlockSpec((tm, D_MODEL), lambda i: (i, 0)),
                  pl.BlockSpec((D_MODEL, PROJ_WIDTH), lambda i: (0, 0)),
                  pl.BlockSpec((1, PROJ_WIDTH), lambda i: (0, 0))],
        out_specs=[pl.BlockSpec((tm, gw), lambda i: (i, 0)) for gw in GROUP_WIDTHS],
        out_shape=[jax.ShapeDtypeStruct((t, gw), F32) for gw in GROUP_WIDTHS],
        compiler_params=pltpu.CompilerParams(dimension_semantics=("arbitrary",),
                                             vmem_limit_bytes=VMEM_LIMIT_BYTES),
        name="inproj",
    )(x2d, w, b)


def _split_terms(x, n):
    terms, rest = [], x
    for _ in range(n):
        terms.append(rest.astype(BF16))
        rest = rest - terms[-1].astype(F32)
    return terms


def _iota(shape, dim):
    return lax.broadcasted_iota(I32, shape, dim)


def _mlstm_kernel(z_ref, gt_ref, cw_ref, ng_ref, o_ref, xpad_ref):
    seq = z_ref.shape[1]
    L, DK, DV, H = M_CHUNK, M_QK_DIM, HEAD_DIM, HEADS
    assert L == DV
    wq, wv = H * DK, H * DV
    lg_dk, lg_dv = int(math.log2(DK)), int(math.log2(DV))
    c_v, c_o, c_i64 = 2 * wq, 2 * wq + wv, 2 * wq + 2 * wv
    c_i32, c_f64 = c_i64 + wv, c_i64 + wv + wq
    xpad_ref[0:8, :] = jnp.zeros((8, 2 * wq), F32)
    xpad_ref[8:, :] = z_ref[0, :, 0:2 * wq]

    one_if = lambda cond: jnp.where(cond, 1.0, 0.0).astype(BF16)
    tri_l = one_if(_iota((L, L), 0) >= _iota((L, L), 1))
    trow = _iota((L, wv), 0)
    tri_heads = trow >= (_iota((L, wv), 1) & (L - 1))
    r_vv, c_vv = _iota((wv, wv), 0), _iota((wv, wv), 1)
    same_head = (r_vv >> lg_dv) == (c_vv >> lg_dv)
    ones_bd = one_if(same_head)
    mean_bd = jnp.where(same_head, 1.0 / DV, 0.0).astype(BF16)
    tri_u_bd = one_if(same_head & ((r_vv & (L - 1)) <= (c_vv & (L - 1))))
    state_mask = (_iota((wq, wv), 0) >> lg_dk) == (_iota((wq, wv), 1) >> lg_dv)
    eye_q = one_if(_iota((wq, wq), 0) == _iota((wq, wq), 1))
    head_of_qlane = _iota((L, wq), 1) >> lg_dk
    head_of_vlane = _iota((L, wv), 1) >> lg_dv
    row8 = _iota((8, wv), 0)
    cw = cw_ref[...]
    ng = ng_ref[...]

    def head_mean(x):
        hi_lo = _split_terms(x, 2)
        r = _dot(jnp.concatenate(hi_lo, axis=0), mean_bd)
        return r[0:L] + r[L:2 * L]

    def chunk(c, carry):
        cbd, nbd, m64, m32 = carry
        s0 = pl.multiple_of(c * L, L)
        rows = pl.ds(s0, L)
        xw = xpad_ref[pl.ds(s0, L + 8), :]
        y = sum(cw[j:j + 1, :] * xw[5 + j:5 + j + L, :] for j in range(M_CONV))
        qk = y * _sigmoid(y)
        q = qk[:, 0:wq]
        k = qk[:, wq:] * (DK ** -0.5)
        qb = q.astype(BF16)
        v = z_ref[0, rows, c_v:c_v + wv]
        i64 = z_ref[0, rows, c_i64:c_i64 + wv]
        i32 = z_ref[0, rows, c_i32:c_i32 + wq]
        gr = gt_ref[0, c]

        flog = _log_sigmoid(z_ref[0, rows, c_f64:c_f64 + wv + wq])
        bsum = _dot(tri_l, jnp.concatenate(_split_terms(flog, 3), axis=1))
        w3 = wv + wq
        ball = bsum[:, 0:w3] + bsum[:, w3:2 * w3] + bsum[:, 2 * w3:3 * w3]
        b64, b32 = ball[:, 0:wv], ball[:, wv:w3]
        fterms = [t.astype(F32) for t in _split_terms(_log_sigmoid(gr[1:2, :]), 3)]
        frows = jnp.where(row8 == 0, fterms[0], jnp.where(row8 == 1, fterms[1], jnp.where(row8 == 2, fterms[2], 0.0)))
        bparts = _dot(frows.astype(BF16), tri_u_bd)
        brow = bparts[0:1, :] + bparts[1:2, :] + bparts[2:3, :]

        dall = jnp.where(tri_heads, b64 - brow + gr[0:1, :], NEG)
        cm = i64 - b64
        for sh in (1, 2, 4, 8, 16, 32):
            cm = jnp.where(trow >= sh, jnp.maximum(cm, pltpu.roll(cm, sh, 0)), cm)
        inter = b64 + m64
        m_t = jnp.maximum(inter, b64 + cm)
        kbd = jnp.concatenate([jnp.where(head_of_qlane == h, k, 0.0) for h in range(H)], axis=0).astype(BF16)
        sc = _nt_dot(qb, kbd) * jnp.exp(dall - m_t)
        wi = jnp.exp(inter - m_t)
        vb = v.astype(BF16)
        vbd = jnp.concatenate([jnp.where(head_of_vlane == h, v, 0.0) for h in range(H)], axis=0).astype(BF16)
        pv = _dot(sc.astype(BF16), jnp.concatenate([vbd, ones_bd], axis=1))
        qst = _dot(qb, jnp.concatenate([cbd, nbd], axis=1).astype(BF16))
        num = pv[:, 0:wv] + wi * qst[:, 0:wv]
        den = pv[:, wv:] + wi * qst[:, wv:]
        hh = num / jnp.maximum(jnp.abs(den), jnp.exp(-m_t))

        og = _sigmoid(z_ref[0, rows, c_o:c_o + wv]) * hh
        dev = og - head_mean(og)
        o_ref[0, rows, :] = dev * lax.rsqrt(head_mean(dev * dev) + LN_EPS) * ng

        bl64, bl32 = b64[L - 1:L, :], b32[L - 1:L, :]
        m64_new = jnp.maximum(bl64 + m64, jnp.max(bl64 - b64 + i64, axis=0, keepdims=True))
        g32 = bl32 - b32 + i32
        m32_new = jnp.maximum(bl32 + m32, jnp.max(g32, axis=0, keepdims=True))
        kw = k * jnp.exp(g32 - m32_new)
        wc = jnp.exp(bl64 + m64 - m64_new)
        kwt = _nt_dot(eye_q, kw.astype(BF16)).astype(BF16)
        upd = _dot(kwt, jnp.concatenate([vb, jnp.ones((L, wv), BF16)], axis=1))
        cbd = wc * cbd + jnp.where(state_mask, upd[:, 0:wv], 0.0)
        nbd = wc * nbd + jnp.where(state_mask, upd[:, wv:], 0.0)
        return cbd, nbd, m64_new, m32_new

    init = (jnp.zeros((wq, wv), F32), jnp.zeros((wq, wv), F32), jnp.zeros((1, wv), F32), jnp.zeros((1, wq), F32))
    lax.fori_loop(0, seq // L, chunk, init, unroll=2)


def _mlstm(za, gates_t, conv_w, norm_g):
    b, s, wa = za.shape
    nc = s // M_CHUNK
    return pl.pallas_call(
        _mlstm_kernel,
        grid=(b,),
        in_specs=[pl.BlockSpec((1, s, wa), lambda i: (i, 0, 0)),
                  pl.BlockSpec((1, nc) + gates_t.shape[2:], lambda i: (i, 0, 0, 0)),
                  pl.BlockSpec((M_CONV, 2 * HEADS * M_QK_DIM), lambda i: (0, 0)),
                  pl.BlockSpec((1, GROUP_WIDTH), lambda i: (0, 0))],
        out_specs=pl.BlockSpec((1, s, GROUP_WIDTH), lambda i: (i, 0, 0)),
        out_shape=jax.ShapeDtypeStruct((b, s, GROUP_WIDTH), F32),
        scratch_shapes=[pltpu.VMEM((s + 8, 2 * HEADS * M_QK_DIM), F32)],
        compiler_params=pltpu.CompilerParams(dimension_semantics=("arbitrary",),
                                             vmem_limit_bytes=VMEM_LIMIT_BYTES),
        name="mlstm",
    )(za, gates_t, conv_w, norm_g)


def _mlstm_gate_rows(zg):
    b, s, _ = zg.shape
    gates = zg[:, :, 0:2 * HEADS].reshape(b, s // M_CHUNK, M_CHUNK, 2, HEADS)
    return jnp.transpose(gates, (0, 1, 3, 4, 2)).reshape(b, s // M_CHUNK, 2, HEADS * M_CHUNK)


def _dilated_kernel(q0_ref, q1_ref, k0_ref, k1_ref, v0_ref, v1_ref, bias_ref, o_ref, acc_ref, mx_ref, den_ref):
    seq = q0_ref.shape[1]
    W = BLK
    hd = HEAD_DIM
    npair = HEADS // 2
    scale = hd ** -0.5
    assert math.log2(scale).is_integer()
    nk = 2 * W
    qi = _iota((W, 2 * nk), 0)
    ki = _iota((W, 2 * nk), 1) & (nk - 1)
    j = W + qi - ki
    band = (j >= 0) & (j <= W)
    head_of_lane = _iota((nk, 2 * hd), 1) >> int(math.log2(hd))
    ones_bd = jnp.concatenate([jnp.where(head_of_lane == hh, 1.0, 0.0) for hh in range(2)], axis=0).astype(BF16)
    q_refs, k_refs, v_refs = (q0_ref, q1_ref), (k0_ref, k1_ref), (v0_ref, v1_ref)

    def block_diag(x):
        return jnp.concatenate([jnp.where(head_of_lane == hh, x, 0.0) for hh in range(2)], axis=0).astype(BF16)

    for br, (window, dil) in enumerate(DIL_PATTERNS):
        assert window // dil == W
        nb = (seq // dil) // W

        def piece(idx, _, br=br, dil=dil, nb=nb):
            n = idx % nb
            if dil == 1:
                rows_q = pl.ds(pl.multiple_of(W * n, W), W)
                rows_p = pl.ds(pl.multiple_of(W * jnp.maximum(n - 1, 0), W), W)
            else:
                r = idx // nb
                rows_q = pl.ds(r + dil * W * n, W, stride=dil)
                rows_p = pl.ds(r + dil * W * jnp.maximum(n - 1, 0), W, stride=dil)
            mask = band & (ki >= jnp.where(n > 0, 0, W))
            for pr in range(npair):
                q = (q_refs[pr][0, rows_q, :] * scale).astype(BF16)
                k2 = jnp.concatenate([k_refs[pr][0, rows_p, :], k_refs[pr][0, rows_q, :]], axis=0)
                v2 = jnp.concatenate([v_refs[pr][0, rows_p, :], v_refs[pr][0, rows_q, :]], axis=0)
                lg = jnp.where(mask, _nt_dot(q, block_diag(k2)) + bias_ref[br, pr], NEG)
                ps, mxs = [], []
                for hh in range(2):
                    sl = slice(nk * hh, nk * (hh + 1))
                    m = jnp.max(lg[:, sl], axis=-1, keepdims=True)
                    ps.append(jnp.where(mask[:, sl], jnp.exp(lg[:, sl] - m), 0.0))
                    mxs.append(jnp.broadcast_to(m, (W, hd)))
                p = jnp.concatenate(ps, axis=-1).astype(BF16)
                pv = _dot(p, jnp.concatenate([block_diag(v2), ones_bd], axis=1))
                acc_ref[br, pr, rows_q, :] = pv[:, 0:2 * hd]
                den_ref[br, pr, rows_q, :] = pv[:, 2 * hd:]
                mx_ref[br, pr, rows_q, :] = jnp.concatenate(mxs, axis=-1)
            return 0

        lax.fori_loop(0, dil * nb, piece, 0, unroll=2)

    def combine(i, _):
        rows = pl.ds(pl.multiple_of(i * W, W), W)
        outs = []
        for pr in range(npair):
            ms = [mx_ref[b, pr, rows, :] for b in range(len(DIL_PATTERNS))]
            top = functools.reduce(jnp.maximum, ms)
            es = [jnp.exp(m - top) for m in ms]
            num = sum(e * acc_ref[b, pr, rows, :] for b, e in enumerate(es))
            den = sum(e * jnp.maximum(den_ref[b, pr, rows, :], 1e-30) for b, e in enumerate(es))
            outs.append(num / den)
        o_ref[0, rows, :] = jnp.concatenate(outs, axis=-1)
        return 0

    lax.fori_loop(0, seq // W, combine, 0)


def _dilated(zb, bias):
    b, s, wb = zb.shape
    nbr = len(DIL_PATTERNS)
    pair_spec = lambda c: pl.BlockSpec((1, s, LANES), lambda i: (i, 0, c))
    return pl.pallas_call(
        _dilated_kernel,
        grid=(b,),
        in_specs=[pair_spec(c) for c in range(wb // LANES)]
                 + [pl.BlockSpec(bias.shape, lambda i: (0, 0, 0, 0))],
        out_specs=pl.BlockSpec((1, s, GROUP_WIDTH), lambda i: (i, 0, 0)),
        out_shape=jax.ShapeDtypeStruct((b, s, GROUP_WIDTH), F32),
        scratch_shapes=[pltpu.VMEM((nbr, HEADS // 2, s, LANES), F32) for _ in range(3)],
        compiler_params=pltpu.CompilerParams(dimension_semantics=("arbitrary",),
                                             vmem_limit_bytes=VMEM_LIMIT_BYTES),
        name="dilated",
    )(*([zb] * (wb // LANES)), bias)


def _stack_heads(q):
    return jnp.concatenate([q[:, HEAD_DIM * h:HEAD_DIM * (h + 1)] for h in range(HEADS)], axis=0)


def _scaled_query_stack(q):
    scale = HEAD_DIM ** -0.5
    assert math.log2(scale).is_integer()
    return (_stack_heads(q) * scale).astype(BF16)


def _unstack_heads_t(per_head_t):
    halves = []
    for h in range(0, HEADS, 2):
        halves.append(jnp.concatenate([per_head_t[h], per_head_t[h + 1]], axis=0).T)
    return jnp.concatenate(halves, axis=-1)


def _online_step(lg, mask, vt, m, den, acc):
    lg = jnp.where(mask, lg, NEG)
    m_new = jnp.maximum(m, jnp.max(lg, axis=0, keepdims=True))
    p = jnp.where(mask, jnp.exp(lg - m_new), 0.0)
    corr = jnp.exp(m - m_new)
    den = den * corr + jnp.sum(p, axis=0, keepdims=True)
    acc = acc * corr + _dot(vt, p.astype(BF16))
    return m_new, den, acc


def _pair_bias(btab_ref, behind):
    return jnp.concatenate([btab_ref[jnp.minimum(behind, 2)], btab_ref[jnp.clip(behind - 1, 0, 2)]], axis=0)


def _online_init():
    return (tuple(jnp.full((1, BLK), NEG, F32) for _ in range(HEADS)),
            tuple(jnp.zeros((1, BLK), F32) for _ in range(HEADS)),
            tuple(jnp.zeros((HEAD_DIM, BLK), F32) for _ in range(HEADS)))


def _dsa_kernel(z_ref, btab_ref, o_ref, vt_ref, key_ref, hi_ref, lo_ref, lom_ref):
    i = pl.program_id(1)
    seq = z_ref.shape[1]
    nkb = seq // BLK
    hd = HEAD_DIM
    topk = min(DSA_TOPK, seq // 4)
    scale = hd ** -0.5
    t0 = pl.multiple_of(i * BLK, BLK)

    npair = i // 2 + 1

    @pl.when(i == 0)
    def _():
        for kb in range(nkb):
            vt = z_ref[0, kb * BLK:(kb + 1) * BLK, 512:640].T[hd:2 * hd, :]
            vt_ref[kb // 2, :, (kb % 2) * BLK:(kb % 2 + 1) * BLK] = vt.astype(BF16)
        key_ref[...] = jnp.full(key_ref.shape, -2 ** 31, I32)
        for half_ref in (hi_ref, lo_ref, lom_ref):
            half_ref[...] = jnp.full(half_ref.shape, I16_MIN, I16)

    zq = z_ref[0, pl.ds(t0, BLK), :]
    cq = zq[:, 0:256]
    ciq = _stack_heads(zq[:, 256:512]).astype(BF16)
    iw = zq[:, 640:768].T[IDX_DIM:IDX_DIM + IDX_HEADS, :] * ((IDX_HEADS * IDX_DIM) ** -0.5)
    s_loc = lax.broadcasted_iota(I32, (KEYS, BLK), 0)
    t_glob = t0 + lax.broadcasted_iota(I32, (KEYS, BLK), 1)

    def score_block(kp, _):
        r0 = pl.multiple_of(kp * KEYS, KEYS)
        ik = z_ref[0, pl.ds(r0, KEYS), 640:704].astype(BF16)
        rel = _nt_dot(ik, ciq)
        sc = jnp.zeros((KEYS, BLK), F32)
        for h in range(IDX_HEADS):
            sc = sc + jnp.maximum(rel[:, BLK * h:BLK * (h + 1)], 0.0) * iw[h:h + 1, :]
        sc = jnp.where(r0 + s_loc <= t_glob, sc, NEG)
        key = _sortable_key(sc)
        key_ref[kp] = key
        hi_ref[kp] = (key >> 16).astype(I16)
        lo_ref[kp] = ((key & 0xFFFF) - 2 ** 15).astype(I16)
        return 0

    lax.fori_loop(0, npair, score_block, 0)

    def count(ref, pred, pairs):
        dt = ref.dtype
        rows = 8 * 4 // dt.itemsize
        def body(kp, acc):
            hit = jnp.where(pred(kp, ref[kp]), jnp.ones((), dt), jnp.zeros((), dt))
            hit = hit.reshape(KEYS // rows, rows, BLK)
            parts = [hit[n] for n in range(KEYS // rows)]
            while len(parts) > 1:
                parts = [a + b for a, b in zip(parts[0::2], parts[1::2])]
            return acc + parts[0]
        acc = jnp.zeros((rows, BLK), dt)
        if isinstance(pairs, int):
            for kp in range(pairs):
                acc = body(kp, acc)
        else:
            acc = lax.fori_loop(0, pairs, body, acc)
        return jnp.sum(acc.astype(I32), axis=0, keepdims=True)

    def threshold(pairs):
        def half_search(ref, k):
            def bit(it, thr):
                cand = thr + lax.shift_left(jnp.int32(1), 15 - it)
                c = count(ref, lambda kp, half: half >= cand.astype(I16), pairs)
                return jnp.where(c >= k, cand, thr)
            return lax.fori_loop(0, 16, bit, jnp.full((1, BLK), I16_MIN, I32))

        def run(_):
            thr_hi = half_search(hi_ref, topk)
            thr_hi16 = thr_hi.astype(I16)
            above_hi = count(hi_ref, lambda kp, half: half > thr_hi16, pairs)
            for kp in range(pairs):
                lom_ref[kp] = jnp.where(hi_ref[kp] == thr_hi16, lo_ref[kp], jnp.int16(I16_MIN))
            thr_lo = half_search(lom_ref, topk - above_hi)
            thr = (thr_hi << 16) | (thr_lo + 2 ** 15)
            n_gt = count(key_ref, lambda kp, key: key > thr, pairs)
            n_eq = count(key_ref, lambda kp, key: key == thr, pairs)
            return thr, n_gt, n_eq
        return run

    walks = list(range(2, seq // KEYS + 1, 2))
    thr, n_gt, n_eq = lax.switch((npair - 1) // 2, [threshold(p) for p in walks], 0)
    need = topk - n_gt

    def tie_search(_):
        def idx_bit(it, jm):
            cand = jm + lax.shift_left(jnp.int32(1), 10 - it)
            c = count(key_ref, lambda kp, key: (key == thr) & (kp * KEYS + s_loc < cand), npair)
            return jnp.where(c < need, cand, jm)
        return lax.fori_loop(0, 11, idx_bit, jnp.zeros((1, BLK), I32))

    assert seq == 2 ** 11
    jmax = lax.cond(jnp.max(n_eq - need) > 0, tie_search,
                    lambda _: jnp.full((1, BLK), seq - 1, I32), 0)

    qs = _scaled_query_stack(cq)

    def attend(kp, carry):
        ms, dens, accs = carry
        r0 = pl.multiple_of(kp * KEYS, KEYS)
        kblk = z_ref[0, pl.ds(r0, KEYS), 512:576].astype(BF16)
        lg = _nt_dot(kblk, qs) + _pair_bias(btab_ref, i - 2 * kp)
        key = key_ref[kp]
        s_glob = r0 + s_loc
        mask = ((key > thr) | ((key == thr) & (s_glob <= jmax))) & (s_glob <= t_glob)
        vt = vt_ref[kp]
        out = [_online_step(lg[:, BLK * h:BLK * (h + 1)], mask, vt, ms[h], dens[h], accs[h])
               for h in range(HEADS)]
        return tuple(o[0] for o in out), tuple(o[1] for o in out), tuple(o[2] for o in out)

    ms, dens, accs = lax.fori_loop(0, npair, attend, _online_init())
    o_ref[0] = _unstack_heads_t([accs[h] / jnp.maximum(dens[h], 1e-30) for h in range(HEADS)])


def _dsa(zc, btab):
    b, s, wc = zc.shape
    nq = s // BLK
    return pl.pallas_call(
        _dsa_kernel,
        grid=(b, nq),
        in_specs=[pl.BlockSpec((1, s, wc), lambda bi, i: (bi, 0, 0)),
                  pl.BlockSpec(btab.shape, lambda bi, i: (0, 0, 0))],
        out_specs=pl.BlockSpec((1, BLK, GROUP_WIDTH), lambda bi, i: (bi, i, 0)),
        out_shape=jax.ShapeDtypeStruct((b, s, GROUP_WIDTH), F32),
        scratch_shapes=[pltpu.VMEM((s // KEYS, HEAD_DIM, KEYS), BF16), pltpu.VMEM((s // KEYS, KEYS, BLK), I32)]
                       + [pltpu.VMEM((s // KEYS, KEYS, BLK), I16)] * 3,
        compiler_params=pltpu.CompilerParams(dimension_semantics=("arbitrary", "arbitrary"),
                                             vmem_limit_bytes=VMEM_LIMIT_BYTES),
        name="dsa",
    )(zc, btab)


def _nsa_kernel(z_ref, zc_ref, w1_ref, pos_ref, w2_ref, bsel_ref, bcmp_ref, ovt_ref, exp_ref, o_ref,
                vst_ref, vwt_ref, cmp_ref, cmpt_ref):
    i = pl.program_id(1)
    seq = z_ref.shape[1]
    nkb = seq // BLK
    hd = HEAD_DIM
    scale = hd ** -0.5
    n_cmp = (seq - NSA_CMP_LEN) // NSA_CMP_STRIDE + 1
    n_sel = seq // NSA_SEL_LEN
    topn = min(NSA_TOPN, n_sel)
    half = NSA_CMP_LEN // 2
    assert half == NSA_CMP_STRIDE and n_cmp + 1 == seq // NSA_CMP_STRIDE == BLK and n_sel <= BLK
    t0 = pl.multiple_of(i * BLK, BLK)
    hi = lax.Precision.HIGHEST

    @pl.when(i == 0)
    def _():
        for kb in range(nkb):
            rows = slice(kb * BLK, (kb + 1) * BLK)
            cols = slice((kb % 2) * BLK, (kb % 2 + 1) * BLK)
            vst_ref[kb // 2, :, cols] = z_ref[0, rows, 384:512].T[hd:2 * hd, :].astype(BF16)
            vwt_ref[kb // 2, :, cols] = z_ref[0, rows, 512:640].T[hd:2 * hd, :].astype(BF16)
        first = jnp.zeros((BLK, 2 * NSA_CMP_HIDDEN), F32)
        second = jnp.zeros((BLK, 2 * NSA_CMP_HIDDEN), F32)
        for j in range(half):
            xj = zc_ref[0, pl.ds(j, BLK, stride=NSA_CMP_STRIDE), :]
            first = first + _dot((xj + pos_ref[j:j + 1, :]).astype(BF16), w1_ref[j])
            second = second + _dot((xj + pos_ref[half + j:half + j + 1, :]).astype(BF16), w1_ref[half + j])
        hid = first + pltpu.roll(second, BLK - 1, 0)
        hid = hid * _sigmoid(hid)
        cmp = _dot(hid.astype(BF16), w2_ref[...])
        cmp_ref[...] = cmp
        cmpt_ref[...] = cmp.T

    zq = z_ref[0, pl.ds(t0, BLK), :]
    qs = _scaled_query_stack(zq[:, 0:256])
    gates = _sigmoid(zq[:, 640:768].T[0:16, :])
    row = lax.broadcasted_iota(I32, (BLK, BLK), 0)
    t_glob = t0 + lax.broadcasted_iota(I32, (BLK, BLK), 1)

    kcmp = cmp_ref[:, 0:hd].astype(BF16)
    vcmpt = cmpt_ref[hd:2 * hd, :].astype(BF16)
    lgc = _nt_dot(kcmp, qs) + bcmp_ref[0]
    mask_c = (t_glob - (row * NSA_CMP_STRIDE + NSA_CMP_LEN - 1) >= 0) & (row < n_cmp)
    o_cmp, psum = [], jnp.zeros((BLK, BLK), F32)
    for h in range(HEADS):
        lg = jnp.where(mask_c, lgc[:, BLK * h:BLK * (h + 1)], NEG)
        m = jnp.max(lg, axis=0, keepdims=True)
        p = jnp.where(mask_c, jnp.exp(lg - m), 0.0)
        p = p / jnp.maximum(jnp.sum(p, axis=0, keepdims=True), 1e-30)
        o_cmp.append(_dot(vcmpt, p.astype(BF16)))
        psum = psum + p

    imp = _dot(ovt_ref[...], psum, precision=hi)
    cur = t_glob >> int(math.log2(NSA_SEL_LEN))
    forced = (row == 0) | (row == cur) | (row == cur - 1)
    imp = jnp.where(forced, NSA_FORCE, imp)
    imp = jnp.where(row * NSA_SEL_LEN <= t_glob, imp, NEG)
    imp = imp[0:n_sel, :]
    jrow = row[0:n_sel, :]
    rank = jnp.zeros((n_sel, BLK), I32)
    for jp in range(n_sel):
        other = imp[jp:jp + 1, :]
        rank = rank + ((other > imp) | ((other == imp) & (jp < jrow))).astype(I32)
    chosen = jnp.where(rank < topn, 1.0, 0.0)
    chosen = jnp.concatenate([chosen, jnp.zeros((BLK - n_sel, BLK), F32)], axis=0).astype(BF16)

    s_loc = lax.broadcasted_iota(I32, (KEYS, BLK), 0)
    t_keys = t0 + lax.broadcasted_iota(I32, (KEYS, BLK), 1)

    def attend(vt_ref, lanes, mask_fn):
        def body(kp, carry):
            ms, dens, accs = carry
            r0 = pl.multiple_of(kp * KEYS, KEYS)
            kblk = z_ref[0, pl.ds(r0, KEYS), lanes].astype(BF16)
            lg = _nt_dot(kblk, qs) + _pair_bias(bsel_ref, i - 2 * kp)
            mask = mask_fn(r0)
            vt = vt_ref[kp]
            out = [_online_step(lg[:, BLK * h:BLK * (h + 1)], mask, vt, ms[h], dens[h], accs[h])
                   for h in range(HEADS)]
            return tuple(o[0] for o in out), tuple(o[1] for o in out), tuple(o[2] for o in out)
        return body

    def mask_sel(r0):
        picked = _dot(exp_ref[pl.ds(r0, KEYS), :], chosen) > 0.5
        return picked & (r0 + s_loc <= t_keys)

    def mask_win(r0):
        dist = t_keys - (r0 + s_loc)
        return (dist >= 0) & (dist < NSA_WINDOW)

    last_pair = i // 2 + 1
    _, den_s, acc_s = lax.fori_loop(0, last_pair, attend(vst_ref, slice(384, 448), mask_sel), _online_init())
    first_w = jnp.maximum(i - NSA_WINDOW // BLK, 0) // 2
    _, den_w, acc_w = lax.fori_loop(first_w, last_pair, attend(vwt_ref, slice(512, 576), mask_win), _online_init())

    outs = []
    for h in range(HEADS):
        o_s = acc_s[h] / jnp.maximum(den_s[h], 1e-30)
        o_w = acc_w[h] / jnp.maximum(den_w[h], 1e-30)
        outs.append(gates[3 * h:3 * h + 1, :] * o_cmp[h] + gates[3 * h + 1:3 * h + 2, :] * o_s
                    + gates[3 * h + 2:3 * h + 3, :] * o_w)
    o_ref[0] = _unstack_heads_t(outs)


def _nsa(zd, w1, pos, w2, bsel, bcmp, ovt, expand):
    b, s, wd = zd.shape
    nq = s // BLK
    full = lambda a: pl.BlockSpec(a.shape, lambda bi, i: (0,) * a.ndim)
    return pl.pallas_call(
        _nsa_kernel,
        grid=(b, nq),
        in_specs=[pl.BlockSpec((1, s, wd), lambda bi, i: (bi, 0, 0)),
                  pl.BlockSpec((1, s, LANES), lambda bi, i: (bi, 0, GROUP_WIDTH // LANES)),
                  full(w1), full(pos), full(w2), full(bsel),
                  pl.BlockSpec((1, BLK, HEADS * BLK), lambda bi, i: (i, 0, 0)),
                  full(ovt), full(expand)],
        out_specs=pl.BlockSpec((1, BLK, GROUP_WIDTH), lambda bi, i: (bi, i, 0)),
        out_shape=jax.ShapeDtypeStruct((b, s, GROUP_WIDTH), F32),
        scratch_shapes=[pltpu.VMEM((s // KEYS, HEAD_DIM, KEYS), BF16), pltpu.VMEM((s // KEYS, HEAD_DIM, KEYS), BF16),
                        pltpu.VMEM((BLK, BLK), F32), pltpu.VMEM((BLK, BLK), F32)],
        compiler_params=pltpu.CompilerParams(dimension_semantics=("arbitrary", "arbitrary"),
                                             vmem_limit_bytes=VMEM_LIMIT_BYTES),
        name="nsa",
    )(zd, zd, w1, pos, w2, bsel, bcmp, ovt, expand)


def _outproj_kernel(alpha, x_ref, a_ref, b_ref, c_ref, d_ref, w_ref, bo_ref, g_ref, beta_ref, o_ref):
    acc = bo_ref[...] + _dot(a_ref[...].astype(BF16), w_ref[0:GROUP_WIDTH, :])
    for n, m_ref in enumerate((b_ref, c_ref, d_ref), start=1):
        acc = acc + _dot(m_ref[...].astype(BF16), w_ref[n * GROUP_WIDTH:(n + 1) * GROUP_WIDTH, :])
    o_ref[...] = _layer_norm_rows(alpha * x_ref[...] + acc, g_ref[...], beta_ref[...])


def _outproj(alpha, x2d, mixed, w, bo, g, beta, tm=512):
    t = x2d.shape[0]
    row_spec = lambda width: pl.BlockSpec((tm, width), lambda i: (i, 0))
    const = lambda a: pl.BlockSpec(a.shape, lambda i: (0, 0))
    return pl.pallas_call(
        functools.partial(_outproj_kernel, alpha),
        grid=(t // tm,),
        in_specs=[row_spec(D_MODEL)] + [row_spec(GROUP_WIDTH)] * N_MIXERS
                 + [const(w), const(bo), const(g), const(beta)],
        out_specs=row_spec(D_MODEL),
        out_shape=jax.ShapeDtypeStruct((t, D_MODEL), F32),
        compiler_params=pltpu.CompilerParams(dimension_semantics=("arbitrary",),
                                             vmem_limit_bytes=VMEM_LIMIT_BYTES),
        name="outproj_ln",
    )(x2d, *mixed, w, bo, g, beta)


def _ffn_kernel(alpha, x_ref, w1_ref, b1_ref, w2_ref, b2_ref, g_ref, beta_ref, o_ref, xb_ref, acc_ref):
    j = pl.program_id(1)

    @pl.when(j == 0)
    def _():
        xb_ref[...] = x_ref[...].astype(BF16)
        acc_ref[...] = jnp.zeros_like(acc_ref)

    hdn = jnp.maximum(_dot(xb_ref[...], w1_ref[...]) + b1_ref[...], 0.0)
    acc_ref[...] += _dot(jnp.square(hdn).astype(BF16), w2_ref[...])

    @pl.when(j == pl.num_programs(1) - 1)
    def _():
        y = alpha * x_ref[...] + (acc_ref[...] + b2_ref[...])
        o_ref[...] = _layer_norm_rows(y, g_ref[...], beta_ref[...])


def _ffn(alpha, x2d, w1, b1, w2, b2, g, beta, tm=1024, tf=1024):
    t = x2d.shape[0]
    return pl.pallas_call(
        functools.partial(_ffn_kernel, alpha),
        grid=(t // tm, D_FF // tf),
        in_specs=[pl.BlockSpec((tm, D_MODEL), lambda i, j: (i, 0)),
                  pl.BlockSpec((D_MODEL, tf), lambda i, j: (0, j)),
                  pl.BlockSpec((1, tf), lambda i, j: (0, j)),
                  pl.BlockSpec((tf, D_MODEL), lambda i, j: (j, 0)),
                  pl.BlockSpec((1, D_MODEL), lambda i, j: (0, 0)),
                  pl.BlockSpec((1, D_MODEL), lambda i, j: (0, 0)),
                  pl.BlockSpec((1, D_MODEL), lambda i, j: (0, 0))],
        out_specs=pl.BlockSpec((tm, D_MODEL), lambda i, j: (i, 0)),
        out_shape=jax.ShapeDtypeStruct((t, D_MODEL), F32),
        scratch_shapes=[pltpu.VMEM((tm, D_MODEL), BF16), pltpu.VMEM((tm, D_MODEL), F32)],
        compiler_params=pltpu.CompilerParams(dimension_semantics=("arbitrary", "arbitrary"),
                                             vmem_limit_bytes=VMEM_LIMIT_BYTES),
        name="ffn_ln",
    )(x2d, w1, b1, w2, b2, g, beta)


def _bias_of_distance(rel_bias_heads, dist):
    onehot = np.eye(NUM_BUCKETS, dtype=np.float32)[_t5_bucket_np(np.asarray(dist))]
    return jnp.dot(rel_bias_heads.T, jnp.asarray(onehot.T), precision=lax.Precision.HIGHEST)


def _shifted_rows(v, n_rows, n_cols, step):
    period = v.shape[-1]
    assert n_cols <= period - step
    flat = jnp.tile(v, (1,) * (v.ndim - 1) + (n_rows,))[..., :n_rows * (period - step)]
    return flat.reshape(v.shape[:-1] + (n_rows, period - step))[..., :n_cols]


def _wrapped(period):
    idx = np.arange(period)
    return np.where(idx < period // 2, idx, idx - period)


def _dilated_bias(rel_bias):
    x = _wrapped(4 * BLK)
    tabs = [_shifted_rows(_bias_of_distance(rel_bias[:, 0:HEADS], (BLK - x) * dil), BLK, 2 * BLK, 1)
            for _, dil in DIL_PATTERNS]
    tabs = jnp.stack(tabs).reshape(len(DIL_PATTERNS), HEADS // 2, 2, BLK, 2 * BLK)
    return jnp.transpose(tabs, (0, 1, 3, 2, 4)).reshape(len(DIL_PATTERNS), HEADS // 2, BLK, 4 * BLK)


def _toeplitz_bias_t(rel_bias_heads):
    assert (_t5_bucket_np(np.arange(BLK + 1, 64 * BLK)) == NUM_BUCKETS - 1).all()
    x = _wrapped(2 * BLK)
    tabs = [_shifted_rows(_bias_of_distance(rel_bias_heads, BLK * delta + x), BLK, BLK, 1)
            for delta in range(3)]
    return jnp.transpose(jnp.stack(tabs), (0, 2, 1, 3)).reshape(3, BLK, HEADS * BLK)


def _compressed_bias_t(rel_bias_heads, seq):
    nq = seq // BLK
    x = _wrapped(2 * seq + BLK)
    v = _bias_of_distance(rel_bias_heads, x - (NSA_CMP_LEN - 1))
    tab = _shifted_rows(v, BLK, seq, NSA_CMP_STRIDE)
    tab = tab.reshape(HEADS, BLK, nq, BLK)
    return jnp.transpose(tab, (2, 1, 0, 3)).reshape(nq, BLK, HEADS * BLK)


def _nsa_constants(seq):
    n_cmp = (seq - NSA_CMP_LEN) // NSA_CMP_STRIDE + 1
    n_sel = seq // NSA_SEL_LEN
    cs = np.arange(n_cmp)[:, None] * NSA_CMP_STRIDE
    ss = np.arange(n_sel)[None, :] * NSA_SEL_LEN
    ov = np.clip(np.minimum(cs + NSA_CMP_LEN, ss + NSA_SEL_LEN) - np.maximum(cs, ss), 0, None) / NSA_CMP_LEN
    ovt = np.zeros((BLK, BLK), np.float32)
    ovt[:n_sel, :n_cmp] = ov.T
    expand = np.zeros((seq, BLK), np.float32)
    expand[np.arange(seq), np.arange(seq) // NSA_SEL_LEN] = 1.0
    return jnp.asarray(ovt), jnp.asarray(expand, BF16)


def _nsa_weights(cmp_pos, cmp_w1, cmp_w2):
    hd, hid = HEAD_DIM, NSA_CMP_HIDDEN
    w1 = cmp_w1.reshape(2, NSA_CMP_LEN, hd, hid)
    zeros = jnp.zeros((NSA_CMP_LEN, hd, hid), F32)
    w1 = jnp.concatenate([jnp.concatenate([w1[0], zeros], axis=-1),
                          jnp.concatenate([zeros, w1[1]], axis=-1)], axis=1)
    pos = jnp.concatenate([cmp_pos[0], cmp_pos[1]], axis=-1)
    z2 = jnp.zeros((hid, hd), F32)
    w2 = jnp.concatenate([jnp.concatenate([cmp_w2[0], z2], axis=-1),
                          jnp.concatenate([z2, cmp_w2[1]], axis=-1)], axis=0)
    return w1.astype(BF16), pos, w2.astype(BF16)


def kernel(x, w_in, b_in, a_conv, a_norm, d_cmp_pos, d_cmp_w1, d_cmp_w2, w_out, b_out, ln1_g, ln1_b,
           w_ff1, b_ff1, w_ff2, b_ff2, ln2_g, ln2_b, rel_bias):
    bsz, seq, _ = x.shape
    depth = w_in.shape[0]
    alpha = (2 * depth) ** 0.25
    nc = seq // M_CHUNK
    bias_dil = _dilated_bias(rel_bias)
    btab_dsa = _toeplitz_bias_t(rel_bias[:, HEADS:2 * HEADS])
    btab_nsa = _toeplitz_bias_t(rel_bias[:, 2 * HEADS:3 * HEADS])
    bcmp_nsa = _compressed_bias_t(rel_bias[:, 2 * HEADS:3 * HEADS], seq)
    ovt, expand = _nsa_constants(seq)

    h = x.reshape(bsz * seq, D_MODEL)
    for l in range(depth):
        w_l = _permute_columns(w_in[l]).astype(BF16)
        b_l = _permute_columns(b_in[l])[None, :]
        za, zb, zc, zd, zg = (z.reshape(bsz, seq, -1) for z in _inproj(h, w_l, b_l))
        out_a = _mlstm(za, _mlstm_gate_rows(zg), a_conv[l], a_norm[l][None, :])
        out_b = _dilated(zb, bias_dil)
        out_c = _dsa(zc, btab_dsa)
        nsa_w1, nsa_pos, nsa_w2 = _nsa_weights(d_cmp_pos[l], d_cmp_w1[l], d_cmp_w2[l])
        out_d = _nsa(zd, nsa_w1, nsa_pos, nsa_w2, btab_nsa, bcmp_nsa, ovt, expand)
        mixed = [o.reshape(bsz * seq, GROUP_WIDTH) for o in (out_a, out_b, out_c, out_d)]
        h = _outproj(alpha, h, mixed, w_out[l].astype(BF16), b_out[l][None, :],
                     ln1_g[l][None, :], ln1_b[l][None, :])
        h = _ffn(alpha, h, w_ff1[l].astype(BF16), b_ff1[l][None, :], w_ff2[l].astype(BF16),
                 b_ff2[l][None, :], ln2_g[l][None, :], ln2_b[l][None, :])
    return h.reshape(bsz, seq, D_MODEL)
```

```python
import functools
import math

import numpy as np
import jax
import jax.numpy as jnp
from jax import lax
from jax.experimental import pallas as pl
from jax.experimental.pallas import tpu as pltpu

F32 = jnp.float32
BF16 = jnp.bfloat16
I32 = jnp.int32
I16 = jnp.int16
I16_MIN = -2 ** 15

D_MODEL = 1024
N_MIXERS = 4
HEADS = 4
HEAD_DIM = D_MODEL // (N_MIXERS * HEADS)
GROUP_WIDTH = HEADS * HEAD_DIM
D_FF = 4 * D_MODEL
LN_EPS = 1e-5
NEG = -1e30

M_QK_DIM = HEAD_DIM // 2
M_CHUNK = 64
M_CONV = 4
DIL_PATTERNS = ((128, 1), (512, 4), (2048, 16))
IDX_HEADS = 4
IDX_DIM = 64
DSA_TOPK = 256
NSA_CMP_LEN = 32
NSA_CMP_STRIDE = 16
NSA_SEL_LEN = 64
NSA_TOPN = 16
NSA_WINDOW = 512
NSA_CMP_HIDDEN = 256
NSA_FORCE = 1e9
NUM_BUCKETS = 32
MAX_DISTANCE = 128

LANES = 128
BLK = 128
KEYS = 2 * BLK
QRY = KEYS
VMEM_LIMIT_BYTES = 56 * 1024 * 1024

IN_SPLITS = (
    ('a_q', HEADS * M_QK_DIM), ('a_k', HEADS * M_QK_DIM), ('a_v', GROUP_WIDTH),
    ('a_i', HEADS), ('a_f', HEADS), ('a_o', GROUP_WIDTH),
    ('b_q', GROUP_WIDTH), ('b_k', GROUP_WIDTH), ('b_v', GROUP_WIDTH),
    ('c_q', GROUP_WIDTH), ('c_k', HEAD_DIM), ('c_v', HEAD_DIM),
    ('c_iq', IDX_HEADS * IDX_DIM), ('c_ik', IDX_DIM), ('c_iw', IDX_HEADS),
    ('d_q', GROUP_WIDTH), ('d_kc', HEAD_DIM), ('d_vc', HEAD_DIM),
    ('d_ks', HEAD_DIM), ('d_vs', HEAD_DIM), ('d_kw', HEAD_DIM), ('d_vw', HEAD_DIM),
    ('d_g', 3 * HEADS),
)

GROUP_LAYOUT = (
    (('a_q', 'a_k'), ('a_v',), ('a_o',), (('a_i', HEAD_DIM),), (('a_i', M_QK_DIM),),
     (('a_f', HEAD_DIM),), (('a_f', M_QK_DIM),)),
    (('b_q',), ('b_k',), ('b_v',)),
    (('c_q',), ('c_iq',), ('c_k', 'c_v'), ('c_ik', 'c_iw')),
    (('d_q',), ('d_kc', 'd_vc'), ('d_ks', 'd_vs'), ('d_kw', 'd_vw'), ('d_g',)),
    (('a_i', 'a_f'),),
)


def _round_up(n, m):
    return -(-n // m) * m


def _projection_layout():
    offs, off = {}, 0
    for name, width in IN_SPLITS:
        offs[name] = (off, width)
        off += width
    runs, group_widths = [], []
    for group in GROUP_LAYOUT:
        gwidth = 0
        for chunk in group:
            cwidth = 0
            for entry in chunk:
                name, rep = entry if isinstance(entry, tuple) else (entry, 1)
                o, w = offs[name]
                runs.append((o, w, rep))
                cwidth += w * rep
            pad = _round_up(cwidth, LANES) - cwidth
            if pad:
                runs.append((-1, pad, 1))
            gwidth += cwidth + pad
        group_widths.append(gwidth)
    return tuple(runs), tuple(group_widths)


PROJ_RUNS, GROUP_WIDTHS = _projection_layout()
PROJ_WIDTH = int(sum(GROUP_WIDTHS))
PROJ_PERM = np.concatenate([np.repeat(np.arange(o, o + w), r) if o >= 0 else np.full(w, -1)
                            for o, w, r in PROJ_RUNS]).astype(np.int32)


def _permute_columns(a):
    parts = []
    for o, w, r in PROJ_RUNS:
        if o < 0:
            parts.append(jnp.zeros(a.shape[:-1] + (w,), a.dtype))
        else:
            parts.append(a[..., o:o + w] if r == 1 else jnp.repeat(a[..., o:o + w], r, axis=-1))
    return jnp.concatenate(parts, axis=-1)


def _t5_bucket_np(dist):
    n = np.maximum(dist, 0)
    max_exact = NUM_BUCKETS // 2
    nf = np.maximum(n, max_exact).astype(np.float32)
    large = max_exact + (np.log(nf / max_exact) / math.log(MAX_DISTANCE / max_exact)
                         * (NUM_BUCKETS - max_exact)).astype(np.int32)
    large = np.minimum(large, NUM_BUCKETS - 1)
    return np.where(n < max_exact, n, large).astype(np.int32)


def _nt_dot(a, b, precision=None):
    return lax.dot_general(a, b, (((1,), (1,)), ((), ())), precision=precision,
                           preferred_element_type=F32)


def _dot(a, b, precision=None):
    return jnp.dot(a, b, precision=precision, preferred_element_type=F32)


def _layer_norm_rows(y, g, b):
    mu = jnp.mean(y, axis=-1, keepdims=True)
    var = jnp.mean(jnp.square(y - mu), axis=-1, keepdims=True)
    return (y - mu) * lax.rsqrt(var + LN_EPS) * g + b


def _sigmoid(x):
    return 1.0 / (1.0 + jnp.exp(-x))


def _log_sigmoid(x):
    return -(jnp.maximum(-x, 0.0) + jnp.log1p(jnp.exp(-jnp.abs(x))))


def _sortable_key(x):
    bits = pltpu.bitcast(x, I32)
    return bits ^ ((bits >> 31) & jnp.int32(0x7FFFFFFF))


def _inproj_kernel(x_ref, w_ref, b_ref, *out_refs):
    xb = x_ref[...].astype(BF16)
    off = 0
    for o_ref, width in zip(out_refs, GROUP_WIDTHS):
        o_ref[...] = _dot(xb, w_ref[:, off:off + width]) + b_ref[:, off:off + width]
        off += width


def _inproj(x2d, w, b, tm=512):
    t = x2d.shape[0]
    return pl.pallas_call(
        _inproj_kernel,
        grid=(t // tm,),
        in_specs=[pl.BlockSpec((tm, D_MODEL), lambda i: (i, 0)),
                  pl.BlockSpec((D_MODEL, PROJ_WIDTH), lambda i: (0, 0)),
                  pl.BlockSpec((1, PROJ_WIDTH), lambda i: (0, 0))],
        out_specs=[pl.BlockSpec((tm, gw), lambda i: (i, 0)) for gw in GROUP_WIDTHS],
        out_shape=[jax.ShapeDtypeStruct((t, gw), F32) for gw in GROUP_WIDTHS],
        compiler_params=pltpu.CompilerParams(dimension_semantics=("arbitrary",),
                                             vmem_limit_bytes=VMEM_LIMIT_BYTES),
        name="inproj",
    )(x2d, w, b)


def _split_terms(x, n):
    terms, rest = [], x
    for _ in range(n):
        terms.append(rest.astype(BF16))
        rest = rest - terms[-1].astype(F32)
    return terms


def _iota(shape, dim):
    return lax.broadcasted_iota(I32, shape, dim)


def _mlstm_kernel(z_ref, gt_ref, cw_ref, ng_ref, o_ref, xpad_ref):
    seq = z_ref.shape[1]
    L, DK, DV, H = M_CHUNK, M_QK_DIM, HEAD_DIM, HEADS
    assert L == DV
    wq, wv = H * DK, H * DV
    lg_dk, lg_dv = int(math.log2(DK)), int(math.log2(DV))
    c_v, c_o, c_i64 = 2 * wq, 2 * wq + wv, 2 * wq + 2 * wv
    c_i32, c_f64 = c_i64 + wv, c_i64 + wv + wq
    xpad_ref[0:8, :] = jnp.zeros((8, 2 * wq), F32)
    xpad_ref[8:, :] = z_ref[0, :, 0:2 * wq]

    one_if = lambda cond: jnp.where(cond, 1.0, 0.0).astype(BF16)
    tri_l = one_if(_iota((L, L), 0) >= _iota((L, L), 1))
    trow = _iota((L, wv), 0)
    tri_heads = trow >= (_iota((L, wv), 1) & (L - 1))
    r_vv, c_vv = _iota((wv, wv), 0), _iota((wv, wv), 1)
    same_head = (r_vv >> lg_dv) == (c_vv >> lg_dv)
    ones_bd = one_if(same_head)
    mean_bd = jnp.where(same_head, 1.0 / DV, 0.0).astype(BF16)
    tri_u_bd = one_if(same_head & ((r_vv & (L - 1)) <= (c_vv & (L - 1))))
    state_mask = (_iota((wq, wv), 0) >> lg_dk) == (_iota((wq, wv), 1) >> lg_dv)
    eye_q = one_if(_iota((wq, wq), 0) == _iota((wq, wq), 1))
    head_of_qlane = _iota((L, wq), 1) >> lg_dk
    head_of_vlane = _iota((L, wv), 1) >> lg_dv
    row8 = _iota((8, wv), 0)
    cw = cw_ref[...]
    ng = ng_ref[...]

    def head_mean(x):
        hi_lo = _split_terms(x, 2)
        r = _dot(jnp.concatenate(hi_lo, axis=0), mean_bd)
        return r[0:L] + r[L:2 * L]

    def chunk(c, carry):
        cbd, nbd, m64, m32 = carry
        s0 = pl.multiple_of(c * L, L)
        rows = pl.ds(s0, L)
        xw = xpad_ref[pl.ds(s0, L + 8), :]
        y = sum(cw[j:j + 1, :] * xw[5 + j:5 + j + L, :] for j in range(M_CONV))
        qk = y * _sigmoid(y)
        q = qk[:, 0:wq]
        k = qk[:, wq:] * (DK ** -0.5)
        qb = q.astype(BF16)
        v = z_ref[0, rows, c_v:c_v + wv]
        i64 = z_ref[0, rows, c_i64:c_i64 + wv]
        i32 = z_ref[0, rows, c_i32:c_i32 + wq]
        gr = gt_ref[0, c]

        flog = _log_sigmoid(z_ref[0, rows, c_f64:c_f64 + wv + wq])
        bsum = _dot(tri_l, jnp.concatenate(_split_terms(flog, 3), axis=1))
        w3 = wv + wq
        ball = bsum[:, 0:w3] + bsum[:, w3:2 * w3] + bsum[:, 2 * w3:3 * w3]
        b64, b32 = ball[:, 0:wv], ball[:, wv:w3]
        fterms = [t.astype(F32) for t in _split_terms(_log_sigmoid(gr[1:2, :]), 3)]
        frows = jnp.where(row8 == 0, fterms[0], jnp.where(row8 == 1, fterms[1], jnp.where(row8 == 2, fterms[2], 0.0)))
        bparts = _dot(frows.astype(BF16), tri_u_bd)
        brow = bparts[0:1, :] + bparts[1:2, :] + bparts[2:3, :]

        dall = jnp.where(tri_heads, b64 - brow + gr[0:1, :], NEG)
        cm = i64 - b64
        for sh in (1, 2, 4, 8, 16, 32):
            cm = jnp.where(trow >= sh, jnp.maximum(cm, pltpu.roll(cm, sh, 0)), cm)
        inter = b64 + m64
        m_t = jnp.maximum(inter, b64 + cm)
        kbd = jnp.concatenate([jnp.where(head_of_qlane == h, k, 0.0) for h in range(H)], axis=0).astype(BF16)
        sc = _nt_dot(qb, kbd) * jnp.exp(dall - m_t)
        wi = jnp.exp(inter - m_t)
        vb = v.astype(BF16)
        vbd = jnp.concatenate([jnp.where(head_of_vlane == h, v, 0.0) for h in range(H)], axis=0).astype(BF16)
        pv = _dot(sc.astype(BF16), jnp.concatenate([vbd, ones_bd], axis=1))
        qst = _dot(qb, jnp.concatenate([cbd, nbd], axis=1).astype(BF16))
        num = pv[:, 0:wv] + wi * qst[:, 0:wv]
        den = pv[:, wv:] + wi * qst[:, wv:]
        hh = num / jnp.maximum(jnp.abs(den), jnp.exp(-m_t))

        og = _sigmoid(z_ref[0, rows, c_o:c_o + wv]) * hh
        dev = og - head_mean(og)
        o_ref[0, rows, :] = dev * lax.rsqrt(head_mean(dev * dev) + LN_EPS) * ng

        bl64, bl32 = b64[L - 1:L, :], b32[L - 1:L, :]
        m64_new = jnp.maximum(bl64 + m64, jnp.max(bl64 - b64 + i64, axis=0, keepdims=True))
        g32 = bl32 - b32 + i32
        m32_new = jnp.maximum(bl32 + m32, jnp.max(g32, axis=0, keepdims=True))
        kw = k * jnp.exp(g32 - m32_new)
        wc = jnp.exp(bl64 + m64 - m64_new)
        kwt = _nt_dot(eye_q, kw.astype(BF16)).astype(BF16)
        upd = _dot(kwt, jnp.concatenate([vb, jnp.ones((L, wv), BF16)], axis=1))
        cbd = wc * cbd + jnp.where(state_mask, upd[:, 0:wv], 0.0)
        nbd = wc * nbd + jnp.where(state_mask, upd[:, wv:], 0.0)
        return cbd, nbd, m64_new, m32_new

    init = (jnp.zeros((wq, wv), F32), jnp.zeros((wq, wv), F32), jnp.zeros((1, wv), F32), jnp.zeros((1, wq), F32))
    lax.fori_loop(0, seq // L, chunk, init, unroll=2)


def _mlstm(za, gates_t, conv_w, norm_g):
    b, s, wa = za.shape
    nc = s // M_CHUNK
    return pl.pallas_call(
        _mlstm_kernel,
        grid=(b,),
        in_specs=[pl.BlockSpec((1, s, wa), lambda i: (i, 0, 0)),
                  pl.BlockSpec((1, nc) + gates_t.shape[2:], lambda i: (i, 0, 0, 0)),
                  pl.BlockSpec((M_CONV, 2 * HEADS * M_QK_DIM), lambda i: (0, 0)),
                  pl.BlockSpec((1, GROUP_WIDTH), lambda i: (0, 0))],
        out_specs=pl.BlockSpec((1, s, GROUP_WIDTH), lambda i: (i, 0, 0)),
        out_shape=jax.ShapeDtypeStruct((b, s, GROUP_WIDTH), F32),
        scratch_shapes=[pltpu.VMEM((s + 8, 2 * HEADS * M_QK_DIM), F32)],
        compiler_params=pltpu.CompilerParams(dimension_semantics=("arbitrary",),
                                             vmem_limit_bytes=VMEM_LIMIT_BYTES),
        name="mlstm",
    )(za, gates_t, conv_w, norm_g)


def _mlstm_gate_rows(zg):
    b, s, _ = zg.shape
    gates = zg[:, :, 0:2 * HEADS].reshape(b, s // M_CHUNK, M_CHUNK, 2, HEADS)
    return jnp.transpose(gates, (0, 1, 3, 4, 2)).reshape(b, s // M_CHUNK, 2, HEADS * M_CHUNK)


def _dilated_kernel(q0_ref, q1_ref, k0_ref, k1_ref, v0_ref, v1_ref, bias_ref, o_ref, acc_ref, mx_ref, den_ref):
    seq = q0_ref.shape[1]
    W = BLK
    hd = HEAD_DIM
    npair = HEADS // 2
    scale = hd ** -0.5
    assert math.log2(scale).is_integer()
    nk = 2 * W
    qi = _iota((W, 2 * nk), 0)
    ki = _iota((W, 2 * nk), 1) & (nk - 1)
    j = W + qi - ki
    band = (j >= 0) & (j <= W)
    head_of_lane = _iota((nk, 2 * hd), 1) >> int(math.log2(hd))
    ones_bd = jnp.concatenate([jnp.where(head_of_lane == hh, 1.0, 0.0) for hh in range(2)], axis=0).astype(BF16)
    q_refs, k_refs, v_refs = (q0_ref, q1_ref), (k0_ref, k1_ref), (v0_ref, v1_ref)

    def block_diag(x):
        return jnp.concatenate([jnp.where(head_of_lane == hh, x, 0.0) for hh in range(2)], axis=0).astype(BF16)

    for br, (window, dil) in enumerate(DIL_PATTERNS):
        assert window // dil == W
        nb = (seq // dil) // W

        def piece(idx, _, br=br, dil=dil, nb=nb):
            n = idx % nb
            if dil == 1:
                rows_q = pl.ds(pl.multiple_of(W * n, W), W)
                rows_p = pl.ds(pl.multiple_of(W * jnp.maximum(n - 1, 0), W), W)
            else:
                r = idx // nb
                rows_q = pl.ds(r + dil * W * n, W, stride=dil)
                rows_p = pl.ds(r + dil * W * jnp.maximum(n - 1, 0), W, stride=dil)
            mask = band & (ki >= jnp.where(n > 0, 0, W))
            for pr in range(npair):
                q = (q_refs[pr][0, rows_q, :] * scale).astype(BF16)
                k2 = jnp.concatenate([k_refs[pr][0, rows_p, :], k_refs[pr][0, rows_q, :]], axis=0)
                v2 = jnp.concatenate([v_refs[pr][0, rows_p, :], v_refs[pr][0, rows_q, :]], axis=0)
                lg = jnp.where(mask, _nt_dot(q, block_diag(k2)) + bias_ref[br, pr], NEG)
                ps, mxs = [], []
                for hh in range(2):
                    sl = slice(nk * hh, nk * (hh + 1))
                    m = jnp.max(lg[:, sl], axis=-1, keepdims=True)
                    ps.append(jnp.where(mask[:, sl], jnp.exp(lg[:, sl] - m), 0.0))
                    mxs.append(jnp.broadcast_to(m, (W, hd)))
                p = jnp.concatenate(ps, axis=-1).astype(BF16)
                pv = _dot(p, jnp.concatenate([block_diag(v2), ones_bd], axis=1))
                acc_ref[br, pr, rows_q, :] = pv[:, 0:2 * hd]
                den_ref[br, pr, rows_q, :] = pv[:, 2 * hd:]
                mx_ref[br, pr, rows_q, :] = jnp.concatenate(mxs, axis=-1)
            return 0

        lax.fori_loop(0, dil * nb, piece, 0, unroll=2)

    def combine(i, _):
        rows = pl.ds(pl.multiple_of(i * W, W), W)
        outs = []
        for pr in range(npair):
            ms = [mx_ref[b, pr, rows, :] for b in range(len(DIL_PATTERNS))]
            top = functools.reduce(jnp.maximum, ms)
            es = [jnp.exp(m - top) for m in ms]
            num = sum(e * acc_ref[b, pr, rows, :] for b, e in enumerate(es))
            den = sum(e * jnp.maximum(den_ref[b, pr, rows, :], 1e-30) for b, e in enumerate(es))
            outs.append(num / den)
        o_ref[0, rows, :] = jnp.concatenate(outs, axis=-1)
        return 0

    lax.fori_loop(0, seq // W, combine, 0)


def _dilated(zb, bias):
    b, s, wb = zb.shape
    nbr = len(DIL_PATTERNS)
    pair_spec = lambda c: pl.BlockSpec((1, s, LANES), lambda i: (i, 0, c))
    return pl.pallas_call(
        _dilated_kernel,
        grid=(b,),
        in_specs=[pair_spec(c) for c in range(wb // LANES)]
                 + [pl.BlockSpec(bias.shape, lambda i: (0, 0, 0, 0))],
        out_specs=pl.BlockSpec((1, s, GROUP_WIDTH), lambda i: (i, 0, 0)),
        out_shape=jax.ShapeDtypeStruct((b, s, GROUP_WIDTH), F32),
        scratch_shapes=[pltpu.VMEM((nbr, HEADS // 2, s, LANES), F32) for _ in range(3)],
        compiler_params=pltpu.CompilerParams(dimension_semantics=("arbitrary",),
                                             vmem_limit_bytes=VMEM_LIMIT_BYTES),
        name="dilated",
    )(*([zb] * (wb // LANES)), bias)


def _stack_heads(q):
    return jnp.concatenate([q[:, HEAD_DIM * h:HEAD_DIM * (h + 1)] for h in range(HEADS)], axis=0)


def _scaled_query_stack(q):
    scale = HEAD_DIM ** -0.5
    assert math.log2(scale).is_integer()
    return (_stack_heads(q) * scale).astype(BF16)


def _unstack_heads_t(per_head_t):
    halves = []
    for h in range(0, HEADS, 2):
        halves.append(jnp.concatenate([per_head_t[h], per_head_t[h + 1]], axis=0).T)
    return jnp.concatenate(halves, axis=-1)


def _online_step(lg, mask, vt, m, den, acc):
    lg = jnp.where(mask, lg, NEG)
    m_new = jnp.maximum(m, jnp.max(lg, axis=0, keepdims=True))
    p = jnp.where(mask, jnp.exp(lg - m_new), 0.0)
    corr = jnp.exp(m - m_new)
    den = den * corr + jnp.sum(p, axis=0, keepdims=True)
    acc = acc * corr + _dot(vt, p.astype(BF16))
    return m_new, den, acc


def _pair_bias(btab_ref, behind):
    return jnp.concatenate([btab_ref[jnp.minimum(behind, 2)], btab_ref[jnp.clip(behind - 1, 0, 2)]], axis=0)


def _online_init(nq=BLK):
    return (tuple(jnp.full((1, nq), NEG, F32) for _ in range(HEADS)),
            tuple(jnp.zeros((1, nq), F32) for _ in range(HEADS)),
            tuple(jnp.zeros((HEAD_DIM, nq), F32) for _ in range(HEADS)))


def _dsa_kernel(z_ref, btab_ref, o_ref, vt_ref, key_ref, hi_ref, lo_ref, lom_ref):
    i = pl.program_id(1)
    seq = z_ref.shape[1]
    nkb = seq // BLK
    nq = QRY
    hd = HEAD_DIM
    topk = min(DSA_TOPK, seq // 4)
    t0 = pl.multiple_of(i * nq, nq)

    assert nq == KEYS
    npair = i + 1

    @pl.when(i == 0)
    def _():
        for kb in range(nkb):
            vt = z_ref[0, kb * BLK:(kb + 1) * BLK, 512:640].T[hd:2 * hd, :]
            vt_ref[kb // 2, :, (kb % 2) * BLK:(kb % 2 + 1) * BLK] = vt.astype(BF16)
        key_ref[...] = jnp.full(key_ref.shape, -2 ** 31, I32)
        for half_ref in (hi_ref, lo_ref, lom_ref):
            half_ref[...] = jnp.full(half_ref.shape, I16_MIN, I16)

    zq = z_ref[0, pl.ds(t0, nq), :]
    cq = zq[:, 0:256]
    ciq = _stack_heads(zq[:, 256:512]).astype(BF16)
    iw = zq[:, 640:768].T[IDX_DIM:IDX_DIM + IDX_HEADS, :] * ((IDX_HEADS * IDX_DIM) ** -0.5)
    s_loc = lax.broadcasted_iota(I32, (KEYS, nq), 0)
    t_glob = t0 + lax.broadcasted_iota(I32, (KEYS, nq), 1)

    def score_block(kp, _):
        r0 = pl.multiple_of(kp * KEYS, KEYS)
        ik = z_ref[0, pl.ds(r0, KEYS), 640:704].astype(BF16)
        rel = _nt_dot(ik, ciq)
        sc = jnp.zeros((KEYS, nq), F32)
        for h in range(IDX_HEADS):
            sc = sc + jnp.maximum(rel[:, nq * h:nq * (h + 1)], 0.0) * iw[h:h + 1, :]
        sc = jnp.where(r0 + s_loc <= t_glob, sc, NEG)
        key = _sortable_key(sc)
        key_ref[kp] = key
        hi_ref[kp] = (key >> 16).astype(I16)
        lo_ref[kp] = ((key & 0xFFFF) - 2 ** 15).astype(I16)
        return 0

    lax.fori_loop(0, npair, score_block, 0)

    def count(ref, pred, pairs):
        dt = ref.dtype
        rows = 8 * 4 // dt.itemsize
        def body(kp, acc):
            hit = jnp.where(pred(kp, ref[kp]), jnp.ones((), dt), jnp.zeros((), dt))
            hit = hit.reshape(KEYS // rows, rows, nq)
            parts = [hit[n] for n in range(KEYS // rows)]
            while len(parts) > 1:
                parts = [a + b for a, b in zip(parts[0::2], parts[1::2])]
            return acc + parts[0]
        acc = jnp.zeros((rows, nq), dt)
        if isinstance(pairs, int):
            for kp in range(pairs):
                acc = body(kp, acc)
        else:
            acc = lax.fori_loop(0, pairs, body, acc)
        return jnp.sum(acc.astype(I32), axis=0, keepdims=True)

    def threshold(pairs):
        def half_search(ref, k):
            def bit(it, thr):
                cand = thr + lax.shift_left(jnp.int32(1), 15 - it)
                c = count(ref, lambda kp, half: half >= cand.astype(I16), pairs)
                return jnp.where(c >= k, cand, thr)
            return lax.fori_loop(0, 16, bit, jnp.full((1, nq), I16_MIN, I32))

        def run(_):
            thr_hi = half_search(hi_ref, topk)
            thr_hi16 = thr_hi.astype(I16)
            above_hi = count(hi_ref, lambda kp, half: half > thr_hi16, pairs)
            for kp in range(pairs):
                lom_ref[kp] = jnp.where(hi_ref[kp] == thr_hi16, lo_ref[kp], jnp.int16(I16_MIN))
            thr_lo = half_search(lom_ref, topk - above_hi)
            thr = (thr_hi << 16) | (thr_lo + 2 ** 15)
            n_gt = count(key_ref, lambda kp, key: key > thr, pairs)
            n_eq = count(key_ref, lambda kp, key: key == thr, pairs)
            return thr, n_gt, n_eq
        return run

    walks = list(range(2, seq // KEYS + 1, 2))
    thr, n_gt, n_eq = lax.switch((npair - 1) // 2, [threshold(p) for p in walks], 0)
    need = topk - n_gt

    def tie_search(_):
        def idx_bit(it, jm):
            cand = jm + lax.shift_left(jnp.int32(1), 10 - it)
            c = count(key_ref, lambda kp, key: (key == thr) & (kp * KEYS + s_loc < cand), npair)
            return jnp.where(c < need, cand, jm)
        return lax.fori_loop(0, 11, idx_bit, jnp.zeros((1, nq), I32))

    assert seq == 2 ** 11
    jmax = lax.cond(jnp.max(n_eq - need) > 0, tie_search,
                    lambda _: jnp.full((1, nq), seq - 1, I32), 0)

    qs = _scaled_query_stack(cq)

    def attend(kp, carry):
        ms, dens, accs = carry
        r0 = pl.multiple_of(kp * KEYS, KEYS)
        kblk = z_ref[0, pl.ds(r0, KEYS), 512:576].astype(BF16)
        lg = _nt_dot(kblk, qs) + btab_ref[jnp.minimum(i - kp, 2)]
        key = key_ref[kp]
        s_glob = r0 + s_loc
        mask = ((key > thr) | ((key == thr) & (s_glob <= jmax))) & (s_glob <= t_glob)
        vt = vt_ref[kp]
        out = [_online_step(lg[:, nq * h:nq * (h + 1)], mask, vt, ms[h], dens[h], accs[h])
               for h in range(HEADS)]
        return tuple(o[0] for o in out), tuple(o[1] for o in out), tuple(o[2] for o in out)

    ms, dens, accs = lax.fori_loop(0, npair, attend, _online_init(nq))
    o_ref[0] = _unstack_heads_t([accs[h] / jnp.maximum(dens[h], 1e-30) for h in range(HEADS)])


def _dsa(zc, btab):
    b, s, wc = zc.shape
    return pl.pallas_call(
        _dsa_kernel,
        grid=(b, s // QRY),
        in_specs=[pl.BlockSpec((1, s, wc), lambda bi, i: (bi, 0, 0)),
                  pl.BlockSpec(btab.shape, lambda bi, i: (0, 0, 0))],
        out_specs=pl.BlockSpec((1, QRY, GROUP_WIDTH), lambda bi, i: (bi, i, 0)),
        out_shape=jax.ShapeDtypeStruct((b, s, GROUP_WIDTH), F32),
        scratch_shapes=[pltpu.VMEM((s // KEYS, HEAD_DIM, KEYS), BF16), pltpu.VMEM((s // KEYS, KEYS, QRY), I32)]
                       + [pltpu.VMEM((s // KEYS, KEYS, QRY), I16)] * 3,
        compiler_params=pltpu.CompilerParams(dimension_semantics=("arbitrary", "arbitrary"),
                                             vmem_limit_bytes=VMEM_LIMIT_BYTES),
        name="dsa",
    )(zc, btab)


def _nsa_kernel(z_ref, zc_ref, w1_ref, pos_ref, w2_ref, bsel_ref, bcmp_ref, ovt_ref, exp_ref, o_ref,
                vst_ref, vwt_ref, cmp_ref, cmpt_ref):
    i = pl.program_id(1)
    seq = z_ref.shape[1]
    nkb = seq // BLK
    hd = HEAD_DIM
    scale = hd ** -0.5
    n_cmp = (seq - NSA_CMP_LEN) // NSA_CMP_STRIDE + 1
    n_sel = seq // NSA_SEL_LEN
    topn = min(NSA_TOPN, n_sel)
    half = NSA_CMP_LEN // 2
    assert half == NSA_CMP_STRIDE and n_cmp + 1 == seq // NSA_CMP_STRIDE == BLK and n_sel <= BLK
    nq = QRY
    assert nq == KEYS
    t0 = pl.multiple_of(i * nq, nq)
    hi = lax.Precision.HIGHEST

    @pl.when(i == 0)
    def _():
        for kb in range(nkb):
            rows = slice(kb * BLK, (kb + 1) * BLK)
            cols = slice((kb % 2) * BLK, (kb % 2 + 1) * BLK)
            vst_ref[kb // 2, :, cols] = z_ref[0, rows, 384:512].T[hd:2 * hd, :].astype(BF16)
            vwt_ref[kb // 2, :, cols] = z_ref[0, rows, 512:640].T[hd:2 * hd, :].astype(BF16)
        first = jnp.zeros((BLK, 2 * NSA_CMP_HIDDEN), F32)
        second = jnp.zeros((BLK, 2 * NSA_CMP_HIDDEN), F32)
        for j in range(half):
            xj = zc_ref[0, pl.ds(j, BLK, stride=NSA_CMP_STRIDE), :]
            first = first + _dot((xj + pos_ref[j:j + 1, :]).astype(BF16), w1_ref[j])
            second = second + _dot((xj + pos_ref[half + j:half + j + 1, :]).astype(BF16), w1_ref[half + j])
        hid = first + pltpu.roll(second, BLK - 1, 0)
        hid = hid * _sigmoid(hid)
        cmp = _dot(hid.astype(BF16), w2_ref[...])
        cmp_ref[...] = cmp
        cmpt_ref[...] = cmp.T

    zq = z_ref[0, pl.ds(t0, nq), :]
    qs = _scaled_query_stack(zq[:, 0:256])
    gates = _sigmoid(zq[:, 640:768].T[0:16, :])
    row = lax.broadcasted_iota(I32, (BLK, nq), 0)
    t_glob = t0 + lax.broadcasted_iota(I32, (BLK, nq), 1)

    kcmp = cmp_ref[:, 0:hd].astype(BF16)
    vcmpt = cmpt_ref[hd:2 * hd, :].astype(BF16)
    lgc = _nt_dot(kcmp, qs) + bcmp_ref[0]
    mask_c = (t_glob - (row * NSA_CMP_STRIDE + NSA_CMP_LEN - 1) >= 0) & (row < n_cmp)
    o_cmp, psum = [], jnp.zeros((BLK, nq), F32)
    for h in range(HEADS):
        lg = jnp.where(mask_c, lgc[:, nq * h:nq * (h + 1)], NEG)
        m = jnp.max(lg, axis=0, keepdims=True)
        p = jnp.where(mask_c, jnp.exp(lg - m), 0.0)
        p = p / jnp.maximum(jnp.sum(p, axis=0, keepdims=True), 1e-30)
        o_cmp.append(_dot(vcmpt, p.astype(BF16)))
        psum = psum + p

    imp = _dot(ovt_ref[...], psum, precision=hi)
    cur = t_glob >> int(math.log2(NSA_SEL_LEN))
    forced = (row == 0) | (row == cur) | (row == cur - 1)
    imp = jnp.where(forced, NSA_FORCE, imp)
    imp = jnp.where(row * NSA_SEL_LEN <= t_glob, imp, NEG)
    imp = imp[0:n_sel, :]
    jrow = lax.broadcasted_iota(I32, (n_sel, nq), 0)
    rank = jnp.zeros((n_sel, nq), I32)
    for jp in range(n_sel):
        other = imp[jp:jp + 1, :]
        rank = rank + ((other > imp) | ((other == imp) & (jp < jrow))).astype(I32)
    chosen = jnp.where(rank < topn, 1.0, 0.0)
    chosen = jnp.concatenate([chosen, jnp.zeros((BLK - n_sel, nq), F32)], axis=0).astype(BF16)

    s_loc = lax.broadcasted_iota(I32, (KEYS, nq), 0)
    t_keys = t0 + lax.broadcasted_iota(I32, (KEYS, nq), 1)

    def attend(vt_ref, lanes, mask_fn):
        def body(kp, carry):
            ms, dens, accs = carry
            r0 = pl.multiple_of(kp * KEYS, KEYS)
            kblk = z_ref[0, pl.ds(r0, KEYS), lanes].astype(BF16)
            lg = _nt_dot(kblk, qs) + bsel_ref[jnp.minimum(i - kp, 2)]
            mask = mask_fn(r0)
            vt = vt_ref[kp]
            out = [_online_step(lg[:, nq * h:nq * (h + 1)], mask, vt, ms[h], dens[h], accs[h])
                   for h in range(HEADS)]
            return tuple(o[0] for o in out), tuple(o[1] for o in out), tuple(o[2] for o in out)
        return body

    def mask_sel(r0):
        picked = _dot(exp_ref[pl.ds(r0, KEYS), :], chosen) > 0.5
        return picked & (r0 + s_loc <= t_keys)

    def mask_win(r0):
        dist = t_keys - (r0 + s_loc)
        return (dist >= 0) & (dist < NSA_WINDOW)

    _, den_s, acc_s = lax.fori_loop(0, i + 1, attend(vst_ref, slice(384, 448), mask_sel), _online_init(nq))
    first_w = jnp.maximum(i - NSA_WINDOW // KEYS, 0)
    _, den_w, acc_w = lax.fori_loop(first_w, i + 1, attend(vwt_ref, slice(512, 576), mask_win), _online_init(nq))

    outs = []
    for h in range(HEADS):
        o_s = acc_s[h] / jnp.maximum(den_s[h], 1e-30)
        o_w = acc_w[h] / jnp.maximum(den_w[h], 1e-30)
        outs.append(gates[3 * h:3 * h + 1, :] * o_cmp[h] + gates[3 * h + 1:3 * h + 2, :] * o_s
                    + gates[3 * h + 2:3 * h + 3, :] * o_w)
    o_ref[0] = _unstack_heads_t(outs)


def _nsa(zd, w1, pos, w2, bsel, bcmp, ovt, expand):
    b, s, wd = zd.shape
    full = lambda a: pl.BlockSpec(a.shape, lambda bi, i: (0,) * a.ndim)
    return pl.pallas_call(
        _nsa_kernel,
        grid=(b, s // QRY),
        in_specs=[pl.BlockSpec((1, s, wd), lambda bi, i: (bi, 0, 0)),
                  pl.BlockSpec((1, s, LANES), lambda bi, i: (bi, 0, GROUP_WIDTH // LANES)),
                  full(w1), full(pos), full(w2), full(bsel),
                  pl.BlockSpec((1, BLK, HEADS * QRY), lambda bi, i: (i, 0, 0)),
                  full(ovt), full(expand)],
        out_specs=pl.BlockSpec((1, QRY, GROUP_WIDTH), lambda bi, i: (bi, i, 0)),
        out_shape=jax.ShapeDtypeStruct((b, s, GROUP_WIDTH), F32),
        scratch_shapes=[pltpu.VMEM((s // KEYS, HEAD_DIM, KEYS), BF16), pltpu.VMEM((s // KEYS, HEAD_DIM, KEYS), BF16),
                        pltpu.VMEM((BLK, BLK), F32), pltpu.VMEM((BLK, BLK), F32)],
        compiler_params=pltpu.CompilerParams(dimension_semantics=("arbitrary", "arbitrary"),
                                             vmem_limit_bytes=VMEM_LIMIT_BYTES),
        name="nsa",
    )(zd, zd, w1, pos, w2, bsel, bcmp, ovt, expand)


def _outproj_kernel(alpha, x_ref, a_ref, b_ref, c_ref, d_ref, w_ref, bo_ref, g_ref, beta_ref, o_ref):
    acc = bo_ref[...] + _dot(a_ref[...].astype(BF16), w_ref[0:GROUP_WIDTH, :])
    for n, m_ref in enumerate((b_ref, c_ref, d_ref), start=1):
        acc = acc + _dot(m_ref[...].astype(BF16), w_ref[n * GROUP_WIDTH:(n + 1) * GROUP_WIDTH, :])
    o_ref[...] = _layer_norm_rows(alpha * x_ref[...] + acc, g_ref[...], beta_ref[...])


def _outproj(alpha, x2d, mixed, w, bo, g, beta, tm=512):
    t = x2d.shape[0]
    row_spec = lambda width: pl.BlockSpec((tm, width), lambda i: (i, 0))
    const = lambda a: pl.BlockSpec(a.shape, lambda i: (0, 0))
    return pl.pallas_call(
        functools.partial(_outproj_kernel, alpha),
        grid=(t // tm,),
        in_specs=[row_spec(D_MODEL)] + [row_spec(GROUP_WIDTH)] * N_MIXERS
                 + [const(w), const(bo), const(g), const(beta)],
        out_specs=row_spec(D_MODEL),
        out_shape=jax.ShapeDtypeStruct((t, D_MODEL), F32),
        compiler_params=pltpu.CompilerParams(dimension_semantics=("arbitrary",),
                                             vmem_limit_bytes=VMEM_LIMIT_BYTES),
        name="outproj_ln",
    )(x2d, *mixed, w, bo, g, beta)


def _ffn_kernel(alpha, x_ref, w1_ref, b1_ref, w2_ref, b2_ref, g_ref, beta_ref, o_ref, xb_ref, acc_ref):
    j = pl.program_id(1)

    @pl.when(j == 0)
    def _():
        xb_ref[...] = x_ref[...].astype(BF16)
        acc_ref[...] = jnp.zeros_like(acc_ref)

    hdn = jnp.maximum(_dot(xb_ref[...], w1_ref[...]) + b1_ref[...], 0.0)
    acc_ref[...] += _dot(jnp.square(hdn).astype(BF16), w2_ref[...])

    @pl.when(j == pl.num_programs(1) - 1)
    def _():
        y = alpha * x_ref[...] + (acc_ref[...] + b2_ref[...])
        o_ref[...] = _layer_norm_rows(y, g_ref[...], beta_ref[...])


def _ffn(alpha, x2d, w1, b1, w2, b2, g, beta, tm=1024, tf=1024):
    t = x2d.shape[0]
    return pl.pallas_call(
        functools.partial(_ffn_kernel, alpha),
        grid=(t // tm, D_FF // tf),
        in_specs=[pl.BlockSpec((tm, D_MODEL), lambda i, j: (i, 0)),
                  pl.BlockSpec((D_MODEL, tf), lambda i, j: (0, j)),
                  pl.BlockSpec((1, tf), lambda i, j: (0, j)),
                  pl.BlockSpec((tf, D_MODEL), lambda i, j: (j, 0)),
                  pl.BlockSpec((1, D_MODEL), lambda i, j: (0, 0)),
                  pl.BlockSpec((1, D_MODEL), lambda i, j: (0, 0)),
                  pl.BlockSpec((1, D_MODEL), lambda i, j: (0, 0))],
        out_specs=pl.BlockSpec((tm, D_MODEL), lambda i, j: (i, 0)),
        out_shape=jax.ShapeDtypeStruct((t, D_MODEL), F32),
        scratch_shapes=[pltpu.VMEM((tm, D_MODEL), BF16), pltpu.VMEM((tm, D_MODEL), F32)],
        compiler_params=pltpu.CompilerParams(dimension_semantics=("arbitrary", "arbitrary"),
                                             vmem_limit_bytes=VMEM_LIMIT_BYTES),
        name="ffn_ln",
    )(x2d, w1, b1, w2, b2, g, beta)


def _bias_of_distance(rel_bias_heads, dist):
    onehot = np.eye(NUM_BUCKETS, dtype=np.float32)[_t5_bucket_np(np.asarray(dist))]
    return jnp.dot(rel_bias_heads.T, jnp.asarray(onehot.T), precision=lax.Precision.HIGHEST)


def _shifted_rows(v, n_rows, n_cols, step):
    period = v.shape[-1]
    assert n_cols <= period - step
    flat = jnp.tile(v, (1,) * (v.ndim - 1) + (n_rows,))[..., :n_rows * (period - step)]
    return flat.reshape(v.shape[:-1] + (n_rows, period - step))[..., :n_cols]


def _wrapped(period):
    idx = np.arange(period)
    return np.where(idx < period // 2, idx, idx - period)


def _dilated_bias(rel_bias):
    x = _wrapped(4 * BLK)
    tabs = [_shifted_rows(_bias_of_distance(rel_bias[:, 0:HEADS], (BLK - x) * dil), BLK, 2 * BLK, 1)
            for _, dil in DIL_PATTERNS]
    tabs = jnp.stack(tabs).reshape(len(DIL_PATTERNS), HEADS // 2, 2, BLK, 2 * BLK)
    return jnp.transpose(tabs, (0, 1, 3, 2, 4)).reshape(len(DIL_PATTERNS), HEADS // 2, BLK, 4 * BLK)


def _toeplitz_bias_t(rel_bias_heads, blk=BLK):
    assert (_t5_bucket_np(np.arange(blk + 1, 64 * blk)) == NUM_BUCKETS - 1).all()
    x = _wrapped(2 * blk)
    tabs = [_shifted_rows(_bias_of_distance(rel_bias_heads, blk * delta + x), blk, blk, 1)
            for delta in range(3)]
    return jnp.transpose(jnp.stack(tabs), (0, 2, 1, 3)).reshape(3, blk, HEADS * blk)


def _compressed_bias_t(rel_bias_heads, seq):
    nq = seq // QRY
    x = _wrapped(2 * seq + BLK)
    v = _bias_of_distance(rel_bias_heads, x - (NSA_CMP_LEN - 1))
    tab = _shifted_rows(v, BLK, seq, NSA_CMP_STRIDE)
    tab = tab.reshape(HEADS, BLK, nq, QRY)
    return jnp.transpose(tab, (2, 1, 0, 3)).reshape(nq, BLK, HEADS * QRY)


def _nsa_constants(seq):
    n_cmp = (seq - NSA_CMP_LEN) // NSA_CMP_STRIDE + 1
    n_sel = seq // NSA_SEL_LEN
    cs = np.arange(n_cmp)[:, None] * NSA_CMP_STRIDE
    ss = np.arange(n_sel)[None, :] * NSA_SEL_LEN
    ov = np.clip(np.minimum(cs + NSA_CMP_LEN, ss + NSA_SEL_LEN) - np.maximum(cs, ss), 0, None) / NSA_CMP_LEN
    ovt = np.zeros((BLK, BLK), np.float32)
    ovt[:n_sel, :n_cmp] = ov.T
    expand = np.zeros((seq, BLK), np.float32)
    expand[np.arange(seq), np.arange(seq) // NSA_SEL_LEN] = 1.0
    return jnp.asarray(ovt), jnp.asarray(expand, BF16)


def _nsa_weights(cmp_pos, cmp_w1, cmp_w2):
    hd, hid = HEAD_DIM, NSA_CMP_HIDDEN
    w1 = cmp_w1.reshape(2, NSA_CMP_LEN, hd, hid)
    zeros = jnp.zeros((NSA_CMP_LEN, hd, hid), F32)
    w1 = jnp.concatenate([jnp.concatenate([w1[0], zeros], axis=-1),
                          jnp.concatenate([zeros, w1[1]], axis=-1)], axis=1)
    pos = jnp.concatenate([cmp_pos[0], cmp_pos[1]], axis=-1)
    z2 = jnp.zeros((hid, hd), F32)
    w2 = jnp.concatenate([jnp.concatenate([cmp_w2[0], z2], axis=-1),
                          jnp.concatenate([z2, cmp_w2[1]], axis=-1)], axis=0)
    return w1.astype(BF16), pos, w2.astype(BF16)


def kernel(x, w_in, b_in, a_conv, a_norm, d_cmp_pos, d_cmp_w1, d_cmp_w2, w_out, b_out, ln1_g, ln1_b,
           w_ff1, b_ff1, w_ff2, b_ff2, ln2_g, ln2_b, rel_bias):
    bsz, seq, _ = x.shape
    depth = w_in.shape[0]
    alpha = (2 * depth) ** 0.25
    nc = seq // M_CHUNK
    bias_dil = _dilated_bias(rel_bias)
    btab_dsa = _toeplitz_bias_t(rel_bias[:, HEADS:2 * HEADS], QRY)
    btab_nsa = _toeplitz_bias_t(rel_bias[:, 2 * HEADS:3 * HEADS], QRY)
    bcmp_nsa = _compressed_bias_t(rel_bias[:, 2 * HEADS:3 * HEADS], seq)
    ovt, expand = _nsa_constants(seq)

    h = x.reshape(bsz * seq, D_MODEL)
    for l in range(depth):
        w_l = _permute_columns(w_in[l]).astype(BF16)
        b_l = _permute_columns(b_in[l])[None, :]
        za, zb, zc, zd, zg = (z.reshape(bsz, seq, -1) for z in _inproj(h, w_l, b_l))
        out_a = _mlstm(za, _mlstm_gate_rows(zg), a_conv[l], a_norm[l][None, :])
        out_b = _dilated(zb, bias_dil)
        out_c = _dsa(zc, btab_dsa)
        nsa_w1, nsa_pos, nsa_w2 = _nsa_weights(d_cmp_pos[l], d_cmp_w1[l], d_cmp_w2[l])
        out_d = _nsa(zd, nsa_w1, nsa_pos, nsa_w2, btab_nsa, bcmp_nsa, ovt, expand)
        mixed = [o.reshape(bsz * seq, GROUP_WIDTH) for o in (out_a, out_b, out_c, out_d)]
        h = _outproj(alpha, h, mixed, w_out[l].astype(BF16), b_out[l][None, :],
                     ln1_g[l][None, :], ln1_b[l][None, :])
        h = _ffn(alpha, h, w_ff1[l].astype(BF16), b_ff1[l][None, :], w_ff2[l].astype(BF16),
                 b_ff2[l][None, :], ln2_g[l][None, :], ln2_b[l][None, :])
    return h.reshape(bsz, seq, D_MODEL)
```

```python
import functools
import math

import numpy as np
import jax
import jax.numpy as jnp
from jax import lax
from jax.experimental import pallas as pl
from jax.experimental.pallas import tpu as pltpu

F32 = jnp.float32
BF16 = jnp.bfloat16
I32 = jnp.int32
I16 = jnp.int16
I16_MIN = -2 ** 15

D_MODEL = 1024
N_MIXERS = 4
HEADS = 4
HEAD_DIM = D_MODEL // (N_MIXERS * HEADS)
GROUP_WIDTH = HEADS * HEAD_DIM
D_FF = 4 * D_MODEL
LN_EPS = 1e-5
NEG = -1e30

M_QK_DIM = HEAD_DIM // 2
M_CHUNK = 64
M_CONV = 4
DIL_PATTERNS = ((128, 1), (512, 4), (2048, 16))
IDX_HEADS = 4
IDX_DIM = 64
DSA_TOPK = 256
NSA_CMP_LEN = 32
NSA_CMP_STRIDE = 16
NSA_SEL_LEN = 64
NSA_TOPN = 16
NSA_WINDOW = 512
NSA_CMP_HIDDEN = 256
NSA_FORCE = 1e9
NUM_BUCKETS = 32
MAX_DISTANCE = 128

LANES = 128
BLK = 128
KEYS = 2 * BLK
QRY = KEYS
VMEM_LIMIT_BYTES = 56 * 1024 * 1024

IN_SPLITS = (
    ('a_q', HEADS * M_QK_DIM), ('a_k', HEADS * M_QK_DIM), ('a_v', GROUP_WIDTH),
    ('a_i', HEADS), ('a_f', HEADS), ('a_o', GROUP_WIDTH),
    ('b_q', GROUP_WIDTH), ('b_k', GROUP_WIDTH), ('b_v', GROUP_WIDTH),
    ('c_q', GROUP_WIDTH), ('c_k', HEAD_DIM), ('c_v', HEAD_DIM),
    ('c_iq', IDX_HEADS * IDX_DIM), ('c_ik', IDX_DIM), ('c_iw', IDX_HEADS),
    ('d_q', GROUP_WIDTH), ('d_kc', HEAD_DIM), ('d_vc', HEAD_DIM),
    ('d_ks', HEAD_DIM), ('d_vs', HEAD_DIM), ('d_kw', HEAD_DIM), ('d_vw', HEAD_DIM),
    ('d_g', 3 * HEADS),
)

GROUP_LAYOUT = (
    (('a_q', 'a_k'), ('a_v',), ('a_o',), (('a_i', HEAD_DIM),), (('a_i', M_QK_DIM),),
     (('a_f', HEAD_DIM),), (('a_f', M_QK_DIM),)),
    (('b_q',), ('b_k',), ('b_v',)),
    (('c_q',), ('c_iq',), ('c_k', 'c_v'), ('c_ik', 'c_iw')),
    (('d_q',), ('d_kc', 'd_vc'), ('d_ks', 'd_vs'), ('d_kw', 'd_vw'), ('d_g',)),
    (('a_i', 'a_f'),),
)


def _round_up(n, m):
    return -(-n // m) * m


def _projection_layout():
    offs, off = {}, 0
    for name, width in IN_SPLITS:
        offs[name] = (off, width)
        off += width
    runs, group_widths = [], []
    for group in GROUP_LAYOUT:
        gwidth = 0
        for chunk in group:
            cwidth = 0
            for entry in chunk:
                name, rep = entry if isinstance(entry, tuple) else (entry, 1)
                o, w = offs[name]
                runs.append((o, w, rep))
                cwidth += w * rep
            pad = _round_up(cwidth, LANES) - cwidth
            if pad:
                runs.append((-1, pad, 1))
            gwidth += cwidth + pad
        group_widths.append(gwidth)
    return tuple(runs), tuple(group_widths)


PROJ_RUNS, GROUP_WIDTHS = _projection_layout()
PROJ_WIDTH = int(sum(GROUP_WIDTHS))
PROJ_PERM = np.concatenate([np.repeat(np.arange(o, o + w), r) if o >= 0 else np.full(w, -1)
                            for o, w, r in PROJ_RUNS]).astype(np.int32)


def _permute_columns(a):
    parts = []
    for o, w, r in PROJ_RUNS:
        if o < 0:
            parts.append(jnp.zeros(a.shape[:-1] + (w,), a.dtype))
        else:
            parts.append(a[..., o:o + w] if r == 1 else jnp.repeat(a[..., o:o + w], r, axis=-1))
    return jnp.concatenate(parts, axis=-1)


def _t5_bucket_np(dist):
    n = np.maximum(dist, 0)
    max_exact = NUM_BUCKETS // 2
    nf = np.maximum(n, max_exact).astype(np.float32)
    large = max_exact + (np.log(nf / max_exact) / math.log(MAX_DISTANCE / max_exact)
                         * (NUM_BUCKETS - max_exact)).astype(np.int32)
    large = np.minimum(large, NUM_BUCKETS - 1)
    return np.where(n < max_exact, n, large).astype(np.int32)


def _nt_dot(a, b, precision=None):
    return lax.dot_general(a, b, (((1,), (1,)), ((), ())), precision=precision,
                           preferred_element_type=F32)


def _dot(a, b, precision=None):
    return jnp.dot(a, b, precision=precision, preferred_element_type=F32)


def _layer_norm_rows(y, g, b):
    mu = jnp.mean(y, axis=-1, keepdims=True)
    var = jnp.mean(jnp.square(y - mu), axis=-1, keepdims=True)
    return (y - mu) * lax.rsqrt(var + LN_EPS) * g + b


def _sigmoid(x):
    return 1.0 / (1.0 + jnp.exp(-x))


def _log_sigmoid(x):
    return -(jnp.maximum(-x, 0.0) + jnp.log1p(jnp.exp(-jnp.abs(x))))


def _sortable_key(x):
    bits = pltpu.bitcast(x, I32)
    return bits ^ ((bits >> 31) & jnp.int32(0x7FFFFFFF))


def _inproj_kernel(x_ref, w_ref, b_ref, *out_refs):
    xb = x_ref[...].astype(BF16)
    off = 0
    for o_ref, width in zip(out_refs, GROUP_WIDTHS):
        o_ref[...] = _dot(xb, w_ref[:, off:off + width]) + b_ref[:, off:off + width]
        off += width


def _inproj(x2d, w, b, tm=512):
    t = x2d.shape[0]
    return pl.pallas_call(
        _inproj_kernel,
        grid=(t // tm,),
        in_specs=[pl.BlockSpec((tm, D_MODEL), lambda i: (i, 0)),
                  pl.BlockSpec((D_MODEL, PROJ_WIDTH), lambda i: (0, 0)),
                  pl.BlockSpec((1, PROJ_WIDTH), lambda i: (0, 0))],
        out_specs=[pl.BlockSpec((tm, gw), lambda i: (i, 0)) for gw in GROUP_WIDTHS],
        out_shape=[jax.ShapeDtypeStruct((t, gw), F32) for gw in GROUP_WIDTHS],
        compiler_params=pltpu.CompilerParams(dimension_semantics=("arbitrary",),
                                             vmem_limit_bytes=VMEM_LIMIT_BYTES),
        name="inproj",
    )(x2d, w, b)


def _split_terms(x, n):
    terms, rest = [], x
    for _ in range(n):
        terms.append(rest.astype(BF16))
        rest = rest - terms[-1].astype(F32)
    return terms


def _iota(shape, dim):
    return lax.broadcasted_iota(I32, shape, dim)


def _mlstm_kernel(z_ref, gt_ref, cw_ref, ng_ref, o_ref, xpad_ref):
    seq = z_ref.shape[1]
    L, DK, DV, H = M_CHUNK, M_QK_DIM, HEAD_DIM, HEADS
    assert L == DV
    wq, wv = H * DK, H * DV
    lg_dk, lg_dv = int(math.log2(DK)), int(math.log2(DV))
    c_v, c_o, c_i64 = 2 * wq, 2 * wq + wv, 2 * wq + 2 * wv
    c_i32, c_f64 = c_i64 + wv, c_i64 + wv + wq
    xpad_ref[0:8, :] = jnp.zeros((8, 2 * wq), F32)
    xpad_ref[8:, :] = z_ref[0, :, 0:2 * wq]

    one_if = lambda cond: jnp.where(cond, 1.0, 0.0).astype(BF16)
    tri_l = one_if(_iota((L, L), 0) >= _iota((L, L), 1))
    trow = _iota((L, wv), 0)
    tri_heads = trow >= (_iota((L, wv), 1) & (L - 1))
    r_vv, c_vv = _iota((wv, wv), 0), _iota((wv, wv), 1)
    same_head = (r_vv >> lg_dv) == (c_vv >> lg_dv)
    ones_bd = one_if(same_head)
    mean_bd = jnp.where(same_head, 1.0 / DV, 0.0).astype(BF16)
    tri_u_bd = one_if(same_head & ((r_vv & (L - 1)) <= (c_vv & (L - 1))))
    state_mask = (_iota((wq, wv), 0) >> lg_dk) == (_iota((wq, wv), 1) >> lg_dv)
    eye_q = one_if(_iota((wq, wq), 0) == _iota((wq, wq), 1))
    head_of_qlane = _iota((L, wq), 1) >> lg_dk
    head_of_vlane = _iota((L, wv), 1) >> lg_dv
    row8 = _iota((8, wv), 0)
    cw = cw_ref[...]
    ng = ng_ref[...]

    def head_mean(x):
        hi_lo = _split_terms(x, 2)
        r = _dot(jnp.concatenate(hi_lo, axis=0), mean_bd)
        return r[0:L] + r[L:2 * L]

    def chunk(c, carry):
        cbd, nbd, m64, m32 = carry
        s0 = pl.multiple_of(c * L, L)
        rows = pl.ds(s0, L)
        xw = xpad_ref[pl.ds(s0, L + 8), :]
        y = sum(cw[j:j + 1, :] * xw[5 + j:5 + j + L, :] for j in range(M_CONV))
        qk = y * _sigmoid(y)
        q = qk[:, 0:wq]
        k = qk[:, wq:] * (DK ** -0.5)
        qb = q.astype(BF16)
        v = z_ref[0, rows, c_v:c_v + wv]
        i64 = z_ref[0, rows, c_i64:c_i64 + wv]
        i32 = z_ref[0, rows, c_i32:c_i32 + wq]
        gr = gt_ref[0, c]

        flog = _log_sigmoid(z_ref[0, rows, c_f64:c_f64 + wv + wq])
        bsum = _dot(tri_l, jnp.concatenate(_split_terms(flog, 3), axis=1))
        w3 = wv + wq
        ball = bsum[:, 0:w3] + bsum[:, w3:2 * w3] + bsum[:, 2 * w3:3 * w3]
        b64, b32 = ball[:, 0:wv], ball[:, wv:w3]
        fterms = [t.astype(F32) for t in _split_terms(_log_sigmoid(gr[1:2, :]), 3)]
        frows = jnp.where(row8 == 0, fterms[0], jnp.where(row8 == 1, fterms[1], jnp.where(row8 == 2, fterms[2], 0.0)))
        bparts = _dot(frows.astype(BF16), tri_u_bd)
        brow = bparts[0:1, :] + bparts[1:2, :] + bparts[2:3, :]

        dall = jnp.where(tri_heads, b64 - brow + gr[0:1, :], NEG)
        cm = i64 - b64
        for sh in (1, 2, 4, 8, 16, 32):
            cm = jnp.where(trow >= sh, jnp.maximum(cm, pltpu.roll(cm, sh, 0)), cm)
        inter = b64 + m64
        m_t = jnp.maximum(inter, b64 + cm)
        kbd = jnp.concatenate([jnp.where(head_of_qlane == h, k, 0.0) for h in range(H)], axis=0).astype(BF16)
        sc = _nt_dot(qb, kbd) * jnp.exp(dall - m_t)
        wi = jnp.exp(inter - m_t)
        vb = v.astype(BF16)
        vbd = jnp.concatenate([jnp.where(head_of_vlane == h, v, 0.0) for h in range(H)], axis=0).astype(BF16)
        pv = _dot(sc.astype(BF16), jnp.concatenate([vbd, ones_bd], axis=1))
        qst = _dot(qb, jnp.concatenate([cbd, nbd], axis=1).astype(BF16))
        num = pv[:, 0:wv] + wi * qst[:, 0:wv]
        den = pv[:, wv:] + wi * qst[:, wv:]
        hh = num / jnp.maximum(jnp.abs(den), jnp.exp(-m_t))

        og = _sigmoid(z_ref[0, rows, c_o:c_o + wv]) * hh
        dev = og - head_mean(og)
        o_ref[0, rows, :] = dev * lax.rsqrt(head_mean(dev * dev) + LN_EPS) * ng

        bl64, bl32 = b64[L - 1:L, :], b32[L - 1:L, :]
        m64_new = jnp.maximum(bl64 + m64, jnp.max(bl64 - b64 + i64, axis=0, keepdims=True))
        g32 = bl32 - b32 + i32
        m32_new = jnp.maximum(bl32 + m32, jnp.max(g32, axis=0, keepdims=True))
        kw = k * jnp.exp(g32 - m32_new)
        wc = jnp.exp(bl64 + m64 - m64_new)
        kwt = _nt_dot(eye_q, kw.astype(BF16)).astype(BF16)
        upd = _dot(kwt, jnp.concatenate([vb, jnp.ones((L, wv), BF16)], axis=1))
        cbd = wc * cbd + jnp.where(state_mask, upd[:, 0:wv], 0.0)
        nbd = wc * nbd + jnp.where(state_mask, upd[:, wv:], 0.0)
        return cbd, nbd, m64_new, m32_new

    init = (jnp.zeros((wq, wv), F32), jnp.zeros((wq, wv), F32), jnp.zeros((1, wv), F32), jnp.zeros((1, wq), F32))
    lax.fori_loop(0, seq // L, chunk, init, unroll=4)


def _mlstm(za, gates_t, conv_w, norm_g):
    b, s, wa = za.shape
    nc = s // M_CHUNK
    return pl.pallas_call(
        _mlstm_kernel,
        grid=(b,),
        in_specs=[pl.BlockSpec((1, s, wa), lambda i: (i, 0, 0)),
                  pl.BlockSpec((1, nc) + gates_t.shape[2:], lambda i: (i, 0, 0, 0)),
                  pl.BlockSpec((M_CONV, 2 * HEADS * M_QK_DIM), lambda i: (0, 0)),
                  pl.BlockSpec((1, GROUP_WIDTH), lambda i: (0, 0))],
        out_specs=pl.BlockSpec((1, s, GROUP_WIDTH), lambda i: (i, 0, 0)),
        out_shape=jax.ShapeDtypeStruct((b, s, GROUP_WIDTH), F32),
        scratch_shapes=[pltpu.VMEM((s + 8, 2 * HEADS * M_QK_DIM), F32)],
        compiler_params=pltpu.CompilerParams(dimension_semantics=("arbitrary",),
                                             vmem_limit_bytes=VMEM_LIMIT_BYTES),
        name="mlstm",
    )(za, gates_t, conv_w, norm_g)


def _mlstm_gate_rows(zg):
    b, s, _ = zg.shape
    gates = zg[:, :, 0:2 * HEADS].reshape(b, s // M_CHUNK, M_CHUNK, 2, HEADS)
    return jnp.transpose(gates, (0, 1, 3, 4, 2)).reshape(b, s // M_CHUNK, 2, HEADS * M_CHUNK)


def _dilated_kernel(q0_ref, q1_ref, k0_ref, k1_ref, v0_ref, v1_ref, bias_ref, o_ref, acc_ref, mx_ref, den_ref):
    seq = q0_ref.shape[1]
    W = BLK
    hd = HEAD_DIM
    npair = HEADS // 2
    scale = hd ** -0.5
    assert math.log2(scale).is_integer()
    nk = 2 * W
    qi = _iota((W, 2 * nk), 0)
    ki = _iota((W, 2 * nk), 1) & (nk - 1)
    j = W + qi - ki
    band = (j >= 0) & (j <= W)
    head_of_lane = _iota((nk, 2 * hd), 1) >> int(math.log2(hd))
    ones_bd = jnp.concatenate([jnp.where(head_of_lane == hh, 1.0, 0.0) for hh in range(2)], axis=0).astype(BF16)
    q_refs, k_refs, v_refs = (q0_ref, q1_ref), (k0_ref, k1_ref), (v0_ref, v1_ref)

    def block_diag(x):
        return jnp.concatenate([jnp.where(head_of_lane == hh, x, 0.0) for hh in range(2)], axis=0).astype(BF16)

    for br, (window, dil) in enumerate(DIL_PATTERNS):
        assert window // dil == W
        nb = (seq // dil) // W

        def piece(idx, _, br=br, dil=dil, nb=nb):
            n = idx % nb
            if dil == 1:
                rows_q = pl.ds(pl.multiple_of(W * n, W), W)
                rows_p = pl.ds(pl.multiple_of(W * jnp.maximum(n - 1, 0), W), W)
            else:
                r = idx // nb
                rows_q = pl.ds(r + dil * W * n, W, stride=dil)
                rows_p = pl.ds(r + dil * W * jnp.maximum(n - 1, 0), W, stride=dil)
            mask = band & (ki >= jnp.where(n > 0, 0, W))
            for pr in range(npair):
                q = (q_refs[pr][0, rows_q, :] * scale).astype(BF16)
                k2 = jnp.concatenate([k_refs[pr][0, rows_p, :], k_refs[pr][0, rows_q, :]], axis=0)
                v2 = jnp.concatenate([v_refs[pr][0, rows_p, :], v_refs[pr][0, rows_q, :]], axis=0)
                lg = jnp.where(mask, _nt_dot(q, block_diag(k2)) + bias_ref[br, pr], NEG)
                ps, mxs = [], []
                for hh in range(2):
                    sl = slice(nk * hh, nk * (hh + 1))
                    m = jnp.max(lg[:, sl], axis=-1, keepdims=True)
                    ps.append(jnp.where(mask[:, sl], jnp.exp(lg[:, sl] - m), 0.0))
                    mxs.append(jnp.broadcast_to(m, (W, hd)))
                p = jnp.concatenate(ps, axis=-1).astype(BF16)
                pv = _dot(p, jnp.concatenate([block_diag(v2), ones_bd], axis=1))
                acc_ref[br, pr, rows_q, :] = pv[:, 0:2 * hd]
                den_ref[br, pr, rows_q, :] = pv[:, 2 * hd:]
                mx_ref[br, pr, rows_q, :] = jnp.concatenate(mxs, axis=-1)
            return 0

        lax.fori_loop(0, dil * nb, piece, 0, unroll=2)

    def combine(i, _):
        rows = pl.ds(pl.multiple_of(i * W, W), W)
        outs = []
        for pr in range(npair):
            ms = [mx_ref[b, pr, rows, :] for b in range(len(DIL_PATTERNS))]
            top = functools.reduce(jnp.maximum, ms)
            es = [jnp.exp(m - top) for m in ms]
            num = sum(e * acc_ref[b, pr, rows, :] for b, e in enumerate(es))
            den = sum(e * jnp.maximum(den_ref[b, pr, rows, :], 1e-30) for b, e in enumerate(es))
            outs.append(num / den)
        o_ref[0, rows, :] = jnp.concatenate(outs, axis=-1)
        return 0

    lax.fori_loop(0, seq // W, combine, 0)


def _dilated(zb, bias):
    b, s, wb = zb.shape
    nbr = len(DIL_PATTERNS)
    pair_spec = lambda c: pl.BlockSpec((1, s, LANES), lambda i: (i, 0, c))
    return pl.pallas_call(
        _dilated_kernel,
        grid=(b,),
        in_specs=[pair_spec(c) for c in range(wb // LANES)]
                 + [pl.BlockSpec(bias.shape, lambda i: (0, 0, 0, 0))],
        out_specs=pl.BlockSpec((1, s, GROUP_WIDTH), lambda i: (i, 0, 0)),
        out_shape=jax.ShapeDtypeStruct((b, s, GROUP_WIDTH), F32),
        scratch_shapes=[pltpu.VMEM((nbr, HEADS // 2, s, LANES), F32) for _ in range(3)],
        compiler_params=pltpu.CompilerParams(dimension_semantics=("arbitrary",),
                                             vmem_limit_bytes=VMEM_LIMIT_BYTES),
        name="dilated",
    )(*([zb] * (wb // LANES)), bias)


def _head_columns_t(q):
    qt = q.T
    return jnp.concatenate([qt[HEAD_DIM * h:HEAD_DIM * (h + 1), :] for h in range(HEADS)], axis=1)


def _scaled_query_columns_t(q):
    scale = HEAD_DIM ** -0.5
    assert math.log2(scale).is_integer()
    return (_head_columns_t(q) * scale).astype(BF16)


def _unstack_heads_t(per_head_t):
    halves = []
    for h in range(0, HEADS, 2):
        halves.append(jnp.concatenate([per_head_t[h], per_head_t[h + 1]], axis=0).T)
    return jnp.concatenate(halves, axis=-1)


def _online_step(lg, mask, vt, m, den, acc):
    lg = jnp.where(mask, lg, NEG)
    m_new = jnp.maximum(m, jnp.max(lg, axis=0, keepdims=True))
    p = jnp.where(mask, jnp.exp(lg - m_new), 0.0)
    corr = jnp.exp(m - m_new)
    den = den * corr + jnp.sum(p, axis=0, keepdims=True)
    acc = acc * corr + _dot(vt, p.astype(BF16))
    return m_new, den, acc


def _pair_bias(btab_ref, behind):
    return jnp.concatenate([btab_ref[jnp.minimum(behind, 2)], btab_ref[jnp.clip(behind - 1, 0, 2)]], axis=0)


def _online_init(nq=BLK):
    return (tuple(jnp.full((1, nq), NEG, F32) for _ in range(HEADS)),
            tuple(jnp.zeros((1, nq), F32) for _ in range(HEADS)),
            tuple(jnp.zeros((HEAD_DIM, nq), F32) for _ in range(HEADS)))


def _dsa_kernel(z_ref, btab_ref, o_ref, vt_ref, key_ref, hi_ref, lo_ref, lom_ref):
    i = pl.program_id(1)
    seq = z_ref.shape[1]
    nkb = seq // BLK
    nq = QRY
    hd = HEAD_DIM
    topk = min(DSA_TOPK, seq // 4)
    t0 = pl.multiple_of(i * nq, nq)

    assert nq == KEYS
    npair = i + 1

    @pl.when(i == 0)
    def _():
        for kb in range(nkb):
            vt = z_ref[0, kb * BLK:(kb + 1) * BLK, 512:640].T[hd:2 * hd, :]
            vt_ref[kb // 2, :, (kb % 2) * BLK:(kb % 2 + 1) * BLK] = vt.astype(BF16)
        key_ref[...] = jnp.full(key_ref.shape, -2 ** 31, I32)
        for half_ref in (hi_ref, lo_ref, lom_ref):
            half_ref[...] = jnp.full(half_ref.shape, I16_MIN, I16)

    zq = z_ref[0, pl.ds(t0, nq), :]
    cq = zq[:, 0:256]
    ciq = _head_columns_t(zq[:, 256:512]).astype(BF16)
    iw = zq[:, 640:768].T[IDX_DIM:IDX_DIM + IDX_HEADS, :] * ((IDX_HEADS * IDX_DIM) ** -0.5)
    s_loc = lax.broadcasted_iota(I32, (KEYS, nq), 0)
    t_glob = t0 + lax.broadcasted_iota(I32, (KEYS, nq), 1)

    def score_block(kp, _):
        r0 = pl.multiple_of(kp * KEYS, KEYS)
        ik = z_ref[0, pl.ds(r0, KEYS), 640:704].astype(BF16)
        rel = _dot(ik, ciq)
        sc = jnp.zeros((KEYS, nq), F32)
        for h in range(IDX_HEADS):
            sc = sc + jnp.maximum(rel[:, nq * h:nq * (h + 1)], 0.0) * iw[h:h + 1, :]
        sc = jnp.where(r0 + s_loc <= t_glob, sc, NEG)
        key = _sortable_key(sc)
        key_ref[kp] = key
        hi_ref[kp] = (key >> 16).astype(I16)
        lo_ref[kp] = ((key & 0xFFFF) - 2 ** 15).astype(I16)
        return 0

    lax.fori_loop(0, npair, score_block, 0)

    def count(ref, pred, pairs):
        dt = ref.dtype
        rows = 8 * 4 // dt.itemsize
        def body(kp, acc):
            hit = jnp.where(pred(kp, ref[kp]), jnp.ones((), dt), jnp.zeros((), dt))
            hit = hit.reshape(KEYS // rows, rows, nq)
            parts = [hit[n] for n in range(KEYS // rows)]
            while len(parts) > 1:
                parts = [a + b for a, b in zip(parts[0::2], parts[1::2])]
            return acc + parts[0]
        acc = jnp.zeros((rows, nq), dt)
        if isinstance(pairs, int):
            for kp in range(pairs):
                acc = body(kp, acc)
        else:
            acc = lax.fori_loop(0, pairs, body, acc)
        return jnp.sum(acc.astype(I32), axis=0, keepdims=True)

    def threshold(pairs):
        def half_search(ref, k):
            def bit(it, thr):
                cand = thr + lax.shift_left(jnp.int32(1), 15 - it)
                c = count(ref, lambda kp, half: half >= cand.astype(I16), pairs)
                return jnp.where(c >= k, cand, thr)
            return lax.fori_loop(0, 16, bit, jnp.full((1, nq), I16_MIN, I32))

        def run(_):
            thr_hi = half_search(hi_ref, topk)
            thr_hi16 = thr_hi.astype(I16)
            above_hi = count(hi_ref, lambda kp, half: half > thr_hi16, pairs)
            for kp in range(pairs):
                lom_ref[kp] = jnp.where(hi_ref[kp] == thr_hi16, lo_ref[kp], jnp.int16(I16_MIN))
            thr_lo = half_search(lom_ref, topk - above_hi)
            thr = (thr_hi << 16) | (thr_lo + 2 ** 15)
            n_gt = count(key_ref, lambda kp, key: key > thr, pairs)
            n_eq = count(key_ref, lambda kp, key: key == thr, pairs)
            return thr, n_gt, n_eq
        return run

    walks = list(range(2, seq // KEYS + 1, 2))
    thr, n_gt, n_eq = lax.switch((npair - 1) // 2, [threshold(p) for p in walks], 0)
    need = topk - n_gt

    def tie_search(_):
        def idx_bit(it, jm):
            cand = jm + lax.shift_left(jnp.int32(1), 10 - it)
            c = count(key_ref, lambda kp, key: (key == thr) & (kp * KEYS + s_loc < cand), npair)
            return jnp.where(c < need, cand, jm)
        return lax.fori_loop(0, 11, idx_bit, jnp.zeros((1, nq), I32))

    assert seq == 2 ** 11
    jmax = lax.cond(jnp.max(n_eq - need) > 0, tie_search,
                    lambda _: jnp.full((1, nq), seq - 1, I32), 0)

    qs = _scaled_query_columns_t(cq)

    def attend(kp, carry):
        ms, dens, accs = carry
        r0 = pl.multiple_of(kp * KEYS, KEYS)
        kblk = z_ref[0, pl.ds(r0, KEYS), 512:576].astype(BF16)
        lg = _dot(kblk, qs) + btab_ref[jnp.minimum(i - kp, 2)]
        key = key_ref[kp]
        s_glob = r0 + s_loc
        mask = ((key > thr) | ((key == thr) & (s_glob <= jmax))) & (s_glob <= t_glob)
        vt = vt_ref[kp]
        out = [_online_step(lg[:, nq * h:nq * (h + 1)], mask, vt, ms[h], dens[h], accs[h])
               for h in range(HEADS)]
        return tuple(o[0] for o in out), tuple(o[1] for o in out), tuple(o[2] for o in out)

    ms, dens, accs = lax.fori_loop(0, npair, attend, _online_init(nq))
    o_ref[0] = _unstack_heads_t([accs[h] / jnp.maximum(dens[h], 1e-30) for h in range(HEADS)])


def _dsa(zc, btab):
    b, s, wc = zc.shape
    return pl.pallas_call(
        _dsa_kernel,
        grid=(b, s // QRY),
        in_specs=[pl.BlockSpec((1, s, wc), lambda bi, i: (bi, 0, 0)),
                  pl.BlockSpec(btab.shape, lambda bi, i: (0, 0, 0))],
        out_specs=pl.BlockSpec((1, QRY, GROUP_WIDTH), lambda bi, i: (bi, i, 0)),
        out_shape=jax.ShapeDtypeStruct((b, s, GROUP_WIDTH), F32),
        scratch_shapes=[pltpu.VMEM((s // KEYS, HEAD_DIM, KEYS), BF16), pltpu.VMEM((s // KEYS, KEYS, QRY), I32)]
                       + [pltpu.VMEM((s // KEYS, KEYS, QRY), I16)] * 3,
        compiler_params=pltpu.CompilerParams(dimension_semantics=("arbitrary", "arbitrary"),
                                             vmem_limit_bytes=VMEM_LIMIT_BYTES),
        name="dsa",
    )(zc, btab)


def _nsa_kernel(z_ref, zc_ref, w1_ref, pos_ref, w2_ref, bsel_ref, bcmp_ref, ovt_ref, exp_ref, o_ref,
                vst_ref, vwt_ref, cmp_ref, cmpt_ref):
    i = pl.program_id(1)
    seq = z_ref.shape[1]
    nkb = seq // BLK
    hd = HEAD_DIM
    scale = hd ** -0.5
    n_cmp = (seq - NSA_CMP_LEN) // NSA_CMP_STRIDE + 1
    n_sel = seq // NSA_SEL_LEN
    topn = min(NSA_TOPN, n_sel)
    half = NSA_CMP_LEN // 2
    assert half == NSA_CMP_STRIDE and n_cmp + 1 == seq // NSA_CMP_STRIDE == BLK and n_sel <= BLK
    nq = QRY
    assert nq == KEYS
    t0 = pl.multiple_of(i * nq, nq)
    hi = lax.Precision.HIGHEST

    @pl.when(i == 0)
    def _():
        for kb in range(nkb):
            rows = slice(kb * BLK, (kb + 1) * BLK)
            cols = slice((kb % 2) * BLK, (kb % 2 + 1) * BLK)
            vst_ref[kb // 2, :, cols] = z_ref[0, rows, 384:512].T[hd:2 * hd, :].astype(BF16)
            vwt_ref[kb // 2, :, cols] = z_ref[0, rows, 512:640].T[hd:2 * hd, :].astype(BF16)
        first = jnp.zeros((BLK, 2 * NSA_CMP_HIDDEN), F32)
        second = jnp.zeros((BLK, 2 * NSA_CMP_HIDDEN), F32)
        for j in range(half):
            xj = zc_ref[0, pl.ds(j, BLK, stride=NSA_CMP_STRIDE), :]
            first = first + _dot((xj + pos_ref[j:j + 1, :]).astype(BF16), w1_ref[j])
            second = second + _dot((xj + pos_ref[half + j:half + j + 1, :]).astype(BF16), w1_ref[half + j])
        hid = first + pltpu.roll(second, BLK - 1, 0)
        hid = hid * _sigmoid(hid)
        cmp = _dot(hid.astype(BF16), w2_ref[...])
        cmp_ref[...] = cmp
        cmpt_ref[...] = cmp.T

    zq = z_ref[0, pl.ds(t0, nq), :]
    qs = _scaled_query_columns_t(zq[:, 0:256])
    gates = _sigmoid(zq[:, 640:768].T[0:16, :])
    row = lax.broadcasted_iota(I32, (BLK, nq), 0)
    t_glob = t0 + lax.broadcasted_iota(I32, (BLK, nq), 1)

    kcmp = cmp_ref[:, 0:hd].astype(BF16)
    vcmpt = cmpt_ref[hd:2 * hd, :].astype(BF16)
    lgc = _dot(kcmp, qs) + bcmp_ref[0]
    mask_c = (t_glob - (row * NSA_CMP_STRIDE + NSA_CMP_LEN - 1) >= 0) & (row < n_cmp)
    o_cmp, psum = [], jnp.zeros((BLK, nq), F32)
    for h in range(HEADS):
        lg = jnp.where(mask_c, lgc[:, nq * h:nq * (h + 1)], NEG)
        m = jnp.max(lg, axis=0, keepdims=True)
        p = jnp.where(mask_c, jnp.exp(lg - m), 0.0)
        p = p / jnp.maximum(jnp.sum(p, axis=0, keepdims=True), 1e-30)
        o_cmp.append(_dot(vcmpt, p.astype(BF16)))
        psum = psum + p

    imp = _dot(ovt_ref[...], psum, precision=hi)
    cur = t_glob >> int(math.log2(NSA_SEL_LEN))
    forced = (row == 0) | (row == cur) | (row == cur - 1)
    imp = jnp.where(forced, NSA_FORCE, imp)
    imp = jnp.where(row * NSA_SEL_LEN <= t_glob, imp, NEG)
    imp = imp[0:n_sel, :]
    jrow = lax.broadcasted_iota(I32, (n_sel, nq), 0)
    rank = jnp.zeros((n_sel, nq), I32)
    for jp in range(n_sel):
        other = imp[jp:jp + 1, :]
        rank = rank + ((other > imp) | ((other == imp) & (jp < jrow))).astype(I32)
    chosen = jnp.where(rank < topn, 1.0, 0.0)
    chosen = jnp.concatenate([chosen, jnp.zeros((BLK - n_sel, nq), F32)], axis=0).astype(BF16)

    s_loc = lax.broadcasted_iota(I32, (KEYS, nq), 0)
    t_keys = t0 + lax.broadcasted_iota(I32, (KEYS, nq), 1)

    def attend(vt_ref, lanes, mask_fn):
        def body(kp, carry):
            ms, dens, accs = carry
            r0 = pl.multiple_of(kp * KEYS, KEYS)
            kblk = z_ref[0, pl.ds(r0, KEYS), lanes].astype(BF16)
            lg = _dot(kblk, qs) + bsel_ref[jnp.minimum(i - kp, 2)]
            mask = mask_fn(r0)
            vt = vt_ref[kp]
            out = [_online_step(lg[:, nq * h:nq * (h + 1)], mask, vt, ms[h], dens[h], accs[h])
                   for h in range(HEADS)]
            return tuple(o[0] for o in out), tuple(o[1] for o in out), tuple(o[2] for o in out)
        return body

    def mask_sel(r0):
        picked = _dot(exp_ref[pl.ds(r0, KEYS), :], chosen) > 0.5
        return picked & (r0 + s_loc <= t_keys)

    def mask_win(r0):
        dist = t_keys - (r0 + s_loc)
        return (dist >= 0) & (dist < NSA_WINDOW)

    _, den_s, acc_s = lax.fori_loop(0, i + 1, attend(vst_ref, slice(384, 448), mask_sel), _online_init(nq))
    first_w = jnp.maximum(i - NSA_WINDOW // KEYS, 0)
    _, den_w, acc_w = lax.fori_loop(first_w, i + 1, attend(vwt_ref, slice(512, 576), mask_win), _online_init(nq))

    outs = []
    for h in range(HEADS):
        o_s = acc_s[h] / jnp.maximum(den_s[h], 1e-30)
        o_w = acc_w[h] / jnp.maximum(den_w[h], 1e-30)
        outs.append(gates[3 * h:3 * h + 1, :] * o_cmp[h] + gates[3 * h + 1:3 * h + 2, :] * o_s
                    + gates[3 * h + 2:3 * h + 3, :] * o_w)
    o_ref[0] = _unstack_heads_t(outs)


def _nsa(zd, w1, pos, w2, bsel, bcmp, ovt, expand):
    b, s, wd = zd.shape
    full = lambda a: pl.BlockSpec(a.shape, lambda bi, i: (0,) * a.ndim)
    return pl.pallas_call(
        _nsa_kernel,
        grid=(b, s // QRY),
        in_specs=[pl.BlockSpec((1, s, wd), lambda bi, i: (bi, 0, 0)),
                  pl.BlockSpec((1, s, LANES), lambda bi, i: (bi, 0, GROUP_WIDTH // LANES)),
                  full(w1), full(pos), full(w2), full(bsel),
                  pl.BlockSpec((1, BLK, HEADS * QRY), lambda bi, i: (i, 0, 0)),
                  full(ovt), full(expand)],
        out_specs=pl.BlockSpec((1, QRY, GROUP_WIDTH), lambda bi, i: (bi, i, 0)),
        out_shape=jax.ShapeDtypeStruct((b, s, GROUP_WIDTH), F32),
        scratch_shapes=[pltpu.VMEM((s // KEYS, HEAD_DIM, KEYS), BF16), pltpu.VMEM((s // KEYS, HEAD_DIM, KEYS), BF16),
                        pltpu.VMEM((BLK, BLK), F32), pltpu.VMEM((BLK, BLK), F32)],
        compiler_params=pltpu.CompilerParams(dimension_semantics=("arbitrary", "arbitrary"),
                                             vmem_limit_bytes=VMEM_LIMIT_BYTES),
        name="nsa",
    )(zd, zd, w1, pos, w2, bsel, bcmp, ovt, expand)


def _outproj_kernel(alpha, x_ref, a_ref, b_ref, c_ref, d_ref, w_ref, bo_ref, g_ref, beta_ref, o_ref):
    acc = bo_ref[...] + _dot(a_ref[...].astype(BF16), w_ref[0:GROUP_WIDTH, :])
    for n, m_ref in enumerate((b_ref, c_ref, d_ref), start=1):
        acc = acc + _dot(m_ref[...].astype(BF16), w_ref[n * GROUP_WIDTH:(n + 1) * GROUP_WIDTH, :])
    o_ref[...] = _layer_norm_rows(alpha * x_ref[...] + acc, g_ref[...], beta_ref[...])


def _outproj(alpha, x2d, mixed, w, bo, g, beta, tm=512):
    t = x2d.shape[0]
    row_spec = lambda width: pl.BlockSpec((tm, width), lambda i: (i, 0))
    const = lambda a: pl.BlockSpec(a.shape, lambda i: (0, 0))
    return pl.pallas_call(
        functools.partial(_outproj_kernel, alpha),
        grid=(t // tm,),
        in_specs=[row_spec(D_MODEL)] + [row_spec(GROUP_WIDTH)] * N_MIXERS
                 + [const(w), const(bo), const(g), const(beta)],
        out_specs=row_spec(D_MODEL),
        out_shape=jax.ShapeDtypeStruct((t, D_MODEL), F32),
        compiler_params=pltpu.CompilerParams(dimension_semantics=("arbitrary",),
                                             vmem_limit_bytes=VMEM_LIMIT_BYTES),
        name="outproj_ln",
    )(x2d, *mixed, w, bo, g, beta)


def _ffn_kernel(alpha, x_ref, w1_ref, b1_ref, w2_ref, b2_ref, g_ref, beta_ref, o_ref, xb_ref, acc_ref):
    j = pl.program_id(1)

    @pl.when(j == 0)
    def _():
        xb_ref[...] = x_ref[...].astype(BF16)
        acc_ref[...] = jnp.zeros_like(acc_ref)

    hdn = jnp.maximum(_dot(xb_ref[...], w1_ref[...]) + b1_ref[...], 0.0)
    acc_ref[...] += _dot(jnp.square(hdn).astype(BF16), w2_ref[...])

    @pl.when(j == pl.num_programs(1) - 1)
    def _():
        y = alpha * x_ref[...] + (acc_ref[...] + b2_ref[...])
        o_ref[...] = _layer_norm_rows(y, g_ref[...], beta_ref[...])


def _ffn(alpha, x2d, w1, b1, w2, b2, g, beta, tm=1024, tf=1024):
    t = x2d.shape[0]
    return pl.pallas_call(
        functools.partial(_ffn_kernel, alpha),
        grid=(t // tm, D_FF // tf),
        in_specs=[pl.BlockSpec((tm, D_MODEL), lambda i, j: (i, 0)),
                  pl.BlockSpec((D_MODEL, tf), lambda i, j: (0, j)),
                  pl.BlockSpec((1, tf), lambda i, j: (0, j)),
                  pl.BlockSpec((tf, D_MODEL), lambda i, j: (j, 0)),
                  pl.BlockSpec((1, D_MODEL), lambda i, j: (0, 0)),
                  pl.BlockSpec((1, D_MODEL), lambda i, j: (0, 0)),
                  pl.BlockSpec((1, D_MODEL), lambda i, j: (0, 0))],
        out_specs=pl.BlockSpec((tm, D_MODEL), lambda i, j: (i, 0)),
        out_shape=jax.ShapeDtypeStruct((t, D_MODEL), F32),
        scratch_shapes=[pltpu.VMEM((tm, D_MODEL), BF16), pltpu.VMEM((tm, D_MODEL), F32)],
        compiler_params=pltpu.CompilerParams(dimension_semantics=("arbitrary", "arbitrary"),
                                             vmem_limit_bytes=VMEM_LIMIT_BYTES),
        name="ffn_ln",
    )(x2d, w1, b1, w2, b2, g, beta)


def _bias_of_distance(rel_bias_heads, dist):
    onehot = np.eye(NUM_BUCKETS, dtype=np.float32)[_t5_bucket_np(np.asarray(dist))]
    return jnp.dot(rel_bias_heads.T, jnp.asarray(onehot.T), precision=lax.Precision.HIGHEST)


def _shifted_rows(v, n_rows, n_cols, step):
    period = v.shape[-1]
    assert n_cols <= period - step
    flat = jnp.tile(v, (1,) * (v.ndim - 1) + (n_rows,))[..., :n_rows * (period - step)]
    return flat.reshape(v.shape[:-1] + (n_rows, period - step))[..., :n_cols]


def _wrapped(period):
    idx = np.arange(period)
    return np.where(idx < period // 2, idx, idx - period)


def _dilated_bias(rel_bias):
    x = _wrapped(4 * BLK)
    tabs = [_shifted_rows(_bias_of_distance(rel_bias[:, 0:HEADS], (BLK - x) * dil), BLK, 2 * BLK, 1)
            for _, dil in DIL_PATTERNS]
    tabs = jnp.stack(tabs).reshape(len(DIL_PATTERNS), HEADS // 2, 2, BLK, 2 * BLK)
    return jnp.transpose(tabs, (0, 1, 3, 2, 4)).reshape(len(DIL_PATTERNS), HEADS // 2, BLK, 4 * BLK)


def _toeplitz_bias_t(rel_bias_heads, blk=BLK):
    assert (_t5_bucket_np(np.arange(blk + 1, 64 * blk)) == NUM_BUCKETS - 1).all()
    x = _wrapped(2 * blk)
    tabs = [_shifted_rows(_bias_of_distance(rel_bias_heads, blk * delta + x), blk, blk, 1)
            for delta in range(3)]
    return jnp.transpose(jnp.stack(tabs), (0, 2, 1, 3)).reshape(3, blk, HEADS * blk)


def _compressed_bias_t(rel_bias_heads, seq):
    nq = seq // QRY
    x = _wrapped(2 * seq + BLK)
    v = _bias_of_distance(rel_bias_heads, x - (NSA_CMP_LEN - 1))
    tab = _shifted_rows(v, BLK, seq, NSA_CMP_STRIDE)
    tab = tab.reshape(HEADS, BLK, nq, QRY)
    return jnp.transpose(tab, (2, 1, 0, 3)).reshape(nq, BLK, HEADS * QRY)


def _nsa_constants(seq):
    n_cmp = (seq - NSA_CMP_LEN) // NSA_CMP_STRIDE + 1
    n_sel = seq // NSA_SEL_LEN
    cs = np.arange(n_cmp)[:, None] * NSA_CMP_STRIDE
    ss = np.arange(n_sel)[None, :] * NSA_SEL_LEN
    ov = np.clip(np.minimum(cs + NSA_CMP_LEN, ss + NSA_SEL_LEN) - np.maximum(cs, ss), 0, None) / NSA_CMP_LEN
    ovt = np.zeros((BLK, BLK), np.float32)
    ovt[:n_sel, :n_cmp] = ov.T
    expand = np.zeros((seq, BLK), np.float32)
    expand[np.arange(seq), np.arange(seq) // NSA_SEL_LEN] = 1.0
    return jnp.asarray(ovt), jnp.asarray(expand, BF16)


def _nsa_weights(cmp_pos, cmp_w1, cmp_w2):
    hd, hid = HEAD_DIM, NSA_CMP_HIDDEN
    w1 = cmp_w1.reshape(2, NSA_CMP_LEN, hd, hid)
    zeros = jnp.zeros((NSA_CMP_LEN, hd, hid), F32)
    w1 = jnp.concatenate([jnp.concatenate([w1[0], zeros], axis=-1),
                          jnp.concatenate([zeros, w1[1]], axis=-1)], axis=1)
    pos = jnp.concatenate([cmp_pos[0], cmp_pos[1]], axis=-1)
    z2 = jnp.zeros((hid, hd), F32)
    w2 = jnp.concatenate([jnp.concatenate([cmp_w2[0], z2], axis=-1),
                          jnp.concatenate([z2, cmp_w2[1]], axis=-1)], axis=0)
    return w1.astype(BF16), pos, w2.astype(BF16)


def kernel(x, w_in, b_in, a_conv, a_norm, d_cmp_pos, d_cmp_w1, d_cmp_w2, w_out, b_out, ln1_g, ln1_b,
           w_ff1, b_ff1, w_ff2, b_ff2, ln2_g, ln2_b, rel_bias):
    bsz, seq, _ = x.shape
    depth = w_in.shape[0]
    alpha = (2 * depth) ** 0.25
    nc = seq // M_CHUNK
    bias_dil = _dilated_bias(rel_bias)
    btab_dsa = _toeplitz_bias_t(rel_bias[:, HEADS:2 * HEADS], QRY)
    btab_nsa = _toeplitz_bias_t(rel_bias[:, 2 * HEADS:3 * HEADS], QRY)
    bcmp_nsa = _compressed_bias_t(rel_bias[:, 2 * HEADS:3 * HEADS], seq)
    ovt, expand = _nsa_constants(seq)

    h = x.reshape(bsz * seq, D_MODEL)
    for l in range(depth):
        w_l = _permute_columns(w_in[l]).astype(BF16)
        b_l = _permute_columns(b_in[l])[None, :]
        za, zb, zc, zd, zg = (z.reshape(bsz, seq, -1) for z in _inproj(h, w_l, b_l))
        out_a = _mlstm(za, _mlstm_gate_rows(zg), a_conv[l], a_norm[l][None, :])
        out_b = _dilated(zb, bias_dil)
        out_c = _dsa(zc, btab_dsa)
        nsa_w1, nsa_pos, nsa_w2 = _nsa_weights(d_cmp_pos[l], d_cmp_w1[l], d_cmp_w2[l])
        out_d = _nsa(zd, nsa_w1, nsa_pos, nsa_w2, btab_nsa, bcmp_nsa, ovt, expand)
        mixed = [o.reshape(bsz * seq, GROUP_WIDTH) for o in (out_a, out_b, out_c, out_d)]
        h = _outproj(alpha, h, mixed, w_out[l].astype(BF16), b_out[l][None, :],
                     ln1_g[l][None, :], ln1_b[l][None, :])
        h = _ffn(alpha, h, w_ff1[l].astype(BF16), b_ff1[l][None, :], w_ff2[l].astype(BF16),
                 b_ff2[l][None, :], ln2_g[l][None, :], ln2_b[l][None, :])
    return h.reshape(bsz, seq, D_MODEL)
```

```python
import functools
import math

import numpy as np
import jax
import jax.numpy as jnp
from jax import lax
from jax.experimental import pallas as pl
from jax.experimental.pallas import tpu as pltpu

F32 = jnp.float32
BF16 = jnp.bfloat16
I32 = jnp.int32
I16 = jnp.int16
I16_MIN = -2 ** 15

D_MODEL = 1024
N_MIXERS = 4
HEADS = 4
HEAD_DIM = D_MODEL // (N_MIXERS * HEADS)
GROUP_WIDTH = HEADS * HEAD_DIM
D_FF = 4 * D_MODEL
LN_EPS = 1e-5
NEG = -1e30

M_QK_DIM = HEAD_DIM // 2
M_CHUNK = 64
M_CONV = 4
DIL_PATTERNS = ((128, 1), (512, 4), (2048, 16))
IDX_HEADS = 4
IDX_DIM = 64
DSA_TOPK = 256
NSA_CMP_LEN = 32
NSA_CMP_STRIDE = 16
NSA_SEL_LEN = 64
NSA_TOPN = 16
NSA_WINDOW = 512
NSA_CMP_HIDDEN = 256
NSA_FORCE = 1e9
NUM_BUCKETS = 32
MAX_DISTANCE = 128

LANES = 128
BLK = 128
KEYS = 2 * BLK
QRY = KEYS
VMEM_LIMIT_BYTES = 56 * 1024 * 1024

IN_SPLITS = (
    ('a_q', HEADS * M_QK_DIM), ('a_k', HEADS * M_QK_DIM), ('a_v', GROUP_WIDTH),
    ('a_i', HEADS), ('a_f', HEADS), ('a_o', GROUP_WIDTH),
    ('b_q', GROUP_WIDTH), ('b_k', GROUP_WIDTH), ('b_v', GROUP_WIDTH),
    ('c_q', GROUP_WIDTH), ('c_k', HEAD_DIM), ('c_v', HEAD_DIM),
    ('c_iq', IDX_HEADS * IDX_DIM), ('c_ik', IDX_DIM), ('c_iw', IDX_HEADS),
    ('d_q', GROUP_WIDTH), ('d_kc', HEAD_DIM), ('d_vc', HEAD_DIM),
    ('d_ks', HEAD_DIM), ('d_vs', HEAD_DIM), ('d_kw', HEAD_DIM), ('d_vw', HEAD_DIM),
    ('d_g', 3 * HEADS),
)

GROUP_LAYOUT = (
    (('a_q', 'a_k'), ('a_v',), ('a_o',), (('a_i', HEAD_DIM),), (('a_i', M_QK_DIM),),
     (('a_f', HEAD_DIM),), (('a_f', M_QK_DIM),)),
    (('b_q',), ('b_k',), ('b_v',)),
    (('c_k',), ('c_ik',)),
    (('d_kc', 'd_vc'), ('d_ks',), ('d_kw',)),
    (('a_i', 'a_f'),),
)

ROW_TILE = 16
TRANSPOSED_LAYOUT = (
    ('c_q', 'c_iq', 'c_iw', 'c_v'),
    ('d_q', 'd_g', 'd_vs', 'd_vw'),
)


def _round_up(n, m):
    return -(-n // m) * m


def _projection_layout():
    offs, off = {}, 0
    for name, width in IN_SPLITS:
        offs[name] = (off, width)
        off += width
    runs, group_widths = [], []
    for group in GROUP_LAYOUT:
        gwidth = 0
        for chunk in group:
            cwidth = 0
            for entry in chunk:
                name, rep = entry if isinstance(entry, tuple) else (entry, 1)
                o, w = offs[name]
                runs.append((o, w, rep))
                cwidth += w * rep
            pad = _round_up(cwidth, LANES) - cwidth
            if pad:
                runs.append((-1, pad, 1))
            gwidth += cwidth + pad
        group_widths.append(gwidth)
    return tuple(runs), tuple(group_widths)


def _transposed_layout():
    offs, off = {}, 0
    for name, width in IN_SPLITS:
        offs[name] = (off, width)
        off += width
    runs, groups = [], []
    for group in TRANSPOSED_LAYOUT:
        rows, first = 0, {}
        for name in group:
            o, w = offs[name]
            first[name] = rows
            runs.append((o, w))
            pad = _round_up(w, ROW_TILE) - w
            if pad:
                runs.append((-1, pad))
            rows += w + pad
        groups.append((rows, first))
    return tuple(runs), tuple(groups)


PROJ_RUNS, GROUP_WIDTHS = _projection_layout()
PROJ_WIDTH = int(sum(GROUP_WIDTHS))
ROW_RUNS, ROW_GROUPS = _transposed_layout()
ROW_COUNTS = tuple(rows for rows, _ in ROW_GROUPS)


def _transposed_rows(a):
    parts = [a[..., o:o + w] if o >= 0 else jnp.zeros(a.shape[:-1] + (w,), a.dtype) for o, w in ROW_RUNS]
    return jnp.concatenate(parts, axis=-1)
PROJ_PERM = np.concatenate([np.repeat(np.arange(o, o + w), r) if o >= 0 else np.full(w, -1)
                            for o, w, r in PROJ_RUNS]).astype(np.int32)


def _permute_columns(a):
    parts = []
    for o, w, r in PROJ_RUNS:
        if o < 0:
            parts.append(jnp.zeros(a.shape[:-1] + (w,), a.dtype))
        else:
            parts.append(a[..., o:o + w] if r == 1 else jnp.repeat(a[..., o:o + w], r, axis=-1))
    return jnp.concatenate(parts, axis=-1)


def _t5_bucket_np(dist):
    n = np.maximum(dist, 0)
    max_exact = NUM_BUCKETS // 2
    nf = np.maximum(n, max_exact).astype(np.float32)
    large = max_exact + (np.log(nf / max_exact) / math.log(MAX_DISTANCE / max_exact)
                         * (NUM_BUCKETS - max_exact)).astype(np.int32)
    large = np.minimum(large, NUM_BUCKETS - 1)
    return np.where(n < max_exact, n, large).astype(np.int32)


def _nt_dot(a, b, precision=None):
    return lax.dot_general(a, b, (((1,), (1,)), ((), ())), precision=precision,
                           preferred_element_type=F32)


def _dot(a, b, precision=None):
    return jnp.dot(a, b, precision=precision, preferred_element_type=F32)


def _layer_norm_rows(y, g, b):
    mu = jnp.mean(y, axis=-1, keepdims=True)
    var = jnp.mean(jnp.square(y - mu), axis=-1, keepdims=True)
    return (y - mu) * lax.rsqrt(var + LN_EPS) * g + b


def _sigmoid(x):
    return 1.0 / (1.0 + jnp.exp(-x))


def _log_sigmoid(x):
    return -(jnp.maximum(-x, 0.0) + jnp.log1p(jnp.exp(-jnp.abs(x))))


def _sortable_key(x):
    bits = pltpu.bitcast(x, I32)
    return bits ^ ((bits >> 31) & jnp.int32(0x7FFFFFFF))


def _inproj_kernel(x_ref, w_ref, b_ref, wt_ref, bt_ref, *out_refs):
    xb = x_ref[...].astype(BF16)
    off = 0
    for o_ref, width in zip(out_refs, GROUP_WIDTHS):
        o_ref[...] = _dot(xb, w_ref[:, off:off + width]) + b_ref[:, off:off + width]
        off += width
    off = 0
    for o_ref, rows in zip(out_refs[len(GROUP_WIDTHS):], ROW_COUNTS):
        o_ref[...] = _nt_dot(wt_ref[off:off + rows, :], xb) + bt_ref[off:off + rows, :]
        off += rows


def _inproj(x2d, w, b, wt, bt, tm=512):
    t = x2d.shape[0]
    const = lambda a: pl.BlockSpec(a.shape, lambda i: (0, 0))
    return pl.pallas_call(
        _inproj_kernel,
        grid=(t // tm,),
        in_specs=[pl.BlockSpec((tm, D_MODEL), lambda i: (i, 0)), const(w), const(b), const(wt), const(bt)],
        out_specs=[pl.BlockSpec((tm, gw), lambda i: (i, 0)) for gw in GROUP_WIDTHS]
                  + [pl.BlockSpec((rows, tm), lambda i: (0, i)) for rows in ROW_COUNTS],
        out_shape=[jax.ShapeDtypeStruct((t, gw), F32) for gw in GROUP_WIDTHS]
                  + [jax.ShapeDtypeStruct((rows, t), F32) for rows in ROW_COUNTS],
        compiler_params=pltpu.CompilerParams(dimension_semantics=("arbitrary",),
                                             vmem_limit_bytes=VMEM_LIMIT_BYTES),
        name="inproj",
    )(x2d, w, b, wt, bt)


def _split_terms(x, n):
    terms, rest = [], x
    for _ in range(n):
        terms.append(rest.astype(BF16))
        rest = rest - terms[-1].astype(F32)
    return terms


def _iota(shape, dim):
    return lax.broadcasted_iota(I32, shape, dim)


def _mlstm_kernel(z_ref, gt_ref, cw_ref, ng_ref, o_ref, xpad_ref):
    seq = z_ref.shape[1]
    L, DK, DV, H = M_CHUNK, M_QK_DIM, HEAD_DIM, HEADS
    assert L == DV
    wq, wv = H * DK, H * DV
    lg_dk, lg_dv = int(math.log2(DK)), int(math.log2(DV))
    c_v, c_o, c_i64 = 2 * wq, 2 * wq + wv, 2 * wq + 2 * wv
    c_i32, c_f64 = c_i64 + wv, c_i64 + wv + wq
    xpad_ref[0:8, :] = jnp.zeros((8, 2 * wq), F32)
    xpad_ref[8:, :] = z_ref[0, :, 0:2 * wq]

    one_if = lambda cond: jnp.where(cond, 1.0, 0.0).astype(BF16)
    tri_l = one_if(_iota((L, L), 0) >= _iota((L, L), 1))
    trow = _iota((L, wv), 0)
    tri_heads = trow >= (_iota((L, wv), 1) & (L - 1))
    r_vv, c_vv = _iota((wv, wv), 0), _iota((wv, wv), 1)
    same_head = (r_vv >> lg_dv) == (c_vv >> lg_dv)
    ones_bd = one_if(same_head)
    mean_bd = jnp.where(same_head, 1.0 / DV, 0.0).astype(BF16)
    tri_u_bd = one_if(same_head & ((r_vv & (L - 1)) <= (c_vv & (L - 1))))
    state_mask = (_iota((wq, wv), 0) >> lg_dk) == (_iota((wq, wv), 1) >> lg_dv)
    eye_q = one_if(_iota((wq, wq), 0) == _iota((wq, wq), 1))
    head_of_qlane = _iota((L, wq), 1) >> lg_dk
    head_of_vlane = _iota((L, wv), 1) >> lg_dv
    row8 = _iota((8, wv), 0)
    cw = cw_ref[...]
    ng = ng_ref[...]

    def head_mean(x):
        hi_lo = _split_terms(x, 2)
        r = _dot(jnp.concatenate(hi_lo, axis=0), mean_bd)
        return r[0:L] + r[L:2 * L]

    def chunk(c, carry):
        cbd, nbd, m64, m32 = carry
        s0 = pl.multiple_of(c * L, L)
        rows = pl.ds(s0, L)
        xw = xpad_ref[pl.ds(s0, L + 8), :]
        y = sum(cw[j:j + 1, :] * xw[5 + j:5 + j + L, :] for j in range(M_CONV))
        qk = y * _sigmoid(y)
        q = qk[:, 0:wq]
        k = qk[:, wq:] * (DK ** -0.5)
        qb = q.astype(BF16)
        v = z_ref[0, rows, c_v:c_v + wv]
        i64 = z_ref[0, rows, c_i64:c_i64 + wv]
        i32 = z_ref[0, rows, c_i32:c_i32 + wq]
        gr = gt_ref[0, c]

        flog = _log_sigmoid(z_ref[0, rows, c_f64:c_f64 + wv + wq])
        bsum = _dot(tri_l, jnp.concatenate(_split_terms(flog, 3), axis=1))
        w3 = wv + wq
        ball = bsum[:, 0:w3] + bsum[:, w3:2 * w3] + bsum[:, 2 * w3:3 * w3]
        b64, b32 = ball[:, 0:wv], ball[:, wv:w3]
        fterms = [t.astype(F32) for t in _split_terms(_log_sigmoid(gr[1:2, :]), 3)]
        frows = jnp.where(row8 == 0, fterms[0], jnp.where(row8 == 1, fterms[1], jnp.where(row8 == 2, fterms[2], 0.0)))
        bparts = _dot(frows.astype(BF16), tri_u_bd)
        brow = bparts[0:1, :] + bparts[1:2, :] + bparts[2:3, :]

        dall = jnp.where(tri_heads, b64 - brow + gr[0:1, :], NEG)
        cm = i64 - b64
        for sh in (1, 2, 4, 8, 16, 32):
            cm = jnp.where(trow >= sh, jnp.maximum(cm, pltpu.roll(cm, sh, 0)), cm)
        inter = b64 + m64
        m_t = jnp.maximum(inter, b64 + cm)
        kbd = jnp.concatenate([jnp.where(head_of_qlane == h, k, 0.0) for h in range(H)], axis=0).astype(BF16)
        sc = _nt_dot(qb, kbd) * jnp.exp(dall - m_t)
        wi = jnp.exp(inter - m_t)
        vb = v.astype(BF16)
        vbd = jnp.concatenate([jnp.where(head_of_vlane == h, v, 0.0) for h in range(H)], axis=0).astype(BF16)
        pv = _dot(sc.astype(BF16), jnp.concatenate([vbd, ones_bd], axis=1))
        qst = _dot(qb, jnp.concatenate([cbd, nbd], axis=1).astype(BF16))
        num = pv[:, 0:wv] + wi * qst[:, 0:wv]
        den = pv[:, wv:] + wi * qst[:, wv:]
        hh = num / jnp.maximum(jnp.abs(den), jnp.exp(-m_t))

        og = _sigmoid(z_ref[0, rows, c_o:c_o + wv]) * hh
        dev = og - head_mean(og)
        o_ref[0, rows, :] = dev * lax.rsqrt(head_mean(dev * dev) + LN_EPS) * ng

        bl64, bl32 = b64[L - 1:L, :], b32[L - 1:L, :]
        m64_new = jnp.maximum(bl64 + m64, jnp.max(bl64 - b64 + i64, axis=0, keepdims=True))
        g32 = bl32 - b32 + i32
        m32_new = jnp.maximum(bl32 + m32, jnp.max(g32, axis=0, keepdims=True))
        kw = k * jnp.exp(g32 - m32_new)
        wc = jnp.exp(bl64 + m64 - m64_new)
        kwt = _nt_dot(eye_q, kw.astype(BF16)).astype(BF16)
        upd = _dot(kwt, jnp.concatenate([vb, jnp.ones((L, wv), BF16)], axis=1))
        cbd = wc * cbd + jnp.where(state_mask, upd[:, 0:wv], 0.0)
        nbd = wc * nbd + jnp.where(state_mask, upd[:, wv:], 0.0)
        return cbd, nbd, m64_new, m32_new

    init = (jnp.zeros((wq, wv), F32), jnp.zeros((wq, wv), F32), jnp.zeros((1, wv), F32), jnp.zeros((1, wq), F32))
    lax.fori_loop(0, seq // L, chunk, init, unroll=4)


def _mlstm(za, gates_t, conv_w, norm_g):
    b, s, wa = za.shape
    nc = s // M_CHUNK
    return pl.pallas_call(
        _mlstm_kernel,
        grid=(b,),
        in_specs=[pl.BlockSpec((1, s, wa), lambda i: (i, 0, 0)),
                  pl.BlockSpec((1, nc) + gates_t.shape[2:], lambda i: (i, 0, 0, 0)),
                  pl.BlockSpec((M_CONV, 2 * HEADS * M_QK_DIM), lambda i: (0, 0)),
                  pl.BlockSpec((1, GROUP_WIDTH), lambda i: (0, 0))],
        out_specs=pl.BlockSpec((1, s, GROUP_WIDTH), lambda i: (i, 0, 0)),
        out_shape=jax.ShapeDtypeStruct((b, s, GROUP_WIDTH), F32),
        scratch_shapes=[pltpu.VMEM((s + 8, 2 * HEADS * M_QK_DIM), F32)],
        compiler_params=pltpu.CompilerParams(dimension_semantics=("arbitrary",),
                                             vmem_limit_bytes=VMEM_LIMIT_BYTES),
        name="mlstm",
    )(za, gates_t, conv_w, norm_g)


def _mlstm_gate_rows(zg):
    b, s, _ = zg.shape
    gates = zg[:, :, 0:2 * HEADS].reshape(b, s // M_CHUNK, M_CHUNK, 2, HEADS)
    return jnp.transpose(gates, (0, 1, 3, 4, 2)).reshape(b, s // M_CHUNK, 2, HEADS * M_CHUNK)


def _dilated_kernel(q0_ref, q1_ref, k0_ref, k1_ref, v0_ref, v1_ref, bias_ref, o_ref, acc_ref, mx_ref, den_ref):
    seq = q0_ref.shape[1]
    W = BLK
    hd = HEAD_DIM
    npair = HEADS // 2
    scale = hd ** -0.5
    assert math.log2(scale).is_integer()
    nk = 2 * W
    qi = _iota((W, 2 * nk), 0)
    ki = _iota((W, 2 * nk), 1) & (nk - 1)
    j = W + qi - ki
    band = (j >= 0) & (j <= W)
    head_of_lane = _iota((nk, 2 * hd), 1) >> int(math.log2(hd))
    ones_bd = jnp.concatenate([jnp.where(head_of_lane == hh, 1.0, 0.0) for hh in range(2)], axis=0).astype(BF16)
    q_refs, k_refs, v_refs = (q0_ref, q1_ref), (k0_ref, k1_ref), (v0_ref, v1_ref)

    def block_diag(x):
        return jnp.concatenate([jnp.where(head_of_lane == hh, x, 0.0) for hh in range(2)], axis=0).astype(BF16)

    for br, (window, dil) in enumerate(DIL_PATTERNS):
        assert window // dil == W
        nb = (seq // dil) // W

        def piece(idx, _, br=br, dil=dil, nb=nb):
            n = idx % nb
            if dil == 1:
                rows_q = pl.ds(pl.multiple_of(W * n, W), W)
                rows_p = pl.ds(pl.multiple_of(W * jnp.maximum(n - 1, 0), W), W)
            else:
                r = idx // nb
                rows_q = pl.ds(r + dil * W * n, W, stride=dil)
                rows_p = pl.ds(r + dil * W * jnp.maximum(n - 1, 0), W, stride=dil)
            mask = band & (ki >= jnp.where(n > 0, 0, W))
            for pr in range(npair):
                q = (q_refs[pr][0, rows_q, :] * scale).astype(BF16)
                k2 = jnp.concatenate([k_refs[pr][0, rows_p, :], k_refs[pr][0, rows_q, :]], axis=0)
                v2 = jnp.concatenate([v_refs[pr][0, rows_p, :], v_refs[pr][0, rows_q, :]], axis=0)
                lg = jnp.where(mask, _nt_dot(q, block_diag(k2)) + bias_ref[br, pr], NEG)
                ps, mxs = [], []
                for hh in range(2):
                    sl = slice(nk * hh, nk * (hh + 1))
                    m = jnp.max(lg[:, sl], axis=-1, keepdims=True)
                    ps.append(jnp.where(mask[:, sl], jnp.exp(lg[:, sl] - m), 0.0))
                    mxs.append(jnp.broadcast_to(m, (W, hd)))
                p = jnp.concatenate(ps, axis=-1).astype(BF16)
                pv = _dot(p, jnp.concatenate([block_diag(v2), ones_bd], axis=1))
                acc_ref[br, pr, rows_q, :] = pv[:, 0:2 * hd]
                den_ref[br, pr, rows_q, :] = pv[:, 2 * hd:]
                mx_ref[br, pr, rows_q, :] = jnp.concatenate(mxs, axis=-1)
            return 0

        lax.fori_loop(0, dil * nb, piece, 0, unroll=2)

    def combine(i, _):
        rows = pl.ds(pl.multiple_of(i * W, W), W)
        outs = []
        for pr in range(npair):
            ms = [mx_ref[b, pr, rows, :] for b in range(len(DIL_PATTERNS))]
            top = functools.reduce(jnp.maximum, ms)
            es = [jnp.exp(m - top) for m in ms]
            num = sum(e * acc_ref[b, pr, rows, :] for b, e in enumerate(es))
            den = sum(e * jnp.maximum(den_ref[b, pr, rows, :], 1e-30) for b, e in enumerate(es))
            outs.append(num / den)
        o_ref[0, rows, :] = jnp.concatenate(outs, axis=-1)
        return 0

    lax.fori_loop(0, seq // W, combine, 0)


def _dilated(zb, bias):
    b, s, wb = zb.shape
    nbr = len(DIL_PATTERNS)
    pair_spec = lambda c: pl.BlockSpec((1, s, LANES), lambda i: (i, 0, c))
    return pl.pallas_call(
        _dilated_kernel,
        grid=(b,),
        in_specs=[pair_spec(c) for c in range(wb // LANES)]
                 + [pl.BlockSpec(bias.shape, lambda i: (0, 0, 0, 0))],
        out_specs=pl.BlockSpec((1, s, GROUP_WIDTH), lambda i: (i, 0, 0)),
        out_shape=jax.ShapeDtypeStruct((b, s, GROUP_WIDTH), F32),
        scratch_shapes=[pltpu.VMEM((nbr, HEADS // 2, s, LANES), F32) for _ in range(3)],
        compiler_params=pltpu.CompilerParams(dimension_semantics=("arbitrary",),
                                             vmem_limit_bytes=VMEM_LIMIT_BYTES),
        name="dilated",
    )(*([zb] * (wb // LANES)), bias)


def _head_columns(qt):
    return jnp.concatenate([qt[HEAD_DIM * h:HEAD_DIM * (h + 1), :] for h in range(HEADS)], axis=1)


def _scaled_query_columns(qt):
    scale = HEAD_DIM ** -0.5
    assert math.log2(scale).is_integer()
    return (_head_columns(qt) * scale).astype(BF16)


def _token_major_bf16(per_head_t):
    ot = jnp.concatenate(per_head_t, axis=0).astype(BF16)
    nq = ot.shape[1]
    eye = jnp.where(_iota((nq, nq), 0) == _iota((nq, nq), 1), 1.0, 0.0).astype(BF16)
    return _nt_dot(eye, ot).astype(BF16)


def _online_step(lg, mask, vt, m, den, acc):
    lg = jnp.where(mask, lg, NEG)
    m_new = jnp.maximum(m, jnp.max(lg, axis=0, keepdims=True))
    p = jnp.where(mask, jnp.exp(lg - m_new), 0.0)
    corr = jnp.exp(m - m_new)
    den = den * corr + jnp.sum(p, axis=0, keepdims=True)
    acc = acc * corr + _dot(vt, p.astype(BF16))
    return m_new, den, acc


def _pair_bias(btab_ref, behind):
    return jnp.concatenate([btab_ref[jnp.minimum(behind, 2)], btab_ref[jnp.clip(behind - 1, 0, 2)]], axis=0)


def _online_init(nq=BLK):
    return (tuple(jnp.full((1, nq), NEG, F32) for _ in range(HEADS)),
            tuple(jnp.zeros((1, nq), F32) for _ in range(HEADS)),
            tuple(jnp.zeros((HEAD_DIM, nq), F32) for _ in range(HEADS)))


def _dsa_kernel(z_ref, qt_ref, vt_ref, btab_ref, o_ref, key_ref, hi_ref, lo_ref, lom_ref):
    i = pl.program_id(1)
    seq = z_ref.shape[1]
    nq = QRY
    row_of = ROW_GROUPS[0][1]
    hd = HEAD_DIM
    topk = min(DSA_TOPK, seq // 4)
    t0 = pl.multiple_of(i * nq, nq)

    assert nq == KEYS
    npair = i + 1

    @pl.when(i == 0)
    def _():
        key_ref[...] = jnp.full(key_ref.shape, -2 ** 31, I32)
        for half_ref in (hi_ref, lo_ref, lom_ref):
            half_ref[...] = jnp.full(half_ref.shape, I16_MIN, I16)

    assert IDX_DIM == hd and IDX_HEADS == HEADS
    ciq = _head_columns(qt_ref[row_of['c_iq']:row_of['c_iq'] + HEADS * hd, :]).astype(BF16)
    iw = qt_ref[row_of['c_iw']:row_of['c_iw'] + IDX_HEADS, :] * ((IDX_HEADS * IDX_DIM) ** -0.5)
    s_loc = lax.broadcasted_iota(I32, (KEYS, nq), 0)
    t_glob = t0 + lax.broadcasted_iota(I32, (KEYS, nq), 1)

    def score_block(kp, _):
        r0 = pl.multiple_of(kp * KEYS, KEYS)
        ik = z_ref[0, pl.ds(r0, KEYS), LANES:LANES + IDX_DIM].astype(BF16)
        rel = _dot(ik, ciq)
        sc = jnp.zeros((KEYS, nq), F32)
        for h in range(IDX_HEADS):
            sc = sc + jnp.maximum(rel[:, nq * h:nq * (h + 1)], 0.0) * iw[h:h + 1, :]
        sc = jnp.where(r0 + s_loc <= t_glob, sc, NEG)
        key = _sortable_key(sc)
        key_ref[kp] = key
        hi_ref[kp] = (key >> 16).astype(I16)
        lo_ref[kp] = ((key & 0xFFFF) - 2 ** 15).astype(I16)
        return 0

    lax.fori_loop(0, npair, score_block, 0)

    def count(ref, pred, pairs):
        dt = ref.dtype
        rows = 8 * 4 // dt.itemsize
        def body(kp, acc):
            hit = jnp.where(pred(kp, ref[kp]), jnp.ones((), dt), jnp.zeros((), dt))
            hit = hit.reshape(KEYS // rows, rows, nq)
            parts = [hit[n] for n in range(KEYS // rows)]
            while len(parts) > 1:
                parts = [a + b for a, b in zip(parts[0::2], parts[1::2])]
            return acc + parts[0]
        acc = jnp.zeros((rows, nq), dt)
        if isinstance(pairs, int):
            for kp in range(pairs):
                acc = body(kp, acc)
        else:
            acc = lax.fori_loop(0, pairs, body, acc)
        return jnp.sum(acc.astype(I32), axis=0, keepdims=True)

    def threshold(pairs):
        def half_search(ref, k):
            def bit(it, thr):
                cand = thr + lax.shift_left(jnp.int32(1), 15 - it)
                c = count(ref, lambda kp, half: half >= cand.astype(I16), pairs)
                return jnp.where(c >= k, cand, thr)
            return lax.fori_loop(0, 16, bit, jnp.full((1, nq), I16_MIN, I32))

        def run(_):
            thr_hi = half_search(hi_ref, topk)
            thr_hi16 = thr_hi.astype(I16)
            above_hi = count(hi_ref, lambda kp, half: half > thr_hi16, pairs)
            for kp in range(pairs):
                lom_ref[kp] = jnp.where(hi_ref[kp] == thr_hi16, lo_ref[kp], jnp.int16(I16_MIN))
            thr_lo = half_search(lom_ref, topk - above_hi)
            thr = (thr_hi << 16) | (thr_lo + 2 ** 15)
            n_gt = count(key_ref, lambda kp, key: key > thr, pairs)
            n_eq = count(key_ref, lambda kp, key: key == thr, pairs)
            return thr, n_gt, n_eq
        return run

    walks = list(range(2, seq // KEYS + 1, 2))
    thr, n_gt, n_eq = lax.switch((npair - 1) // 2, [threshold(p) for p in walks], 0)
    need = topk - n_gt

    def tie_search(_):
        def idx_bit(it, jm):
            cand = jm + lax.shift_left(jnp.int32(1), 10 - it)
            c = count(key_ref, lambda kp, key: (key == thr) & (kp * KEYS + s_loc < cand), npair)
            return jnp.where(c < need, cand, jm)
        return lax.fori_loop(0, 11, idx_bit, jnp.zeros((1, nq), I32))

    assert seq == 2 ** 11
    jmax = lax.cond(jnp.max(n_eq - need) > 0, tie_search,
                    lambda _: jnp.full((1, nq), seq - 1, I32), 0)

    qs = _scaled_query_columns(qt_ref[row_of['c_q']:row_of['c_q'] + HEADS * hd, :])

    def attend(kp, carry):
        ms, dens, accs = carry
        r0 = pl.multiple_of(kp * KEYS, KEYS)
        kblk = z_ref[0, pl.ds(r0, KEYS), 0:hd].astype(BF16)
        lg = _dot(kblk, qs) + btab_ref[jnp.minimum(i - kp, 2)]
        key = key_ref[kp]
        s_glob = r0 + s_loc
        mask = ((key > thr) | ((key == thr) & (s_glob <= jmax))) & (s_glob <= t_glob)
        vt = vt_ref[kp]
        out = [_online_step(lg[:, nq * h:nq * (h + 1)], mask, vt, ms[h], dens[h], accs[h])
               for h in range(HEADS)]
        return tuple(o[0] for o in out), tuple(o[1] for o in out), tuple(o[2] for o in out)

    ms, dens, accs = lax.fori_loop(0, npair, attend, _online_init(nq))
    o_ref[0] = _token_major_bf16([accs[h] / jnp.maximum(dens[h], 1e-30) for h in range(HEADS)])


def _paired_values_t(vt, seq):
    hd, t = vt.shape
    return jnp.transpose(vt.astype(BF16).reshape(hd, t // KEYS, KEYS), (1, 0, 2))


def _dsa(zc, zct, btab):
    b, s, wc = zc.shape
    npair, nqb = s // KEYS, s // QRY
    r_v = ROW_GROUPS[0][1]['c_v']
    vt = _paired_values_t(zct[r_v:r_v + HEAD_DIM], s)
    return pl.pallas_call(
        _dsa_kernel,
        grid=(b, nqb),
        in_specs=[pl.BlockSpec((1, s, wc), lambda bi, i: (bi, 0, 0)),
                  pl.BlockSpec((zct.shape[0], QRY), lambda bi, i: (0, bi * nqb + i)),
                  pl.BlockSpec((npair, HEAD_DIM, KEYS), lambda bi, i: (bi, 0, 0)),
                  pl.BlockSpec(btab.shape, lambda bi, i: (0, 0, 0))],
        out_specs=pl.BlockSpec((1, QRY, GROUP_WIDTH), lambda bi, i: (bi, i, 0)),
        out_shape=jax.ShapeDtypeStruct((b, s, GROUP_WIDTH), BF16),
        scratch_shapes=[pltpu.VMEM((npair, KEYS, QRY), I32)] + [pltpu.VMEM((npair, KEYS, QRY), I16)] * 3,
        compiler_params=pltpu.CompilerParams(dimension_semantics=("arbitrary", "arbitrary"),
                                             vmem_limit_bytes=VMEM_LIMIT_BYTES),
        name="dsa",
    )(zc, zct, vt, btab)


def _nsa_kernel(z_ref, zc_ref, qt_ref, vst_ref, vwt_ref, w1_ref, pos_ref, w2_ref, bsel_ref, bcmp_ref, ovt_ref,
                exp_ref, o_ref, cmp_ref, cmpt_ref):
    i = pl.program_id(1)
    seq = z_ref.shape[1]
    hd = HEAD_DIM
    row_of = ROW_GROUPS[1][1]
    n_cmp = (seq - NSA_CMP_LEN) // NSA_CMP_STRIDE + 1
    n_sel = seq // NSA_SEL_LEN
    topn = min(NSA_TOPN, n_sel)
    half = NSA_CMP_LEN // 2
    assert half == NSA_CMP_STRIDE and n_cmp + 1 == seq // NSA_CMP_STRIDE == BLK and n_sel <= BLK
    nq = QRY
    assert nq == KEYS
    t0 = pl.multiple_of(i * nq, nq)
    hi = lax.Precision.HIGHEST

    @pl.when(i == 0)
    def _():
        first = jnp.zeros((BLK, 2 * NSA_CMP_HIDDEN), F32)
        second = jnp.zeros((BLK, 2 * NSA_CMP_HIDDEN), F32)
        for j in range(half):
            xj = zc_ref[0, pl.ds(j, BLK, stride=NSA_CMP_STRIDE), :]
            first = first + _dot((xj + pos_ref[j:j + 1, :]).astype(BF16), w1_ref[j])
            second = second + _dot((xj + pos_ref[half + j:half + j + 1, :]).astype(BF16), w1_ref[half + j])
        hid = first + pltpu.roll(second, BLK - 1, 0)
        hid = hid * _sigmoid(hid)
        cmp = _dot(hid.astype(BF16), w2_ref[...])
        cmp_ref[...] = cmp
        cmpt_ref[...] = cmp.T

    qs = _scaled_query_columns(qt_ref[row_of['d_q']:row_of['d_q'] + HEADS * hd, :])
    gates = _sigmoid(qt_ref[row_of['d_g']:row_of['d_g'] + ROW_TILE, :])
    row = lax.broadcasted_iota(I32, (BLK, nq), 0)
    t_glob = t0 + lax.broadcasted_iota(I32, (BLK, nq), 1)

    kcmp = cmp_ref[:, 0:hd].astype(BF16)
    vcmpt = cmpt_ref[hd:2 * hd, :].astype(BF16)
    lgc = _dot(kcmp, qs) + bcmp_ref[0]
    mask_c = (t_glob - (row * NSA_CMP_STRIDE + NSA_CMP_LEN - 1) >= 0) & (row < n_cmp)
    o_cmp, psum = [], jnp.zeros((BLK, nq), F32)
    for h in range(HEADS):
        lg = jnp.where(mask_c, lgc[:, nq * h:nq * (h + 1)], NEG)
        m = jnp.max(lg, axis=0, keepdims=True)
        p = jnp.where(mask_c, jnp.exp(lg - m), 0.0)
        p = p / jnp.maximum(jnp.sum(p, axis=0, keepdims=True), 1e-30)
        o_cmp.append(_dot(vcmpt, p.astype(BF16)))
        psum = psum + p

    imp = _dot(ovt_ref[...], psum, precision=hi)
    cur = t_glob >> int(math.log2(NSA_SEL_LEN))
    forced = (row == 0) | (row == cur) | (row == cur - 1)
    imp = jnp.where(forced, NSA_FORCE, imp)
    imp = jnp.where(row * NSA_SEL_LEN <= t_glob, imp, NEG)
    imp = imp[0:n_sel, :]
    jrow = lax.broadcasted_iota(I32, (n_sel, nq), 0)
    rank = jnp.zeros((n_sel, nq), I32)
    for jp in range(n_sel):
        other = imp[jp:jp + 1, :]
        rank = rank + ((other > imp) | ((other == imp) & (jp < jrow))).astype(I32)
    chosen = jnp.where(rank < topn, 1.0, 0.0)
    chosen = jnp.concatenate([chosen, jnp.zeros((BLK - n_sel, nq), F32)], axis=0).astype(BF16)

    s_loc = lax.broadcasted_iota(I32, (KEYS, nq), 0)
    t_keys = t0 + lax.broadcasted_iota(I32, (KEYS, nq), 1)

    def attend(vt_ref, lanes, mask_fn):
        def body(kp, carry):
            ms, dens, accs = carry
            r0 = pl.multiple_of(kp * KEYS, KEYS)
            kblk = z_ref[0, pl.ds(r0, KEYS), lanes].astype(BF16)
            lg = _dot(kblk, qs) + bsel_ref[jnp.minimum(i - kp, 2)]
            mask = mask_fn(r0)
            vt = vt_ref[kp]
            out = [_online_step(lg[:, nq * h:nq * (h + 1)], mask, vt, ms[h], dens[h], accs[h])
                   for h in range(HEADS)]
            return tuple(o[0] for o in out), tuple(o[1] for o in out), tuple(o[2] for o in out)
        return body

    def mask_sel(r0):
        picked = _dot(exp_ref[pl.ds(r0, KEYS), :], chosen) > 0.5
        return picked & (r0 + s_loc <= t_keys)

    def mask_win(r0):
        dist = t_keys - (r0 + s_loc)
        return (dist >= 0) & (dist < NSA_WINDOW)

    ks_lanes, kw_lanes = slice(LANES, LANES + hd), slice(2 * LANES, 2 * LANES + hd)
    _, den_s, acc_s = lax.fori_loop(0, i + 1, attend(vst_ref, ks_lanes, mask_sel), _online_init(nq))
    first_w = jnp.maximum(i - NSA_WINDOW // KEYS, 0)
    _, den_w, acc_w = lax.fori_loop(first_w, i + 1, attend(vwt_ref, kw_lanes, mask_win), _online_init(nq))

    outs = []
    for h in range(HEADS):
        o_s = acc_s[h] / jnp.maximum(den_s[h], 1e-30)
        o_w = acc_w[h] / jnp.maximum(den_w[h], 1e-30)
        outs.append(gates[3 * h:3 * h + 1, :] * o_cmp[h] + gates[3 * h + 1:3 * h + 2, :] * o_s
                    + gates[3 * h + 2:3 * h + 3, :] * o_w)
    o_ref[0] = _token_major_bf16(outs)


def _nsa(zd, zdt, w1, pos, w2, bsel, bcmp, ovt, expand):
    b, s, wd = zd.shape
    npair, nqb = s // KEYS, s // QRY
    row_of = ROW_GROUPS[1][1]
    vst = _paired_values_t(zdt[row_of['d_vs']:row_of['d_vs'] + HEAD_DIM], s)
    vwt = _paired_values_t(zdt[row_of['d_vw']:row_of['d_vw'] + HEAD_DIM], s)
    full = lambda a: pl.BlockSpec(a.shape, lambda bi, i: (0,) * a.ndim)
    pair_spec = pl.BlockSpec((npair, HEAD_DIM, KEYS), lambda bi, i: (bi, 0, 0))
    return pl.pallas_call(
        _nsa_kernel,
        grid=(b, nqb),
        in_specs=[pl.BlockSpec((1, s, wd), lambda bi, i: (bi, 0, 0)),
                  pl.BlockSpec((1, s, LANES), lambda bi, i: (bi, 0, 0)),
                  pl.BlockSpec((zdt.shape[0], QRY), lambda bi, i: (0, bi * nqb + i)),
                  pair_spec, pair_spec,
                  full(w1), full(pos), full(w2), full(bsel),
                  pl.BlockSpec((1, BLK, HEADS * QRY), lambda bi, i: (i, 0, 0)),
                  full(ovt), full(expand)],
        out_specs=pl.BlockSpec((1, QRY, GROUP_WIDTH), lambda bi, i: (bi, i, 0)),
        out_shape=jax.ShapeDtypeStruct((b, s, GROUP_WIDTH), BF16),
        scratch_shapes=[pltpu.VMEM((BLK, BLK), F32), pltpu.VMEM((BLK, BLK), F32)],
        compiler_params=pltpu.CompilerParams(dimension_semantics=("arbitrary", "arbitrary"),
                                             vmem_limit_bytes=VMEM_LIMIT_BYTES),
        name="nsa",
    )(zd, zd, zdt, vst, vwt, w1, pos, w2, bsel, bcmp, ovt, expand)


def _outproj_kernel(alpha, x_ref, a_ref, b_ref, c_ref, d_ref, w_ref, bo_ref, g_ref, beta_ref, o_ref):
    acc = bo_ref[...] + _dot(a_ref[...].astype(BF16), w_ref[0:GROUP_WIDTH, :])
    for n, m_ref in enumerate((b_ref, c_ref, d_ref), start=1):
        acc = acc + _dot(m_ref[...].astype(BF16), w_ref[n * GROUP_WIDTH:(n + 1) * GROUP_WIDTH, :])
    o_ref[...] = _layer_norm_rows(alpha * x_ref[...] + acc, g_ref[...], beta_ref[...])


def _outproj(alpha, x2d, mixed, w, bo, g, beta, tm=512):
    t = x2d.shape[0]
    row_spec = lambda width: pl.BlockSpec((tm, width), lambda i: (i, 0))
    const = lambda a: pl.BlockSpec(a.shape, lambda i: (0, 0))
    return pl.pallas_call(
        functools.partial(_outproj_kernel, alpha),
        grid=(t // tm,),
        in_specs=[row_spec(D_MODEL)] + [row_spec(GROUP_WIDTH)] * N_MIXERS
                 + [const(w), const(bo), const(g), const(beta)],
        out_specs=row_spec(D_MODEL),
        out_shape=jax.ShapeDtypeStruct((t, D_MODEL), F32),
        compiler_params=pltpu.CompilerParams(dimension_semantics=("arbitrary",),
                                             vmem_limit_bytes=VMEM_LIMIT_BYTES),
        name="outproj_ln",
    )(x2d, *mixed, w, bo, g, beta)


def _ffn_kernel(alpha, x_ref, w1_ref, b1_ref, w2_ref, b2_ref, g_ref, beta_ref, o_ref, xb_ref, acc_ref):
    j = pl.program_id(1)

    @pl.when(j == 0)
    def _():
        xb_ref[...] = x_ref[...].astype(BF16)
        acc_ref[...] = jnp.zeros_like(acc_ref)

    hdn = jnp.maximum(_dot(xb_ref[...], w1_ref[...]) + b1_ref[...], 0.0)
    acc_ref[...] += _dot(jnp.square(hdn).astype(BF16), w2_ref[...])

    @pl.when(j == pl.num_programs(1) - 1)
    def _():
        y = alpha * x_ref[...] + (acc_ref[...] + b2_ref[...])
        o_ref[...] = _layer_norm_rows(y, g_ref[...], beta_ref[...])


def _ffn(alpha, x2d, w1, b1, w2, b2, g, beta, tm=1024, tf=1024):
    t = x2d.shape[0]
    return pl.pallas_call(
        functools.partial(_ffn_kernel, alpha),
        grid=(t // tm, D_FF // tf),
        in_specs=[pl.BlockSpec((tm, D_MODEL), lambda i, j: (i, 0)),
                  pl.BlockSpec((D_MODEL, tf), lambda i, j: (0, j)),
                  pl.BlockSpec((1, tf), lambda i, j: (0, j)),
                  pl.BlockSpec((tf, D_MODEL), lambda i, j: (j, 0)),
                  pl.BlockSpec((1, D_MODEL), lambda i, j: (0, 0)),
                  pl.BlockSpec((1, D_MODEL), lambda i, j: (0, 0)),
                  pl.BlockSpec((1, D_MODEL), lambda i, j: (0, 0))],
        out_specs=pl.BlockSpec((tm, D_MODEL), lambda i, j: (i, 0)),
        out_shape=jax.ShapeDtypeStruct((t, D_MODEL), F32),
        scratch_shapes=[pltpu.VMEM((tm, D_MODEL), BF16), pltpu.VMEM((tm, D_MODEL), F32)],
        compiler_params=pltpu.CompilerParams(dimension_semantics=("arbitrary", "arbitrary"),
                                             vmem_limit_bytes=VMEM_LIMIT_BYTES),
        name="ffn_ln",
    )(x2d, w1, b1, w2, b2, g, beta)


def _bias_of_distance(rel_bias_heads, dist):
    onehot = np.eye(NUM_BUCKETS, dtype=np.float32)[_t5_bucket_np(np.asarray(dist))]
    return jnp.dot(rel_bias_heads.T, jnp.asarray(onehot.T), precision=lax.Precision.HIGHEST)


def _shifted_rows(v, n_rows, n_cols, step):
    period = v.shape[-1]
    assert n_cols <= period - step
    flat = jnp.tile(v, (1,) * (v.ndim - 1) + (n_rows,))[..., :n_rows * (period - step)]
    return flat.reshape(v.shape[:-1] + (n_rows, period - step))[..., :n_cols]


def _wrapped(period):
    idx = np.arange(period)
    return np.where(idx < period // 2, idx, idx - period)


def _dilated_bias(rel_bias):
    x = _wrapped(4 * BLK)
    tabs = [_shifted_rows(_bias_of_distance(rel_bias[:, 0:HEADS], (BLK - x) * dil), BLK, 2 * BLK, 1)
            for _, dil in DIL_PATTERNS]
    tabs = jnp.stack(tabs).reshape(len(DIL_PATTERNS), HEADS // 2, 2, BLK, 2 * BLK)
    return jnp.transpose(tabs, (0, 1, 3, 2, 4)).reshape(len(DIL_PATTERNS), HEADS // 2, BLK, 4 * BLK)


def _toeplitz_bias_t(rel_bias_heads, blk=BLK):
    assert (_t5_bucket_np(np.arange(blk + 1, 64 * blk)) == NUM_BUCKETS - 1).all()
    x = _wrapped(2 * blk)
    tabs = [_shifted_rows(_bias_of_distance(rel_bias_heads, blk * delta + x), blk, blk, 1)
            for delta in range(3)]
    return jnp.transpose(jnp.stack(tabs), (0, 2, 1, 3)).reshape(3, blk, HEADS * blk)


def _compressed_bias_t(rel_bias_heads, seq):
    nq = seq // QRY
    x = _wrapped(2 * seq + BLK)
    v = _bias_of_distance(rel_bias_heads, x - (NSA_CMP_LEN - 1))
    tab = _shifted_rows(v, BLK, seq, NSA_CMP_STRIDE)
    tab = tab.reshape(HEADS, BLK, nq, QRY)
    return jnp.transpose(tab, (2, 1, 0, 3)).reshape(nq, BLK, HEADS * QRY)


def _nsa_constants(seq):
    n_cmp = (seq - NSA_CMP_LEN) // NSA_CMP_STRIDE + 1
    n_sel = seq // NSA_SEL_LEN
    cs = np.arange(n_cmp)[:, None] * NSA_CMP_STRIDE
    ss = np.arange(n_sel)[None, :] * NSA_SEL_LEN
    ov = np.clip(np.minimum(cs + NSA_CMP_LEN, ss + NSA_SEL_LEN) - np.maximum(cs, ss), 0, None) / NSA_CMP_LEN
    ovt = np.zeros((BLK, BLK), np.float32)
    ovt[:n_sel, :n_cmp] = ov.T
    expand = np.zeros((seq, BLK), np.float32)
    expand[np.arange(seq), np.arange(seq) // NSA_SEL_LEN] = 1.0
    return jnp.asarray(ovt), jnp.asarray(expand, BF16)


def _nsa_weights(cmp_pos, cmp_w1, cmp_w2):
    hd, hid = HEAD_DIM, NSA_CMP_HIDDEN
    w1 = cmp_w1.reshape(2, NSA_CMP_LEN, hd, hid)
    zeros = jnp.zeros((NSA_CMP_LEN, hd, hid), F32)
    w1 = jnp.concatenate([jnp.concatenate([w1[0], zeros], axis=-1),
                          jnp.concatenate([zeros, w1[1]], axis=-1)], axis=1)
    pos = jnp.concatenate([cmp_pos[0], cmp_pos[1]], axis=-1)
    z2 = jnp.zeros((hid, hd), F32)
    w2 = jnp.concatenate([jnp.concatenate([cmp_w2[0], z2], axis=-1),
                          jnp.concatenate([z2, cmp_w2[1]], axis=-1)], axis=0)
    return w1.astype(BF16), pos, w2.astype(BF16)


def kernel(x, w_in, b_in, a_conv, a_norm, d_cmp_pos, d_cmp_w1, d_cmp_w2, w_out, b_out, ln1_g, ln1_b,
           w_ff1, b_ff1, w_ff2, b_ff2, ln2_g, ln2_b, rel_bias):
    bsz, seq, _ = x.shape
    depth = w_in.shape[0]
    alpha = (2 * depth) ** 0.25
    nc = seq // M_CHUNK
    bias_dil = _dilated_bias(rel_bias)
    btab_dsa = _toeplitz_bias_t(rel_bias[:, HEADS:2 * HEADS], QRY)
    btab_nsa = _toeplitz_bias_t(rel_bias[:, 2 * HEADS:3 * HEADS], QRY)
    bcmp_nsa = _compressed_bias_t(rel_bias[:, 2 * HEADS:3 * HEADS], seq)
    ovt, expand = _nsa_constants(seq)

    h = x.reshape(bsz * seq, D_MODEL)
    for l in range(depth):
        w_l = _permute_columns(w_in[l]).astype(BF16)
        b_l = _permute_columns(b_in[l])[None, :]
        wt_l = _transposed_rows(w_in[l]).T.astype(BF16)
        bt_l = _transposed_rows(b_in[l])[:, None]
        proj = _inproj(h, w_l, b_l, wt_l, bt_l)
        za, zb, zc, zd, zg = (z.reshape(bsz, seq, -1) for z in proj[:len(GROUP_WIDTHS)])
        zct, zdt = proj[len(GROUP_WIDTHS):]
        out_a = _mlstm(za, _mlstm_gate_rows(zg), a_conv[l], a_norm[l][None, :])
        out_b = _dilated(zb, bias_dil)
        out_c = _dsa(zc, zct, btab_dsa)
        nsa_w1, nsa_pos, nsa_w2 = _nsa_weights(d_cmp_pos[l], d_cmp_w1[l], d_cmp_w2[l])
        out_d = _nsa(zd, zdt, nsa_w1, nsa_pos, nsa_w2, btab_nsa, bcmp_nsa, ovt, expand)
        mixed = [o.reshape(bsz * seq, GROUP_WIDTH) for o in (out_a, out_b, out_c, out_d)]
        h = _outproj(alpha, h, mixed, w_out[l].astype(BF16), b_out[l][None, :],
                     ln1_g[l][None, :], ln1_b[l][None, :])
        h = _ffn(alpha, h, w_ff1[l].astype(BF16), b_ff1[l][None, :], w_ff2[l].astype(BF16),
                 b_ff2[l][None, :], ln2_g[l][None, :], ln2_b[l][None, :])
    return h.reshape(bsz, seq, D_MODEL)
```

```python
import functools
import math

import numpy as np
import jax
import jax.numpy as jnp
from jax import lax
from jax.experimental import pallas as pl
from jax.experimental.pallas import tpu as pltpu

F32 = jnp.float32
BF16 = jnp.bfloat16
I32 = jnp.int32
I16 = jnp.int16
I16_MIN = -2 ** 15

D_MODEL = 1024
N_MIXERS = 4
HEADS = 4
HEAD_DIM = D_MODEL // (N_MIXERS * HEADS)
GROUP_WIDTH = HEADS * HEAD_DIM
D_FF = 4 * D_MODEL
LN_EPS = 1e-5
NEG = -1e30

M_QK_DIM = HEAD_DIM // 2
M_CHUNK = 64
M_CONV = 4
DIL_PATTERNS = ((128, 1), (512, 4), (2048, 16))
IDX_HEADS = 4
IDX_DIM = 64
DSA_TOPK = 256
NSA_CMP_LEN = 32
NSA_CMP_STRIDE = 16
NSA_SEL_LEN = 64
NSA_TOPN = 16
NSA_WINDOW = 512
NSA_CMP_HIDDEN = 256
NSA_FORCE = 1e9
NUM_BUCKETS = 32
MAX_DISTANCE = 128

LANES = 128
BLK = 128
KEYS = 2 * BLK
QRY = KEYS
VMEM_LIMIT_BYTES = 56 * 1024 * 1024

IN_SPLITS = (
    ('a_q', HEADS * M_QK_DIM), ('a_k', HEADS * M_QK_DIM), ('a_v', GROUP_WIDTH),
    ('a_i', HEADS), ('a_f', HEADS), ('a_o', GROUP_WIDTH),
    ('b_q', GROUP_WIDTH), ('b_k', GROUP_WIDTH), ('b_v', GROUP_WIDTH),
    ('c_q', GROUP_WIDTH), ('c_k', HEAD_DIM), ('c_v', HEAD_DIM),
    ('c_iq', IDX_HEADS * IDX_DIM), ('c_ik', IDX_DIM), ('c_iw', IDX_HEADS),
    ('d_q', GROUP_WIDTH), ('d_kc', HEAD_DIM), ('d_vc', HEAD_DIM),
    ('d_ks', HEAD_DIM), ('d_vs', HEAD_DIM), ('d_kw', HEAD_DIM), ('d_vw', HEAD_DIM),
    ('d_g', 3 * HEADS),
)

GROUP_LAYOUT = (
    (('a_q', 'a_k'), ('a_v',), ('a_o',), (('a_i', HEAD_DIM),), (('a_i', M_QK_DIM),),
     (('a_f', HEAD_DIM),), (('a_f', M_QK_DIM),)),
    (('b_q',), ('b_k',), ('b_v',)),
    (('c_q',), ('c_iq',), ('c_k', 'c_v'), ('c_ik', 'c_iw')),
    (('d_q',), ('d_kc', 'd_vc'), ('d_ks', 'd_vs'), ('d_kw', 'd_vw'), ('d_g',)),
    (('a_i', 'a_f'),),
)


def _round_up(n, m):
    return -(-n // m) * m


def _projection_layout():
    offs, off = {}, 0
    for name, width in IN_SPLITS:
        offs[name] = (off, width)
        off += width
    runs, group_widths = [], []
    for group in GROUP_LAYOUT:
        gwidth = 0
        for chunk in group:
            cwidth = 0
            for entry in chunk:
                name, rep = entry if isinstance(entry, tuple) else (entry, 1)
                o, w = offs[name]
                runs.append((o, w, rep))
                cwidth += w * rep
            pad = _round_up(cwidth, LANES) - cwidth
            if pad:
                runs.append((-1, pad, 1))
            gwidth += cwidth + pad
        group_widths.append(gwidth)
    return tuple(runs), tuple(group_widths)


PROJ_RUNS, GROUP_WIDTHS = _projection_layout()
PROJ_WIDTH = int(sum(GROUP_WIDTHS))
PROJ_PERM = np.concatenate([np.repeat(np.arange(o, o + w), r) if o >= 0 else np.full(w, -1)
                            for o, w, r in PROJ_RUNS]).astype(np.int32)


def _permute_columns(a):
    parts = []
    for o, w, r in PROJ_RUNS:
        if o < 0:
            parts.append(jnp.zeros(a.shape[:-1] + (w,), a.dtype))
        else:
            parts.append(a[..., o:o + w] if r == 1 else jnp.repeat(a[..., o:o + w], r, axis=-1))
    return jnp.concatenate(parts, axis=-1)


def _t5_bucket_np(dist):
    n = np.maximum(dist, 0)
    max_exact = NUM_BUCKETS // 2
    nf = np.maximum(n, max_exact).astype(np.float32)
    large = max_exact + (np.log(nf / max_exact) / math.log(MAX_DISTANCE / max_exact)
                         * (NUM_BUCKETS - max_exact)).astype(np.int32)
    large = np.minimum(large, NUM_BUCKETS - 1)
    return np.where(n < max_exact, n, large).astype(np.int32)


def _nt_dot(a, b, precision=None):
    return lax.dot_general(a, b, (((1,), (1,)), ((), ())), precision=precision,
                           preferred_element_type=F32)


def _dot(a, b, precision=None):
    return jnp.dot(a, b, precision=precision, preferred_element_type=F32)


def _layer_norm_rows(y, g, b):
    mu = jnp.mean(y, axis=-1, keepdims=True)
    var = jnp.mean(jnp.square(y - mu), axis=-1, keepdims=True)
    return (y - mu) * lax.rsqrt(var + LN_EPS) * g + b


def _sigmoid(x):
    return 1.0 / (1.0 + jnp.exp(-x))


def _log_sigmoid(x):
    return -(jnp.maximum(-x, 0.0) + jnp.log1p(jnp.exp(-jnp.abs(x))))


def _sortable_key(x):
    bits = pltpu.bitcast(x, I32)
    return bits ^ ((bits >> 31) & jnp.int32(0x7FFFFFFF))


def _inproj_kernel(x_ref, w_ref, b_ref, *out_refs):
    xb = x_ref[...].astype(BF16)
    off = 0
    for o_ref, width in zip(out_refs, GROUP_WIDTHS):
        o_ref[...] = _dot(xb, w_ref[:, off:off + width]) + b_ref[:, off:off + width]
        off += width


def _inproj(x2d, w, b, tm=512):
    t = x2d.shape[0]
    return pl.pallas_call(
        _inproj_kernel,
        grid=(t // tm,),
        in_specs=[pl.BlockSpec((tm, D_MODEL), lambda i: (i, 0)),
                  pl.BlockSpec((D_MODEL, PROJ_WIDTH), lambda i: (0, 0)),
                  pl.BlockSpec((1, PROJ_WIDTH), lambda i: (0, 0))],
        out_specs=[pl.BlockSpec((tm, gw), lambda i: (i, 0)) for gw in GROUP_WIDTHS],
        out_shape=[jax.ShapeDtypeStruct((t, gw), F32) for gw in GROUP_WIDTHS],
        compiler_params=pltpu.CompilerParams(dimension_semantics=("arbitrary",),
                                             vmem_limit_bytes=VMEM_LIMIT_BYTES),
        name="inproj",
    )(x2d, w, b)


def _split_terms(x, n):
    terms, rest = [], x
    for _ in range(n):
        terms.append(rest.astype(BF16))
        rest = rest - terms[-1].astype(F32)
    return terms


def _iota(shape, dim):
    return lax.broadcasted_iota(I32, shape, dim)


def _mlstm_kernel(z_ref, gt_ref, cw_ref, ng_ref, o_ref, xpad_ref):
    seq = z_ref.shape[1]
    L, DK, DV, H = M_CHUNK, M_QK_DIM, HEAD_DIM, HEADS
    assert L == DV
    wq, wv = H * DK, H * DV
    lg_dk, lg_dv = int(math.log2(DK)), int(math.log2(DV))
    c_v, c_o, c_i64 = 2 * wq, 2 * wq + wv, 2 * wq + 2 * wv
    c_i32, c_f64 = c_i64 + wv, c_i64 + wv + wq
    xpad_ref[0:8, :] = jnp.zeros((8, 2 * wq), F32)
    xpad_ref[8:, :] = z_ref[0, :, 0:2 * wq]

    one_if = lambda cond: jnp.where(cond, 1.0, 0.0).astype(BF16)
    tri_l = one_if(_iota((L, L), 0) >= _iota((L, L), 1))
    trow = _iota((L, wv), 0)
    tri_heads = trow >= (_iota((L, wv), 1) & (L - 1))
    r_vv, c_vv = _iota((wv, wv), 0), _iota((wv, wv), 1)
    same_head = (r_vv >> lg_dv) == (c_vv >> lg_dv)
    ones_bd = one_if(same_head)
    mean_bd = jnp.where(same_head, 1.0 / DV, 0.0).astype(BF16)
    tri_u_bd = one_if(same_head & ((r_vv & (L - 1)) <= (c_vv & (L - 1))))
    state_mask = (_iota((wq, wv), 0) >> lg_dk) == (_iota((wq, wv), 1) >> lg_dv)
    eye_q = one_if(_iota((wq, wq), 0) == _iota((wq, wq), 1))
    head_of_qlane = _iota((L, wq), 1) >> lg_dk
    head_of_vlane = _iota((L, wv), 1) >> lg_dv
    row8 = _iota((8, wv), 0)
    cw = cw_ref[...]
    ng = ng_ref[...]

    def head_mean(x):
        hi_lo = _split_terms(x, 2)
        r = _dot(jnp.concatenate(hi_lo, axis=0), mean_bd)
        return r[0:L] + r[L:2 * L]

    def chunk(c, carry):
        cbd, nbd, m64, m32 = carry
        s0 = pl.multiple_of(c * L, L)
        rows = pl.ds(s0, L)
        xw = xpad_ref[pl.ds(s0, L + 8), :]
        y = sum(cw[j:j + 1, :] * xw[5 + j:5 + j + L, :] for j in range(M_CONV))
        qk = y * _sigmoid(y)
        q = qk[:, 0:wq]
        k = qk[:, wq:] * (DK ** -0.5)
        qb = q.astype(BF16)
        v = z_ref[0, rows, c_v:c_v + wv]
        i64 = z_ref[0, rows, c_i64:c_i64 + wv]
        i32 = z_ref[0, rows, c_i32:c_i32 + wq]
        gr = gt_ref[0, c]

        flog = _log_sigmoid(z_ref[0, rows, c_f64:c_f64 + wv + wq])
        bsum = _dot(tri_l, jnp.concatenate(_split_terms(flog, 3), axis=1))
        w3 = wv + wq
        ball = bsum[:, 0:w3] + bsum[:, w3:2 * w3] + bsum[:, 2 * w3:3 * w3]
        b64, b32 = ball[:, 0:wv], ball[:, wv:w3]
        fterms = [t.astype(F32) for t in _split_terms(_log_sigmoid(gr[1:2, :]), 3)]
        frows = jnp.where(row8 == 0, fterms[0], jnp.where(row8 == 1, fterms[1], jnp.where(row8 == 2, fterms[2], 0.0)))
        bparts = _dot(frows.astype(BF16), tri_u_bd)
        brow = bparts[0:1, :] + bparts[1:2, :] + bparts[2:3, :]

        dall = jnp.where(tri_heads, b64 - brow + gr[0:1, :], NEG)
        cm = i64 - b64
        for sh in (1, 2, 4, 8, 16, 32):
            cm = jnp.where(trow >= sh, jnp.maximum(cm, pltpu.roll(cm, sh, 0)), cm)
        inter = b64 + m64
        m_t = jnp.maximum(inter, b64 + cm)
        kbd = jnp.concatenate([jnp.where(head_of_qlane == h, k, 0.0) for h in range(H)], axis=0).astype(BF16)
        sc = _nt_dot(qb, kbd) * jnp.exp(dall - m_t)
        wi = jnp.exp(inter - m_t)
        vb = v.astype(BF16)
        vbd = jnp.concatenate([jnp.where(head_of_vlane == h, v, 0.0) for h in range(H)], axis=0).astype(BF16)
        pv = _dot(sc.astype(BF16), jnp.concatenate([vbd, ones_bd], axis=1))
        qst = _dot(qb, jnp.concatenate([cbd, nbd], axis=1).astype(BF16))
        num = pv[:, 0:wv] + wi * qst[:, 0:wv]
        den = pv[:, wv:] + wi * qst[:, wv:]
        hh = num / jnp.maximum(jnp.abs(den), jnp.exp(-m_t))

        og = _sigmoid(z_ref[0, rows, c_o:c_o + wv]) * hh
        dev = og - head_mean(og)
        o_ref[0, rows, :] = dev * lax.rsqrt(head_mean(dev * dev) + LN_EPS) * ng

        bl64, bl32 = b64[L - 1:L, :], b32[L - 1:L, :]
        m64_new = jnp.maximum(bl64 + m64, jnp.max(bl64 - b64 + i64, axis=0, keepdims=True))
        g32 = bl32 - b32 + i32
        m32_new = jnp.maximum(bl32 + m32, jnp.max(g32, axis=0, keepdims=True))
        kw = k * jnp.exp(g32 - m32_new)
        wc = jnp.exp(bl64 + m64 - m64_new)
        kwt = _nt_dot(eye_q, kw.astype(BF16)).astype(BF16)
        upd = _dot(kwt, jnp.concatenate([vb, jnp.ones((L, wv), BF16)], axis=1))
        cbd = wc * cbd + jnp.where(state_mask, upd[:, 0:wv], 0.0)
        nbd = wc * nbd + jnp.where(state_mask, upd[:, wv:], 0.0)
        return cbd, nbd, m64_new, m32_new

    init = (jnp.zeros((wq, wv), F32), jnp.zeros((wq, wv), F32), jnp.zeros((1, wv), F32), jnp.zeros((1, wq), F32))
    lax.fori_loop(0, seq // L, chunk, init, unroll=4)


def _mlstm(za, gates_t, conv_w, norm_g):
    b, s, wa = za.shape
    nc = s // M_CHUNK
    return pl.pallas_call(
        _mlstm_kernel,
        grid=(b,),
        in_specs=[pl.BlockSpec((1, s, wa), lambda i: (i, 0, 0)),
                  pl.BlockSpec((1, nc) + gates_t.shape[2:], lambda i: (i, 0, 0, 0)),
                  pl.BlockSpec((M_CONV, 2 * HEADS * M_QK_DIM), lambda i: (0, 0)),
                  pl.BlockSpec((1, GROUP_WIDTH), lambda i: (0, 0))],
        out_specs=pl.BlockSpec((1, s, GROUP_WIDTH), lambda i: (i, 0, 0)),
        out_shape=jax.ShapeDtypeStruct((b, s, GROUP_WIDTH), F32),
        scratch_shapes=[pltpu.VMEM((s + 8, 2 * HEADS * M_QK_DIM), F32)],
        compiler_params=pltpu.CompilerParams(dimension_semantics=("arbitrary",),
                                             vmem_limit_bytes=VMEM_LIMIT_BYTES),
        name="mlstm",
    )(za, gates_t, conv_w, norm_g)


def _mlstm_gate_rows(zg):
    b, s, _ = zg.shape
    gates = zg[:, :, 0:2 * HEADS].reshape(b, s // M_CHUNK, M_CHUNK, 2, HEADS)
    return jnp.transpose(gates, (0, 1, 3, 4, 2)).reshape(b, s // M_CHUNK, 2, HEADS * M_CHUNK)


def _dilated_kernel(q0_ref, q1_ref, k0_ref, k1_ref, v0_ref, v1_ref, bias_ref, o_ref, acc_ref, mx_ref, den_ref):
    seq = q0_ref.shape[1]
    W = BLK
    hd = HEAD_DIM
    npair = HEADS // 2
    scale = hd ** -0.5
    assert math.log2(scale).is_integer()
    nk = 2 * W
    qi = _iota((W, 2 * nk), 0)
    ki = _iota((W, 2 * nk), 1) & (nk - 1)
    j = W + qi - ki
    band = (j >= 0) & (j <= W)
    head_of_lane = _iota((nk, 2 * hd), 1) >> int(math.log2(hd))
    ones_bd = jnp.concatenate([jnp.where(head_of_lane == hh, 1.0, 0.0) for hh in range(2)], axis=0).astype(BF16)
    q_refs, k_refs, v_refs = (q0_ref, q1_ref), (k0_ref, k1_ref), (v0_ref, v1_ref)

    def block_diag(x):
        return jnp.concatenate([jnp.where(head_of_lane == hh, x, 0.0) for hh in range(2)], axis=0).astype(BF16)

    for br, (window, dil) in enumerate(DIL_PATTERNS):
        assert window // dil == W
        nb = (seq // dil) // W

        def piece(idx, _, br=br, dil=dil, nb=nb):
            n = idx % nb
            if dil == 1:
                rows_q = pl.ds(pl.multiple_of(W * n, W), W)
                rows_p = pl.ds(pl.multiple_of(W * jnp.maximum(n - 1, 0), W), W)
            else:
                r = idx // nb
                rows_q = pl.ds(r + dil * W * n, W, stride=dil)
                rows_p = pl.ds(r + dil * W * jnp.maximum(n - 1, 0), W, stride=dil)
            mask = band & (ki >= jnp.where(n > 0, 0, W))
            for pr in range(npair):
                q = (q_refs[pr][0, rows_q, :] * scale).astype(BF16)
                k2 = jnp.concatenate([k_refs[pr][0, rows_p, :], k_refs[pr][0, rows_q, :]], axis=0)
                v2 = jnp.concatenate([v_refs[pr][0, rows_p, :], v_refs[pr][0, rows_q, :]], axis=0)
                lg = jnp.where(mask, _nt_dot(q, block_diag(k2)) + bias_ref[br, pr], NEG)
                ps, mxs = [], []
                for hh in range(2):
                    sl = slice(nk * hh, nk * (hh + 1))
                    m = jnp.max(lg[:, sl], axis=-1, keepdims=True)
                    ps.append(jnp.where(mask[:, sl], jnp.exp(lg[:, sl] - m), 0.0))
                    mxs.append(jnp.broadcast_to(m, (W, hd)))
                p = jnp.concatenate(ps, axis=-1).astype(BF16)
                pv = _dot(p, jnp.concatenate([block_diag(v2), ones_bd], axis=1))
                acc_ref[br, pr, rows_q, :] = pv[:, 0:2 * hd]
                den_ref[br, pr, rows_q, :] = pv[:, 2 * hd:]
                mx_ref[br, pr, rows_q, :] = jnp.concatenate(mxs, axis=-1)
            return 0

        lax.fori_loop(0, dil * nb, piece, 0, unroll=2)

    def combine(i, _):
        rows = pl.ds(pl.multiple_of(i * W, W), W)
        outs = []
        for pr in range(npair):
            ms = [mx_ref[b, pr, rows, :] for b in range(len(DIL_PATTERNS))]
            top = functools.reduce(jnp.maximum, ms)
            es = [jnp.exp(m - top) for m in ms]
            num = sum(e * acc_ref[b, pr, rows, :] for b, e in enumerate(es))
            den = sum(e * jnp.maximum(den_ref[b, pr, rows, :], 1e-30) for b, e in enumerate(es))
            outs.append(num / den)
        o_ref[0, rows, :] = jnp.concatenate(outs, axis=-1)
        return 0

    lax.fori_loop(0, seq // W, combine, 0)


def _dilated(zb, bias):
    b, s, wb = zb.shape
    nbr = len(DIL_PATTERNS)
    pair_spec = lambda c: pl.BlockSpec((1, s, LANES), lambda i: (i, 0, c))
    return pl.pallas_call(
        _dilated_kernel,
        grid=(b,),
        in_specs=[pair_spec(c) for c in range(wb // LANES)]
                 + [pl.BlockSpec(bias.shape, lambda i: (0, 0, 0, 0))],
        out_specs=pl.BlockSpec((1, s, GROUP_WIDTH), lambda i: (i, 0, 0)),
        out_shape=jax.ShapeDtypeStruct((b, s, GROUP_WIDTH), F32),
        scratch_shapes=[pltpu.VMEM((nbr, HEADS // 2, s, LANES), F32) for _ in range(3)],
        compiler_params=pltpu.CompilerParams(dimension_semantics=("arbitrary",),
                                             vmem_limit_bytes=VMEM_LIMIT_BYTES),
        name="dilated",
    )(*([zb] * (wb // LANES)), bias)


def _head_columns_t(q):
    qt = q.T
    return jnp.concatenate([qt[HEAD_DIM * h:HEAD_DIM * (h + 1), :] for h in range(HEADS)], axis=1)


def _scaled_query_columns_t(q):
    scale = HEAD_DIM ** -0.5
    assert math.log2(scale).is_integer()
    return (_head_columns_t(q) * scale).astype(BF16)


def _unstack_heads_t(per_head_t):
    halves = []
    for h in range(0, HEADS, 2):
        halves.append(jnp.concatenate([per_head_t[h], per_head_t[h + 1]], axis=0).T)
    return jnp.concatenate(halves, axis=-1)


def _online_step(lg, mask, vt, m, den, acc):
    lg = jnp.where(mask, lg, NEG)
    m_new = jnp.maximum(m, jnp.max(lg, axis=0, keepdims=True))
    p = jnp.where(mask, jnp.exp(lg - m_new), 0.0)
    corr = jnp.exp(m - m_new)
    den = den * corr + jnp.sum(p, axis=0, keepdims=True)
    acc = acc * corr + _dot(vt, p.astype(BF16))
    return m_new, den, acc


def _pair_bias(btab_ref, behind):
    return jnp.concatenate([btab_ref[jnp.minimum(behind, 2)], btab_ref[jnp.clip(behind - 1, 0, 2)]], axis=0)


def _online_init(nq=BLK):
    return (tuple(jnp.full((1, nq), NEG, F32) for _ in range(HEADS)),
            tuple(jnp.zeros((1, nq), F32) for _ in range(HEADS)),
            tuple(jnp.zeros((HEAD_DIM, nq), F32) for _ in range(HEADS)))


def _dsa_kernel(z_ref, btab_ref, o_ref, vt_ref, key_ref, hi_ref, lo_ref, lom_ref):
    i = pl.program_id(1)
    seq = z_ref.shape[1]
    nkb = seq // BLK
    nq = QRY
    hd = HEAD_DIM
    topk = min(DSA_TOPK, seq // 4)
    t0 = pl.multiple_of(i * nq, nq)

    assert nq == KEYS
    npair = i + 1

    @pl.when(i == 0)
    def _():
        for kb in range(nkb):
            vt = z_ref[0, kb * BLK:(kb + 1) * BLK, 512:640].T[hd:2 * hd, :]
            vt_ref[kb // 2, :, (kb % 2) * BLK:(kb % 2 + 1) * BLK] = vt.astype(BF16)
        key_ref[...] = jnp.full(key_ref.shape, -2 ** 31, I32)
        for half_ref in (hi_ref, lo_ref, lom_ref):
            half_ref[...] = jnp.full(half_ref.shape, I16_MIN, I16)

    zq = z_ref[0, pl.ds(t0, nq), :]
    cq = zq[:, 0:256]
    ciq = _head_columns_t(zq[:, 256:512]).astype(BF16)
    iw = zq[:, 640:768].T[IDX_DIM:IDX_DIM + IDX_HEADS, :] * ((IDX_HEADS * IDX_DIM) ** -0.5)
    s_loc = lax.broadcasted_iota(I32, (KEYS, nq), 0)
    t_glob = t0 + lax.broadcasted_iota(I32, (KEYS, nq), 1)

    def score_block(kp, _):
        r0 = pl.multiple_of(kp * KEYS, KEYS)
        ik = z_ref[0, pl.ds(r0, KEYS), 640:704].astype(BF16)
        rel = _dot(ik, ciq)
        sc = jnp.zeros((KEYS, nq), F32)
        for h in range(IDX_HEADS):
            sc = sc + jnp.maximum(rel[:, nq * h:nq * (h + 1)], 0.0) * iw[h:h + 1, :]
        sc = jnp.where(r0 + s_loc <= t_glob, sc, NEG)
        key = _sortable_key(sc)
        key_ref[kp] = key
        hi_ref[kp] = (key >> 16).astype(I16)
        lo_ref[kp] = ((key & 0xFFFF) - 2 ** 15).astype(I16)
        return 0

    lax.fori_loop(0, npair, score_block, 0)

    def count(ref, pred, pairs):
        dt = ref.dtype
        rows = 8 * 4 // dt.itemsize
        def body(kp, acc):
            hit = jnp.where(pred(kp, ref[kp]), jnp.ones((), dt), jnp.zeros((), dt))
            hit = hit.reshape(KEYS // rows, rows, nq)
            parts = [hit[n] for n in range(KEYS // rows)]
            while len(parts) > 1:
                parts = [a + b for a, b in zip(parts[0::2], parts[1::2])]
            return acc + parts[0]
        acc = jnp.zeros((rows, nq), dt)
        if isinstance(pairs, int):
            for kp in range(pairs):
                acc = body(kp, acc)
        else:
            acc = lax.fori_loop(0, pairs, body, acc)
        return jnp.sum(acc.astype(I32), axis=0, keepdims=True)

    def threshold(pairs):
        def half_search(ref, k):
            def bit(it, thr):
                cand = thr + lax.shift_left(jnp.int32(1), 15 - it)
                c = count(ref, lambda kp, half: half >= cand.astype(I16), pairs)
                return jnp.where(c >= k, cand, thr)
            return lax.fori_loop(0, 16, bit, jnp.full((1, nq), I16_MIN, I32))

        def run(_):
            thr_hi = half_search(hi_ref, topk)
            thr_hi16 = thr_hi.astype(I16)
            above_hi = count(hi_ref, lambda kp, half: half > thr_hi16, pairs)
            for kp in range(pairs):
                lom_ref[kp] = jnp.where(hi_ref[kp] == thr_hi16, lo_ref[kp], jnp.int16(I16_MIN))
            thr_lo = half_search(lom_ref, topk - above_hi)
            thr = (thr_hi << 16) | (thr_lo + 2 ** 15)
            n_gt = count(key_ref, lambda kp, key: key > thr, pairs)
            need = (topk - n_gt).astype(F32)
            earlier = jnp.zeros((1, nq), F32)
            for kp in range(pairs):
                tie = key_ref[kp] == thr
                tie01 = jnp.where(tie, 1.0, 0.0)
                rank = _dot(below, tie01.astype(BF16)) + earlier
                lom_ref[kp] = jnp.where(tie & (rank < need), 1, 0).astype(I16)
                earlier = earlier + jnp.sum(tie01, axis=0, keepdims=True)
            return thr
        return run

    below = jnp.where(_iota((KEYS, KEYS), 1) < _iota((KEYS, KEYS), 0), 1.0, 0.0).astype(BF16)
    walks = list(range(2, seq // KEYS + 1, 2))
    thr = lax.switch((npair - 1) // 2, [threshold(p) for p in walks], 0)

    qs = _scaled_query_columns_t(cq)

    def attend(kp, carry):
        ms, dens, accs = carry
        r0 = pl.multiple_of(kp * KEYS, KEYS)
        kblk = z_ref[0, pl.ds(r0, KEYS), 512:576].astype(BF16)
        lg = _dot(kblk, qs) + btab_ref[jnp.minimum(i - kp, 2)]
        key = key_ref[kp]
        s_glob = r0 + s_loc
        mask = ((key > thr) | (lom_ref[kp].astype(I32) != 0)) & (s_glob <= t_glob)
        vt = vt_ref[kp]
        out = [_online_step(lg[:, nq * h:nq * (h + 1)], mask, vt, ms[h], dens[h], accs[h])
               for h in range(HEADS)]
        return tuple(o[0] for o in out), tuple(o[1] for o in out), tuple(o[2] for o in out)

    ms, dens, accs = lax.fori_loop(0, npair, attend, _online_init(nq))
    o_ref[0] = _unstack_heads_t([accs[h] / jnp.maximum(dens[h], 1e-30) for h in range(HEADS)])


def _dsa(zc, btab):
    b, s, wc = zc.shape
    return pl.pallas_call(
        _dsa_kernel,
        grid=(b, s // QRY),
        in_specs=[pl.BlockSpec((1, s, wc), lambda bi, i: (bi, 0, 0)),
                  pl.BlockSpec(btab.shape, lambda bi, i: (0, 0, 0))],
        out_specs=pl.BlockSpec((1, QRY, GROUP_WIDTH), lambda bi, i: (bi, i, 0)),
        out_shape=jax.ShapeDtypeStruct((b, s, GROUP_WIDTH), F32),
        scratch_shapes=[pltpu.VMEM((s // KEYS, HEAD_DIM, KEYS), BF16), pltpu.VMEM((s // KEYS, KEYS, QRY), I32)]
                       + [pltpu.VMEM((s // KEYS, KEYS, QRY), I16)] * 3,
        compiler_params=pltpu.CompilerParams(dimension_semantics=("arbitrary", "arbitrary"),
                                             vmem_limit_bytes=VMEM_LIMIT_BYTES),
        name="dsa",
    )(zc, btab)


def _nsa_kernel(z_ref, zc_ref, w1_ref, pos_ref, w2_ref, bsel_ref, bcmp_ref, ovt_ref, exp_ref, o_ref,
                vst_ref, vwt_ref, cmp_ref, cmpt_ref):
    i = pl.program_id(1)
    seq = z_ref.shape[1]
    nkb = seq // BLK
    hd = HEAD_DIM
    scale = hd ** -0.5
    n_cmp = (seq - NSA_CMP_LEN) // NSA_CMP_STRIDE + 1
    n_sel = seq // NSA_SEL_LEN
    topn = min(NSA_TOPN, n_sel)
    half = NSA_CMP_LEN // 2
    assert half == NSA_CMP_STRIDE and n_cmp + 1 == seq // NSA_CMP_STRIDE == BLK and n_sel <= BLK
    nq = QRY
    assert nq == KEYS
    t0 = pl.multiple_of(i * nq, nq)
    hi = lax.Precision.HIGHEST

    @pl.when(i == 0)
    def _():
        for kb in range(nkb):
            rows = slice(kb * BLK, (kb + 1) * BLK)
            cols = slice((kb % 2) * BLK, (kb % 2 + 1) * BLK)
            vst_ref[kb // 2, :, cols] = z_ref[0, rows, 384:512].T[hd:2 * hd, :].astype(BF16)
            vwt_ref[kb // 2, :, cols] = z_ref[0, rows, 512:640].T[hd:2 * hd, :].astype(BF16)
        first = jnp.zeros((BLK, 2 * NSA_CMP_HIDDEN), F32)
        second = jnp.zeros((BLK, 2 * NSA_CMP_HIDDEN), F32)
        for j in range(half):
            xj = zc_ref[0, pl.ds(j, BLK, stride=NSA_CMP_STRIDE), :]
            first = first + _dot((xj + pos_ref[j:j + 1, :]).astype(BF16), w1_ref[j])
            second = second + _dot((xj + pos_ref[half + j:half + j + 1, :]).astype(BF16), w1_ref[half + j])
        hid = first + pltpu.roll(second, BLK - 1, 0)
        hid = hid * _sigmoid(hid)
        cmp = _dot(hid.astype(BF16), w2_ref[...])
        cmp_ref[...] = cmp
        cmpt_ref[...] = cmp.T

    zq = z_ref[0, pl.ds(t0, nq), :]
    qs = _scaled_query_columns_t(zq[:, 0:256])
    gates = _sigmoid(zq[:, 640:768].T[0:16, :])
    row = lax.broadcasted_iota(I32, (BLK, nq), 0)
    t_glob = t0 + lax.broadcasted_iota(I32, (BLK, nq), 1)

    kcmp = cmp_ref[:, 0:hd].astype(BF16)
    vcmpt = cmpt_ref[hd:2 * hd, :].astype(BF16)
    lgc = _dot(kcmp, qs) + bcmp_ref[0]
    mask_c = (t_glob - (row * NSA_CMP_STRIDE + NSA_CMP_LEN - 1) >= 0) & (row < n_cmp)
    o_cmp, psum = [], jnp.zeros((BLK, nq), F32)
    for h in range(HEADS):
        lg = jnp.where(mask_c, lgc[:, nq * h:nq * (h + 1)], NEG)
        m = jnp.max(lg, axis=0, keepdims=True)
        p = jnp.where(mask_c, jnp.exp(lg - m), 0.0)
        p = p / jnp.maximum(jnp.sum(p, axis=0, keepdims=True), 1e-30)
        o_cmp.append(_dot(vcmpt, p.astype(BF16)))
        psum = psum + p

    imp = _dot(ovt_ref[...], psum, precision=hi)
    cur = t_glob >> int(math.log2(NSA_SEL_LEN))
    forced = (row == 0) | (row == cur) | (row == cur - 1)
    imp = jnp.where(forced, NSA_FORCE, imp)
    imp = jnp.where(row * NSA_SEL_LEN <= t_glob, imp, NEG)
    imp = imp[0:n_sel, :]
    jrow = lax.broadcasted_iota(I32, (n_sel, nq), 0)
    rank = jnp.zeros((n_sel, nq), I32)
    for jp in range(n_sel):
        other = imp[jp:jp + 1, :]
        rank = rank + ((other > imp) | ((other == imp) & (jp < jrow))).astype(I32)
    chosen = jnp.where(rank < topn, 1.0, 0.0)
    chosen = jnp.concatenate([chosen, jnp.zeros((BLK - n_sel, nq), F32)], axis=0).astype(BF16)

    s_loc = lax.broadcasted_iota(I32, (KEYS, nq), 0)
    t_keys = t0 + lax.broadcasted_iota(I32, (KEYS, nq), 1)

    def attend(vt_ref, lanes, mask_fn):
        def body(kp, carry):
            ms, dens, accs = carry
            r0 = pl.multiple_of(kp * KEYS, KEYS)
            kblk = z_ref[0, pl.ds(r0, KEYS), lanes].astype(BF16)
            lg = _dot(kblk, qs) + bsel_ref[jnp.minimum(i - kp, 2)]
            mask = mask_fn(r0)
            vt = vt_ref[kp]
            out = [_online_step(lg[:, nq * h:nq * (h + 1)], mask, vt, ms[h], dens[h], accs[h])
                   for h in range(HEADS)]
            return tuple(o[0] for o in out), tuple(o[1] for o in out), tuple(o[2] for o in out)
        return body

    def mask_sel(r0):
        picked = _dot(exp_ref[pl.ds(r0, KEYS), :], chosen) > 0.5
        return picked & (r0 + s_loc <= t_keys)

    def mask_win(r0):
        dist = t_keys - (r0 + s_loc)
        return (dist >= 0) & (dist < NSA_WINDOW)

    _, den_s, acc_s = lax.fori_loop(0, i + 1, attend(vst_ref, slice(384, 448), mask_sel), _online_init(nq))
    first_w = jnp.maximum(i - NSA_WINDOW // KEYS, 0)
    _, den_w, acc_w = lax.fori_loop(first_w, i + 1, attend(vwt_ref, slice(512, 576), mask_win), _online_init(nq))

    outs = []
    for h in range(HEADS):
        o_s = acc_s[h] / jnp.maximum(den_s[h], 1e-30)
        o_w = acc_w[h] / jnp.maximum(den_w[h], 1e-30)
        outs.append(gates[3 * h:3 * h + 1, :] * o_cmp[h] + gates[3 * h + 1:3 * h + 2, :] * o_s
                    + gates[3 * h + 2:3 * h + 3, :] * o_w)
    o_ref[0] = _unstack_heads_t(outs)


def _nsa(zd, w1, pos, w2, bsel, bcmp, ovt, expand):
    b, s, wd = zd.shape
    full = lambda a: pl.BlockSpec(a.shape, lambda bi, i: (0,) * a.ndim)
    return pl.pallas_call(
        _nsa_kernel,
        grid=(b, s // QRY),
        in_specs=[pl.BlockSpec((1, s, wd), lambda bi, i: (bi, 0, 0)),
                  pl.BlockSpec((1, s, LANES), lambda bi, i: (bi, 0, GROUP_WIDTH // LANES)),
                  full(w1), full(pos), full(w2), full(bsel),
                  pl.BlockSpec((1, BLK, HEADS * QRY), lambda bi, i: (i, 0, 0)),
                  full(ovt), full(expand)],
        out_specs=pl.BlockSpec((1, QRY, GROUP_WIDTH), lambda bi, i: (bi, i, 0)),
        out_shape=jax.ShapeDtypeStruct((b, s, GROUP_WIDTH), F32),
        scratch_shapes=[pltpu.VMEM((s // KEYS, HEAD_DIM, KEYS), BF16), pltpu.VMEM((s // KEYS, HEAD_DIM, KEYS), BF16),
                        pltpu.VMEM((BLK, BLK), F32), pltpu.VMEM((BLK, BLK), F32)],
        compiler_params=pltpu.CompilerParams(dimension_semantics=("arbitrary", "arbitrary"),
                                             vmem_limit_bytes=VMEM_LIMIT_BYTES),
        name="nsa",
    )(zd, zd, w1, pos, w2, bsel, bcmp, ovt, expand)


def _outproj_kernel(alpha, x_ref, a_ref, b_ref, c_ref, d_ref, w_ref, bo_ref, g_ref, beta_ref, o_ref):
    acc = bo_ref[...] + _dot(a_ref[...].astype(BF16), w_ref[0:GROUP_WIDTH, :])
    for n, m_ref in enumerate((b_ref, c_ref, d_ref), start=1):
        acc = acc + _dot(m_ref[...].astype(BF16), w_ref[n * GROUP_WIDTH:(n + 1) * GROUP_WIDTH, :])
    o_ref[...] = _layer_norm_rows(alpha * x_ref[...] + acc, g_ref[...], beta_ref[...])


def _outproj(alpha, x2d, mixed, w, bo, g, beta, tm=512):
    t = x2d.shape[0]
    row_spec = lambda width: pl.BlockSpec((tm, width), lambda i: (i, 0))
    const = lambda a: pl.BlockSpec(a.shape, lambda i: (0, 0))
    return pl.pallas_call(
        functools.partial(_outproj_kernel, alpha),
        grid=(t // tm,),
        in_specs=[row_spec(D_MODEL)] + [row_spec(GROUP_WIDTH)] * N_MIXERS
                 + [const(w), const(bo), const(g), const(beta)],
        out_specs=row_spec(D_MODEL),
        out_shape=jax.ShapeDtypeStruct((t, D_MODEL), F32),
        compiler_params=pltpu.CompilerParams(dimension_semantics=("arbitrary",),
                                             vmem_limit_bytes=VMEM_LIMIT_BYTES),
        name="outproj_ln",
    )(x2d, *mixed, w, bo, g, beta)


def _ffn_kernel(alpha, x_ref, w1_ref, b1_ref, w2_ref, b2_ref, g_ref, beta_ref, o_ref, xb_ref, acc_ref):
    j = pl.program_id(1)

    @pl.when(j == 0)
    def _():
        xb_ref[...] = x_ref[...].astype(BF16)
        acc_ref[...] = jnp.zeros_like(acc_ref)

    hdn = jnp.maximum(_dot(xb_ref[...], w1_ref[...]) + b1_ref[...], 0.0)
    acc_ref[...] += _dot(jnp.square(hdn).astype(BF16), w2_ref[...])

    @pl.when(j == pl.num_programs(1) - 1)
    def _():
        y = alpha * x_ref[...] + (acc_ref[...] + b2_ref[...])
        o_ref[...] = _layer_norm_rows(y, g_ref[...], beta_ref[...])


def _ffn(alpha, x2d, w1, b1, w2, b2, g, beta, tm=1024, tf=1024):
    t = x2d.shape[0]
    return pl.pallas_call(
        functools.partial(_ffn_kernel, alpha),
        grid=(t // tm, D_FF // tf),
        in_specs=[pl.BlockSpec((tm, D_MODEL), lambda i, j: (i, 0)),
                  pl.BlockSpec((D_MODEL, tf), lambda i, j: (0, j)),
                  pl.BlockSpec((1, tf), lambda i, j: (0, j)),
                  pl.BlockSpec((tf, D_MODEL), lambda i, j: (j, 0)),
                  pl.BlockSpec((1, D_MODEL), lambda i, j: (0, 0)),
                  pl.BlockSpec((1, D_MODEL), lambda i, j: (0, 0)),
                  pl.BlockSpec((1, D_MODEL), lambda i, j: (0, 0))],
        out_specs=pl.BlockSpec((tm, D_MODEL), lambda i, j: (i, 0)),
        out_shape=jax.ShapeDtypeStruct((t, D_MODEL), F32),
        scratch_shapes=[pltpu.VMEM((tm, D_MODEL), BF16), pltpu.VMEM((tm, D_MODEL), F32)],
        compiler_params=pltpu.CompilerParams(dimension_semantics=("arbitrary", "arbitrary"),
                                             vmem_limit_bytes=VMEM_LIMIT_BYTES),
        name="ffn_ln",
    )(x2d, w1, b1, w2, b2, g, beta)


def _bias_of_distance(rel_bias_heads, dist):
    onehot = np.eye(NUM_BUCKETS, dtype=np.float32)[_t5_bucket_np(np.asarray(dist))]
    return jnp.dot(rel_bias_heads.T, jnp.asarray(onehot.T), precision=lax.Precision.HIGHEST)


def _shifted_rows(v, n_rows, n_cols, step):
    period = v.shape[-1]
    assert n_cols <= period - step
    flat = jnp.tile(v, (1,) * (v.ndim - 1) + (n_rows,))[..., :n_rows * (period - step)]
    return flat.reshape(v.shape[:-1] + (n_rows, period - step))[..., :n_cols]


def _wrapped(period):
    idx = np.arange(period)
    return np.where(idx < period // 2, idx, idx - period)


def _dilated_bias(rel_bias):
    x = _wrapped(4 * BLK)
    tabs = [_shifted_rows(_bias_of_distance(rel_bias[:, 0:HEADS], (BLK - x) * dil), BLK, 2 * BLK, 1)
            for _, dil in DIL_PATTERNS]
    tabs = jnp.stack(tabs).reshape(len(DIL_PATTERNS), HEADS // 2, 2, BLK, 2 * BLK)
    return jnp.transpose(tabs, (0, 1, 3, 2, 4)).reshape(len(DIL_PATTERNS), HEADS // 2, BLK, 4 * BLK)


def _toeplitz_bias_t(rel_bias_heads, blk=BLK):
    assert (_t5_bucket_np(np.arange(blk + 1, 64 * blk)) == NUM_BUCKETS - 1).all()
    x = _wrapped(2 * blk)
    tabs = [_shifted_rows(_bias_of_distance(rel_bias_heads, blk * delta + x), blk, blk, 1)
            for delta in range(3)]
    return jnp.transpose(jnp.stack(tabs), (0, 2, 1, 3)).reshape(3, blk, HEADS * blk)


def _compressed_bias_t(rel_bias_heads, seq):
    nq = seq // QRY
    x = _wrapped(2 * seq + BLK)
    v = _bias_of_distance(rel_bias_heads, x - (NSA_CMP_LEN - 1))
    tab = _shifted_rows(v, BLK, seq, NSA_CMP_STRIDE)
    tab = tab.reshape(HEADS, BLK, nq, QRY)
    return jnp.transpose(tab, (2, 1, 0, 3)).reshape(nq, BLK, HEADS * QRY)


def _nsa_constants(seq):
    n_cmp = (seq - NSA_CMP_LEN) // NSA_CMP_STRIDE + 1
    n_sel = seq // NSA_SEL_LEN
    cs = np.arange(n_cmp)[:, None] * NSA_CMP_STRIDE
    ss = np.arange(n_sel)[None, :] * NSA_SEL_LEN
    ov = np.clip(np.minimum(cs + NSA_CMP_LEN, ss + NSA_SEL_LEN) - np.maximum(cs, ss), 0, None) / NSA_CMP_LEN
    ovt = np.zeros((BLK, BLK), np.float32)
    ovt[:n_sel, :n_cmp] = ov.T
    expand = np.zeros((seq, BLK), np.float32)
    expand[np.arange(seq), np.arange(seq) // NSA_SEL_LEN] = 1.0
    return jnp.asarray(ovt), jnp.asarray(expand, BF16)


def _nsa_weights(cmp_pos, cmp_w1, cmp_w2):
    hd, hid = HEAD_DIM, NSA_CMP_HIDDEN
    w1 = cmp_w1.reshape(2, NSA_CMP_LEN, hd, hid)
    zeros = jnp.zeros((NSA_CMP_LEN, hd, hid), F32)
    w1 = jnp.concatenate([jnp.concatenate([w1[0], zeros], axis=-1),
                          jnp.concatenate([zeros, w1[1]], axis=-1)], axis=1)
    pos = jnp.concatenate([cmp_pos[0], cmp_pos[1]], axis=-1)
    z2 = jnp.zeros((hid, hd), F32)
    w2 = jnp.concatenate([jnp.concatenate([cmp_w2[0], z2], axis=-1),
                          jnp.concatenate([z2, cmp_w2[1]], axis=-1)], axis=0)
    return w1.astype(BF16), pos, w2.astype(BF16)


def kernel(x, w_in, b_in, a_conv, a_norm, d_cmp_pos, d_cmp_w1, d_cmp_w2, w_out, b_out, ln1_g, ln1_b,
           w_ff1, b_ff1, w_ff2, b_ff2, ln2_g, ln2_b, rel_bias):
    bsz, seq, _ = x.shape
    depth = w_in.shape[0]
    alpha = (2 * depth) ** 0.25
    nc = seq // M_CHUNK
    bias_dil = _dilated_bias(rel_bias)
    btab_dsa = _toeplitz_bias_t(rel_bias[:, HEADS:2 * HEADS], QRY)
    btab_nsa = _toeplitz_bias_t(rel_bias[:, 2 * HEADS:3 * HEADS], QRY)
    bcmp_nsa = _compressed_bias_t(rel_bias[:, 2 * HEADS:3 * HEADS], seq)
    ovt, expand = _nsa_constants(seq)

    h = x.reshape(bsz * seq, D_MODEL)
    for l in range(depth):
        w_l = _permute_columns(w_in[l]).astype(BF16)
        b_l = _permute_columns(b_in[l])[None, :]
        za, zb, zc, zd, zg = (z.reshape(bsz, seq, -1) for z in _inproj(h, w_l, b_l))
        out_a = _mlstm(za, _mlstm_gate_rows(zg), a_conv[l], a_norm[l][None, :])
        out_b = _dilated(zb, bias_dil)
        out_c = _dsa(zc, btab_dsa)
        nsa_w1, nsa_pos, nsa_w2 = _nsa_weights(d_cmp_pos[l], d_cmp_w1[l], d_cmp_w2[l])
        out_d = _nsa(zd, nsa_w1, nsa_pos, nsa_w2, btab_nsa, bcmp_nsa, ovt, expand)
        mixed = [o.reshape(bsz * seq, GROUP_WIDTH) for o in (out_a, out_b, out_c, out_d)]
        h = _outproj(alpha, h, mixed, w_out[l].astype(BF16), b_out[l][None, :],
                     ln1_g[l][None, :], ln1_b[l][None, :])
        h = _ffn(alpha, h, w_ff1[l].astype(BF16), b_ff1[l][None, :], w_ff2[l].astype(BF16),
                 b_ff2[l][None, :], ln2_g[l][None, :], ln2_b[l][None, :])
    return h.reshape(bsz, seq, D_MODEL)
```

```python
import functools
import math

import numpy as np
import jax
import jax.numpy as jnp
from jax import lax
from jax.experimental import pallas as pl
from jax.experimental.pallas import tpu as pltpu

F32 = jnp.float32
BF16 = jnp.bfloat16
I32 = jnp.int32
I16 = jnp.int16
I16_MIN = -2 ** 15

D_MODEL = 1024
N_MIXERS = 4
HEADS = 4
HEAD_DIM = D_MODEL // (N_MIXERS * HEADS)
GROUP_WIDTH = HEADS * HEAD_DIM
D_FF = 4 * D_MODEL
LN_EPS = 1e-5
NEG = -1e30

M_QK_DIM = HEAD_DIM // 2
M_CHUNK = 64
M_CONV = 4
DIL_PATTERNS = ((128, 1), (512, 4), (2048, 16))
IDX_HEADS = 4
IDX_DIM = 64
DSA_TOPK = 256
NSA_CMP_LEN = 32
NSA_CMP_STRIDE = 16
NSA_SEL_LEN = 64
NSA_TOPN = 16
NSA_WINDOW = 512
NSA_CMP_HIDDEN = 256
NSA_FORCE = 1e9
NUM_BUCKETS = 32
MAX_DISTANCE = 128

LANES = 128
BLK = 128
KEYS = 2 * BLK
QRY = KEYS
VMEM_LIMIT_BYTES = 56 * 1024 * 1024

IN_SPLITS = (
    ('a_q', HEADS * M_QK_DIM), ('a_k', HEADS * M_QK_DIM), ('a_v', GROUP_WIDTH),
    ('a_i', HEADS), ('a_f', HEADS), ('a_o', GROUP_WIDTH),
    ('b_q', GROUP_WIDTH), ('b_k', GROUP_WIDTH), ('b_v', GROUP_WIDTH),
    ('c_q', GROUP_WIDTH), ('c_k', HEAD_DIM), ('c_v', HEAD_DIM),
    ('c_iq', IDX_HEADS * IDX_DIM), ('c_ik', IDX_DIM), ('c_iw', IDX_HEADS),
    ('d_q', GROUP_WIDTH), ('d_kc', HEAD_DIM), ('d_vc', HEAD_DIM),
    ('d_ks', HEAD_DIM), ('d_vs', HEAD_DIM), ('d_kw', HEAD_DIM), ('d_vw', HEAD_DIM),
    ('d_g', 3 * HEADS),
)

GROUP_LAYOUT = (
    (('a_q', 'a_k'), ('a_v',), ('a_o',), (('a_i', HEAD_DIM),), (('a_i', M_QK_DIM),),
     (('a_f', HEAD_DIM),), (('a_f', M_QK_DIM),)),
    (('b_q',), ('b_k',), ('b_v',)),
    (('c_q',), ('c_iq',), ('c_k', 'c_v'), ('c_ik', 'c_iw')),
    (('d_q',), ('d_kc', 'd_vc'), ('d_ks', 'd_vs'), ('d_kw', 'd_vw'), ('d_g',)),
    (('a_i', 'a_f'),),
)


def _round_up(n, m):
    return -(-n // m) * m


def _projection_layout():
    offs, off = {}, 0
    for name, width in IN_SPLITS:
        offs[name] = (off, width)
        off += width
    runs, group_widths = [], []
    for group in GROUP_LAYOUT:
        gwidth = 0
        for chunk in group:
            cwidth = 0
            for entry in chunk:
                name, rep = entry if isinstance(entry, tuple) else (entry, 1)
                o, w = offs[name]
                runs.append((o, w, rep))
                cwidth += w * rep
            pad = _round_up(cwidth, LANES) - cwidth
            if pad:
                runs.append((-1, pad, 1))
            gwidth += cwidth + pad
        group_widths.append(gwidth)
    return tuple(runs), tuple(group_widths)


PROJ_RUNS, GROUP_WIDTHS = _projection_layout()
PROJ_WIDTH = int(sum(GROUP_WIDTHS))
PROJ_PERM = np.concatenate([np.repeat(np.arange(o, o + w), r) if o >= 0 else np.full(w, -1)
                            for o, w, r in PROJ_RUNS]).astype(np.int32)


def _permute_columns(a):
    parts = []
    for o, w, r in PROJ_RUNS:
        if o < 0:
            parts.append(jnp.zeros(a.shape[:-1] + (w,), a.dtype))
        else:
            parts.append(a[..., o:o + w] if r == 1 else jnp.repeat(a[..., o:o + w], r, axis=-1))
    return jnp.concatenate(parts, axis=-1)


def _t5_bucket_np(dist):
    n = np.maximum(dist, 0)
    max_exact = NUM_BUCKETS // 2
    nf = np.maximum(n, max_exact).astype(np.float32)
    large = max_exact + (np.log(nf / max_exact) / math.log(MAX_DISTANCE / max_exact)
                         * (NUM_BUCKETS - max_exact)).astype(np.int32)
    large = np.minimum(large, NUM_BUCKETS - 1)
    return np.where(n < max_exact, n, large).astype(np.int32)


def _nt_dot(a, b, precision=None):
    return lax.dot_general(a, b, (((1,), (1,)), ((), ())), precision=precision,
                           preferred_element_type=F32)


def _dot(a, b, precision=None):
    return jnp.dot(a, b, precision=precision, preferred_element_type=F32)


def _layer_norm_rows(y, g, b):
    mu = jnp.mean(y, axis=-1, keepdims=True)
    var = jnp.mean(jnp.square(y - mu), axis=-1, keepdims=True)
    return (y - mu) * lax.rsqrt(var + LN_EPS) * g + b


def _sigmoid(x):
    return 1.0 / (1.0 + jnp.exp(-x))


def _log_sigmoid(x):
    return -(jnp.maximum(-x, 0.0) + jnp.log1p(jnp.exp(-jnp.abs(x))))


def _sortable_key(x):
    bits = pltpu.bitcast(x, I32)
    return bits ^ ((bits >> 31) & jnp.int32(0x7FFFFFFF))


def _inproj_kernel(x_ref, w_ref, b_ref, *out_refs):
    xb = x_ref[...].astype(BF16)
    off = 0
    for o_ref, width in zip(out_refs, GROUP_WIDTHS):
        o_ref[...] = _dot(xb, w_ref[:, off:off + width]) + b_ref[:, off:off + width]
        off += width


def _inproj(x2d, w, b, tm=512):
    t = x2d.shape[0]
    return pl.pallas_call(
        _inproj_kernel,
        grid=(t // tm,),
        in_specs=[pl.BlockSpec((tm, D_MODEL), lambda i: (i, 0)),
                  pl.BlockSpec((D_MODEL, PROJ_WIDTH), lambda i: (0, 0)),
                  pl.BlockSpec((1, PROJ_WIDTH), lambda i: (0, 0))],
        out_specs=[pl.BlockSpec((tm, gw), lambda i: (i, 0)) for gw in GROUP_WIDTHS],
        out_shape=[jax.ShapeDtypeStruct((t, gw), F32) for gw in GROUP_WIDTHS],
        compiler_params=pltpu.CompilerParams(dimension_semantics=("arbitrary",),
                                             vmem_limit_bytes=VMEM_LIMIT_BYTES),
        name="inproj",
    )(x2d, w, b)


def _split_terms(x, n):
    terms, rest = [], x
    for _ in range(n):
        terms.append(rest.astype(BF16))
        rest = rest - terms[-1].astype(F32)
    return terms


def _iota(shape, dim):
    return lax.broadcasted_iota(I32, shape, dim)


def _mlstm_kernel(z_ref, gt_ref, cw_ref, ng_ref, o_ref, xpad_ref):
    seq = z_ref.shape[1]
    L, DK, DV, H = M_CHUNK, M_QK_DIM, HEAD_DIM, HEADS
    assert L == DV
    wq, wv = H * DK, H * DV
    lg_dk, lg_dv = int(math.log2(DK)), int(math.log2(DV))
    c_v, c_o, c_i64 = 2 * wq, 2 * wq + wv, 2 * wq + 2 * wv
    c_i32, c_f64 = c_i64 + wv, c_i64 + wv + wq
    xpad_ref[0:8, :] = jnp.zeros((8, 2 * wq), F32)
    xpad_ref[8:, :] = z_ref[0, :, 0:2 * wq]

    one_if = lambda cond: jnp.where(cond, 1.0, 0.0).astype(BF16)
    tri_l = one_if(_iota((L, L), 0) >= _iota((L, L), 1))
    trow = _iota((L, wv), 0)
    tri_heads = trow >= (_iota((L, wv), 1) & (L - 1))
    r_vv, c_vv = _iota((wv, wv), 0), _iota((wv, wv), 1)
    same_head = (r_vv >> lg_dv) == (c_vv >> lg_dv)
    ones_bd = one_if(same_head)
    mean_bd = jnp.where(same_head, 1.0 / DV, 0.0).astype(BF16)
    tri_u_bd = one_if(same_head & ((r_vv & (L - 1)) <= (c_vv & (L - 1))))
    state_mask = (_iota((wq, wv), 0) >> lg_dk) == (_iota((wq, wv), 1) >> lg_dv)
    eye_q = one_if(_iota((wq, wq), 0) == _iota((wq, wq), 1))
    head_of_qlane = _iota((L, wq), 1) >> lg_dk
    head_of_vlane = _iota((L, wv), 1) >> lg_dv
    row8 = _iota((8, wv), 0)
    cw = cw_ref[...]
    ng = ng_ref[...]

    def head_mean(x):
        hi_lo = _split_terms(x, 2)
        r = _dot(jnp.concatenate(hi_lo, axis=0), mean_bd)
        return r[0:L] + r[L:2 * L]

    def chunk(c, carry):
        cbd, nbd, m64, m32 = carry
        s0 = pl.multiple_of(c * L, L)
        rows = pl.ds(s0, L)
        xw = xpad_ref[pl.ds(s0, L + 8), :]
        y = sum(cw[j:j + 1, :] * xw[5 + j:5 + j + L, :] for j in range(M_CONV))
        qk = y * _sigmoid(y)
        q = qk[:, 0:wq]
        k = qk[:, wq:] * (DK ** -0.5)
        qb = q.astype(BF16)
        v = z_ref[0, rows, c_v:c_v + wv]
        i64 = z_ref[0, rows, c_i64:c_i64 + wv]
        i32 = z_ref[0, rows, c_i32:c_i32 + wq]
        gr = gt_ref[0, c]

        flog = _log_sigmoid(z_ref[0, rows, c_f64:c_f64 + wv + wq])
        bsum = _dot(tri_l, jnp.concatenate(_split_terms(flog, 3), axis=1))
        w3 = wv + wq
        ball = bsum[:, 0:w3] + bsum[:, w3:2 * w3] + bsum[:, 2 * w3:3 * w3]
        b64, b32 = ball[:, 0:wv], ball[:, wv:w3]
        fterms = [t.astype(F32) for t in _split_terms(_log_sigmoid(gr[1:2, :]), 3)]
        frows = jnp.where(row8 == 0, fterms[0], jnp.where(row8 == 1, fterms[1], jnp.where(row8 == 2, fterms[2], 0.0)))
        bparts = _dot(frows.astype(BF16), tri_u_bd)
        brow = bparts[0:1, :] + bparts[1:2, :] + bparts[2:3, :]

        dall = jnp.where(tri_heads, b64 - brow + gr[0:1, :], NEG)
        cm = i64 - b64
        for sh in (1, 2, 4, 8, 16, 32):
            cm = jnp.where(trow >= sh, jnp.maximum(cm, pltpu.roll(cm, sh, 0)), cm)
        inter = b64 + m64
        m_t = jnp.maximum(inter, b64 + cm)
        kbd = jnp.concatenate([jnp.where(head_of_qlane == h, k, 0.0) for h in range(H)], axis=0).astype(BF16)
        sc = _nt_dot(qb, kbd) * jnp.exp(dall - m_t)
        wi = jnp.exp(inter - m_t)
        vb = v.astype(BF16)
        vbd = jnp.concatenate([jnp.where(head_of_vlane == h, v, 0.0) for h in range(H)], axis=0).astype(BF16)
        pv = _dot(sc.astype(BF16), jnp.concatenate([vbd, ones_bd], axis=1))
        qst = _dot(qb, jnp.concatenate([cbd, nbd], axis=1).astype(BF16))
        num = pv[:, 0:wv] + wi * qst[:, 0:wv]
        den = pv[:, wv:] + wi * qst[:, wv:]
        hh = num / jnp.maximum(jnp.abs(den), jnp.exp(-m_t))

        og = _sigmoid(z_ref[0, rows, c_o:c_o + wv]) * hh
        dev = og - head_mean(og)
        o_ref[0, rows, :] = dev * lax.rsqrt(head_mean(dev * dev) + LN_EPS) * ng

        bl64, bl32 = b64[L - 1:L, :], b32[L - 1:L, :]
        m64_new = jnp.maximum(bl64 + m64, jnp.max(bl64 - b64 + i64, axis=0, keepdims=True))
        g32 = bl32 - b32 + i32
        m32_new = jnp.maximum(bl32 + m32, jnp.max(g32, axis=0, keepdims=True))
        kw = k * jnp.exp(g32 - m32_new)
        wc = jnp.exp(bl64 + m64 - m64_new)
        kwt = _nt_dot(eye_q, kw.astype(BF16)).astype(BF16)
        upd = _dot(kwt, jnp.concatenate([vb, jnp.ones((L, wv), BF16)], axis=1))
        cbd = wc * cbd + jnp.where(state_mask, upd[:, 0:wv], 0.0)
        nbd = wc * nbd + jnp.where(state_mask, upd[:, wv:], 0.0)
        return cbd, nbd, m64_new, m32_new

    init = (jnp.zeros((wq, wv), F32), jnp.zeros((wq, wv), F32), jnp.zeros((1, wv), F32), jnp.zeros((1, wq), F32))
    lax.fori_loop(0, seq // L, chunk, init, unroll=4)


def _mlstm(za, gates_t, conv_w, norm_g):
    b, s, wa = za.shape
    nc = s // M_CHUNK
    return pl.pallas_call(
        _mlstm_kernel,
        grid=(b,),
        in_specs=[pl.BlockSpec((1, s, wa), lambda i: (i, 0, 0)),
                  pl.BlockSpec((1, nc) + gates_t.shape[2:], lambda i: (i, 0, 0, 0)),
                  pl.BlockSpec((M_CONV, 2 * HEADS * M_QK_DIM), lambda i: (0, 0)),
                  pl.BlockSpec((1, GROUP_WIDTH), lambda i: (0, 0))],
        out_specs=pl.BlockSpec((1, s, GROUP_WIDTH), lambda i: (i, 0, 0)),
        out_shape=jax.ShapeDtypeStruct((b, s, GROUP_WIDTH), F32),
        scratch_shapes=[pltpu.VMEM((s + 8, 2 * HEADS * M_QK_DIM), F32)],
        compiler_params=pltpu.CompilerParams(dimension_semantics=("arbitrary",),
                                             vmem_limit_bytes=VMEM_LIMIT_BYTES),
        name="mlstm",
    )(za, gates_t, conv_w, norm_g)


def _mlstm_gate_rows(zg):
    b, s, _ = zg.shape
    gates = zg[:, :, 0:2 * HEADS].reshape(b, s // M_CHUNK, M_CHUNK, 2, HEADS)
    return jnp.transpose(gates, (0, 1, 3, 4, 2)).reshape(b, s // M_CHUNK, 2, HEADS * M_CHUNK)


def _dilated_kernel(q0_ref, q1_ref, k0_ref, k1_ref, v0_ref, v1_ref, bias_ref, o_ref, acc_ref, mx_ref, den_ref):
    seq = q0_ref.shape[1]
    W = BLK
    hd = HEAD_DIM
    npair = HEADS // 2
    scale = hd ** -0.5
    assert math.log2(scale).is_integer()
    nk = 2 * W
    qi = _iota((W, 2 * nk), 0)
    ki = _iota((W, 2 * nk), 1) & (nk - 1)
    j = W + qi - ki
    band = (j >= 0) & (j <= W)
    head_of_lane = _iota((nk, 2 * hd), 1) >> int(math.log2(hd))
    ones_bd = jnp.concatenate([jnp.where(head_of_lane == hh, 1.0, 0.0) for hh in range(2)], axis=0).astype(BF16)
    q_refs, k_refs, v_refs = (q0_ref, q1_ref), (k0_ref, k1_ref), (v0_ref, v1_ref)

    def block_diag(x):
        return jnp.concatenate([jnp.where(head_of_lane == hh, x, 0.0) for hh in range(2)], axis=0).astype(BF16)

    for br, (window, dil) in enumerate(DIL_PATTERNS):
        assert window // dil == W
        nb = (seq // dil) // W

        def piece(idx, _, br=br, dil=dil, nb=nb):
            n = idx % nb
            if dil == 1:
                rows_q = pl.ds(pl.multiple_of(W * n, W), W)
                rows_p = pl.ds(pl.multiple_of(W * jnp.maximum(n - 1, 0), W), W)
            else:
                r = idx // nb
                rows_q = pl.ds(r + dil * W * n, W, stride=dil)
                rows_p = pl.ds(r + dil * W * jnp.maximum(n - 1, 0), W, stride=dil)
            mask = band & (ki >= jnp.where(n > 0, 0, W))
            for pr in range(npair):
                q = (q_refs[pr][0, rows_q, :] * scale).astype(BF16)
                k2 = jnp.concatenate([k_refs[pr][0, rows_p, :], k_refs[pr][0, rows_q, :]], axis=0)
                v2 = jnp.concatenate([v_refs[pr][0, rows_p, :], v_refs[pr][0, rows_q, :]], axis=0)
                lg = jnp.where(mask, _nt_dot(q, block_diag(k2)) + bias_ref[br, pr], NEG)
                ps, mxs = [], []
                for hh in range(2):
                    sl = slice(nk * hh, nk * (hh + 1))
                    m = jnp.max(lg[:, sl], axis=-1, keepdims=True)
                    ps.append(jnp.where(mask[:, sl], jnp.exp(lg[:, sl] - m), 0.0))
                    mxs.append(jnp.broadcast_to(m, (W, hd)))
                p = jnp.concatenate(ps, axis=-1).astype(BF16)
                pv = _dot(p, jnp.concatenate([block_diag(v2), ones_bd], axis=1))
                acc_ref[br, pr, rows_q, :] = pv[:, 0:2 * hd]
                den_ref[br, pr, rows_q, :] = pv[:, 2 * hd:]
                mx_ref[br, pr, rows_q, :] = jnp.concatenate(mxs, axis=-1)
            return 0

        lax.fori_loop(0, dil * nb, piece, 0, unroll=2)

    def combine(i, _):
        rows = pl.ds(pl.multiple_of(i * W, W), W)
        outs = []
        for pr in range(npair):
            ms = [mx_ref[b, pr, rows, :] for b in range(len(DIL_PATTERNS))]
            top = functools.reduce(jnp.maximum, ms)
            es = [jnp.exp(m - top) for m in ms]
            num = sum(e * acc_ref[b, pr, rows, :] for b, e in enumerate(es))
            den = sum(e * jnp.maximum(den_ref[b, pr, rows, :], 1e-30) for b, e in enumerate(es))
            outs.append(num / den)
        o_ref[0, rows, :] = jnp.concatenate(outs, axis=-1)
        return 0

    lax.fori_loop(0, seq // W, combine, 0)


def _dilated(zb, bias):
    b, s, wb = zb.shape
    nbr = len(DIL_PATTERNS)
    pair_spec = lambda c: pl.BlockSpec((1, s, LANES), lambda i: (i, 0, c))
    return pl.pallas_call(
        _dilated_kernel,
        grid=(b,),
        in_specs=[pair_spec(c) for c in range(wb // LANES)]
                 + [pl.BlockSpec(bias.shape, lambda i: (0, 0, 0, 0))],
        out_specs=pl.BlockSpec((1, s, GROUP_WIDTH), lambda i: (i, 0, 0)),
        out_shape=jax.ShapeDtypeStruct((b, s, GROUP_WIDTH), F32),
        scratch_shapes=[pltpu.VMEM((nbr, HEADS // 2, s, LANES), F32) for _ in range(3)],
        compiler_params=pltpu.CompilerParams(dimension_semantics=("arbitrary",),
                                             vmem_limit_bytes=VMEM_LIMIT_BYTES),
        name="dilated",
    )(*([zb] * (wb // LANES)), bias)


def _head_columns_t(q):
    qt = q.T
    return jnp.concatenate([qt[HEAD_DIM * h:HEAD_DIM * (h + 1), :] for h in range(HEADS)], axis=1)


def _scaled_query_columns_t(q):
    scale = HEAD_DIM ** -0.5
    assert math.log2(scale).is_integer()
    return (_head_columns_t(q) * scale).astype(BF16)


def _unstack_heads_t(per_head_t):
    halves = []
    for h in range(0, HEADS, 2):
        halves.append(jnp.concatenate([per_head_t[h], per_head_t[h + 1]], axis=0).T)
    return jnp.concatenate(halves, axis=-1)


VT_ROWS = HEAD_DIM + 16


def _online_step(lg, mask, vt1, m, acc):
    lg = jnp.where(mask, lg, NEG)
    m_new = jnp.maximum(m, jnp.max(lg, axis=0, keepdims=True))
    p = jnp.where(mask, jnp.exp(lg - m_new), 0.0)
    acc = acc * jnp.exp(m - m_new) + _dot(vt1, p.astype(BF16))
    return m_new, acc


def _online_result(acc):
    return acc[0:HEAD_DIM, :] / jnp.maximum(acc[HEAD_DIM:HEAD_DIM + 1, :], 1e-30)


def _values_with_ones_t(z_rows):
    vt = z_rows.T[HEAD_DIM:2 * HEAD_DIM, :]
    return jnp.concatenate([vt, jnp.ones((VT_ROWS - HEAD_DIM, vt.shape[1]), F32)], axis=0).astype(BF16)


def _pair_bias(btab_ref, behind):
    return jnp.concatenate([btab_ref[jnp.minimum(behind, 2)], btab_ref[jnp.clip(behind - 1, 0, 2)]], axis=0)


def _online_init(nq=BLK):
    return (tuple(jnp.full((1, nq), NEG, F32) for _ in range(HEADS)),
            tuple(jnp.zeros((VT_ROWS, nq), F32) for _ in range(HEADS)))


def _dsa_kernel(z_ref, btab_ref, o_ref, vt_ref, key_ref, hi_ref, lo_ref, lom_ref):
    i = pl.program_id(1)
    seq = z_ref.shape[1]
    nkb = seq // BLK
    nq = QRY
    hd = HEAD_DIM
    topk = min(DSA_TOPK, seq // 4)
    t0 = pl.multiple_of(i * nq, nq)

    assert nq == KEYS
    npair = i + 1

    @pl.when(i == 0)
    def _():
        for kb in range(nkb):
            vt_ref[kb // 2, :, (kb % 2) * BLK:(kb % 2 + 1) * BLK] = _values_with_ones_t(
                z_ref[0, kb * BLK:(kb + 1) * BLK, 512:640])
        key_ref[...] = jnp.full(key_ref.shape, -2 ** 31, I32)
        for half_ref in (hi_ref, lo_ref, lom_ref):
            half_ref[...] = jnp.full(half_ref.shape, I16_MIN, I16)

    zq = z_ref[0, pl.ds(t0, nq), :]
    cq = zq[:, 0:256]
    ciq = _head_columns_t(zq[:, 256:512]).astype(BF16)
    iw = zq[:, 640:768].T[IDX_DIM:IDX_DIM + IDX_HEADS, :] * ((IDX_HEADS * IDX_DIM) ** -0.5)
    s_loc = lax.broadcasted_iota(I32, (KEYS, nq), 0)
    t_glob = t0 + lax.broadcasted_iota(I32, (KEYS, nq), 1)

    def score_block(kp, _):
        r0 = pl.multiple_of(kp * KEYS, KEYS)
        ik = z_ref[0, pl.ds(r0, KEYS), 640:704].astype(BF16)
        rel = _dot(ik, ciq)
        sc = jnp.zeros((KEYS, nq), F32)
        for h in range(IDX_HEADS):
            sc = sc + jnp.maximum(rel[:, nq * h:nq * (h + 1)], 0.0) * iw[h:h + 1, :]
        sc = jnp.where(r0 + s_loc <= t_glob, sc, NEG)
        key = _sortable_key(sc)
        key_ref[kp] = key
        hi_ref[kp] = (key >> 16).astype(I16)
        lo_ref[kp] = ((key & 0xFFFF) - 2 ** 15).astype(I16)
        return 0

    lax.fori_loop(0, npair, score_block, 0)

    def count(ref, pred, pairs):
        dt = ref.dtype
        rows = 8 * 4 // dt.itemsize
        def body(kp, acc):
            hit = jnp.where(pred(kp, ref[kp]), jnp.ones((), dt), jnp.zeros((), dt))
            hit = hit.reshape(KEYS // rows, rows, nq)
            parts = [hit[n] for n in range(KEYS // rows)]
            while len(parts) > 1:
                parts = [a + b for a, b in zip(parts[0::2], parts[1::2])]
            return acc + parts[0]
        acc = jnp.zeros((rows, nq), dt)
        if isinstance(pairs, int):
            for kp in range(pairs):
                acc = body(kp, acc)
        else:
            acc = lax.fori_loop(0, pairs, body, acc)
        return jnp.sum(acc.astype(I32), axis=0, keepdims=True)

    def threshold(pairs):
        def half_search(ref, k):
            def bit(it, thr):
                cand = thr + lax.shift_left(jnp.int32(1), 15 - it)
                c = count(ref, lambda kp, half: half >= cand.astype(I16), pairs)
                return jnp.where(c >= k, cand, thr)
            return lax.fori_loop(0, 16, bit, jnp.full((1, nq), I16_MIN, I32))

        def run(_):
            thr_hi = half_search(hi_ref, topk)
            thr_hi16 = thr_hi.astype(I16)
            above_hi = count(hi_ref, lambda kp, half: half > thr_hi16, pairs)
            for kp in range(pairs):
                lom_ref[kp] = jnp.where(hi_ref[kp] == thr_hi16, lo_ref[kp], jnp.int16(I16_MIN))
            thr_lo = half_search(lom_ref, topk - above_hi)
            thr = (thr_hi << 16) | (thr_lo + 2 ** 15)
            n_gt = count(key_ref, lambda kp, key: key > thr, pairs)
            need = (topk - n_gt).astype(F32)
            earlier = jnp.zeros((1, nq), F32)
            for kp in range(pairs):
                tie = key_ref[kp] == thr
                tie01 = jnp.where(tie, 1.0, 0.0)
                rank = _dot(below, tie01.astype(BF16)) + earlier
                lom_ref[kp] = jnp.where(tie & (rank < need), 1, 0).astype(I16)
                earlier = earlier + jnp.sum(tie01, axis=0, keepdims=True)
            return thr
        return run

    below = jnp.where(_iota((KEYS, KEYS), 1) < _iota((KEYS, KEYS), 0), 1.0, 0.0).astype(BF16)
    walks = list(range(2, seq // KEYS + 1, 2))
    thr = lax.switch((npair - 1) // 2, [threshold(p) for p in walks], 0)

    qs = _scaled_query_columns_t(cq)

    def attend(kp, carry):
        ms, accs = carry
        r0 = pl.multiple_of(kp * KEYS, KEYS)
        kblk = z_ref[0, pl.ds(r0, KEYS), 512:576].astype(BF16)
        lg = _dot(kblk, qs) + btab_ref[jnp.minimum(i - kp, 2)]
        key = key_ref[kp]
        s_glob = r0 + s_loc
        mask = ((key > thr) | (lom_ref[kp].astype(I32) != 0)) & (s_glob <= t_glob)
        vt1 = vt_ref[kp]
        out = [_online_step(lg[:, nq * h:nq * (h + 1)], mask, vt1, ms[h], accs[h]) for h in range(HEADS)]
        return tuple(o[0] for o in out), tuple(o[1] for o in out)

    ms, accs = lax.fori_loop(0, npair, attend, _online_init(nq))
    o_ref[0] = _unstack_heads_t([_online_result(accs[h]) for h in range(HEADS)])


def _dsa(zc, btab):
    b, s, wc = zc.shape
    return pl.pallas_call(
        _dsa_kernel,
        grid=(b, s // QRY),
        in_specs=[pl.BlockSpec((1, s, wc), lambda bi, i: (bi, 0, 0)),
                  pl.BlockSpec(btab.shape, lambda bi, i: (0, 0, 0))],
        out_specs=pl.BlockSpec((1, QRY, GROUP_WIDTH), lambda bi, i: (bi, i, 0)),
        out_shape=jax.ShapeDtypeStruct((b, s, GROUP_WIDTH), F32),
        scratch_shapes=[pltpu.VMEM((s // KEYS, VT_ROWS, KEYS), BF16), pltpu.VMEM((s // KEYS, KEYS, QRY), I32)]
                       + [pltpu.VMEM((s // KEYS, KEYS, QRY), I16)] * 3,
        compiler_params=pltpu.CompilerParams(dimension_semantics=("arbitrary", "arbitrary"),
                                             vmem_limit_bytes=VMEM_LIMIT_BYTES),
        name="dsa",
    )(zc, btab)


def _nsa_kernel(z_ref, zc_ref, w1_ref, pos_ref, w2_ref, bsel_ref, bcmp_ref, ovt_ref, exp_ref, o_ref,
                vst_ref, vwt_ref, cmp_ref, cmpt_ref):
    i = pl.program_id(1)
    seq = z_ref.shape[1]
    nkb = seq // BLK
    hd = HEAD_DIM
    scale = hd ** -0.5
    n_cmp = (seq - NSA_CMP_LEN) // NSA_CMP_STRIDE + 1
    n_sel = seq // NSA_SEL_LEN
    topn = min(NSA_TOPN, n_sel)
    half = NSA_CMP_LEN // 2
    assert half == NSA_CMP_STRIDE and n_cmp + 1 == seq // NSA_CMP_STRIDE == BLK and n_sel <= BLK
    nq = QRY
    assert nq == KEYS
    t0 = pl.multiple_of(i * nq, nq)
    hi = lax.Precision.HIGHEST

    @pl.when(i == 0)
    def _():
        for kb in range(nkb):
            rows = slice(kb * BLK, (kb + 1) * BLK)
            cols = slice((kb % 2) * BLK, (kb % 2 + 1) * BLK)
            vst_ref[kb // 2, :, cols] = _values_with_ones_t(z_ref[0, rows, 384:512])
            vwt_ref[kb // 2, :, cols] = _values_with_ones_t(z_ref[0, rows, 512:640])
        first = jnp.zeros((BLK, 2 * NSA_CMP_HIDDEN), F32)
        second = jnp.zeros((BLK, 2 * NSA_CMP_HIDDEN), F32)
        for j in range(half):
            xj = zc_ref[0, pl.ds(j, BLK, stride=NSA_CMP_STRIDE), :]
            first = first + _dot((xj + pos_ref[j:j + 1, :]).astype(BF16), w1_ref[j])
            second = second + _dot((xj + pos_ref[half + j:half + j + 1, :]).astype(BF16), w1_ref[half + j])
        hid = first + pltpu.roll(second, BLK - 1, 0)
        hid = hid * _sigmoid(hid)
        cmp = _dot(hid.astype(BF16), w2_ref[...])
        cmp_ref[...] = cmp
        cmpt_ref[...] = cmp.T

    zq = z_ref[0, pl.ds(t0, nq), :]
    qs = _scaled_query_columns_t(zq[:, 0:256])
    gates = _sigmoid(zq[:, 640:768].T[0:16, :])
    row = lax.broadcasted_iota(I32, (BLK, nq), 0)
    t_glob = t0 + lax.broadcasted_iota(I32, (BLK, nq), 1)

    kcmp = cmp_ref[:, 0:hd].astype(BF16)
    vcmpt = cmpt_ref[hd:2 * hd, :].astype(BF16)
    lgc = _dot(kcmp, qs) + bcmp_ref[0]
    mask_c = (t_glob - (row * NSA_CMP_STRIDE + NSA_CMP_LEN - 1) >= 0) & (row < n_cmp)
    o_cmp, psum = [], jnp.zeros((BLK, nq), F32)
    for h in range(HEADS):
        lg = jnp.where(mask_c, lgc[:, nq * h:nq * (h + 1)], NEG)
        m = jnp.max(lg, axis=0, keepdims=True)
        p = jnp.where(mask_c, jnp.exp(lg - m), 0.0)
        p = p / jnp.maximum(jnp.sum(p, axis=0, keepdims=True), 1e-30)
        o_cmp.append(_dot(vcmpt, p.astype(BF16)))
        psum = psum + p

    imp = _dot(ovt_ref[...], psum, precision=hi)
    cur = t_glob >> int(math.log2(NSA_SEL_LEN))
    forced = (row == 0) | (row == cur) | (row == cur - 1)
    imp = jnp.where(forced, NSA_FORCE, imp)
    imp = jnp.where(row * NSA_SEL_LEN <= t_glob, imp, NEG)
    imp = imp[0:n_sel, :]
    jrow = lax.broadcasted_iota(I32, (n_sel, nq), 0)
    rank = jnp.zeros((n_sel, nq), I32)
    for jp in range(n_sel):
        other = imp[jp:jp + 1, :]
        rank = rank + ((other > imp) | ((other == imp) & (jp < jrow))).astype(I32)
    chosen = jnp.where(rank < topn, 1.0, 0.0)
    chosen = jnp.concatenate([chosen, jnp.zeros((BLK - n_sel, nq), F32)], axis=0).astype(BF16)

    s_loc = lax.broadcasted_iota(I32, (KEYS, nq), 0)
    t_keys = t0 + lax.broadcasted_iota(I32, (KEYS, nq), 1)

    def attend(vt_ref, lanes, mask_fn):
        def body(kp, carry):
            ms, accs = carry
            r0 = pl.multiple_of(kp * KEYS, KEYS)
            kblk = z_ref[0, pl.ds(r0, KEYS), lanes].astype(BF16)
            lg = _dot(kblk, qs) + bsel_ref[jnp.minimum(i - kp, 2)]
            mask = mask_fn(r0)
            vt1 = vt_ref[kp]
            out = [_online_step(lg[:, nq * h:nq * (h + 1)], mask, vt1, ms[h], accs[h]) for h in range(HEADS)]
            return tuple(o[0] for o in out), tuple(o[1] for o in out)
        return body

    def mask_sel(r0):
        picked = _dot(exp_ref[pl.ds(r0, KEYS), :], chosen) > 0.5
        return picked & (r0 + s_loc <= t_keys)

    def mask_win(r0):
        dist = t_keys - (r0 + s_loc)
        return (dist >= 0) & (dist < NSA_WINDOW)

    _, acc_s = lax.fori_loop(0, i + 1, attend(vst_ref, slice(384, 448), mask_sel), _online_init(nq))
    first_w = jnp.maximum(i - NSA_WINDOW // KEYS, 0)
    _, acc_w = lax.fori_loop(first_w, i + 1, attend(vwt_ref, slice(512, 576), mask_win), _online_init(nq))

    outs = []
    for h in range(HEADS):
        outs.append(gates[3 * h:3 * h + 1, :] * o_cmp[h] + gates[3 * h + 1:3 * h + 2, :] * _online_result(acc_s[h])
                    + gates[3 * h + 2:3 * h + 3, :] * _online_result(acc_w[h]))
    o_ref[0] = _unstack_heads_t(outs)


def _nsa(zd, w1, pos, w2, bsel, bcmp, ovt, expand):
    b, s, wd = zd.shape
    full = lambda a: pl.BlockSpec(a.shape, lambda bi, i: (0,) * a.ndim)
    return pl.pallas_call(
        _nsa_kernel,
        grid=(b, s // QRY),
        in_specs=[pl.BlockSpec((1, s, wd), lambda bi, i: (bi, 0, 0)),
                  pl.BlockSpec((1, s, LANES), lambda bi, i: (bi, 0, GROUP_WIDTH // LANES)),
                  full(w1), full(pos), full(w2), full(bsel),
                  pl.BlockSpec((1, BLK, HEADS * QRY), lambda bi, i: (i, 0, 0)),
                  full(ovt), full(expand)],
        out_specs=pl.BlockSpec((1, QRY, GROUP_WIDTH), lambda bi, i: (bi, i, 0)),
        out_shape=jax.ShapeDtypeStruct((b, s, GROUP_WIDTH), F32),
        scratch_shapes=[pltpu.VMEM((s // KEYS, VT_ROWS, KEYS), BF16), pltpu.VMEM((s // KEYS, VT_ROWS, KEYS), BF16),
                        pltpu.VMEM((BLK, BLK), F32), pltpu.VMEM((BLK, BLK), F32)],
        compiler_params=pltpu.CompilerParams(dimension_semantics=("arbitrary", "arbitrary"),
                                             vmem_limit_bytes=VMEM_LIMIT_BYTES),
        name="nsa",
    )(zd, zd, w1, pos, w2, bsel, bcmp, ovt, expand)


def _outproj_kernel(alpha, x_ref, a_ref, b_ref, c_ref, d_ref, w_ref, bo_ref, g_ref, beta_ref, o_ref):
    acc = bo_ref[...] + _dot(a_ref[...].astype(BF16), w_ref[0:GROUP_WIDTH, :])
    for n, m_ref in enumerate((b_ref, c_ref, d_ref), start=1):
        acc = acc + _dot(m_ref[...].astype(BF16), w_ref[n * GROUP_WIDTH:(n + 1) * GROUP_WIDTH, :])
    o_ref[...] = _layer_norm_rows(alpha * x_ref[...] + acc, g_ref[...], beta_ref[...])


def _outproj(alpha, x2d, mixed, w, bo, g, beta, tm=512):
    t = x2d.shape[0]
    row_spec = lambda width: pl.BlockSpec((tm, width), lambda i: (i, 0))
    const = lambda a: pl.BlockSpec(a.shape, lambda i: (0, 0))
    return pl.pallas_call(
        functools.partial(_outproj_kernel, alpha),
        grid=(t // tm,),
        in_specs=[row_spec(D_MODEL)] + [row_spec(GROUP_WIDTH)] * N_MIXERS
                 + [const(w), const(bo), const(g), const(beta)],
        out_specs=row_spec(D_MODEL),
        out_shape=jax.ShapeDtypeStruct((t, D_MODEL), F32),
        compiler_params=pltpu.CompilerParams(dimension_semantics=("arbitrary",),
                                             vmem_limit_bytes=VMEM_LIMIT_BYTES),
        name="outproj_ln",
    )(x2d, *mixed, w, bo, g, beta)


def _ffn_kernel(alpha, x_ref, w1_ref, b1_ref, w2_ref, b2_ref, g_ref, beta_ref, o_ref, xb_ref, acc_ref):
    j = pl.program_id(1)

    @pl.when(j == 0)
    def _():
        xb_ref[...] = x_ref[...].astype(BF16)
        acc_ref[...] = jnp.zeros_like(acc_ref)

    hdn = jnp.maximum(_dot(xb_ref[...], w1_ref[...]) + b1_ref[...], 0.0)
    acc_ref[...] += _dot(jnp.square(hdn).astype(BF16), w2_ref[...])

    @pl.when(j == pl.num_programs(1) - 1)
    def _():
        y = alpha * x_ref[...] + (acc_ref[...] + b2_ref[...])
        o_ref[...] = _layer_norm_rows(y, g_ref[...], beta_ref[...])


def _ffn(alpha, x2d, w1, b1, w2, b2, g, beta, tm=1024, tf=1024):
    t = x2d.shape[0]
    return pl.pallas_call(
        functools.partial(_ffn_kernel, alpha),
        grid=(t // tm, D_FF // tf),
        in_specs=[pl.BlockSpec((tm, D_MODEL), lambda i, j: (i, 0)),
                  pl.BlockSpec((D_MODEL, tf), lambda i, j: (0, j)),
                  pl.BlockSpec((1, tf), lambda i, j: (0, j)),
                  pl.BlockSpec((tf, D_MODEL), lambda i, j: (j, 0)),
                  pl.BlockSpec((1, D_MODEL), lambda i, j: (0, 0)),
                  pl.BlockSpec((1, D_MODEL), lambda i, j: (0, 0)),
                  pl.BlockSpec((1, D_MODEL), lambda i, j: (0, 0))],
        out_specs=pl.BlockSpec((tm, D_MODEL), lambda i, j: (i, 0)),
        out_shape=jax.ShapeDtypeStruct((t, D_MODEL), F32),
        scratch_shapes=[pltpu.VMEM((tm, D_MODEL), BF16), pltpu.VMEM((tm, D_MODEL), F32)],
        compiler_params=pltpu.CompilerParams(dimension_semantics=("arbitrary", "arbitrary"),
                                             vmem_limit_bytes=VMEM_LIMIT_BYTES),
        name="ffn_ln",
    )(x2d, w1, b1, w2, b2, g, beta)


def _bias_of_distance(rel_bias_heads, dist):
    onehot = np.eye(NUM_BUCKETS, dtype=np.float32)[_t5_bucket_np(np.asarray(dist))]
    return jnp.dot(rel_bias_heads.T, jnp.asarray(onehot.T), precision=lax.Precision.HIGHEST)


def _shifted_rows(v, n_rows, n_cols, step):
    period = v.shape[-1]
    assert n_cols <= period - step
    flat = jnp.tile(v, (1,) * (v.ndim - 1) + (n_rows,))[..., :n_rows * (period - step)]
    return flat.reshape(v.shape[:-1] + (n_rows, period - step))[..., :n_cols]


def _wrapped(period):
    idx = np.arange(period)
    return np.where(idx < period // 2, idx, idx - period)


def _dilated_bias(rel_bias):
    x = _wrapped(4 * BLK)
    tabs = [_shifted_rows(_bias_of_distance(rel_bias[:, 0:HEADS], (BLK - x) * dil), BLK, 2 * BLK, 1)
            for _, dil in DIL_PATTERNS]
    tabs = jnp.stack(tabs).reshape(len(DIL_PATTERNS), HEADS // 2, 2, BLK, 2 * BLK)
    return jnp.transpose(tabs, (0, 1, 3, 2, 4)).reshape(len(DIL_PATTERNS), HEADS // 2, BLK, 4 * BLK)


def _toeplitz_bias_t(rel_bias_heads, blk=BLK):
    assert (_t5_bucket_np(np.arange(blk + 1, 64 * blk)) == NUM_BUCKETS - 1).all()
    x = _wrapped(2 * blk)
    tabs = [_shifted_rows(_bias_of_distance(rel_bias_heads, blk * delta + x), blk, blk, 1)
            for delta in range(3)]
    return jnp.transpose(jnp.stack(tabs), (0, 2, 1, 3)).reshape(3, blk, HEADS * blk)


def _compressed_bias_t(rel_bias_heads, seq):
    nq = seq // QRY
    x = _wrapped(2 * seq + BLK)
    v = _bias_of_distance(rel_bias_heads, x - (NSA_CMP_LEN - 1))
    tab = _shifted_rows(v, BLK, seq, NSA_CMP_STRIDE)
    tab = tab.reshape(HEADS, BLK, nq, QRY)
    return jnp.transpose(tab, (2, 1, 0, 3)).reshape(nq, BLK, HEADS * QRY)


def _nsa_constants(seq):
    n_cmp = (seq - NSA_CMP_LEN) // NSA_CMP_STRIDE + 1
    n_sel = seq // NSA_SEL_LEN
    cs = np.arange(n_cmp)[:, None] * NSA_CMP_STRIDE
    ss = np.arange(n_sel)[None, :] * NSA_SEL_LEN
    ov = np.clip(np.minimum(cs + NSA_CMP_LEN, ss + NSA_SEL_LEN) - np.maximum(cs, ss), 0, None) / NSA_CMP_LEN
    ovt = np.zeros((BLK, BLK), np.float32)
    ovt[:n_sel, :n_cmp] = ov.T
    expand = np.zeros((seq, BLK), np.float32)
    expand[np.arange(seq), np.arange(seq) // NSA_SEL_LEN] = 1.0
    return jnp.asarray(ovt), jnp.asarray(expand, BF16)


def _nsa_weights(cmp_pos, cmp_w1, cmp_w2):
    hd, hid = HEAD_DIM, NSA_CMP_HIDDEN
    w1 = cmp_w1.reshape(2, NSA_CMP_LEN, hd, hid)
    zeros = jnp.zeros((NSA_CMP_LEN, hd, hid), F32)
    w1 = jnp.concatenate([jnp.concatenate([w1[0], zeros], axis=-1),
                          jnp.concatenate([zeros, w1[1]], axis=-1)], axis=1)
    pos = jnp.concatenate([cmp_pos[0], cmp_pos[1]], axis=-1)
    z2 = jnp.zeros((hid, hd), F32)
    w2 = jnp.concatenate([jnp.concatenate([cmp_w2[0], z2], axis=-1),
                          jnp.concatenate([z2, cmp_w2[1]], axis=-1)], axis=0)
    return w1.astype(BF16), pos, w2.astype(BF16)


def kernel(x, w_in, b_in, a_conv, a_norm, d_cmp_pos, d_cmp_w1, d_cmp_w2, w_out, b_out, ln1_g, ln1_b,
           w_ff1, b_ff1, w_ff2, b_ff2, ln2_g, ln2_b, rel_bias):
    bsz, seq, _ = x.shape
    depth = w_in.shape[0]
    alpha = (2 * depth) ** 0.25
    nc = seq // M_CHUNK
    bias_dil = _dilated_bias(rel_bias)
    btab_dsa = _toeplitz_bias_t(rel_bias[:, HEADS:2 * HEADS], QRY)
    btab_nsa = _toeplitz_bias_t(rel_bias[:, 2 * HEADS:3 * HEADS], QRY)
    bcmp_nsa = _compressed_bias_t(rel_bias[:, 2 * HEADS:3 * HEADS], seq)
    ovt, expand = _nsa_constants(seq)

    h = x.reshape(bsz * seq, D_MODEL)
    for l in range(depth):
        w_l = _permute_columns(w_in[l]).astype(BF16)
        b_l = _permute_columns(b_in[l])[None, :]
        za, zb, zc, zd, zg = (z.reshape(bsz, seq, -1) for z in _inproj(h, w_l, b_l))
        out_a = _mlstm(za, _mlstm_gate_rows(zg), a_conv[l], a_norm[l][None, :])
        out_b = _dilated(zb, bias_dil)
        out_c = _dsa(zc, btab_dsa)
        nsa_w1, nsa_pos, nsa_w2 = _nsa_weights(d_cmp_pos[l], d_cmp_w1[l], d_cmp_w2[l])
        out_d = _nsa(zd, nsa_w1, nsa_pos, nsa_w2, btab_nsa, bcmp_nsa, ovt, expand)
        mixed = [o.reshape(bsz * seq, GROUP_WIDTH) for o in (out_a, out_b, out_c, out_d)]
        h = _outproj(alpha, h, mixed, w_out[l].astype(BF16), b_out[l][None, :],
                     ln1_g[l][None, :], ln1_b[l][None, :])
        h = _ffn(alpha, h, w_ff1[l].astype(BF16), b_ff1[l][None, :], w_ff2[l].astype(BF16),
                 b_ff2[l][None, :], ln2_g[l][None, :], ln2_b[l][None, :])
    return h.reshape(bsz, seq, D_MODEL)
```

```python
import functools
import math

import numpy as np
import jax
import jax.numpy as jnp
from jax import lax
from jax.experimental import pallas as pl
from jax.experimental.pallas import tpu as pltpu

F32 = jnp.float32
BF16 = jnp.bfloat16
I32 = jnp.int32
I16 = jnp.int16
I16_MIN = -2 ** 15

D_MODEL = 1024
N_MIXERS = 4
HEADS = 4
HEAD_DIM = D_MODEL // (N_MIXERS * HEADS)
GROUP_WIDTH = HEADS * HEAD_DIM
D_FF = 4 * D_MODEL
LN_EPS = 1e-5
NEG = -1e30

M_QK_DIM = HEAD_DIM // 2
M_CHUNK = 64
M_CONV = 4
DIL_PATTERNS = ((128, 1), (512, 4), (2048, 16))
IDX_HEADS = 4
IDX_DIM = 64
DSA_TOPK = 256
NSA_CMP_LEN = 32
NSA_CMP_STRIDE = 16
NSA_SEL_LEN = 64
NSA_TOPN = 16
NSA_WINDOW = 512
NSA_CMP_HIDDEN = 256
NSA_FORCE = 1e9
NUM_BUCKETS = 32
MAX_DISTANCE = 128

LANES = 128
BLK = 128
KEYS = 2 * BLK
QRY = KEYS
VMEM_LIMIT_BYTES = 56 * 1024 * 1024

IN_SPLITS = (
    ('a_q', HEADS * M_QK_DIM), ('a_k', HEADS * M_QK_DIM), ('a_v', GROUP_WIDTH),
    ('a_i', HEADS), ('a_f', HEADS), ('a_o', GROUP_WIDTH),
    ('b_q', GROUP_WIDTH), ('b_k', GROUP_WIDTH), ('b_v', GROUP_WIDTH),
    ('c_q', GROUP_WIDTH), ('c_k', HEAD_DIM), ('c_v', HEAD_DIM),
    ('c_iq', IDX_HEADS * IDX_DIM), ('c_ik', IDX_DIM), ('c_iw', IDX_HEADS),
    ('d_q', GROUP_WIDTH), ('d_kc', HEAD_DIM), ('d_vc', HEAD_DIM),
    ('d_ks', HEAD_DIM), ('d_vs', HEAD_DIM), ('d_kw', HEAD_DIM), ('d_vw', HEAD_DIM),
    ('d_g', 3 * HEADS),
)

GROUP_LAYOUT = (
    (('a_q', 'a_k'), ('a_v',), ('a_o',), (('a_i', HEAD_DIM),), (('a_i', M_QK_DIM),),
     (('a_f', HEAD_DIM),), (('a_f', M_QK_DIM),)),
    (('b_q',), ('b_k',), ('b_v',)),
    (('c_q',), ('c_iq',), ('c_k', 'c_v'), ('c_ik', 'c_iw')),
    (('d_q',), ('d_kc', 'd_vc'), ('d_ks', 'd_vs'), ('d_kw', 'd_vw'), ('d_g',)),
    (('a_i', 'a_f'),),
)


def _round_up(n, m):
    return -(-n // m) * m


def _projection_layout():
    offs, off = {}, 0
    for name, width in IN_SPLITS:
        offs[name] = (off, width)
        off += width
    runs, group_widths = [], []
    for group in GROUP_LAYOUT:
        gwidth = 0
        for chunk in group:
            cwidth = 0
            for entry in chunk:
                name, rep = entry if isinstance(entry, tuple) else (entry, 1)
                o, w = offs[name]
                runs.append((o, w, rep))
                cwidth += w * rep
            pad = _round_up(cwidth, LANES) - cwidth
            if pad:
                runs.append((-1, pad, 1))
            gwidth += cwidth + pad
        group_widths.append(gwidth)
    return tuple(runs), tuple(group_widths)


PROJ_RUNS, GROUP_WIDTHS = _projection_layout()
PROJ_WIDTH = int(sum(GROUP_WIDTHS))
PROJ_PERM = np.concatenate([np.repeat(np.arange(o, o + w), r) if o >= 0 else np.full(w, -1)
                            for o, w, r in PROJ_RUNS]).astype(np.int32)


def _permute_columns(a):
    parts = []
    for o, w, r in PROJ_RUNS:
        if o < 0:
            parts.append(jnp.zeros(a.shape[:-1] + (w,), a.dtype))
        else:
            parts.append(a[..., o:o + w] if r == 1 else jnp.repeat(a[..., o:o + w], r, axis=-1))
    return jnp.concatenate(parts, axis=-1)


def _t5_bucket_np(dist):
    n = np.maximum(dist, 0)
    max_exact = NUM_BUCKETS // 2
    nf = np.maximum(n, max_exact).astype(np.float32)
    large = max_exact + (np.log(nf / max_exact) / math.log(MAX_DISTANCE / max_exact)
                         * (NUM_BUCKETS - max_exact)).astype(np.int32)
    large = np.minimum(large, NUM_BUCKETS - 1)
    return np.where(n < max_exact, n, large).astype(np.int32)


def _nt_dot(a, b, precision=None):
    return lax.dot_general(a, b, (((1,), (1,)), ((), ())), precision=precision,
                           preferred_element_type=F32)


def _dot(a, b, precision=None):
    return jnp.dot(a, b, precision=precision, preferred_element_type=F32)


def _layer_norm_rows(y, g, b):
    mu = jnp.mean(y, axis=-1, keepdims=True)
    var = jnp.mean(jnp.square(y - mu), axis=-1, keepdims=True)
    return (y - mu) * lax.rsqrt(var + LN_EPS) * g + b


def _sigmoid(x):
    return 1.0 / (1.0 + jnp.exp(-x))


def _log_sigmoid(x):
    return -(jnp.maximum(-x, 0.0) + jnp.log1p(jnp.exp(-jnp.abs(x))))


def _sortable_key(x):
    bits = pltpu.bitcast(x, I32)
    return bits ^ ((bits >> 31) & jnp.int32(0x7FFFFFFF))


def _inproj_kernel(x_ref, w_ref, b_ref, *out_refs):
    xb = x_ref[...].astype(BF16)
    off = 0
    for o_ref, width in zip(out_refs, GROUP_WIDTHS):
        o_ref[...] = _dot(xb, w_ref[:, off:off + width]) + b_ref[:, off:off + width]
        off += width


def _inproj(x2d, w, b, tm=512):
    t = x2d.shape[0]
    return pl.pallas_call(
        _inproj_kernel,
        grid=(t // tm,),
        in_specs=[pl.BlockSpec((tm, D_MODEL), lambda i: (i, 0)),
                  pl.BlockSpec((D_MODEL, PROJ_WIDTH), lambda i: (0, 0)),
                  pl.BlockSpec((1, PROJ_WIDTH), lambda i: (0, 0))],
        out_specs=[pl.BlockSpec((tm, gw), lambda i: (i, 0)) for gw in GROUP_WIDTHS],
        out_shape=[jax.ShapeDtypeStruct((t, gw), F32) for gw in GROUP_WIDTHS],
        compiler_params=pltpu.CompilerParams(dimension_semantics=("arbitrary",),
                                             vmem_limit_bytes=VMEM_LIMIT_BYTES),
        name="inproj",
    )(x2d, w, b)


def _split_terms(x, n):
    terms, rest = [], x
    for _ in range(n):
        terms.append(rest.astype(BF16))
        rest = rest - terms[-1].astype(F32)
    return terms


def _iota(shape, dim):
    return lax.broadcasted_iota(I32, shape, dim)


def _mlstm_kernel(z_ref, gt_ref, cw_ref, ng_ref, o_ref, xpad_ref):
    seq = z_ref.shape[1]
    L, DK, DV, H = M_CHUNK, M_QK_DIM, HEAD_DIM, HEADS
    assert L == DV
    wq, wv = H * DK, H * DV
    lg_dk, lg_dv = int(math.log2(DK)), int(math.log2(DV))
    c_v, c_o, c_i64 = 2 * wq, 2 * wq + wv, 2 * wq + 2 * wv
    c_i32, c_f64 = c_i64 + wv, c_i64 + wv + wq
    xpad_ref[0:8, :] = jnp.zeros((8, 2 * wq), F32)
    xpad_ref[8:, :] = z_ref[0, :, 0:2 * wq]

    one_if = lambda cond: jnp.where(cond, 1.0, 0.0).astype(BF16)
    tri_l = one_if(_iota((L, L), 0) >= _iota((L, L), 1))
    trow = _iota((L, wv), 0)
    tri_heads = trow >= (_iota((L, wv), 1) & (L - 1))
    r_vv, c_vv = _iota((wv, wv), 0), _iota((wv, wv), 1)
    same_head = (r_vv >> lg_dv) == (c_vv >> lg_dv)
    ones_bd = one_if(same_head)
    mean_bd = jnp.where(same_head, 1.0 / DV, 0.0).astype(BF16)
    tri_u_bd = one_if(same_head & ((r_vv & (L - 1)) <= (c_vv & (L - 1))))
    state_mask = (_iota((wq, wv), 0) >> lg_dk) == (_iota((wq, wv), 1) >> lg_dv)
    eye_q = one_if(_iota((wq, wq), 0) == _iota((wq, wq), 1))
    head_of_qlane = _iota((L, wq), 1) >> lg_dk
    head_of_vlane = _iota((L, wv), 1) >> lg_dv
    row8 = _iota((8, wv), 0)
    cw = cw_ref[...]
    ng = ng_ref[...]

    def head_mean(x):
        hi_lo = _split_terms(x, 2)
        r = _dot(jnp.concatenate(hi_lo, axis=0), mean_bd)
        return r[0:L] + r[L:2 * L]

    def chunk(c, carry):
        cbd, nbd, m64, m32 = carry
        s0 = pl.multiple_of(c * L, L)
        rows = pl.ds(s0, L)
        xw = xpad_ref[pl.ds(s0, L + 8), :]
        y = sum(cw[j:j + 1, :] * xw[5 + j:5 + j + L, :] for j in range(M_CONV))
        qk = y * _sigmoid(y)
        q = qk[:, 0:wq]
        k = qk[:, wq:] * (DK ** -0.5)
        qb = q.astype(BF16)
        v = z_ref[0, rows, c_v:c_v + wv]
        i64 = z_ref[0, rows, c_i64:c_i64 + wv]
        i32 = z_ref[0, rows, c_i32:c_i32 + wq]
        gr = gt_ref[0, c]

        flog = _log_sigmoid(z_ref[0, rows, c_f64:c_f64 + wv + wq])
        bsum = _dot(tri_l, jnp.concatenate(_split_terms(flog, 3), axis=1))
        w3 = wv + wq
        ball = bsum[:, 0:w3] + bsum[:, w3:2 * w3] + bsum[:, 2 * w3:3 * w3]
        b64, b32 = ball[:, 0:wv], ball[:, wv:w3]
        fterms = [t.astype(F32) for t in _split_terms(_log_sigmoid(gr[1:2, :]), 3)]
        frows = jnp.where(row8 == 0, fterms[0], jnp.where(row8 == 1, fterms[1], jnp.where(row8 == 2, fterms[2], 0.0)))
        bparts = _dot(frows.astype(BF16), tri_u_bd)
        brow = bparts[0:1, :] + bparts[1:2, :] + bparts[2:3, :]

        dall = jnp.where(tri_heads, b64 - brow + gr[0:1, :], NEG)
        cm = i64 - b64
        for sh in (1, 2, 4, 8, 16, 32):
            cm = jnp.where(trow >= sh, jnp.maximum(cm, pltpu.roll(cm, sh, 0)), cm)
        inter = b64 + m64
        m_t = jnp.maximum(inter, b64 + cm)
        kbd = jnp.concatenate([jnp.where(head_of_qlane == h, k, 0.0) for h in range(H)], axis=0).astype(BF16)
        sc = _nt_dot(qb, kbd) * jnp.exp(dall - m_t)
        wi = jnp.exp(inter - m_t)
        vb = v.astype(BF16)
        vbd = jnp.concatenate([jnp.where(head_of_vlane == h, v, 0.0) for h in range(H)], axis=0).astype(BF16)
        pv = _dot(sc.astype(BF16), jnp.concatenate([vbd, ones_bd], axis=1))
        qst = _dot(qb, jnp.concatenate([cbd, nbd], axis=1).astype(BF16))
        num = pv[:, 0:wv] + wi * qst[:, 0:wv]
        den = pv[:, wv:] + wi * qst[:, wv:]
        hh = num / jnp.maximum(jnp.abs(den), jnp.exp(-m_t))

        og = _sigmoid(z_ref[0, rows, c_o:c_o + wv]) * hh
        dev = og - head_mean(og)
        o_ref[0, rows, :] = dev * lax.rsqrt(head_mean(dev * dev) + LN_EPS) * ng

        bl64, bl32 = b64[L - 1:L, :], b32[L - 1:L, :]
        m64_new = jnp.maximum(bl64 + m64, jnp.max(bl64 - b64 + i64, axis=0, keepdims=True))
        g32 = bl32 - b32 + i32
        m32_new = jnp.maximum(bl32 + m32, jnp.max(g32, axis=0, keepdims=True))
        kw = k * jnp.exp(g32 - m32_new)
        wc = jnp.exp(bl64 + m64 - m64_new)
        kwt = _nt_dot(eye_q, kw.astype(BF16)).astype(BF16)
        upd = _dot(kwt, jnp.concatenate([vb, jnp.ones((L, wv), BF16)], axis=1))
        cbd = wc * cbd + jnp.where(state_mask, upd[:, 0:wv], 0.0)
        nbd = wc * nbd + jnp.where(state_mask, upd[:, wv:], 0.0)
        return cbd, nbd, m64_new, m32_new

    init = (jnp.zeros((wq, wv), F32), jnp.zeros((wq, wv), F32), jnp.zeros((1, wv), F32), jnp.zeros((1, wq), F32))
    lax.fori_loop(0, seq // L, chunk, init, unroll=4)


def _mlstm(za, gates_t, conv_w, norm_g):
    b, s, wa = za.shape
    nc = s // M_CHUNK
    return pl.pallas_call(
        _mlstm_kernel,
        grid=(b,),
        in_specs=[pl.BlockSpec((1, s, wa), lambda i: (i, 0, 0)),
                  pl.BlockSpec((1, nc) + gates_t.shape[2:], lambda i: (i, 0, 0, 0)),
                  pl.BlockSpec((M_CONV, 2 * HEADS * M_QK_DIM), lambda i: (0, 0)),
                  pl.BlockSpec((1, GROUP_WIDTH), lambda i: (0, 0))],
        out_specs=pl.BlockSpec((1, s, GROUP_WIDTH), lambda i: (i, 0, 0)),
        out_shape=jax.ShapeDtypeStruct((b, s, GROUP_WIDTH), F32),
        scratch_shapes=[pltpu.VMEM((s + 8, 2 * HEADS * M_QK_DIM), F32)],
        compiler_params=pltpu.CompilerParams(dimension_semantics=("arbitrary",),
                                             vmem_limit_bytes=VMEM_LIMIT_BYTES),
        name="mlstm",
    )(za, gates_t, conv_w, norm_g)


def _mlstm_gate_rows(zg):
    b, s, _ = zg.shape
    gates = zg[:, :, 0:2 * HEADS].reshape(b, s // M_CHUNK, M_CHUNK, 2, HEADS)
    return jnp.transpose(gates, (0, 1, 3, 4, 2)).reshape(b, s // M_CHUNK, 2, HEADS * M_CHUNK)


def _dilated_kernel(q0_ref, q1_ref, k0_ref, k1_ref, v0_ref, v1_ref, bias_ref, o_ref, acc_ref, mx_ref, den_ref):
    seq = q0_ref.shape[1]
    W = BLK
    hd = HEAD_DIM
    npair = HEADS // 2
    scale = hd ** -0.5
    assert math.log2(scale).is_integer()
    nk = 2 * W
    qi = _iota((W, 2 * nk), 0)
    ki = _iota((W, 2 * nk), 1) & (nk - 1)
    j = W + qi - ki
    band = (j >= 0) & (j <= W)
    head_of_lane = _iota((nk, 2 * hd), 1) >> int(math.log2(hd))
    ones_bd = jnp.concatenate([jnp.where(head_of_lane == hh, 1.0, 0.0) for hh in range(2)], axis=0).astype(BF16)
    q_refs, k_refs, v_refs = (q0_ref, q1_ref), (k0_ref, k1_ref), (v0_ref, v1_ref)

    def block_diag(x):
        return jnp.concatenate([jnp.where(head_of_lane == hh, x, 0.0) for hh in range(2)], axis=0).astype(BF16)

    for br, (window, dil) in enumerate(DIL_PATTERNS):
        assert window // dil == W
        nb = (seq // dil) // W

        def piece(idx, _, br=br, dil=dil, nb=nb):
            n = idx % nb
            if dil == 1:
                rows_q = pl.ds(pl.multiple_of(W * n, W), W)
                rows_p = pl.ds(pl.multiple_of(W * jnp.maximum(n - 1, 0), W), W)
            else:
                r = idx // nb
                rows_q = pl.ds(r + dil * W * n, W, stride=dil)
                rows_p = pl.ds(r + dil * W * jnp.maximum(n - 1, 0), W, stride=dil)
            mask = band & (ki >= jnp.where(n > 0, 0, W))
            for pr in range(npair):
                q = (q_refs[pr][0, rows_q, :] * scale).astype(BF16)
                k2 = jnp.concatenate([k_refs[pr][0, rows_p, :], k_refs[pr][0, rows_q, :]], axis=0)
                v2 = jnp.concatenate([v_refs[pr][0, rows_p, :], v_refs[pr][0, rows_q, :]], axis=0)
                lg = jnp.where(mask, _nt_dot(q, block_diag(k2)) + bias_ref[br, pr], NEG)
                ps, mxs = [], []
                for hh in range(2):
                    sl = slice(nk * hh, nk * (hh + 1))
                    m = jnp.max(lg[:, sl], axis=-1, keepdims=True)
                    ps.append(jnp.where(mask[:, sl], jnp.exp(lg[:, sl] - m), 0.0))
                    mxs.append(jnp.broadcast_to(m, (W, hd)))
                p = jnp.concatenate(ps, axis=-1).astype(BF16)
                pv = _dot(p, jnp.concatenate([block_diag(v2), ones_bd], axis=1))
                acc_ref[br, pr, rows_q, :] = pv[:, 0:2 * hd]
                den_ref[br, pr, rows_q, :] = pv[:, 2 * hd:]
                mx_ref[br, pr, rows_q, :] = jnp.concatenate(mxs, axis=-1)
            return 0

        lax.fori_loop(0, dil * nb, piece, 0, unroll=2)

    def combine(i, _):
        rows = pl.ds(pl.multiple_of(i * W, W), W)
        outs = []
        for pr in range(npair):
            ms = [mx_ref[b, pr, rows, :] for b in range(len(DIL_PATTERNS))]
            top = functools.reduce(jnp.maximum, ms)
            es = [jnp.exp(m - top) for m in ms]
            num = sum(e * acc_ref[b, pr, rows, :] for b, e in enumerate(es))
            den = sum(e * jnp.maximum(den_ref[b, pr, rows, :], 1e-30) for b, e in enumerate(es))
            outs.append(num / den)
        o_ref[0, rows, :] = jnp.concatenate(outs, axis=-1)
        return 0

    lax.fori_loop(0, seq // W, combine, 0)


def _dilated(zb, bias):
    b, s, wb = zb.shape
    nbr = len(DIL_PATTERNS)
    pair_spec = lambda c: pl.BlockSpec((1, s, LANES), lambda i: (i, 0, c))
    return pl.pallas_call(
        _dilated_kernel,
        grid=(b,),
        in_specs=[pair_spec(c) for c in range(wb // LANES)]
                 + [pl.BlockSpec(bias.shape, lambda i: (0, 0, 0, 0))],
        out_specs=pl.BlockSpec((1, s, GROUP_WIDTH), lambda i: (i, 0, 0)),
        out_shape=jax.ShapeDtypeStruct((b, s, GROUP_WIDTH), F32),
        scratch_shapes=[pltpu.VMEM((nbr, HEADS // 2, s, LANES), F32) for _ in range(3)],
        compiler_params=pltpu.CompilerParams(dimension_semantics=("arbitrary",),
                                             vmem_limit_bytes=VMEM_LIMIT_BYTES),
        name="dilated",
    )(*([zb] * (wb // LANES)), bias)


def _head_columns_t(q):
    qt = q.T
    return jnp.concatenate([qt[HEAD_DIM * h:HEAD_DIM * (h + 1), :] for h in range(HEADS)], axis=1)


LOG2_E = math.log2(math.e)


def _scaled_query_columns_t(q):
    return (_head_columns_t(q) * (HEAD_DIM ** -0.5 * LOG2_E)).astype(BF16)


def _unstack_heads_t(per_head_t):
    halves = []
    for h in range(0, HEADS, 2):
        halves.append(jnp.concatenate([per_head_t[h], per_head_t[h + 1]], axis=0).T)
    return jnp.concatenate(halves, axis=-1)


VT_ROWS = HEAD_DIM + 16


def _online_step(lg, mask, vt1, m, acc):
    lg = jnp.where(mask, lg, NEG)
    m_new = jnp.maximum(m, jnp.max(lg, axis=0, keepdims=True))
    p = jnp.where(mask, jnp.exp2(lg - m_new), 0.0)
    acc = acc * jnp.exp2(m - m_new) + _dot(vt1, p.astype(BF16))
    return m_new, acc


def _online_result(acc):
    return acc[0:HEAD_DIM, :] / jnp.maximum(acc[HEAD_DIM:HEAD_DIM + 1, :], 1e-30)


def _values_with_ones_t(z_rows):
    vt = z_rows.T[HEAD_DIM:2 * HEAD_DIM, :]
    return jnp.concatenate([vt, jnp.ones((VT_ROWS - HEAD_DIM, vt.shape[1]), F32)], axis=0).astype(BF16)


def _pair_bias(btab_ref, behind):
    return jnp.concatenate([btab_ref[jnp.minimum(behind, 2)], btab_ref[jnp.clip(behind - 1, 0, 2)]], axis=0)


def _online_init(nq=BLK):
    return (tuple(jnp.full((1, nq), NEG, F32) for _ in range(HEADS)),
            tuple(jnp.zeros((VT_ROWS, nq), F32) for _ in range(HEADS)))


def _dsa_kernel(z_ref, btab_ref, o_ref, vt_ref, key_ref, hi_ref, lo_ref, lom_ref):
    i = pl.program_id(1)
    seq = z_ref.shape[1]
    nkb = seq // BLK
    nq = QRY
    hd = HEAD_DIM
    topk = min(DSA_TOPK, seq // 4)
    t0 = pl.multiple_of(i * nq, nq)

    assert nq == KEYS
    npair = i + 1

    @pl.when(i == 0)
    def _():
        for kb in range(nkb):
            vt_ref[kb // 2, :, (kb % 2) * BLK:(kb % 2 + 1) * BLK] = _values_with_ones_t(
                z_ref[0, kb * BLK:(kb + 1) * BLK, 512:640])
        key_ref[...] = jnp.full(key_ref.shape, -2 ** 31, I32)
        for half_ref in (hi_ref, lo_ref, lom_ref):
            half_ref[...] = jnp.full(half_ref.shape, I16_MIN, I16)

    zq = z_ref[0, pl.ds(t0, nq), :]
    cq = zq[:, 0:256]
    ciq = _head_columns_t(zq[:, 256:512]).astype(BF16)
    iw = zq[:, 640:768].T[IDX_DIM:IDX_DIM + IDX_HEADS, :] * ((IDX_HEADS * IDX_DIM) ** -0.5)
    s_loc = lax.broadcasted_iota(I32, (KEYS, nq), 0)
    t_glob = t0 + lax.broadcasted_iota(I32, (KEYS, nq), 1)

    def score_block(kp, _):
        r0 = pl.multiple_of(kp * KEYS, KEYS)
        ik = z_ref[0, pl.ds(r0, KEYS), 640:704].astype(BF16)
        rel = _dot(ik, ciq)
        sc = jnp.zeros((KEYS, nq), F32)
        for h in range(IDX_HEADS):
            sc = sc + jnp.maximum(rel[:, nq * h:nq * (h + 1)], 0.0) * iw[h:h + 1, :]
        sc = jnp.where(r0 + s_loc <= t_glob, sc, NEG)
        key = _sortable_key(sc)
        key_ref[kp] = key
        hi_ref[kp] = (key >> 16).astype(I16)
        lo_ref[kp] = ((key & 0xFFFF) - 2 ** 15).astype(I16)
        return 0

    lax.fori_loop(0, npair, score_block, 0)

    def count(ref, pred, pairs):
        dt = ref.dtype
        rows = 8 * 4 // dt.itemsize
        def body(kp, acc):
            hit = jnp.where(pred(kp, ref[kp]), jnp.ones((), dt), jnp.zeros((), dt))
            hit = hit.reshape(KEYS // rows, rows, nq)
            parts = [hit[n] for n in range(KEYS // rows)]
            while len(parts) > 1:
                parts = [a + b for a, b in zip(parts[0::2], parts[1::2])]
            return acc + parts[0]
        acc = jnp.zeros((rows, nq), dt)
        if isinstance(pairs, int):
            for kp in range(pairs):
                acc = body(kp, acc)
        else:
            acc = lax.fori_loop(0, pairs, body, acc)
        return jnp.sum(acc.astype(I32), axis=0, keepdims=True)

    def threshold(pairs):
        def half_search(ref, k):
            def bit(it, thr):
                cand = thr + lax.shift_left(jnp.int32(1), 15 - it)
                c = count(ref, lambda kp, half: half >= cand.astype(I16), pairs)
                return jnp.where(c >= k, cand, thr)
            return lax.fori_loop(0, 16, bit, jnp.full((1, nq), I16_MIN, I32))

        def run(_):
            thr_hi = half_search(hi_ref, topk)
            thr_hi16 = thr_hi.astype(I16)
            above_hi = count(hi_ref, lambda kp, half: half > thr_hi16, pairs)
            for kp in range(pairs):
                lom_ref[kp] = jnp.where(hi_ref[kp] == thr_hi16, lo_ref[kp], jnp.int16(I16_MIN))
            thr_lo = half_search(lom_ref, topk - above_hi)
            thr = (thr_hi << 16) | (thr_lo + 2 ** 15)
            n_gt = count(key_ref, lambda kp, key: key > thr, pairs)
            need = (topk - n_gt).astype(F32)
            earlier = jnp.zeros((1, nq), F32)
            for kp in range(pairs):
                tie = key_ref[kp] == thr
                tie01 = jnp.where(tie, 1.0, 0.0)
                rank = _dot(below, tie01.astype(BF16)) + earlier
                lom_ref[kp] = jnp.where(tie & (rank < need), 1, 0).astype(I16)
                earlier = earlier + jnp.sum(tie01, axis=0, keepdims=True)
            return thr
        return run

    below = jnp.where(_iota((KEYS, KEYS), 1) < _iota((KEYS, KEYS), 0), 1.0, 0.0).astype(BF16)
    thr = lax.switch(npair - 1, [threshold(p) for p in range(1, seq // KEYS + 1)], 0)

    qs = _scaled_query_columns_t(cq)

    def attend(kp, carry):
        ms, accs = carry
        r0 = pl.multiple_of(kp * KEYS, KEYS)
        kblk = z_ref[0, pl.ds(r0, KEYS), 512:576].astype(BF16)
        lg = _dot(kblk, qs) + btab_ref[jnp.minimum(i - kp, 2)]
        key = key_ref[kp]
        s_glob = r0 + s_loc
        mask = ((key > thr) | (lom_ref[kp].astype(I32) != 0)) & (s_glob <= t_glob)
        vt1 = vt_ref[kp]
        out = [_online_step(lg[:, nq * h:nq * (h + 1)], mask, vt1, ms[h], accs[h]) for h in range(HEADS)]
        return tuple(o[0] for o in out), tuple(o[1] for o in out)

    ms, accs = lax.fori_loop(0, npair, attend, _online_init(nq))
    o_ref[0] = _unstack_heads_t([_online_result(accs[h]) for h in range(HEADS)])


def _dsa(zc, btab):
    b, s, wc = zc.shape
    return pl.pallas_call(
        _dsa_kernel,
        grid=(b, s // QRY),
        in_specs=[pl.BlockSpec((1, s, wc), lambda bi, i: (bi, 0, 0)),
                  pl.BlockSpec(btab.shape, lambda bi, i: (0, 0, 0))],
        out_specs=pl.BlockSpec((1, QRY, GROUP_WIDTH), lambda bi, i: (bi, i, 0)),
        out_shape=jax.ShapeDtypeStruct((b, s, GROUP_WIDTH), F32),
        scratch_shapes=[pltpu.VMEM((s // KEYS, VT_ROWS, KEYS), BF16), pltpu.VMEM((s // KEYS, KEYS, QRY), I32)]
                       + [pltpu.VMEM((s // KEYS, KEYS, QRY), I16)] * 3,
        compiler_params=pltpu.CompilerParams(dimension_semantics=("arbitrary", "arbitrary"),
                                             vmem_limit_bytes=VMEM_LIMIT_BYTES),
        name="dsa",
    )(zc, btab)


def _nsa_kernel(z_ref, zc_ref, w1_ref, pos_ref, w2_ref, bsel_ref, bcmp_ref, ovt_ref, exp_ref, o_ref,
                vst_ref, vwt_ref, cmp_ref, cmpt_ref):
    i = pl.program_id(1)
    seq = z_ref.shape[1]
    nkb = seq // BLK
    hd = HEAD_DIM
    scale = hd ** -0.5
    n_cmp = (seq - NSA_CMP_LEN) // NSA_CMP_STRIDE + 1
    n_sel = seq // NSA_SEL_LEN
    topn = min(NSA_TOPN, n_sel)
    half = NSA_CMP_LEN // 2
    assert half == NSA_CMP_STRIDE and n_cmp + 1 == seq // NSA_CMP_STRIDE == BLK and n_sel <= BLK
    nq = QRY
    assert nq == KEYS
    t0 = pl.multiple_of(i * nq, nq)
    hi = lax.Precision.HIGHEST

    @pl.when(i == 0)
    def _():
        for kb in range(nkb):
            rows = slice(kb * BLK, (kb + 1) * BLK)
            cols = slice((kb % 2) * BLK, (kb % 2 + 1) * BLK)
            vst_ref[kb // 2, :, cols] = _values_with_ones_t(z_ref[0, rows, 384:512])
            vwt_ref[kb // 2, :, cols] = _values_with_ones_t(z_ref[0, rows, 512:640])
        first = jnp.zeros((BLK, 2 * NSA_CMP_HIDDEN), F32)
        second = jnp.zeros((BLK, 2 * NSA_CMP_HIDDEN), F32)
        for j in range(half):
            xj = zc_ref[0, pl.ds(j, BLK, stride=NSA_CMP_STRIDE), :]
            first = first + _dot((xj + pos_ref[j:j + 1, :]).astype(BF16), w1_ref[j])
            second = second + _dot((xj + pos_ref[half + j:half + j + 1, :]).astype(BF16), w1_ref[half + j])
        hid = first + pltpu.roll(second, BLK - 1, 0)
        hid = hid * _sigmoid(hid)
        cmp = _dot(hid.astype(BF16), w2_ref[...])
        cmp_ref[...] = cmp
        cmpt_ref[...] = cmp.T

    zq = z_ref[0, pl.ds(t0, nq), :]
    qs = _scaled_query_columns_t(zq[:, 0:256])
    gates = _sigmoid(zq[:, 640:768].T[0:16, :])
    row = lax.broadcasted_iota(I32, (BLK, nq), 0)
    t_glob = t0 + lax.broadcasted_iota(I32, (BLK, nq), 1)

    kcmp = cmp_ref[:, 0:hd].astype(BF16)
    vcmpt = cmpt_ref[hd:2 * hd, :].astype(BF16)
    lgc = _dot(kcmp, qs) + bcmp_ref[0]
    mask_c = (t_glob - (row * NSA_CMP_STRIDE + NSA_CMP_LEN - 1) >= 0) & (row < n_cmp)
    o_cmp, psum = [], jnp.zeros((BLK, nq), F32)
    for h in range(HEADS):
        lg = jnp.where(mask_c, lgc[:, nq * h:nq * (h + 1)], NEG)
        m = jnp.max(lg, axis=0, keepdims=True)
        p = jnp.where(mask_c, jnp.exp2(lg - m), 0.0)
        p = p / jnp.maximum(jnp.sum(p, axis=0, keepdims=True), 1e-30)
        o_cmp.append(_dot(vcmpt, p.astype(BF16)))
        psum = psum + p

    imp = _dot(ovt_ref[...], psum, precision=hi)
    cur = t_glob >> int(math.log2(NSA_SEL_LEN))
    forced = (row == 0) | (row == cur) | (row == cur - 1)
    imp = jnp.where(forced, NSA_FORCE, imp)
    imp = jnp.where(row * NSA_SEL_LEN <= t_glob, imp, NEG)
    imp = imp[0:n_sel, :]
    jrow = lax.broadcasted_iota(I32, (n_sel, nq), 0)
    rank = jnp.zeros((n_sel, nq), I32)
    for jp in range(n_sel):
        other = imp[jp:jp + 1, :]
        rank = rank + ((other > imp) | ((other == imp) & (jp < jrow))).astype(I32)
    chosen = jnp.where(rank < topn, 1.0, 0.0)
    chosen = jnp.concatenate([chosen, jnp.zeros((BLK - n_sel, nq), F32)], axis=0).astype(BF16)

    s_loc = lax.broadcasted_iota(I32, (KEYS, nq), 0)
    t_keys = t0 + lax.broadcasted_iota(I32, (KEYS, nq), 1)

    def attend(vt_ref, lanes, mask_fn):
        def body(kp, carry):
            ms, accs = carry
            r0 = pl.multiple_of(kp * KEYS, KEYS)
            kblk = z_ref[0, pl.ds(r0, KEYS), lanes].astype(BF16)
            lg = _dot(kblk, qs) + bsel_ref[jnp.minimum(i - kp, 2)]
            mask = mask_fn(r0)
            vt1 = vt_ref[kp]
            out = [_online_step(lg[:, nq * h:nq * (h + 1)], mask, vt1, ms[h], accs[h]) for h in range(HEADS)]
            return tuple(o[0] for o in out), tuple(o[1] for o in out)
        return body

    def mask_sel(r0):
        picked = _dot(exp_ref[pl.ds(r0, KEYS), :], chosen) > 0.5
        return picked & (r0 + s_loc <= t_keys)

    def mask_win(r0):
        dist = t_keys - (r0 + s_loc)
        return (dist >= 0) & (dist < NSA_WINDOW)

    _, acc_s = lax.fori_loop(0, i + 1, attend(vst_ref, slice(384, 448), mask_sel), _online_init(nq))
    first_w = jnp.maximum(i - NSA_WINDOW // KEYS, 0)
    _, acc_w = lax.fori_loop(first_w, i + 1, attend(vwt_ref, slice(512, 576), mask_win), _online_init(nq))

    outs = []
    for h in range(HEADS):
        outs.append(gates[3 * h:3 * h + 1, :] * o_cmp[h] + gates[3 * h + 1:3 * h + 2, :] * _online_result(acc_s[h])
                    + gates[3 * h + 2:3 * h + 3, :] * _online_result(acc_w[h]))
    o_ref[0] = _unstack_heads_t(outs)


def _nsa(zd, w1, pos, w2, bsel, bcmp, ovt, expand):
    b, s, wd = zd.shape
    full = lambda a: pl.BlockSpec(a.shape, lambda bi, i: (0,) * a.ndim)
    return pl.pallas_call(
        _nsa_kernel,
        grid=(b, s // QRY),
        in_specs=[pl.BlockSpec((1, s, wd), lambda bi, i: (bi, 0, 0)),
                  pl.BlockSpec((1, s, LANES), lambda bi, i: (bi, 0, GROUP_WIDTH // LANES)),
                  full(w1), full(pos), full(w2), full(bsel),
                  pl.BlockSpec((1, BLK, HEADS * QRY), lambda bi, i: (i, 0, 0)),
                  full(ovt), full(expand)],
        out_specs=pl.BlockSpec((1, QRY, GROUP_WIDTH), lambda bi, i: (bi, i, 0)),
        out_shape=jax.ShapeDtypeStruct((b, s, GROUP_WIDTH), F32),
        scratch_shapes=[pltpu.VMEM((s // KEYS, VT_ROWS, KEYS), BF16), pltpu.VMEM((s // KEYS, VT_ROWS, KEYS), BF16),
                        pltpu.VMEM((BLK, BLK), F32), pltpu.VMEM((BLK, BLK), F32)],
        compiler_params=pltpu.CompilerParams(dimension_semantics=("arbitrary", "arbitrary"),
                                             vmem_limit_bytes=VMEM_LIMIT_BYTES),
        name="nsa",
    )(zd, zd, w1, pos, w2, bsel, bcmp, ovt, expand)


def _outproj_kernel(alpha, x_ref, a_ref, b_ref, c_ref, d_ref, w_ref, bo_ref, g_ref, beta_ref, o_ref):
    acc = bo_ref[...] + _dot(a_ref[...].astype(BF16), w_ref[0:GROUP_WIDTH, :])
    for n, m_ref in enumerate((b_ref, c_ref, d_ref), start=1):
        acc = acc + _dot(m_ref[...].astype(BF16), w_ref[n * GROUP_WIDTH:(n + 1) * GROUP_WIDTH, :])
    o_ref[...] = _layer_norm_rows(alpha * x_ref[...] + acc, g_ref[...], beta_ref[...])


def _outproj(alpha, x2d, mixed, w, bo, g, beta, tm=512):
    t = x2d.shape[0]
    row_spec = lambda width: pl.BlockSpec((tm, width), lambda i: (i, 0))
    const = lambda a: pl.BlockSpec(a.shape, lambda i: (0, 0))
    return pl.pallas_call(
        functools.partial(_outproj_kernel, alpha),
        grid=(t // tm,),
        in_specs=[row_spec(D_MODEL)] + [row_spec(GROUP_WIDTH)] * N_MIXERS
                 + [const(w), const(bo), const(g), const(beta)],
        out_specs=row_spec(D_MODEL),
        out_shape=jax.ShapeDtypeStruct((t, D_MODEL), F32),
        compiler_params=pltpu.CompilerParams(dimension_semantics=("arbitrary",),
                                             vmem_limit_bytes=VMEM_LIMIT_BYTES),
        name="outproj_ln",
    )(x2d, *mixed, w, bo, g, beta)


def _ffn_kernel(alpha, x_ref, w1_ref, b1_ref, w2_ref, b2_ref, g_ref, beta_ref, o_ref, xb_ref, acc_ref):
    j = pl.program_id(1)

    @pl.when(j == 0)
    def _():
        xb_ref[...] = x_ref[...].astype(BF16)
        acc_ref[...] = jnp.zeros_like(acc_ref)

    hdn = jnp.maximum(_dot(xb_ref[...], w1_ref[...]) + b1_ref[...], 0.0)
    acc_ref[...] += _dot(jnp.square(hdn).astype(BF16), w2_ref[...])

    @pl.when(j == pl.num_programs(1) - 1)
    def _():
        y = alpha * x_ref[...] + (acc_ref[...] + b2_ref[...])
        o_ref[...] = _layer_norm_rows(y, g_ref[...], beta_ref[...])


def _ffn(alpha, x2d, w1, b1, w2, b2, g, beta, tm=1024, tf=1024):
    t = x2d.shape[0]
    return pl.pallas_call(
        functools.partial(_ffn_kernel, alpha),
        grid=(t // tm, D_FF // tf),
        in_specs=[pl.BlockSpec((tm, D_MODEL), lambda i, j: (i, 0)),
                  pl.BlockSpec((D_MODEL, tf), lambda i, j: (0, j)),
                  pl.BlockSpec((1, tf), lambda i, j: (0, j)),
                  pl.BlockSpec((tf, D_MODEL), lambda i, j: (j, 0)),
                  pl.BlockSpec((1, D_MODEL), lambda i, j: (0, 0)),
                  pl.BlockSpec((1, D_MODEL), lambda i, j: (0, 0)),
                  pl.BlockSpec((1, D_MODEL), lambda i, j: (0, 0))],
        out_specs=pl.BlockSpec((tm, D_MODEL), lambda i, j: (i, 0)),
        out_shape=jax.ShapeDtypeStruct((t, D_MODEL), F32),
        scratch_shapes=[pltpu.VMEM((tm, D_MODEL), BF16), pltpu.VMEM((tm, D_MODEL), F32)],
        compiler_params=pltpu.CompilerParams(dimension_semantics=("arbitrary", "arbitrary"),
                                             vmem_limit_bytes=VMEM_LIMIT_BYTES),
        name="ffn_ln",
    )(x2d, w1, b1, w2, b2, g, beta)


def _bias_of_distance(rel_bias_heads, dist):
    onehot = np.eye(NUM_BUCKETS, dtype=np.float32)[_t5_bucket_np(np.asarray(dist))]
    return jnp.dot(rel_bias_heads.T, jnp.asarray(onehot.T), precision=lax.Precision.HIGHEST)


def _shifted_rows(v, n_rows, n_cols, step):
    period = v.shape[-1]
    assert n_cols <= period - step
    flat = jnp.tile(v, (1,) * (v.ndim - 1) + (n_rows,))[..., :n_rows * (period - step)]
    return flat.reshape(v.shape[:-1] + (n_rows, period - step))[..., :n_cols]


def _wrapped(period):
    idx = np.arange(period)
    return np.where(idx < period // 2, idx, idx - period)


def _dilated_bias(rel_bias):
    x = _wrapped(4 * BLK)
    tabs = [_shifted_rows(_bias_of_distance(rel_bias[:, 0:HEADS], (BLK - x) * dil), BLK, 2 * BLK, 1)
            for _, dil in DIL_PATTERNS]
    tabs = jnp.stack(tabs).reshape(len(DIL_PATTERNS), HEADS // 2, 2, BLK, 2 * BLK)
    return jnp.transpose(tabs, (0, 1, 3, 2, 4)).reshape(len(DIL_PATTERNS), HEADS // 2, BLK, 4 * BLK)


def _toeplitz_bias_t(rel_bias_heads, blk=BLK):
    assert (_t5_bucket_np(np.arange(blk + 1, 64 * blk)) == NUM_BUCKETS - 1).all()
    x = _wrapped(2 * blk)
    tabs = [_shifted_rows(_bias_of_distance(rel_bias_heads, blk * delta + x), blk, blk, 1)
            for delta in range(3)]
    return jnp.transpose(jnp.stack(tabs), (0, 2, 1, 3)).reshape(3, blk, HEADS * blk)


def _compressed_bias_t(rel_bias_heads, seq):
    nq = seq // QRY
    x = _wrapped(2 * seq + BLK)
    v = _bias_of_distance(rel_bias_heads, x - (NSA_CMP_LEN - 1))
    tab = _shifted_rows(v, BLK, seq, NSA_CMP_STRIDE)
    tab = tab.reshape(HEADS, BLK, nq, QRY)
    return jnp.transpose(tab, (2, 1, 0, 3)).reshape(nq, BLK, HEADS * QRY)


def _nsa_constants(seq):
    n_cmp = (seq - NSA_CMP_LEN) // NSA_CMP_STRIDE + 1
    n_sel = seq // NSA_SEL_LEN
    cs = np.arange(n_cmp)[:, None] * NSA_CMP_STRIDE
    ss = np.arange(n_sel)[None, :] * NSA_SEL_LEN
    ov = np.clip(np.minimum(cs + NSA_CMP_LEN, ss + NSA_SEL_LEN) - np.maximum(cs, ss), 0, None) / NSA_CMP_LEN
    ovt = np.zeros((BLK, BLK), np.float32)
    ovt[:n_sel, :n_cmp] = ov.T
    expand = np.zeros((seq, BLK), np.float32)
    expand[np.arange(seq), np.arange(seq) // NSA_SEL_LEN] = 1.0
    return jnp.asarray(ovt), jnp.asarray(expand, BF16)


def _nsa_weights(cmp_pos, cmp_w1, cmp_w2):
    hd, hid = HEAD_DIM, NSA_CMP_HIDDEN
    w1 = cmp_w1.reshape(2, NSA_CMP_LEN, hd, hid)
    zeros = jnp.zeros((NSA_CMP_LEN, hd, hid), F32)
    w1 = jnp.concatenate([jnp.concatenate([w1[0], zeros], axis=-1),
                          jnp.concatenate([zeros, w1[1]], axis=-1)], axis=1)
    pos = jnp.concatenate([cmp_pos[0], cmp_pos[1]], axis=-1)
    z2 = jnp.zeros((hid, hd), F32)
    w2 = jnp.concatenate([jnp.concatenate([cmp_w2[0], z2], axis=-1),
                          jnp.concatenate([z2, cmp_w2[1]], axis=-1)], axis=0)
    return w1.astype(BF16), pos, w2.astype(BF16)


def kernel(x, w_in, b_in, a_conv, a_norm, d_cmp_pos, d_cmp_w1, d_cmp_w2, w_out, b_out, ln1_g, ln1_b,
           w_ff1, b_ff1, w_ff2, b_ff2, ln2_g, ln2_b, rel_bias):
    bsz, seq, _ = x.shape
    depth = w_in.shape[0]
    alpha = (2 * depth) ** 0.25
    nc = seq // M_CHUNK
    bias_dil = _dilated_bias(rel_bias)
    btab_dsa = _toeplitz_bias_t(rel_bias[:, HEADS:2 * HEADS], QRY) * LOG2_E
    btab_nsa = _toeplitz_bias_t(rel_bias[:, 2 * HEADS:3 * HEADS], QRY) * LOG2_E
    bcmp_nsa = _compressed_bias_t(rel_bias[:, 2 * HEADS:3 * HEADS], seq) * LOG2_E
    ovt, expand = _nsa_constants(seq)

    h = x.reshape(bsz * seq, D_MODEL)
    for l in range(depth):
        w_l = _permute_columns(w_in[l]).astype(BF16)
        b_l = _permute_columns(b_in[l])[None, :]
        za, zb, zc, zd, zg = (z.reshape(bsz, seq, -1) for z in _inproj(h, w_l, b_l))
        out_a = _mlstm(za, _mlstm_gate_rows(zg), a_conv[l], a_norm[l][None, :])
        out_b = _dilated(zb, bias_dil)
        out_c = _dsa(zc, btab_dsa)
        nsa_w1, nsa_pos, nsa_w2 = _nsa_weights(d_cmp_pos[l], d_cmp_w1[l], d_cmp_w2[l])
        out_d = _nsa(zd, nsa_w1, nsa_pos, nsa_w2, btab_nsa, bcmp_nsa, ovt, expand)
        mixed = [o.reshape(bsz * seq, GROUP_WIDTH) for o in (out_a, out_b, out_c, out_d)]
        h = _outproj(alpha, h, mixed, w_out[l].astype(BF16), b_out[l][None, :],
                     ln1_g[l][None, :], ln1_b[l][None, :])
        h = _ffn(alpha, h, w_ff1[l].astype(BF16), b_ff1[l][None, :], w_ff2[l].astype(BF16),
                 b_ff2[l][None, :], ln2_g[l][None, :], ln2_b[l][None, :])
    return h.reshape(bsz, seq, D_MODEL)
```

```python
import functools
import math

import numpy as np
import jax
import jax.numpy as jnp
from jax import lax
from jax.experimental import pallas as pl
from jax.experimental.pallas import tpu as pltpu

F32 = jnp.float32
BF16 = jnp.bfloat16
I32 = jnp.int32
I16 = jnp.int16
I16_MIN = -2 ** 15

D_MODEL = 1024
N_MIXERS = 4
HEADS = 4
HEAD_DIM = D_MODEL // (N_MIXERS * HEADS)
GROUP_WIDTH = HEADS * HEAD_DIM
D_FF = 4 * D_MODEL
LN_EPS = 1e-5
NEG = -1e30

M_QK_DIM = HEAD_DIM // 2
M_CHUNK = 64
M_CONV = 4
DIL_PATTERNS = ((128, 1), (512, 4), (2048, 16))
IDX_HEADS = 4
IDX_DIM = 64
DSA_TOPK = 256
NSA_CMP_LEN = 32
NSA_CMP_STRIDE = 16
NSA_SEL_LEN = 64
NSA_TOPN = 16
NSA_WINDOW = 512
NSA_CMP_HIDDEN = 256
NSA_FORCE = 1e9
NUM_BUCKETS = 32
MAX_DISTANCE = 128

LANES = 128
BLK = 128
KEYS = 2 * BLK
QRY = KEYS
VMEM_LIMIT_BYTES = 56 * 1024 * 1024

IN_SPLITS = (
    ('a_q', HEADS * M_QK_DIM), ('a_k', HEADS * M_QK_DIM), ('a_v', GROUP_WIDTH),
    ('a_i', HEADS), ('a_f', HEADS), ('a_o', GROUP_WIDTH),
    ('b_q', GROUP_WIDTH), ('b_k', GROUP_WIDTH), ('b_v', GROUP_WIDTH),
    ('c_q', GROUP_WIDTH), ('c_k', HEAD_DIM), ('c_v', HEAD_DIM),
    ('c_iq', IDX_HEADS * IDX_DIM), ('c_ik', IDX_DIM), ('c_iw', IDX_HEADS),
    ('d_q', GROUP_WIDTH), ('d_kc', HEAD_DIM), ('d_vc', HEAD_DIM),
    ('d_ks', HEAD_DIM), ('d_vs', HEAD_DIM), ('d_kw', HEAD_DIM), ('d_vw', HEAD_DIM),
    ('d_g', 3 * HEADS),
)

GROUP_LAYOUT = (
    (('a_q', 'a_k'), ('a_v',), ('a_o',), (('a_i', HEAD_DIM),), (('a_i', M_QK_DIM),),
     (('a_f', HEAD_DIM),), (('a_f', M_QK_DIM),)),
    (('b_q',), ('b_k',), ('b_v',)),
    (('c_q',), ('c_iq',), ('c_k', 'c_v'), ('c_ik', 'c_iw')),
    (('d_q',), ('d_kc', 'd_vc'), ('d_ks', 'd_vs'), ('d_kw', 'd_vw'), ('d_g',)),
    (('a_i', 'a_f'),),
)


def _round_up(n, m):
    return -(-n // m) * m


def _projection_layout():
    offs, off = {}, 0
    for name, width in IN_SPLITS:
        offs[name] = (off, width)
        off += width
    runs, group_widths = [], []
    for group in GROUP_LAYOUT:
        gwidth = 0
        for chunk in group:
            cwidth = 0
            for entry in chunk:
                name, rep = entry if isinstance(entry, tuple) else (entry, 1)
                o, w = offs[name]
                runs.append((o, w, rep))
                cwidth += w * rep
            pad = _round_up(cwidth, LANES) - cwidth
            if pad:
                runs.append((-1, pad, 1))
            gwidth += cwidth + pad
        group_widths.append(gwidth)
    return tuple(runs), tuple(group_widths)


PROJ_RUNS, GROUP_WIDTHS = _projection_layout()
PROJ_WIDTH = int(sum(GROUP_WIDTHS))
PROJ_PERM = np.concatenate([np.repeat(np.arange(o, o + w), r) if o >= 0 else np.full(w, -1)
                            for o, w, r in PROJ_RUNS]).astype(np.int32)


def _permute_columns(a):
    parts = []
    for o, w, r in PROJ_RUNS:
        if o < 0:
            parts.append(jnp.zeros(a.shape[:-1] + (w,), a.dtype))
        else:
            parts.append(a[..., o:o + w] if r == 1 else jnp.repeat(a[..., o:o + w], r, axis=-1))
    return jnp.concatenate(parts, axis=-1)


def _t5_bucket_np(dist):
    n = np.maximum(dist, 0)
    max_exact = NUM_BUCKETS // 2
    nf = np.maximum(n, max_exact).astype(np.float32)
    large = max_exact + (np.log(nf / max_exact) / math.log(MAX_DISTANCE / max_exact)
                         * (NUM_BUCKETS - max_exact)).astype(np.int32)
    large = np.minimum(large, NUM_BUCKETS - 1)
    return np.where(n < max_exact, n, large).astype(np.int32)


def _nt_dot(a, b, precision=None):
    return lax.dot_general(a, b, (((1,), (1,)), ((), ())), precision=precision,
                           preferred_element_type=F32)


def _dot(a, b, precision=None):
    return jnp.dot(a, b, precision=precision, preferred_element_type=F32)


def _layer_norm_rows(y, g, b):
    mu = jnp.mean(y, axis=-1, keepdims=True)
    var = jnp.mean(jnp.square(y - mu), axis=-1, keepdims=True)
    return (y - mu) * lax.rsqrt(var + LN_EPS) * g + b


def _sigmoid(x):
    return 1.0 / (1.0 + jnp.exp(-x))


def _log_sigmoid(x):
    return -(jnp.maximum(-x, 0.0) + jnp.log1p(jnp.exp(-jnp.abs(x))))


def _sortable_key(x):
    bits = pltpu.bitcast(x, I32)
    return bits ^ ((bits >> 31) & jnp.int32(0x7FFFFFFF))


def _inproj_kernel(x_ref, w_ref, b_ref, *out_refs):
    xb = x_ref[...].astype(BF16)
    off = 0
    for o_ref, width in zip(out_refs, GROUP_WIDTHS):
        o_ref[...] = _dot(xb, w_ref[:, off:off + width]) + b_ref[:, off:off + width]
        off += width


def _inproj(x2d, w, b, tm=512):
    t = x2d.shape[0]
    return pl.pallas_call(
        _inproj_kernel,
        grid=(t // tm,),
        in_specs=[pl.BlockSpec((tm, D_MODEL), lambda i: (i, 0)),
                  pl.BlockSpec((D_MODEL, PROJ_WIDTH), lambda i: (0, 0)),
                  pl.BlockSpec((1, PROJ_WIDTH), lambda i: (0, 0))],
        out_specs=[pl.BlockSpec((tm, gw), lambda i: (i, 0)) for gw in GROUP_WIDTHS],
        out_shape=[jax.ShapeDtypeStruct((t, gw), F32) for gw in GROUP_WIDTHS],
        compiler_params=pltpu.CompilerParams(dimension_semantics=("arbitrary",),
                                             vmem_limit_bytes=VMEM_LIMIT_BYTES),
        name="inproj",
    )(x2d, w, b)


def _split_terms(x, n):
    terms, rest = [], x
    for _ in range(n):
        terms.append(rest.astype(BF16))
        rest = rest - terms[-1].astype(F32)
    return terms


def _iota(shape, dim):
    return lax.broadcasted_iota(I32, shape, dim)


def _mlstm_kernel(z_ref, gt_ref, cw_ref, ng_ref, o_ref, xpad_ref):
    seq = z_ref.shape[1]
    L, DK, DV, H = M_CHUNK, M_QK_DIM, HEAD_DIM, HEADS
    assert L == DV
    wq, wv = H * DK, H * DV
    lg_dk, lg_dv = int(math.log2(DK)), int(math.log2(DV))
    c_v, c_o, c_i64 = 2 * wq, 2 * wq + wv, 2 * wq + 2 * wv
    c_i32, c_f64 = c_i64 + wv, c_i64 + wv + wq
    xpad_ref[0:8, :] = jnp.zeros((8, 2 * wq), F32)
    xpad_ref[8:, :] = z_ref[0, :, 0:2 * wq]

    one_if = lambda cond: jnp.where(cond, 1.0, 0.0).astype(BF16)
    tri_l = one_if(_iota((L, L), 0) >= _iota((L, L), 1))
    trow = _iota((L, wv), 0)
    tri_heads = trow >= (_iota((L, wv), 1) & (L - 1))
    r_vv, c_vv = _iota((wv, wv), 0), _iota((wv, wv), 1)
    same_head = (r_vv >> lg_dv) == (c_vv >> lg_dv)
    ones_bd = one_if(same_head)
    mean_bd = jnp.where(same_head, 1.0 / DV, 0.0).astype(BF16)
    tri_u_bd = one_if(same_head & ((r_vv & (L - 1)) <= (c_vv & (L - 1))))
    state_mask = (_iota((wq, wv), 0) >> lg_dk) == (_iota((wq, wv), 1) >> lg_dv)
    eye_q = one_if(_iota((wq, wq), 0) == _iota((wq, wq), 1))
    head_of_qlane = _iota((L, wq), 1) >> lg_dk
    head_of_vlane = _iota((L, wv), 1) >> lg_dv
    row8 = _iota((8, wv), 0)
    cw = cw_ref[...]
    ng = ng_ref[...]

    def head_mean(x):
        hi_lo = _split_terms(x, 2)
        r = _dot(jnp.concatenate(hi_lo, axis=0), mean_bd)
        return r[0:L] + r[L:2 * L]

    def chunk(c, carry):
        cbd, nbd, m64, m32 = carry
        s0 = pl.multiple_of(c * L, L)
        rows = pl.ds(s0, L)
        xw = xpad_ref[pl.ds(s0, L + 8), :]
        y = sum(cw[j:j + 1, :] * xw[5 + j:5 + j + L, :] for j in range(M_CONV))
        qk = y * _sigmoid(y)
        q = qk[:, 0:wq]
        k = qk[:, wq:] * (DK ** -0.5)
        qb = q.astype(BF16)
        v = z_ref[0, rows, c_v:c_v + wv]
        i64 = z_ref[0, rows, c_i64:c_i64 + wv]
        i32 = z_ref[0, rows, c_i32:c_i32 + wq]
        gr = gt_ref[0, c]

        flog = _log_sigmoid(z_ref[0, rows, c_f64:c_f64 + wv + wq])
        bsum = _dot(tri_l, jnp.concatenate(_split_terms(flog, 3), axis=1))
        w3 = wv + wq
        ball = bsum[:, 0:w3] + bsum[:, w3:2 * w3] + bsum[:, 2 * w3:3 * w3]
        b64, b32 = ball[:, 0:wv], ball[:, wv:w3]
        fterms = [t.astype(F32) for t in _split_terms(_log_sigmoid(gr[1:2, :]), 3)]
        frows = jnp.where(row8 == 0, fterms[0], jnp.where(row8 == 1, fterms[1], jnp.where(row8 == 2, fterms[2], 0.0)))
        bparts = _dot(frows.astype(BF16), tri_u_bd)
        brow = bparts[0:1, :] + bparts[1:2, :] + bparts[2:3, :]

        dall = jnp.where(tri_heads, b64 - brow + gr[0:1, :], NEG)
        cm = i64 - b64
        for sh in (1, 2, 4, 8, 16, 32):
            cm = jnp.where(trow >= sh, jnp.maximum(cm, pltpu.roll(cm, sh, 0)), cm)
        inter = b64 + m64
        m_t = jnp.maximum(inter, b64 + cm)
        kbd = jnp.concatenate([jnp.where(head_of_qlane == h, k, 0.0) for h in range(H)], axis=0).astype(BF16)
        sc = _nt_dot(qb, kbd) * jnp.exp(dall - m_t)
        wi = jnp.exp(inter - m_t)
        vb = v.astype(BF16)
        vbd = jnp.concatenate([jnp.where(head_of_vlane == h, v, 0.0) for h in range(H)], axis=0).astype(BF16)
        pv = _dot(sc.astype(BF16), jnp.concatenate([vbd, ones_bd], axis=1))
        qst = _dot(qb, jnp.concatenate([cbd, nbd], axis=1).astype(BF16))
        num = pv[:, 0:wv] + wi * qst[:, 0:wv]
        den = pv[:, wv:] + wi * qst[:, wv:]
        hh = num / jnp.maximum(jnp.abs(den), jnp.exp(-m_t))

        og = _sigmoid(z_ref[0, rows, c_o:c_o + wv]) * hh
        dev = og - head_mean(og)
        o_ref[0, rows, :] = dev * lax.rsqrt(head_mean(dev * dev) + LN_EPS) * ng

        bl64, bl32 = b64[L - 1:L, :], b32[L - 1:L, :]
        m64_new = jnp.maximum(bl64 + m64, jnp.max(bl64 - b64 + i64, axis=0, keepdims=True))
        g32 = bl32 - b32 + i32
        m32_new = jnp.maximum(bl32 + m32, jnp.max(g32, axis=0, keepdims=True))
        kw = k * jnp.exp(g32 - m32_new)
        wc = jnp.exp(bl64 + m64 - m64_new)
        kwt = _nt_dot(eye_q, kw.astype(BF16)).astype(BF16)
        upd = _dot(kwt, jnp.concatenate([vb, jnp.ones((L, wv), BF16)], axis=1))
        cbd = wc * cbd + jnp.where(state_mask, upd[:, 0:wv], 0.0)
        nbd = wc * nbd + jnp.where(state_mask, upd[:, wv:], 0.0)
        return cbd, nbd, m64_new, m32_new

    init = (jnp.zeros((wq, wv), F32), jnp.zeros((wq, wv), F32), jnp.zeros((1, wv), F32), jnp.zeros((1, wq), F32))
    lax.fori_loop(0, seq // L, chunk, init, unroll=4)


def _mlstm(za, gates_t, conv_w, norm_g):
    b, s, wa = za.shape
    nc = s // M_CHUNK
    return pl.pallas_call(
        _mlstm_kernel,
        grid=(b,),
        in_specs=[pl.BlockSpec((1, s, wa), lambda i: (i, 0, 0)),
                  pl.BlockSpec((1, nc) + gates_t.shape[2:], lambda i: (i, 0, 0, 0)),
                  pl.BlockSpec((M_CONV, 2 * HEADS * M_QK_DIM), lambda i: (0, 0)),
                  pl.BlockSpec((1, GROUP_WIDTH), lambda i: (0, 0))],
        out_specs=pl.BlockSpec((1, s, GROUP_WIDTH), lambda i: (i, 0, 0)),
        out_shape=jax.ShapeDtypeStruct((b, s, GROUP_WIDTH), F32),
        scratch_shapes=[pltpu.VMEM((s + 8, 2 * HEADS * M_QK_DIM), F32)],
        compiler_params=pltpu.CompilerParams(dimension_semantics=("arbitrary",),
                                             vmem_limit_bytes=VMEM_LIMIT_BYTES),
        name="mlstm",
    )(za, gates_t, conv_w, norm_g)


def _mlstm_gate_rows(zg):
    b, s, _ = zg.shape
    gates = zg[:, :, 0:2 * HEADS].reshape(b, s // M_CHUNK, M_CHUNK, 2, HEADS)
    return jnp.transpose(gates, (0, 1, 3, 4, 2)).reshape(b, s // M_CHUNK, 2, HEADS * M_CHUNK)


def _dilated_kernel(q0_ref, q1_ref, k0_ref, k1_ref, v0_ref, v1_ref, bias_ref, o_ref, acc_ref, mx_ref, den_ref):
    seq = q0_ref.shape[1]
    W = BLK
    hd = HEAD_DIM
    npair = HEADS // 2
    scale = hd ** -0.5
    assert math.log2(scale).is_integer()
    nk = 2 * W
    qi = _iota((W, 2 * nk), 0)
    ki = _iota((W, 2 * nk), 1) & (nk - 1)
    j = W + qi - ki
    band = (j >= 0) & (j <= W)
    head_of_lane = _iota((nk, 2 * hd), 1) >> int(math.log2(hd))
    ones_bd = jnp.concatenate([jnp.where(head_of_lane == hh, 1.0, 0.0) for hh in range(2)], axis=0).astype(BF16)
    q_refs, k_refs, v_refs = (q0_ref, q1_ref), (k0_ref, k1_ref), (v0_ref, v1_ref)

    def block_diag(x):
        return jnp.concatenate([jnp.where(head_of_lane == hh, x, 0.0) for hh in range(2)], axis=0).astype(BF16)

    for br, (window, dil) in enumerate(DIL_PATTERNS):
        assert window // dil == W
        nb = (seq // dil) // W

        def piece(idx, _, br=br, dil=dil, nb=nb):
            n = idx % nb
            if dil == 1:
                rows_q = pl.ds(pl.multiple_of(W * n, W), W)
                rows_p = pl.ds(pl.multiple_of(W * jnp.maximum(n - 1, 0), W), W)
            else:
                r = idx // nb
                rows_q = pl.ds(r + dil * W * n, W, stride=dil)
                rows_p = pl.ds(r + dil * W * jnp.maximum(n - 1, 0), W, stride=dil)
            mask = band & (ki >= jnp.where(n > 0, 0, W))
            for pr in range(npair):
                q = (q_refs[pr][0, rows_q, :] * scale).astype(BF16)
                k2 = jnp.concatenate([k_refs[pr][0, rows_p, :], k_refs[pr][0, rows_q, :]], axis=0)
                v2 = jnp.concatenate([v_refs[pr][0, rows_p, :], v_refs[pr][0, rows_q, :]], axis=0)
                lg = jnp.where(mask, _nt_dot(q, block_diag(k2)) + bias_ref[br, pr], NEG)
                ps, mxs = [], []
                for hh in range(2):
                    sl = slice(nk * hh, nk * (hh + 1))
                    m = jnp.max(lg[:, sl], axis=-1, keepdims=True)
                    ps.append(jnp.where(mask[:, sl], jnp.exp(lg[:, sl] - m), 0.0))
                    mxs.append(jnp.broadcast_to(m, (W, hd)))
                p = jnp.concatenate(ps, axis=-1).astype(BF16)
                pv = _dot(p, jnp.concatenate([block_diag(v2), ones_bd], axis=1))
                acc_ref[br, pr, rows_q, :] = pv[:, 0:2 * hd]
                den_ref[br, pr, rows_q, :] = pv[:, 2 * hd:]
                mx_ref[br, pr, rows_q, :] = jnp.concatenate(mxs, axis=-1)
            return 0

        lax.fori_loop(0, dil * nb, piece, 0, unroll=4)

    def combine(i, _):
        rows = pl.ds(pl.multiple_of(i * W, W), W)
        outs = []
        for pr in range(npair):
            ms = [mx_ref[b, pr, rows, :] for b in range(len(DIL_PATTERNS))]
            top = functools.reduce(jnp.maximum, ms)
            es = [jnp.exp(m - top) for m in ms]
            num = sum(e * acc_ref[b, pr, rows, :] for b, e in enumerate(es))
            den = sum(e * jnp.maximum(den_ref[b, pr, rows, :], 1e-30) for b, e in enumerate(es))
            outs.append(num / den)
        o_ref[0, rows, :] = jnp.concatenate(outs, axis=-1)
        return 0

    lax.fori_loop(0, seq // W, combine, 0)


def _dilated(zb, bias):
    b, s, wb = zb.shape
    nbr = len(DIL_PATTERNS)
    pair_spec = lambda c: pl.BlockSpec((1, s, LANES), lambda i: (i, 0, c))
    return pl.pallas_call(
        _dilated_kernel,
        grid=(b,),
        in_specs=[pair_spec(c) for c in range(wb // LANES)]
                 + [pl.BlockSpec(bias.shape, lambda i: (0, 0, 0, 0))],
        out_specs=pl.BlockSpec((1, s, GROUP_WIDTH), lambda i: (i, 0, 0)),
        out_shape=jax.ShapeDtypeStruct((b, s, GROUP_WIDTH), F32),
        scratch_shapes=[pltpu.VMEM((nbr, HEADS // 2, s, LANES), F32) for _ in range(3)],
        compiler_params=pltpu.CompilerParams(dimension_semantics=("arbitrary",),
                                             vmem_limit_bytes=VMEM_LIMIT_BYTES),
        name="dilated",
    )(*([zb] * (wb // LANES)), bias)


def _head_columns_t(q):
    qt = q.T
    return jnp.concatenate([qt[HEAD_DIM * h:HEAD_DIM * (h + 1), :] for h in range(HEADS)], axis=1)


LOG2_E = math.log2(math.e)


def _scaled_query_columns_t(q):
    return (_head_columns_t(q) * (HEAD_DIM ** -0.5 * LOG2_E)).astype(BF16)


def _unstack_heads_t(per_head_t):
    halves = []
    for h in range(0, HEADS, 2):
        halves.append(jnp.concatenate([per_head_t[h], per_head_t[h + 1]], axis=0).T)
    return jnp.concatenate(halves, axis=-1)


VT_ROWS = HEAD_DIM + 16


def _online_step(lg, mask, vt1, m, acc):
    lg = jnp.where(mask, lg, NEG)
    m_new = jnp.maximum(m, jnp.max(lg, axis=0, keepdims=True))
    p = jnp.where(mask, jnp.exp2(lg - m_new), 0.0)
    acc = acc * jnp.exp2(m - m_new) + _dot(vt1, p.astype(BF16))
    return m_new, acc


def _online_result(acc):
    return acc[0:HEAD_DIM, :] / jnp.maximum(acc[HEAD_DIM:HEAD_DIM + 1, :], 1e-30)


def _values_with_ones_t(z_rows):
    vt = z_rows.T[HEAD_DIM:2 * HEAD_DIM, :]
    return jnp.concatenate([vt, jnp.ones((VT_ROWS - HEAD_DIM, vt.shape[1]), F32)], axis=0).astype(BF16)


def _pair_bias(btab_ref, behind):
    return jnp.concatenate([btab_ref[jnp.minimum(behind, 2)], btab_ref[jnp.clip(behind - 1, 0, 2)]], axis=0)


def _online_init(nq=BLK):
    return (tuple(jnp.full((1, nq), NEG, F32) for _ in range(HEADS)),
            tuple(jnp.zeros((VT_ROWS, nq), F32) for _ in range(HEADS)))


def _dsa_kernel(z_ref, btab_ref, o_ref, vt_ref, key_ref, hi_ref, lo_ref, lom_ref):
    i = pl.program_id(1)
    seq = z_ref.shape[1]
    nkb = seq // BLK
    nq = QRY
    hd = HEAD_DIM
    topk = min(DSA_TOPK, seq // 4)
    t0 = pl.multiple_of(i * nq, nq)

    assert nq == KEYS
    npair = i + 1

    @pl.when(i == 0)
    def _():
        for kb in range(nkb):
            vt_ref[kb // 2, :, (kb % 2) * BLK:(kb % 2 + 1) * BLK] = _values_with_ones_t(
                z_ref[0, kb * BLK:(kb + 1) * BLK, 512:640])
        key_ref[...] = jnp.full(key_ref.shape, -2 ** 31, I32)
        for half_ref in (hi_ref, lo_ref, lom_ref):
            half_ref[...] = jnp.full(half_ref.shape, I16_MIN, I16)

    zq = z_ref[0, pl.ds(t0, nq), :]
    cq = zq[:, 0:256]
    ciq = _head_columns_t(zq[:, 256:512]).astype(BF16)
    iw = zq[:, 640:768].T[IDX_DIM:IDX_DIM + IDX_HEADS, :] * ((IDX_HEADS * IDX_DIM) ** -0.5)
    s_loc = lax.broadcasted_iota(I32, (KEYS, nq), 0)
    t_glob = t0 + lax.broadcasted_iota(I32, (KEYS, nq), 1)

    def score_block(kp, _):
        r0 = pl.multiple_of(kp * KEYS, KEYS)
        ik = z_ref[0, pl.ds(r0, KEYS), 640:704].astype(BF16)
        rel = _dot(ik, ciq)
        sc = jnp.zeros((KEYS, nq), F32)
        for h in range(IDX_HEADS):
            sc = sc + jnp.maximum(rel[:, nq * h:nq * (h + 1)], 0.0) * iw[h:h + 1, :]
        sc = jnp.where(r0 + s_loc <= t_glob, sc, NEG)
        key = _sortable_key(sc)
        key_ref[kp] = key
        hi_ref[kp] = (key >> 16).astype(I16)
        lo_ref[kp] = ((key & 0xFFFF) - 2 ** 15).astype(I16)
        return 0

    lax.fori_loop(0, npair, score_block, 0)

    def count(ref, pred, pairs):
        dt = ref.dtype
        rows = 8 * 4 // dt.itemsize
        def body(kp, acc):
            hit = jnp.where(pred(kp, ref[kp]), jnp.ones((), dt), jnp.zeros((), dt))
            hit = hit.reshape(KEYS // rows, rows, nq)
            parts = [hit[n] for n in range(KEYS // rows)]
            while len(parts) > 1:
                parts = [a + b for a, b in zip(parts[0::2], parts[1::2])]
            return acc + parts[0]
        acc = jnp.zeros((rows, nq), dt)
        if isinstance(pairs, int):
            for kp in range(pairs):
                acc = body(kp, acc)
        else:
            acc = lax.fori_loop(0, pairs, body, acc)
        return jnp.sum(acc.astype(I32), axis=0, keepdims=True)

    def threshold(pairs):
        def half_search(ref, k):
            def bit(it, thr):
                cand = thr + lax.shift_left(jnp.int32(1), 15 - it)
                c = count(ref, lambda kp, half: half >= cand.astype(I16), pairs)
                return jnp.where(c >= k, cand, thr)
            return lax.fori_loop(0, 16, bit, jnp.full((1, nq), I16_MIN, I32))

        def run(_):
            thr_hi = half_search(hi_ref, topk)
            thr_hi16 = thr_hi.astype(I16)
            above_hi = count(hi_ref, lambda kp, half: half > thr_hi16, pairs)
            for kp in range(pairs):
                lom_ref[kp] = jnp.where(hi_ref[kp] == thr_hi16, lo_ref[kp], jnp.int16(I16_MIN))
            thr_lo = half_search(lom_ref, topk - above_hi)
            thr = (thr_hi << 16) | (thr_lo + 2 ** 15)
            n_gt = count(key_ref, lambda kp, key: key > thr, pairs)
            need = (topk - n_gt).astype(F32)
            earlier = jnp.zeros((1, nq), F32)
            for kp in range(pairs):
                tie = key_ref[kp] == thr
                tie01 = jnp.where(tie, 1.0, 0.0)
                rank = _dot(below, tie01.astype(BF16)) + earlier
                lom_ref[kp] = jnp.where(tie & (rank < need), 1, 0).astype(I16)
                earlier = earlier + jnp.sum(tie01, axis=0, keepdims=True)
            return thr
        return run

    below = jnp.where(_iota((KEYS, KEYS), 1) < _iota((KEYS, KEYS), 0), 1.0, 0.0).astype(BF16)
    thr = lax.switch(npair - 1, [threshold(p) for p in range(1, seq // KEYS + 1)], 0)

    qs = _scaled_query_columns_t(cq)

    def attend(kp, carry):
        ms, accs = carry
        r0 = pl.multiple_of(kp * KEYS, KEYS)
        kblk = z_ref[0, pl.ds(r0, KEYS), 512:576].astype(BF16)
        lg = _dot(kblk, qs) + btab_ref[jnp.minimum(i - kp, 2)]
        key = key_ref[kp]
        s_glob = r0 + s_loc
        mask = ((key > thr) | (lom_ref[kp].astype(I32) != 0)) & (s_glob <= t_glob)
        vt1 = vt_ref[kp]
        out = [_online_step(lg[:, nq * h:nq * (h + 1)], mask, vt1, ms[h], accs[h]) for h in range(HEADS)]
        return tuple(o[0] for o in out), tuple(o[1] for o in out)

    ms, accs = lax.fori_loop(0, npair, attend, _online_init(nq))
    o_ref[0] = _unstack_heads_t([_online_result(accs[h]) for h in range(HEADS)])


def _dsa(zc, btab):
    b, s, wc = zc.shape
    return pl.pallas_call(
        _dsa_kernel,
        grid=(b, s // QRY),
        in_specs=[pl.BlockSpec((1, s, wc), lambda bi, i: (bi, 0, 0)),
                  pl.BlockSpec(btab.shape, lambda bi, i: (0, 0, 0))],
        out_specs=pl.BlockSpec((1, QRY, GROUP_WIDTH), lambda bi, i: (bi, i, 0)),
        out_shape=jax.ShapeDtypeStruct((b, s, GROUP_WIDTH), F32),
        scratch_shapes=[pltpu.VMEM((s // KEYS, VT_ROWS, KEYS), BF16), pltpu.VMEM((s // KEYS, KEYS, QRY), I32)]
                       + [pltpu.VMEM((s // KEYS, KEYS, QRY), I16)] * 3,
        compiler_params=pltpu.CompilerParams(dimension_semantics=("arbitrary", "arbitrary"),
                                             vmem_limit_bytes=VMEM_LIMIT_BYTES),
        name="dsa",
    )(zc, btab)


def _nsa_kernel(z_ref, zc_ref, w1_ref, pos_ref, w2_ref, bsel_ref, bcmp_ref, ovt_ref, exp_ref, o_ref,
                vst_ref, vwt_ref, cmp_ref, cmpt_ref):
    i = pl.program_id(1)
    seq = z_ref.shape[1]
    nkb = seq // BLK
    hd = HEAD_DIM
    scale = hd ** -0.5
    n_cmp = (seq - NSA_CMP_LEN) // NSA_CMP_STRIDE + 1
    n_sel = seq // NSA_SEL_LEN
    topn = min(NSA_TOPN, n_sel)
    half = NSA_CMP_LEN // 2
    assert half == NSA_CMP_STRIDE and n_cmp + 1 == seq // NSA_CMP_STRIDE == BLK and n_sel <= BLK
    nq = QRY
    assert nq == KEYS
    t0 = pl.multiple_of(i * nq, nq)
    hi = lax.Precision.HIGHEST

    @pl.when(i == 0)
    def _():
        for kb in range(nkb):
            rows = slice(kb * BLK, (kb + 1) * BLK)
            cols = slice((kb % 2) * BLK, (kb % 2 + 1) * BLK)
            vst_ref[kb // 2, :, cols] = _values_with_ones_t(z_ref[0, rows, 384:512])
            vwt_ref[kb // 2, :, cols] = _values_with_ones_t(z_ref[0, rows, 512:640])
        first = jnp.zeros((BLK, 2 * NSA_CMP_HIDDEN), F32)
        second = jnp.zeros((BLK, 2 * NSA_CMP_HIDDEN), F32)
        for j in range(half):
            xj = zc_ref[0, pl.ds(j, BLK, stride=NSA_CMP_STRIDE), :]
            first = first + _dot((xj + pos_ref[j:j + 1, :]).astype(BF16), w1_ref[j])
            second = second + _dot((xj + pos_ref[half + j:half + j + 1, :]).astype(BF16), w1_ref[half + j])
        hid = first + pltpu.roll(second, BLK - 1, 0)
        hid = hid * _sigmoid(hid)
        cmp = _dot(hid.astype(BF16), w2_ref[...])
        cmp_ref[...] = cmp
        cmpt_ref[...] = cmp.T

    zq = z_ref[0, pl.ds(t0, nq), :]
    qs = _scaled_query_columns_t(zq[:, 0:256])
    gates = _sigmoid(zq[:, 640:768].T[0:16, :])
    row = lax.broadcasted_iota(I32, (BLK, nq), 0)
    t_glob = t0 + lax.broadcasted_iota(I32, (BLK, nq), 1)

    kcmp = cmp_ref[:, 0:hd].astype(BF16)
    vcmpt = cmpt_ref[hd:2 * hd, :].astype(BF16)
    lgc = _dot(kcmp, qs) + bcmp_ref[0]
    mask_c = (t_glob - (row * NSA_CMP_STRIDE + NSA_CMP_LEN - 1) >= 0) & (row < n_cmp)
    o_cmp, psum = [], jnp.zeros((BLK, nq), F32)
    for h in range(HEADS):
        lg = jnp.where(mask_c, lgc[:, nq * h:nq * (h + 1)], NEG)
        m = jnp.max(lg, axis=0, keepdims=True)
        p = jnp.where(mask_c, jnp.exp2(lg - m), 0.0)
        p = p / jnp.maximum(jnp.sum(p, axis=0, keepdims=True), 1e-30)
        o_cmp.append(_dot(vcmpt, p.astype(BF16)))
        psum = psum + p

    imp = _dot(ovt_ref[...], psum, precision=hi)
    cur = t_glob >> int(math.log2(NSA_SEL_LEN))
    forced = (row == 0) | (row == cur) | (row == cur - 1)
    imp = jnp.where(forced, NSA_FORCE, imp)
    imp = jnp.where(row * NSA_SEL_LEN <= t_glob, imp, NEG)
    imp = imp[0:n_sel, :]
    jrow = lax.broadcasted_iota(I32, (n_sel, nq), 0)
    rank = jnp.zeros((n_sel, nq), I32)
    for jp in range(n_sel):
        other = imp[jp:jp + 1, :]
        rank = rank + ((other > imp) | ((other == imp) & (jp < jrow))).astype(I32)
    chosen = jnp.where(rank < topn, 1.0, 0.0)
    chosen = jnp.concatenate([chosen, jnp.zeros((BLK - n_sel, nq), F32)], axis=0).astype(BF16)

    s_loc = lax.broadcasted_iota(I32, (KEYS, nq), 0)
    t_keys = t0 + lax.broadcasted_iota(I32, (KEYS, nq), 1)

    def attend(vt_ref, lanes, mask_fn):
        def body(kp, carry):
            ms, accs = carry
            r0 = pl.multiple_of(kp * KEYS, KEYS)
            kblk = z_ref[0, pl.ds(r0, KEYS), lanes].astype(BF16)
            lg = _dot(kblk, qs) + bsel_ref[jnp.minimum(i - kp, 2)]
            mask = mask_fn(r0)
            vt1 = vt_ref[kp]
            out = [_online_step(lg[:, nq * h:nq * (h + 1)], mask, vt1, ms[h], accs[h]) for h in range(HEADS)]
            return tuple(o[0] for o in out), tuple(o[1] for o in out)
        return body

    def mask_sel(r0):
        picked = _dot(exp_ref[pl.ds(r0, KEYS), :], chosen) > 0.5
        return picked & (r0 + s_loc <= t_keys)

    def mask_win(r0):
        dist = t_keys - (r0 + s_loc)
        return (dist >= 0) & (dist < NSA_WINDOW)

    sel_body = attend(vst_ref, slice(384, 448), mask_sel)
    win_body = attend(vwt_ref, slice(512, 576), mask_win)
    first_w = jnp.maximum(i - NSA_WINDOW // KEYS, 0)
    sel_carry = lax.fori_loop(0, first_w, sel_body, _online_init(nq))
    (_, acc_s), (_, acc_w) = lax.fori_loop(
        first_w, i + 1, lambda kp, c: (sel_body(kp, c[0]), win_body(kp, c[1])), (sel_carry, _online_init(nq)))

    outs = []
    for h in range(HEADS):
        outs.append(gates[3 * h:3 * h + 1, :] * o_cmp[h] + gates[3 * h + 1:3 * h + 2, :] * _online_result(acc_s[h])
                    + gates[3 * h + 2:3 * h + 3, :] * _online_result(acc_w[h]))
    o_ref[0] = _unstack_heads_t(outs)


def _nsa(zd, w1, pos, w2, bsel, bcmp, ovt, expand):
    b, s, wd = zd.shape
    full = lambda a: pl.BlockSpec(a.shape, lambda bi, i: (0,) * a.ndim)
    return pl.pallas_call(
        _nsa_kernel,
        grid=(b, s // QRY),
        in_specs=[pl.BlockSpec((1, s, wd), lambda bi, i: (bi, 0, 0)),
                  pl.BlockSpec((1, s, LANES), lambda bi, i: (bi, 0, GROUP_WIDTH // LANES)),
                  full(w1), full(pos), full(w2), full(bsel),
                  pl.BlockSpec((1, BLK, HEADS * QRY), lambda bi, i: (i, 0, 0)),
                  full(ovt), full(expand)],
        out_specs=pl.BlockSpec((1, QRY, GROUP_WIDTH), lambda bi, i: (bi, i, 0)),
        out_shape=jax.ShapeDtypeStruct((b, s, GROUP_WIDTH), F32),
        scratch_shapes=[pltpu.VMEM((s // KEYS, VT_ROWS, KEYS), BF16), pltpu.VMEM((s // KEYS, VT_ROWS, KEYS), BF16),
                        pltpu.VMEM((BLK, BLK), F32), pltpu.VMEM((BLK, BLK), F32)],
        compiler_params=pltpu.CompilerParams(dimension_semantics=("arbitrary", "arbitrary"),
                                             vmem_limit_bytes=VMEM_LIMIT_BYTES),
        name="nsa",
    )(zd, zd, w1, pos, w2, bsel, bcmp, ovt, expand)


def _outproj_kernel(alpha, x_ref, a_ref, b_ref, c_ref, d_ref, w_ref, bo_ref, g_ref, beta_ref, o_ref):
    acc = bo_ref[...] + _dot(a_ref[...].astype(BF16), w_ref[0:GROUP_WIDTH, :])
    for n, m_ref in enumerate((b_ref, c_ref, d_ref), start=1):
        acc = acc + _dot(m_ref[...].astype(BF16), w_ref[n * GROUP_WIDTH:(n + 1) * GROUP_WIDTH, :])
    o_ref[...] = _layer_norm_rows(alpha * x_ref[...] + acc, g_ref[...], beta_ref[...])


def _outproj(alpha, x2d, mixed, w, bo, g, beta, tm=512):
    t = x2d.shape[0]
    row_spec = lambda width: pl.BlockSpec((tm, width), lambda i: (i, 0))
    const = lambda a: pl.BlockSpec(a.shape, lambda i: (0, 0))
    return pl.pallas_call(
        functools.partial(_outproj_kernel, alpha),
        grid=(t // tm,),
        in_specs=[row_spec(D_MODEL)] + [row_spec(GROUP_WIDTH)] * N_MIXERS
                 + [const(w), const(bo), const(g), const(beta)],
        out_specs=row_spec(D_MODEL),
        out_shape=jax.ShapeDtypeStruct((t, D_MODEL), F32),
        compiler_params=pltpu.CompilerParams(dimension_semantics=("arbitrary",),
                                             vmem_limit_bytes=VMEM_LIMIT_BYTES),
        name="outproj_ln",
    )(x2d, *mixed, w, bo, g, beta)


def _ffn_kernel(alpha, x_ref, w1_ref, b1_ref, w2_ref, b2_ref, g_ref, beta_ref, o_ref, xb_ref, acc_ref):
    j = pl.program_id(1)

    @pl.when(j == 0)
    def _():
        xb_ref[...] = x_ref[...].astype(BF16)
        acc_ref[...] = jnp.zeros_like(acc_ref)

    hdn = jnp.maximum(_dot(xb_ref[...], w1_ref[...]) + b1_ref[...], 0.0)
    acc_ref[...] += _dot(jnp.square(hdn).astype(BF16), w2_ref[...])

    @pl.when(j == pl.num_programs(1) - 1)
    def _():
        y = alpha * x_ref[...] + (acc_ref[...] + b2_ref[...])
        o_ref[...] = _layer_norm_rows(y, g_ref[...], beta_ref[...])


def _ffn(alpha, x2d, w1, b1, w2, b2, g, beta, tm=1024, tf=1024):
    t = x2d.shape[0]
    return pl.pallas_call(
        functools.partial(_ffn_kernel, alpha),
        grid=(t // tm, D_FF // tf),
        in_specs=[pl.BlockSpec((tm, D_MODEL), lambda i, j: (i, 0)),
                  pl.BlockSpec((D_MODEL, tf), lambda i, j: (0, j)),
                  pl.BlockSpec((1, tf), lambda i, j: (0, j)),
                  pl.BlockSpec((tf, D_MODEL), lambda i, j: (j, 0)),
                  pl.BlockSpec((1, D_MODEL), lambda i, j: (0, 0)),
                  pl.BlockSpec((1, D_MODEL), lambda i, j: (0, 0)),
                  pl.BlockSpec((1, D_MODEL), lambda i, j: (0, 0))],
        out_specs=pl.BlockSpec((tm, D_MODEL), lambda i, j: (i, 0)),
        out_shape=jax.ShapeDtypeStruct((t, D_MODEL), F32),
        scratch_shapes=[pltpu.VMEM((tm, D_MODEL), BF16), pltpu.VMEM((tm, D_MODEL), F32)],
        compiler_params=pltpu.CompilerParams(dimension_semantics=("arbitrary", "arbitrary"),
                                             vmem_limit_bytes=VMEM_LIMIT_BYTES),
        name="ffn_ln",
    )(x2d, w1, b1, w2, b2, g, beta)


def _bias_of_distance(rel_bias_heads, dist):
    onehot = np.eye(NUM_BUCKETS, dtype=np.float32)[_t5_bucket_np(np.asarray(dist))]
    return jnp.dot(rel_bias_heads.T, jnp.asarray(onehot.T), precision=lax.Precision.HIGHEST)


def _shifted_rows(v, n_rows, n_cols, step):
    period = v.shape[-1]
    assert n_cols <= period - step
    flat = jnp.tile(v, (1,) * (v.ndim - 1) + (n_rows,))[..., :n_rows * (period - step)]
    return flat.reshape(v.shape[:-1] + (n_rows, period - step))[..., :n_cols]


def _wrapped(period):
    idx = np.arange(period)
    return np.where(idx < period // 2, idx, idx - period)


def _dilated_bias(rel_bias):
    x = _wrapped(4 * BLK)
    tabs = [_shifted_rows(_bias_of_distance(rel_bias[:, 0:HEADS], (BLK - x) * dil), BLK, 2 * BLK, 1)
            for _, dil in DIL_PATTERNS]
    tabs = jnp.stack(tabs).reshape(len(DIL_PATTERNS), HEADS // 2, 2, BLK, 2 * BLK)
    return jnp.transpose(tabs, (0, 1, 3, 2, 4)).reshape(len(DIL_PATTERNS), HEADS // 2, BLK, 4 * BLK)


def _toeplitz_bias_t(rel_bias_heads, blk=BLK):
    assert (_t5_bucket_np(np.arange(blk + 1, 64 * blk)) == NUM_BUCKETS - 1).all()
    x = _wrapped(2 * blk)
    tabs = [_shifted_rows(_bias_of_distance(rel_bias_heads, blk * delta + x), blk, blk, 1)
            for delta in range(3)]
    return jnp.transpose(jnp.stack(tabs), (0, 2, 1, 3)).reshape(3, blk, HEADS * blk)


def _compressed_bias_t(rel_bias_heads, seq):
    nq = seq // QRY
    x = _wrapped(2 * seq + BLK)
    v = _bias_of_distance(rel_bias_heads, x - (NSA_CMP_LEN - 1))
    tab = _shifted_rows(v, BLK, seq, NSA_CMP_STRIDE)
    tab = tab.reshape(HEADS, BLK, nq, QRY)
    return jnp.transpose(tab, (2, 1, 0, 3)).reshape(nq, BLK, HEADS * QRY)


def _nsa_constants(seq):
    n_cmp = (seq - NSA_CMP_LEN) // NSA_CMP_STRIDE + 1
    n_sel = seq // NSA_SEL_LEN
    cs = np.arange(n_cmp)[:, None] * NSA_CMP_STRIDE
    ss = np.arange(n_sel)[None, :] * NSA_SEL_LEN
    ov = np.clip(np.minimum(cs + NSA_CMP_LEN, ss + NSA_SEL_LEN) - np.maximum(cs, ss), 0, None) / NSA_CMP_LEN
    ovt = np.zeros((BLK, BLK), np.float32)
    ovt[:n_sel, :n_cmp] = ov.T
    expand = np.zeros((seq, BLK), np.float32)
    expand[np.arange(seq), np.arange(seq) // NSA_SEL_LEN] = 1.0
    return jnp.asarray(ovt), jnp.asarray(expand, BF16)


def _nsa_weights(cmp_pos, cmp_w1, cmp_w2):
    hd, hid = HEAD_DIM, NSA_CMP_HIDDEN
    w1 = cmp_w1.reshape(2, NSA_CMP_LEN, hd, hid)
    zeros = jnp.zeros((NSA_CMP_LEN, hd, hid), F32)
    w1 = jnp.concatenate([jnp.concatenate([w1[0], zeros], axis=-1),
                          jnp.concatenate([zeros, w1[1]], axis=-1)], axis=1)
    pos = jnp.concatenate([cmp_pos[0], cmp_pos[1]], axis=-1)
    z2 = jnp.zeros((hid, hd), F32)
    w2 = jnp.concatenate([jnp.concatenate([cmp_w2[0], z2], axis=-1),
                          jnp.concatenate([z2, cmp_w2[1]], axis=-1)], axis=0)
    return w1.astype(BF16), pos, w2.astype(BF16)


def kernel(x, w_in, b_in, a_conv, a_norm, d_cmp_pos, d_cmp_w1, d_cmp_w2, w_out, b_out, ln1_g, ln1_b,
           w_ff1, b_ff1, w_ff2, b_ff2, ln2_g, ln2_b, rel_bias):
    bsz, seq, _ = x.shape
    depth = w_in.shape[0]
    alpha = (2 * depth) ** 0.25
    nc = seq // M_CHUNK
    bias_dil = _dilated_bias(rel_bias)
    btab_dsa = _toeplitz_bias_t(rel_bias[:, HEADS:2 * HEADS], QRY) * LOG2_E
    btab_nsa = _toeplitz_bias_t(rel_bias[:, 2 * HEADS:3 * HEADS], QRY) * LOG2_E
    bcmp_nsa = _compressed_bias_t(rel_bias[:, 2 * HEADS:3 * HEADS], seq) * LOG2_E
    ovt, expand = _nsa_constants(seq)

    h = x.reshape(bsz * seq, D_MODEL)
    for l in range(depth):
        w_l = _permute_columns(w_in[l]).astype(BF16)
        b_l = _permute_columns(b_in[l])[None, :]
        za, zb, zc, zd, zg = (z.reshape(bsz, seq, -1) for z in _inproj(h, w_l, b_l))
        out_a = _mlstm(za, _mlstm_gate_rows(zg), a_conv[l], a_norm[l][None, :])
        out_b = _dilated(zb, bias_dil)
        out_c = _dsa(zc, btab_dsa)
        nsa_w1, nsa_pos, nsa_w2 = _nsa_weights(d_cmp_pos[l], d_cmp_w1[l], d_cmp_w2[l])
        out_d = _nsa(zd, nsa_w1, nsa_pos, nsa_w2, btab_nsa, bcmp_nsa, ovt, expand)
        mixed = [o.reshape(bsz * seq, GROUP_WIDTH) for o in (out_a, out_b, out_c, out_d)]
        h = _outproj(alpha, h, mixed, w_out[l].astype(BF16), b_out[l][None, :],
                     ln1_g[l][None, :], ln1_b[l][None, :])
        h = _ffn(alpha, h, w_ff1[l].astype(BF16), b_ff1[l][None, :], w_ff2[l].astype(BF16),
                 b_ff2[l][None, :], ln2_g[l][None, :], ln2_b[l][None, :])
    return h.reshape(bsz, seq, D_MODEL)
```

```python
import functools
import math

import numpy as np
import jax
import jax.numpy as jnp
from jax import lax
from jax.experimental import pallas as pl
from jax.experimental.pallas import tpu as pltpu

F32 = jnp.float32
BF16 = jnp.bfloat16
I32 = jnp.int32
I16 = jnp.int16
I16_MIN = -2 ** 15

D_MODEL = 1024
N_MIXERS = 4
HEADS = 4
HEAD_DIM = D_MODEL // (N_MIXERS * HEADS)
GROUP_WIDTH = HEADS * HEAD_DIM
D_FF = 4 * D_MODEL
LN_EPS = 1e-5
NEG = -1e30

M_QK_DIM = HEAD_DIM // 2
M_CHUNK = 64
M_CONV = 4
DIL_PATTERNS = ((128, 1), (512, 4), (2048, 16))
IDX_HEADS = 4
IDX_DIM = 64
DSA_TOPK = 256
NSA_CMP_LEN = 32
NSA_CMP_STRIDE = 16
NSA_SEL_LEN = 64
NSA_TOPN = 16
NSA_WINDOW = 512
NSA_CMP_HIDDEN = 256
NSA_FORCE = 1e9
NUM_BUCKETS = 32
MAX_DISTANCE = 128

LANES = 128
BLK = 128
KEYS = 2 * BLK
QRY = KEYS
VMEM_LIMIT_BYTES = 56 * 1024 * 1024

IN_SPLITS = (
    ('a_q', HEADS * M_QK_DIM), ('a_k', HEADS * M_QK_DIM), ('a_v', GROUP_WIDTH),
    ('a_i', HEADS), ('a_f', HEADS), ('a_o', GROUP_WIDTH),
    ('b_q', GROUP_WIDTH), ('b_k', GROUP_WIDTH), ('b_v', GROUP_WIDTH),
    ('c_q', GROUP_WIDTH), ('c_k', HEAD_DIM), ('c_v', HEAD_DIM),
    ('c_iq', IDX_HEADS * IDX_DIM), ('c_ik', IDX_DIM), ('c_iw', IDX_HEADS),
    ('d_q', GROUP_WIDTH), ('d_kc', HEAD_DIM), ('d_vc', HEAD_DIM),
    ('d_ks', HEAD_DIM), ('d_vs', HEAD_DIM), ('d_kw', HEAD_DIM), ('d_vw', HEAD_DIM),
    ('d_g', 3 * HEADS),
)

GROUP_LAYOUT = (
    (('a_q', 'a_k'), ('a_v',), ('a_o',), (('a_i', HEAD_DIM),), (('a_i', M_QK_DIM),),
     (('a_f', HEAD_DIM),), (('a_f', M_QK_DIM),)),
    (('b_q',), ('b_k',), ('b_v',)),
    (('c_q',), ('c_iq',), ('c_k', 'c_v'), ('c_ik', 'c_iw')),
    (('d_q',), ('d_kc', 'd_vc'), ('d_ks', 'd_vs'), ('d_kw', 'd_vw'), ('d_g',)),
    (('a_i', 'a_f'),),
)


def _round_up(n, m):
    return -(-n // m) * m


def _projection_layout():
    offs, off = {}, 0
    for name, width in IN_SPLITS:
        offs[name] = (off, width)
        off += width
    runs, group_widths, chunk_starts = [], [], []
    for group in GROUP_LAYOUT:
        gwidth = 0
        chunk_starts.append([])
        for chunk in group:
            chunk_starts[-1].append(gwidth)
            cwidth = 0
            for entry in chunk:
                name, rep = entry if isinstance(entry, tuple) else (entry, 1)
                o, w = offs[name]
                runs.append((o, w, rep))
                cwidth += w * rep
            pad = _round_up(cwidth, LANES) - cwidth
            if pad:
                runs.append((-1, pad, 1))
            gwidth += cwidth + pad
        group_widths.append(gwidth)
    return tuple(runs), tuple(group_widths), tuple(tuple(c) for c in chunk_starts)


PROJ_RUNS, GROUP_WIDTHS, CHUNK_STARTS = _projection_layout()
PROJ_WIDTH = int(sum(GROUP_WIDTHS))


def _permute_columns(a):
    parts = []
    for o, w, r in PROJ_RUNS:
        if o < 0:
            parts.append(jnp.zeros(a.shape[:-1] + (w,), a.dtype))
        else:
            parts.append(a[..., o:o + w] if r == 1 else jnp.repeat(a[..., o:o + w], r, axis=-1))
    return jnp.concatenate(parts, axis=-1)


def _t5_bucket_np(dist):
    n = np.maximum(dist, 0)
    max_exact = NUM_BUCKETS // 2
    nf = np.maximum(n, max_exact).astype(np.float32)
    large = max_exact + (np.log(nf / max_exact) / math.log(MAX_DISTANCE / max_exact)
                         * (NUM_BUCKETS - max_exact)).astype(np.int32)
    large = np.minimum(large, NUM_BUCKETS - 1)
    return np.where(n < max_exact, n, large).astype(np.int32)


def _nt_dot(a, b, precision=None):
    return lax.dot_general(a, b, (((1,), (1,)), ((), ())), precision=precision,
                           preferred_element_type=F32)


def _dot(a, b, precision=None):
    return jnp.dot(a, b, precision=precision, preferred_element_type=F32)


def _layer_norm_rows(y, g, b):
    mu = jnp.mean(y, axis=-1, keepdims=True)
    var = jnp.mean(jnp.square(y - mu), axis=-1, keepdims=True)
    return (y - mu) * lax.rsqrt(var + LN_EPS) * g + b


def _sigmoid(x):
    return 1.0 / (1.0 + jnp.exp(-x))


def _log_sigmoid(x):
    return -(jnp.maximum(-x, 0.0) + jnp.log1p(jnp.exp(-jnp.abs(x))))


def _sortable_key(x):
    bits = pltpu.bitcast(x, I32)
    return bits ^ ((bits >> 31) & jnp.int32(0x7FFFFFFF))


def _inproj_kernel(x_ref, w_ref, b_ref, *out_refs):
    xb = x_ref[...].astype(BF16)
    off = 0
    for o_ref, width in zip(out_refs, GROUP_WIDTHS):
        o_ref[...] = _dot(xb, w_ref[:, off:off + width]) + b_ref[:, off:off + width]
        off += width


def _inproj(x2d, w, b, tm=512):
    t = x2d.shape[0]
    return pl.pallas_call(
        _inproj_kernel,
        grid=(t // tm,),
        in_specs=[pl.BlockSpec((tm, D_MODEL), lambda i: (i, 0)),
                  pl.BlockSpec((D_MODEL, PROJ_WIDTH), lambda i: (0, 0)),
                  pl.BlockSpec((1, PROJ_WIDTH), lambda i: (0, 0))],
        out_specs=[pl.BlockSpec((tm, gw), lambda i: (i, 0)) for gw in GROUP_WIDTHS],
        out_shape=[jax.ShapeDtypeStruct((t, gw), F32) for gw in GROUP_WIDTHS],
        compiler_params=pltpu.CompilerParams(dimension_semantics=("arbitrary",),
                                             vmem_limit_bytes=VMEM_LIMIT_BYTES),
        name="inproj",
    )(x2d, w, b)


def _split_terms(x, n):
    terms, rest = [], x
    for _ in range(n):
        terms.append(rest.astype(BF16))
        rest = rest - terms[-1].astype(F32)
    return terms


def _iota(shape, dim):
    return lax.broadcasted_iota(I32, shape, dim)


def _mlstm_kernel(z_ref, gt_ref, cw_ref, ng_ref, o_ref, xpad_ref):
    seq = z_ref.shape[1]
    L, DK, DV, H = M_CHUNK, M_QK_DIM, HEAD_DIM, HEADS
    assert L == DV
    wq, wv = H * DK, H * DV
    lg_dk, lg_dv = int(math.log2(DK)), int(math.log2(DV))
    _, c_v, c_o, c_i64, c_i32, c_f64, c_f32 = CHUNK_STARTS[0]
    assert c_f32 == c_f64 + wv
    xpad_ref[0:8, :] = jnp.zeros((8, 2 * wq), F32)
    xpad_ref[8:, :] = z_ref[0, :, 0:2 * wq]

    one_if = lambda cond: jnp.where(cond, 1.0, 0.0).astype(BF16)
    tri_l = one_if(_iota((L, L), 0) >= _iota((L, L), 1))
    trow = _iota((L, wv), 0)
    tri_heads = trow >= (_iota((L, wv), 1) & (L - 1))
    r_vv, c_vv = _iota((wv, wv), 0), _iota((wv, wv), 1)
    same_head = (r_vv >> lg_dv) == (c_vv >> lg_dv)
    ones_bd = one_if(same_head)
    mean_bd = jnp.where(same_head, 1.0 / DV, 0.0).astype(BF16)
    tri_u_bd = one_if(same_head & ((r_vv & (L - 1)) <= (c_vv & (L - 1))))
    state_mask = (_iota((wq, wv), 0) >> lg_dk) == (_iota((wq, wv), 1) >> lg_dv)
    eye_q = one_if(_iota((wq, wq), 0) == _iota((wq, wq), 1))
    head_of_qlane = _iota((L, wq), 1) >> lg_dk
    head_of_vlane = _iota((L, wv), 1) >> lg_dv
    row8 = _iota((8, wv), 0)
    cw = cw_ref[...]
    ng = ng_ref[...]

    def head_mean(x):
        hi_lo = _split_terms(x, 2)
        r = _dot(jnp.concatenate(hi_lo, axis=0), mean_bd)
        return r[0:L] + r[L:2 * L]

    def chunk(c, carry):
        cbd, nbd, m64, m32 = carry
        s0 = pl.multiple_of(c * L, L)
        rows = pl.ds(s0, L)
        xw = xpad_ref[pl.ds(s0, L + 8), :]
        y = sum(cw[j:j + 1, :] * xw[5 + j:5 + j + L, :] for j in range(M_CONV))
        qk = y * _sigmoid(y)
        q = qk[:, 0:wq]
        k = qk[:, wq:] * (DK ** -0.5)
        qb = q.astype(BF16)
        v = z_ref[0, rows, c_v:c_v + wv]
        i64 = z_ref[0, rows, c_i64:c_i64 + wv]
        i32 = z_ref[0, rows, c_i32:c_i32 + wq]
        gr = gt_ref[0, c]

        flog = _log_sigmoid(z_ref[0, rows, c_f64:c_f64 + wv + wq])
        bsum = _dot(tri_l, jnp.concatenate(_split_terms(flog, 3), axis=1))
        w3 = wv + wq
        ball = bsum[:, 0:w3] + bsum[:, w3:2 * w3] + bsum[:, 2 * w3:3 * w3]
        b64, b32 = ball[:, 0:wv], ball[:, wv:w3]
        fterms = [t.astype(F32) for t in _split_terms(_log_sigmoid(gr[1:2, :]), 3)]
        frows = jnp.where(row8 == 0, fterms[0], jnp.where(row8 == 1, fterms[1], jnp.where(row8 == 2, fterms[2], 0.0)))
        bparts = _dot(frows.astype(BF16), tri_u_bd)
        brow = bparts[0:1, :] + bparts[1:2, :] + bparts[2:3, :]

        dall = jnp.where(tri_heads, b64 - brow + gr[0:1, :], NEG)
        cm = i64 - b64
        for sh in (1, 2, 4, 8, 16, 32):
            cm = jnp.where(trow >= sh, jnp.maximum(cm, pltpu.roll(cm, sh, 0)), cm)
        inter = b64 + m64
        m_t = jnp.maximum(inter, b64 + cm)
        kbd = jnp.concatenate([jnp.where(head_of_qlane == h, k, 0.0) for h in range(H)], axis=0).astype(BF16)
        sc = _nt_dot(qb, kbd) * jnp.exp(dall - m_t)
        wi = jnp.exp(inter - m_t)
        vb = v.astype(BF16)
        vbd = jnp.concatenate([jnp.where(head_of_vlane == h, v, 0.0) for h in range(H)], axis=0).astype(BF16)
        pv = _dot(sc.astype(BF16), jnp.concatenate([vbd, ones_bd], axis=1))
        qst = _dot(qb, jnp.concatenate([cbd, nbd], axis=1).astype(BF16))
        num = pv[:, 0:wv] + wi * qst[:, 0:wv]
        den = pv[:, wv:] + wi * qst[:, wv:]
        hh = num / jnp.maximum(jnp.abs(den), jnp.exp(-m_t))

        og = _sigmoid(z_ref[0, rows, c_o:c_o + wv]) * hh
        dev = og - head_mean(og)
        o_ref[0, rows, :] = dev * lax.rsqrt(head_mean(dev * dev) + LN_EPS) * ng

        bl64, bl32 = b64[L - 1:L, :], b32[L - 1:L, :]
        m64_new = jnp.maximum(bl64 + m64, jnp.max(bl64 - b64 + i64, axis=0, keepdims=True))
        g32 = bl32 - b32 + i32
        m32_new = jnp.maximum(bl32 + m32, jnp.max(g32, axis=0, keepdims=True))
        kw = k * jnp.exp(g32 - m32_new)
        wc = jnp.exp(bl64 + m64 - m64_new)
        kwt = _nt_dot(eye_q, kw.astype(BF16)).astype(BF16)
        upd = _dot(kwt, jnp.concatenate([vb, jnp.ones((L, wv), BF16)], axis=1))
        cbd = wc * cbd + jnp.where(state_mask, upd[:, 0:wv], 0.0)
        nbd = wc * nbd + jnp.where(state_mask, upd[:, wv:], 0.0)
        return cbd, nbd, m64_new, m32_new

    init = (jnp.zeros((wq, wv), F32), jnp.zeros((wq, wv), F32), jnp.zeros((1, wv), F32), jnp.zeros((1, wq), F32))
    lax.fori_loop(0, seq // L, chunk, init, unroll=4)


def _mlstm(za, gates_t, conv_w, norm_g):
    b, s, wa = za.shape
    nc = s // M_CHUNK
    return pl.pallas_call(
        _mlstm_kernel,
        grid=(b,),
        in_specs=[pl.BlockSpec((1, s, wa), lambda i: (i, 0, 0)),
                  pl.BlockSpec((1, nc) + gates_t.shape[2:], lambda i: (i, 0, 0, 0)),
                  pl.BlockSpec((M_CONV, 2 * HEADS * M_QK_DIM), lambda i: (0, 0)),
                  pl.BlockSpec((1, GROUP_WIDTH), lambda i: (0, 0))],
        out_specs=pl.BlockSpec((1, s, GROUP_WIDTH), lambda i: (i, 0, 0)),
        out_shape=jax.ShapeDtypeStruct((b, s, GROUP_WIDTH), F32),
        scratch_shapes=[pltpu.VMEM((s + 8, 2 * HEADS * M_QK_DIM), F32)],
        compiler_params=pltpu.CompilerParams(dimension_semantics=("arbitrary",),
                                             vmem_limit_bytes=VMEM_LIMIT_BYTES),
        name="mlstm",
    )(za, gates_t, conv_w, norm_g)


def _mlstm_gate_rows(zg):
    b, s, _ = zg.shape
    gates = zg[:, :, 0:2 * HEADS].reshape(b, s // M_CHUNK, M_CHUNK, 2, HEADS)
    return jnp.transpose(gates, (0, 1, 3, 4, 2)).reshape(b, s // M_CHUNK, 2, HEADS * M_CHUNK)


def _dilated_kernel(q0_ref, q1_ref, k0_ref, k1_ref, v0_ref, v1_ref, bias_ref, o_ref, acc_ref, mx_ref, den_ref):
    seq = q0_ref.shape[1]
    W = BLK
    hd = HEAD_DIM
    npair = HEADS // 2
    scale = hd ** -0.5
    assert math.log2(scale).is_integer()
    nk = 2 * W
    qi = _iota((W, 2 * nk), 0)
    ki = _iota((W, 2 * nk), 1) & (nk - 1)
    j = W + qi - ki
    band = (j >= 0) & (j <= W)
    head_of_lane = _iota((nk, 2 * hd), 1) >> int(math.log2(hd))
    ones_bd = jnp.concatenate([jnp.where(head_of_lane == hh, 1.0, 0.0) for hh in range(2)], axis=0).astype(BF16)
    q_refs, k_refs, v_refs = (q0_ref, q1_ref), (k0_ref, k1_ref), (v0_ref, v1_ref)

    def block_diag(x):
        return jnp.concatenate([jnp.where(head_of_lane == hh, x, 0.0) for hh in range(2)], axis=0).astype(BF16)

    for br, (window, dil) in enumerate(DIL_PATTERNS):
        assert window // dil == W
        nb = (seq // dil) // W

        def piece(idx, _, br=br, dil=dil, nb=nb):
            n = idx % nb
            if dil == 1:
                rows_q = pl.ds(pl.multiple_of(W * n, W), W)
                rows_p = pl.ds(pl.multiple_of(W * jnp.maximum(n - 1, 0), W), W)
            else:
                r = idx // nb
                rows_q = pl.ds(r + dil * W * n, W, stride=dil)
                rows_p = pl.ds(r + dil * W * jnp.maximum(n - 1, 0), W, stride=dil)
            mask = band & (ki >= jnp.where(n > 0, 0, W))
            for pr in range(npair):
                q = (q_refs[pr][0, rows_q, :] * scale).astype(BF16)
                k2 = jnp.concatenate([k_refs[pr][0, rows_p, :], k_refs[pr][0, rows_q, :]], axis=0)
                v2 = jnp.concatenate([v_refs[pr][0, rows_p, :], v_refs[pr][0, rows_q, :]], axis=0)
                lg = jnp.where(mask, _nt_dot(q, block_diag(k2)) + bias_ref[br, pr], NEG)
                ps, mxs = [], []
                for hh in range(2):
                    sl = slice(nk * hh, nk * (hh + 1))
                    m = jnp.max(lg[:, sl], axis=-1, keepdims=True)
                    ps.append(jnp.where(mask[:, sl], jnp.exp(lg[:, sl] - m), 0.0))
                    mxs.append(jnp.broadcast_to(m, (W, hd)))
                p = jnp.concatenate(ps, axis=-1).astype(BF16)
                pv = _dot(p, jnp.concatenate([block_diag(v2), ones_bd], axis=1))
                acc_ref[br, pr, rows_q, :] = pv[:, 0:2 * hd]
                den_ref[br, pr, rows_q, :] = pv[:, 2 * hd:]
                mx_ref[br, pr, rows_q, :] = jnp.concatenate(mxs, axis=-1)
            return 0

        lax.fori_loop(0, dil * nb, piece, 0, unroll=4)

    def combine(i, _):
        rows = pl.ds(pl.multiple_of(i * W, W), W)
        outs = []
        for pr in range(npair):
            ms = [mx_ref[b, pr, rows, :] for b in range(len(DIL_PATTERNS))]
            top = functools.reduce(jnp.maximum, ms)
            es = [jnp.exp(m - top) for m in ms]
            num = sum(e * acc_ref[b, pr, rows, :] for b, e in enumerate(es))
            den = sum(e * jnp.maximum(den_ref[b, pr, rows, :], 1e-30) for b, e in enumerate(es))
            outs.append(num / den)
        o_ref[0, rows, :] = jnp.concatenate(outs, axis=-1)
        return 0

    lax.fori_loop(0, seq // W, combine, 0)


def _dilated(zb, bias):
    b, s, wb = zb.shape
    nbr = len(DIL_PATTERNS)
    pair_spec = lambda c: pl.BlockSpec((1, s, LANES), lambda i: (i, 0, c))
    return pl.pallas_call(
        _dilated_kernel,
        grid=(b,),
        in_specs=[pair_spec(c) for c in range(wb // LANES)]
                 + [pl.BlockSpec(bias.shape, lambda i: (0, 0, 0, 0))],
        out_specs=pl.BlockSpec((1, s, GROUP_WIDTH), lambda i: (i, 0, 0)),
        out_shape=jax.ShapeDtypeStruct((b, s, GROUP_WIDTH), F32),
        scratch_shapes=[pltpu.VMEM((nbr, HEADS // 2, s, LANES), F32) for _ in range(3)],
        compiler_params=pltpu.CompilerParams(dimension_semantics=("arbitrary",),
                                             vmem_limit_bytes=VMEM_LIMIT_BYTES),
        name="dilated",
    )(*([zb] * (wb // LANES)), bias)


def _head_columns_t(q):
    qt = q.T
    return jnp.concatenate([qt[HEAD_DIM * h:HEAD_DIM * (h + 1), :] for h in range(HEADS)], axis=1)


LOG2_E = math.log2(math.e)


def _scaled_query_columns_t(q):
    return (_head_columns_t(q) * (HEAD_DIM ** -0.5 * LOG2_E)).astype(BF16)


def _unstack_heads_t(per_head_t):
    halves = []
    for h in range(0, HEADS, 2):
        halves.append(jnp.concatenate([per_head_t[h], per_head_t[h + 1]], axis=0).T)
    return jnp.concatenate(halves, axis=-1)


VT_ROWS = HEAD_DIM + 16


def _online_step(lg, mask, vt1, m, acc):
    lg = jnp.where(mask, lg, NEG)
    m_new = jnp.maximum(m, jnp.max(lg, axis=0, keepdims=True))
    p = jnp.where(mask, jnp.exp2(lg - m_new), 0.0)
    acc = acc * jnp.exp2(m - m_new) + _dot(vt1, p.astype(BF16))
    return m_new, acc


def _online_result(acc):
    return acc[0:HEAD_DIM, :] / jnp.maximum(acc[HEAD_DIM:HEAD_DIM + 1, :], 1e-30)


def _values_with_ones_t(z_rows):
    vt = z_rows.T[HEAD_DIM:2 * HEAD_DIM, :]
    return jnp.concatenate([vt, jnp.ones((VT_ROWS - HEAD_DIM, vt.shape[1]), F32)], axis=0).astype(BF16)


def _online_init(nq=BLK):
    return (tuple(jnp.full((1, nq), NEG, F32) for _ in range(HEADS)),
            tuple(jnp.zeros((VT_ROWS, nq), F32) for _ in range(HEADS)))


def _dsa_kernel(z_ref, btab_ref, o_ref, vt_ref, key_ref, hi_ref, lo_ref, lom_ref):
    i = pl.program_id(1)
    seq = z_ref.shape[1]
    nkb = seq // BLK
    nq = QRY
    hd = HEAD_DIM
    topk = min(DSA_TOPK, seq // 4)
    t0 = pl.multiple_of(i * nq, nq)

    assert nq == KEYS
    npair = i + 1
    c_q, c_iq, c_kv, c_ikw = CHUNK_STARTS[2]

    @pl.when(i == 0)
    def _():
        for kb in range(nkb):
            vt_ref[kb // 2, :, (kb % 2) * BLK:(kb % 2 + 1) * BLK] = _values_with_ones_t(
                z_ref[0, kb * BLK:(kb + 1) * BLK, c_kv:c_kv + 2 * hd])

    zq = z_ref[0, pl.ds(t0, nq), :]
    cq = zq[:, c_q:c_q + HEADS * hd]
    ciq = _head_columns_t(zq[:, c_iq:c_iq + IDX_HEADS * IDX_DIM]).astype(BF16)
    iw = (zq[:, c_ikw:c_ikw + LANES].T[IDX_DIM:IDX_DIM + IDX_HEADS, :]
          * ((IDX_HEADS * IDX_DIM) ** -0.5))
    s_loc = lax.broadcasted_iota(I32, (KEYS, nq), 0)
    t_glob = t0 + lax.broadcasted_iota(I32, (KEYS, nq), 1)

    def score_block(kp, _):
        r0 = pl.multiple_of(kp * KEYS, KEYS)
        ik = z_ref[0, pl.ds(r0, KEYS), c_ikw:c_ikw + IDX_DIM].astype(BF16)
        rel = _dot(ik, ciq)
        sc = jnp.zeros((KEYS, nq), F32)
        for h in range(IDX_HEADS):
            sc = sc + jnp.maximum(rel[:, nq * h:nq * (h + 1)], 0.0) * iw[h:h + 1, :]
        sc = jnp.where(r0 + s_loc <= t_glob, sc, NEG)
        key = _sortable_key(sc)
        key_ref[kp] = key
        hi_ref[kp] = (key >> 16).astype(I16)
        lo_ref[kp] = ((key & 0xFFFF) - 2 ** 15).astype(I16)
        return 0

    lax.fori_loop(0, npair, score_block, 0)

    def count(ref, pred, pairs):
        dt = ref.dtype
        rows = 8 * 4 // dt.itemsize
        def body(kp, acc):
            hit = jnp.where(pred(kp, ref[kp]), jnp.ones((), dt), jnp.zeros((), dt))
            hit = hit.reshape(KEYS // rows, rows, nq)
            parts = [hit[n] for n in range(KEYS // rows)]
            while len(parts) > 1:
                parts = [a + b for a, b in zip(parts[0::2], parts[1::2])]
            return acc + parts[0]
        acc = jnp.zeros((rows, nq), dt)
        for kp in range(pairs):
            acc = body(kp, acc)
        return jnp.sum(acc.astype(I32), axis=0, keepdims=True)

    def threshold(pairs):
        def half_search(ref, k):
            def bit(it, thr):
                cand = thr + lax.shift_left(jnp.int32(1), 15 - it)
                c = count(ref, lambda kp, half: half >= cand.astype(I16), pairs)
                return jnp.where(c >= k, cand, thr)
            return lax.fori_loop(0, 16, bit, jnp.full((1, nq), I16_MIN, I32))

        def run(_):
            thr_hi = half_search(hi_ref, topk)
            thr_hi16 = thr_hi.astype(I16)
            above_hi = count(hi_ref, lambda kp, half: half > thr_hi16, pairs)
            for kp in range(pairs):
                lom_ref[kp] = jnp.where(hi_ref[kp] == thr_hi16, lo_ref[kp], jnp.int16(I16_MIN))
            thr_lo = half_search(lom_ref, topk - above_hi)
            thr = (thr_hi << 16) | (thr_lo + 2 ** 15)
            n_gt = count(key_ref, lambda kp, key: key > thr, pairs)
            need = (topk - n_gt).astype(F32)
            earlier = jnp.zeros((1, nq), F32)
            for kp in range(pairs):
                tie = key_ref[kp] == thr
                tie01 = jnp.where(tie, 1.0, 0.0)
                rank = _dot(below, tie01.astype(BF16)) + earlier
                lom_ref[kp] = jnp.where(tie & (rank < need), 1, 0).astype(I16)
                earlier = earlier + jnp.sum(tie01, axis=0, keepdims=True)
            return thr
        return run

    below = jnp.where(_iota((KEYS, KEYS), 1) < _iota((KEYS, KEYS), 0), 1.0, 0.0).astype(BF16)
    thr = lax.switch(npair - 1, [threshold(p) for p in range(1, seq // KEYS + 1)], 0)

    qs = _scaled_query_columns_t(cq)

    def attend(kp, carry):
        ms, accs = carry
        r0 = pl.multiple_of(kp * KEYS, KEYS)
        kblk = z_ref[0, pl.ds(r0, KEYS), c_kv:c_kv + hd].astype(BF16)
        lg = _dot(kblk, qs) + btab_ref[jnp.minimum(i - kp, 2)]
        key = key_ref[kp]
        s_glob = r0 + s_loc
        mask = ((key > thr) | (lom_ref[kp].astype(I32) != 0)) & (s_glob <= t_glob)
        vt1 = vt_ref[kp]
        out = [_online_step(lg[:, nq * h:nq * (h + 1)], mask, vt1, ms[h], accs[h]) for h in range(HEADS)]
        return tuple(o[0] for o in out), tuple(o[1] for o in out)

    ms, accs = lax.fori_loop(0, npair, attend, _online_init(nq))
    o_ref[0] = _unstack_heads_t([_online_result(accs[h]) for h in range(HEADS)])


def _dsa(zc, btab):
    b, s, wc = zc.shape
    return pl.pallas_call(
        _dsa_kernel,
        grid=(b, s // QRY),
        in_specs=[pl.BlockSpec((1, s, wc), lambda bi, i: (bi, 0, 0)),
                  pl.BlockSpec(btab.shape, lambda bi, i: (0, 0, 0))],
        out_specs=pl.BlockSpec((1, QRY, GROUP_WIDTH), lambda bi, i: (bi, i, 0)),
        out_shape=jax.ShapeDtypeStruct((b, s, GROUP_WIDTH), F32),
        scratch_shapes=[pltpu.VMEM((s // KEYS, VT_ROWS, KEYS), BF16), pltpu.VMEM((s // KEYS, KEYS, QRY), I32)]
                       + [pltpu.VMEM((s // KEYS, KEYS, QRY), I16)] * 3,
        compiler_params=pltpu.CompilerParams(dimension_semantics=("arbitrary", "arbitrary"),
                                             vmem_limit_bytes=VMEM_LIMIT_BYTES),
        name="dsa",
    )(zc, btab)


def _nsa_kernel(z_ref, zc_ref, w1_ref, pos_ref, w2_ref, bsel_ref, bcmp_ref, ovt_ref, exp_ref, o_ref,
                vst_ref, vwt_ref, cmp_ref, cmpt_ref):
    i = pl.program_id(1)
    seq = z_ref.shape[1]
    nkb = seq // BLK
    hd = HEAD_DIM
    c_q, _, c_s, c_w, c_g = CHUNK_STARTS[3]
    n_cmp = (seq - NSA_CMP_LEN) // NSA_CMP_STRIDE + 1
    n_sel = seq // NSA_SEL_LEN
    topn = min(NSA_TOPN, n_sel)
    half = NSA_CMP_LEN // 2
    assert half == NSA_CMP_STRIDE and n_cmp + 1 == seq // NSA_CMP_STRIDE == BLK and n_sel <= BLK
    nq = QRY
    assert nq == KEYS
    t0 = pl.multiple_of(i * nq, nq)
    hi = lax.Precision.HIGHEST

    @pl.when(i == 0)
    def _():
        for kb in range(nkb):
            rows = slice(kb * BLK, (kb + 1) * BLK)
            cols = slice((kb % 2) * BLK, (kb % 2 + 1) * BLK)
            vst_ref[kb // 2, :, cols] = _values_with_ones_t(z_ref[0, rows, c_s:c_s + 2 * hd])
            vwt_ref[kb // 2, :, cols] = _values_with_ones_t(z_ref[0, rows, c_w:c_w + 2 * hd])
        first = jnp.zeros((BLK, 2 * NSA_CMP_HIDDEN), F32)
        second = jnp.zeros((BLK, 2 * NSA_CMP_HIDDEN), F32)
        for j in range(half):
            xj = zc_ref[0, pl.ds(j, BLK, stride=NSA_CMP_STRIDE), :]
            first = first + _dot((xj + pos_ref[j:j + 1, :]).astype(BF16), w1_ref[j])
            second = second + _dot((xj + pos_ref[half + j:half + j + 1, :]).astype(BF16), w1_ref[half + j])
        hid = first + pltpu.roll(second, BLK - 1, 0)
        hid = hid * _sigmoid(hid)
        cmp = _dot(hid.astype(BF16), w2_ref[...])
        cmp_ref[...] = cmp
        cmpt_ref[...] = cmp.T

    zq = z_ref[0, pl.ds(t0, nq), :]
    qs = _scaled_query_columns_t(zq[:, c_q:c_q + HEADS * hd])
    gates = _sigmoid(zq[:, c_g:c_g + LANES].T[0:16, :])
    row = lax.broadcasted_iota(I32, (BLK, nq), 0)
    t_glob = t0 + lax.broadcasted_iota(I32, (BLK, nq), 1)

    kcmp = cmp_ref[:, 0:hd].astype(BF16)
    vcmpt = cmpt_ref[hd:2 * hd, :].astype(BF16)
    lgc = _dot(kcmp, qs) + bcmp_ref[0]
    mask_c = (t_glob - (row * NSA_CMP_STRIDE + NSA_CMP_LEN - 1) >= 0) & (row < n_cmp)
    o_cmp, psum = [], jnp.zeros((BLK, nq), F32)
    for h in range(HEADS):
        lg = jnp.where(mask_c, lgc[:, nq * h:nq * (h + 1)], NEG)
        m = jnp.max(lg, axis=0, keepdims=True)
        p = jnp.where(mask_c, jnp.exp2(lg - m), 0.0)
        p = p / jnp.maximum(jnp.sum(p, axis=0, keepdims=True), 1e-30)
        o_cmp.append(_dot(vcmpt, p.astype(BF16)))
        psum = psum + p

    imp = _dot(ovt_ref[...], psum, precision=hi)
    cur = t_glob >> int(math.log2(NSA_SEL_LEN))
    forced = (row == 0) | (row == cur) | (row == cur - 1)
    imp = jnp.where(forced, NSA_FORCE, imp)
    imp = jnp.where(row * NSA_SEL_LEN <= t_glob, imp, NEG)
    imp = imp[0:n_sel, :]
    jrow = lax.broadcasted_iota(I32, (n_sel, nq), 0)
    rank = jnp.zeros((n_sel, nq), I32)
    for jp in range(n_sel):
        other = imp[jp:jp + 1, :]
        rank = rank + ((other > imp) | ((other == imp) & (jp < jrow))).astype(I32)
    chosen = jnp.where(rank < topn, 1.0, 0.0)
    chosen = jnp.concatenate([chosen, jnp.zeros((BLK - n_sel, nq), F32)], axis=0).astype(BF16)

    s_loc = lax.broadcasted_iota(I32, (KEYS, nq), 0)
    t_keys = t0 + lax.broadcasted_iota(I32, (KEYS, nq), 1)

    def attend(vt_ref, lanes, mask_fn):
        def body(kp, carry):
            ms, accs = carry
            r0 = pl.multiple_of(kp * KEYS, KEYS)
            kblk = z_ref[0, pl.ds(r0, KEYS), lanes].astype(BF16)
            lg = _dot(kblk, qs) + bsel_ref[jnp.minimum(i - kp, 2)]
            mask = mask_fn(r0)
            vt1 = vt_ref[kp]
            out = [_online_step(lg[:, nq * h:nq * (h + 1)], mask, vt1, ms[h], accs[h]) for h in range(HEADS)]
            return tuple(o[0] for o in out), tuple(o[1] for o in out)
        return body

    def mask_sel(r0):
        picked = _dot(exp_ref[pl.ds(r0, KEYS), :], chosen) > 0.5
        return picked & (r0 + s_loc <= t_keys)

    def mask_win(r0):
        dist = t_keys - (r0 + s_loc)
        return (dist >= 0) & (dist < NSA_WINDOW)

    sel_body = attend(vst_ref, slice(c_s, c_s + hd), mask_sel)
    win_body = attend(vwt_ref, slice(c_w, c_w + hd), mask_win)
    first_w = jnp.maximum(i - NSA_WINDOW // KEYS, 0)
    sel_carry = lax.fori_loop(0, first_w, sel_body, _online_init(nq))
    (_, acc_s), (_, acc_w) = lax.fori_loop(
        first_w, i + 1, lambda kp, c: (sel_body(kp, c[0]), win_body(kp, c[1])), (sel_carry, _online_init(nq)))

    outs = []
    for h in range(HEADS):
        outs.append(gates[3 * h:3 * h + 1, :] * o_cmp[h] + gates[3 * h + 1:3 * h + 2, :] * _online_result(acc_s[h])
                    + gates[3 * h + 2:3 * h + 3, :] * _online_result(acc_w[h]))
    o_ref[0] = _unstack_heads_t(outs)


def _nsa(zd, w1, pos, w2, bsel, bcmp, ovt, expand):
    b, s, wd = zd.shape
    full = lambda a: pl.BlockSpec(a.shape, lambda bi, i: (0,) * a.ndim)
    return pl.pallas_call(
        _nsa_kernel,
        grid=(b, s // QRY),
        in_specs=[pl.BlockSpec((1, s, wd), lambda bi, i: (bi, 0, 0)),
                  pl.BlockSpec((1, s, LANES), lambda bi, i: (bi, 0, CHUNK_STARTS[3][1] // LANES)),
                  full(w1), full(pos), full(w2), full(bsel),
                  pl.BlockSpec((1, BLK, HEADS * QRY), lambda bi, i: (i, 0, 0)),
                  full(ovt), full(expand)],
        out_specs=pl.BlockSpec((1, QRY, GROUP_WIDTH), lambda bi, i: (bi, i, 0)),
        out_shape=jax.ShapeDtypeStruct((b, s, GROUP_WIDTH), F32),
        scratch_shapes=[pltpu.VMEM((s // KEYS, VT_ROWS, KEYS), BF16), pltpu.VMEM((s // KEYS, VT_ROWS, KEYS), BF16),
                        pltpu.VMEM((BLK, BLK), F32), pltpu.VMEM((BLK, BLK), F32)],
        compiler_params=pltpu.CompilerParams(dimension_semantics=("arbitrary", "arbitrary"),
                                             vmem_limit_bytes=VMEM_LIMIT_BYTES),
        name="nsa",
    )(zd, zd, w1, pos, w2, bsel, bcmp, ovt, expand)


def _outproj_kernel(alpha, x_ref, a_ref, b_ref, c_ref, d_ref, w_ref, bo_ref, g_ref, beta_ref, o_ref):
    acc = bo_ref[...] + _dot(a_ref[...].astype(BF16), w_ref[0:GROUP_WIDTH, :])
    for n, m_ref in enumerate((b_ref, c_ref, d_ref), start=1):
        acc = acc + _dot(m_ref[...].astype(BF16), w_ref[n * GROUP_WIDTH:(n + 1) * GROUP_WIDTH, :])
    o_ref[...] = _layer_norm_rows(alpha * x_ref[...] + acc, g_ref[...], beta_ref[...])


def _outproj(alpha, x2d, mixed, w, bo, g, beta, tm=512):
    t = x2d.shape[0]
    row_spec = lambda width: pl.BlockSpec((tm, width), lambda i: (i, 0))
    const = lambda a: pl.BlockSpec(a.shape, lambda i: (0, 0))
    return pl.pallas_call(
        functools.partial(_outproj_kernel, alpha),
        grid=(t // tm,),
        in_specs=[row_spec(D_MODEL)] + [row_spec(GROUP_WIDTH)] * N_MIXERS
                 + [const(w), const(bo), const(g), const(beta)],
        out_specs=row_spec(D_MODEL),
        out_shape=jax.ShapeDtypeStruct((t, D_MODEL), F32),
        compiler_params=pltpu.CompilerParams(dimension_semantics=("arbitrary",),
                                             vmem_limit_bytes=VMEM_LIMIT_BYTES),
        name="outproj_ln",
    )(x2d, *mixed, w, bo, g, beta)


def _ffn_kernel(alpha, x_ref, w1_ref, b1_ref, w2_ref, b2_ref, g_ref, beta_ref, o_ref, xb_ref, acc_ref):
    j = pl.program_id(1)

    @pl.when(j == 0)
    def _():
        xb_ref[...] = x_ref[...].astype(BF16)
        acc_ref[...] = jnp.zeros_like(acc_ref)

    hdn = jnp.maximum(_dot(xb_ref[...], w1_ref[...]) + b1_ref[...], 0.0)
    acc_ref[...] += _dot(jnp.square(hdn).astype(BF16), w2_ref[...])

    @pl.when(j == pl.num_programs(1) - 1)
    def _():
        y = alpha * x_ref[...] + (acc_ref[...] + b2_ref[...])
        o_ref[...] = _layer_norm_rows(y, g_ref[...], beta_ref[...])


def _ffn(alpha, x2d, w1, b1, w2, b2, g, beta, tm=1024, tf=1024):
    t = x2d.shape[0]
    return pl.pallas_call(
        functools.partial(_ffn_kernel, alpha),
        grid=(t // tm, D_FF // tf),
        in_specs=[pl.BlockSpec((tm, D_MODEL), lambda i, j: (i, 0)),
                  pl.BlockSpec((D_MODEL, tf), lambda i, j: (0, j)),
                  pl.BlockSpec((1, tf), lambda i, j: (0, j)),
                  pl.BlockSpec((tf, D_MODEL), lambda i, j: (j, 0)),
                  pl.BlockSpec((1, D_MODEL), lambda i, j: (0, 0)),
                  pl.BlockSpec((1, D_MODEL), lambda i, j: (0, 0)),
                  pl.BlockSpec((1, D_MODEL), lambda i, j: (0, 0))],
        out_specs=pl.BlockSpec((tm, D_MODEL), lambda i, j: (i, 0)),
        out_shape=jax.ShapeDtypeStruct((t, D_MODEL), F32),
        scratch_shapes=[pltpu.VMEM((tm, D_MODEL), BF16), pltpu.VMEM((tm, D_MODEL), F32)],
        compiler_params=pltpu.CompilerParams(dimension_semantics=("arbitrary", "arbitrary"),
                                             vmem_limit_bytes=VMEM_LIMIT_BYTES),
        name="ffn_ln",
    )(x2d, w1, b1, w2, b2, g, beta)


def _bias_of_distance(rel_bias_heads, dist):
    onehot = np.eye(NUM_BUCKETS, dtype=np.float32)[_t5_bucket_np(np.asarray(dist))]
    return jnp.dot(rel_bias_heads.T, jnp.asarray(onehot.T), precision=lax.Precision.HIGHEST)


def _shifted_rows(v, n_rows, n_cols, step):
    period = v.shape[-1]
    assert n_cols <= period - step
    flat = jnp.tile(v, (1,) * (v.ndim - 1) + (n_rows,))[..., :n_rows * (period - step)]
    return flat.reshape(v.shape[:-1] + (n_rows, period - step))[..., :n_cols]


def _wrapped(period):
    idx = np.arange(period)
    return np.where(idx < period // 2, idx, idx - period)


def _dilated_bias(rel_bias):
    x = _wrapped(4 * BLK)
    tabs = [_shifted_rows(_bias_of_distance(rel_bias[:, 0:HEADS], (BLK - x) * dil), BLK, 2 * BLK, 1)
            for _, dil in DIL_PATTERNS]
    tabs = jnp.stack(tabs).reshape(len(DIL_PATTERNS), HEADS // 2, 2, BLK, 2 * BLK)
    return jnp.transpose(tabs, (0, 1, 3, 2, 4)).reshape(len(DIL_PATTERNS), HEADS // 2, BLK, 4 * BLK)


def _toeplitz_bias_t(rel_bias_heads, blk=BLK):
    assert (_t5_bucket_np(np.arange(blk + 1, 64 * blk)) == NUM_BUCKETS - 1).all()
    x = _wrapped(2 * blk)
    tabs = [_shifted_rows(_bias_of_distance(rel_bias_heads, blk * delta + x), blk, blk, 1)
            for delta in range(3)]
    return jnp.transpose(jnp.stack(tabs), (0, 2, 1, 3)).reshape(3, blk, HEADS * blk)


def _compressed_bias_t(rel_bias_heads, seq):
    nq = seq // QRY
    x = _wrapped(2 * seq + BLK)
    v = _bias_of_distance(rel_bias_heads, x - (NSA_CMP_LEN - 1))
    tab = _shifted_rows(v, BLK, seq, NSA_CMP_STRIDE)
    tab = tab.reshape(HEADS, BLK, nq, QRY)
    return jnp.transpose(tab, (2, 1, 0, 3)).reshape(nq, BLK, HEADS * QRY)


def _nsa_constants(seq):
    n_cmp = (seq - NSA_CMP_LEN) // NSA_CMP_STRIDE + 1
    n_sel = seq // NSA_SEL_LEN
    cs = np.arange(n_cmp)[:, None] * NSA_CMP_STRIDE
    ss = np.arange(n_sel)[None, :] * NSA_SEL_LEN
    ov = np.clip(np.minimum(cs + NSA_CMP_LEN, ss + NSA_SEL_LEN) - np.maximum(cs, ss), 0, None) / NSA_CMP_LEN
    ovt = np.zeros((BLK, BLK), np.float32)
    ovt[:n_sel, :n_cmp] = ov.T
    expand = np.zeros((seq, BLK), np.float32)
    expand[np.arange(seq), np.arange(seq) // NSA_SEL_LEN] = 1.0
    return jnp.asarray(ovt), jnp.asarray(expand, BF16)


def _nsa_weights(cmp_pos, cmp_w1, cmp_w2):
    hd, hid = HEAD_DIM, NSA_CMP_HIDDEN
    w1 = cmp_w1.reshape(2, NSA_CMP_LEN, hd, hid)
    zeros = jnp.zeros((NSA_CMP_LEN, hd, hid), F32)
    w1 = jnp.concatenate([jnp.concatenate([w1[0], zeros], axis=-1),
                          jnp.concatenate([zeros, w1[1]], axis=-1)], axis=1)
    pos = jnp.concatenate([cmp_pos[0], cmp_pos[1]], axis=-1)
    z2 = jnp.zeros((hid, hd), F32)
    w2 = jnp.concatenate([jnp.concatenate([cmp_w2[0], z2], axis=-1),
                          jnp.concatenate([z2, cmp_w2[1]], axis=-1)], axis=0)
    return w1.astype(BF16), pos, w2.astype(BF16)


def kernel(x, w_in, b_in, a_conv, a_norm, d_cmp_pos, d_cmp_w1, d_cmp_w2, w_out, b_out, ln1_g, ln1_b,
           w_ff1, b_ff1, w_ff2, b_ff2, ln2_g, ln2_b, rel_bias):
    bsz, seq, _ = x.shape
    depth = w_in.shape[0]
    alpha = (2 * depth) ** 0.25
    bias_dil = _dilated_bias(rel_bias)
    btab_dsa = _toeplitz_bias_t(rel_bias[:, HEADS:2 * HEADS], QRY) * LOG2_E
    btab_nsa = _toeplitz_bias_t(rel_bias[:, 2 * HEADS:3 * HEADS], QRY) * LOG2_E
    bcmp_nsa = _compressed_bias_t(rel_bias[:, 2 * HEADS:3 * HEADS], seq) * LOG2_E
    ovt, expand = _nsa_constants(seq)

    h = x.reshape(bsz * seq, D_MODEL)
    for l in range(depth):
        w_l = _permute_columns(w_in[l]).astype(BF16)
        b_l = _permute_columns(b_in[l])[None, :]
        za, zb, zc, zd, zg = (z.reshape(bsz, seq, -1) for z in _inproj(h, w_l, b_l))
        out_a = _mlstm(za, _mlstm_gate_rows(zg), a_conv[l], a_norm[l][None, :])
        out_b = _dilated(zb, bias_dil)
        out_c = _dsa(zc, btab_dsa)
        nsa_w1, nsa_pos, nsa_w2 = _nsa_weights(d_cmp_pos[l], d_cmp_w1[l], d_cmp_w2[l])
        out_d = _nsa(zd, nsa_w1, nsa_pos, nsa_w2, btab_nsa, bcmp_nsa, ovt, expand)
        mixed = [o.reshape(bsz * seq, GROUP_WIDTH) for o in (out_a, out_b, out_c, out_d)]
        h = _outproj(alpha, h, mixed, w_out[l].astype(BF16), b_out[l][None, :],
                     ln1_g[l][None, :], ln1_b[l][None, :])
        h = _ffn(alpha, h, w_ff1[l].astype(BF16), b_ff1[l][None, :], w_ff2[l].astype(BF16),
                 b_ff2[l][None, :], ln2_g[l][None, :], ln2_b[l][None, :])
    return h.reshape(bsz, seq, D_MODEL)
```

```python
import functools
import math

import numpy as np
import jax
import jax.numpy as jnp
from jax import lax
from jax.experimental import pallas as pl
from jax.experimental.pallas import tpu as pltpu

F32 = jnp.float32
BF16 = jnp.bfloat16
I32 = jnp.int32
I16 = jnp.int16
I16_MIN = -2 ** 15

D_MODEL = 1024
N_MIXERS = 4
HEADS = 4
HEAD_DIM = D_MODEL // (N_MIXERS * HEADS)
GROUP_WIDTH = HEADS * HEAD_DIM
D_FF = 4 * D_MODEL
LN_EPS = 1e-5
NEG = -1e30

M_QK_DIM = HEAD_DIM // 2
M_CHUNK = 64
M_CONV = 4
DIL_PATTERNS = ((128, 1), (512, 4), (2048, 16))
IDX_HEADS = 4
IDX_DIM = 64
DSA_TOPK = 256
NSA_CMP_LEN = 32
NSA_CMP_STRIDE = 16
NSA_SEL_LEN = 64
NSA_TOPN = 16
NSA_WINDOW = 512
NSA_CMP_HIDDEN = 256
NSA_FORCE = 1e9
NUM_BUCKETS = 32
MAX_DISTANCE = 128

LANES = 128
BLK = 128
KEYS = 2 * BLK
QRY = KEYS
VMEM_LIMIT_BYTES = 56 * 1024 * 1024

IN_SPLITS = (
    ('a_q', HEADS * M_QK_DIM), ('a_k', HEADS * M_QK_DIM), ('a_v', GROUP_WIDTH),
    ('a_i', HEADS), ('a_f', HEADS), ('a_o', GROUP_WIDTH),
    ('b_q', GROUP_WIDTH), ('b_k', GROUP_WIDTH), ('b_v', GROUP_WIDTH),
    ('c_q', GROUP_WIDTH), ('c_k', HEAD_DIM), ('c_v', HEAD_DIM),
    ('c_iq', IDX_HEADS * IDX_DIM), ('c_ik', IDX_DIM), ('c_iw', IDX_HEADS),
    ('d_q', GROUP_WIDTH), ('d_kc', HEAD_DIM), ('d_vc', HEAD_DIM),
    ('d_ks', HEAD_DIM), ('d_vs', HEAD_DIM), ('d_kw', HEAD_DIM), ('d_vw', HEAD_DIM),
    ('d_g', 3 * HEADS),
)

GROUP_LAYOUT = (
    (('a_q', 'a_k'), ('a_v',), ('a_o',), (('a_i', HEAD_DIM),), (('a_i', M_QK_DIM),),
     (('a_f', HEAD_DIM),), (('a_f', M_QK_DIM),)),
    (('b_q',), ('b_k',), ('b_v',)),
    (('c_q',), ('c_iq',), ('c_k', 'c_v'), ('c_ik', 'c_iw')),
    (('d_q',), ('d_kc', 'd_vc'), ('d_ks', 'd_vs'), ('d_kw', 'd_vw'), ('d_g',)),
    (('a_i', 'a_f'),),
)


def _round_up(n, m):
    return -(-n // m) * m


def _projection_layout():
    offs, off = {}, 0
    for name, width in IN_SPLITS:
        offs[name] = (off, width)
        off += width
    runs, group_widths, chunk_starts = [], [], []
    for group in GROUP_LAYOUT:
        gwidth = 0
        chunk_starts.append([])
        for chunk in group:
            chunk_starts[-1].append(gwidth)
            cwidth = 0
            for entry in chunk:
                name, rep = entry if isinstance(entry, tuple) else (entry, 1)
                o, w = offs[name]
                runs.append((o, w, rep))
                cwidth += w * rep
            pad = _round_up(cwidth, LANES) - cwidth
            if pad:
                runs.append((-1, pad, 1))
            gwidth += cwidth + pad
        group_widths.append(gwidth)
    return tuple(runs), tuple(group_widths), tuple(tuple(c) for c in chunk_starts)


PROJ_RUNS, GROUP_WIDTHS, CHUNK_STARTS = _projection_layout()
PROJ_WIDTH = int(sum(GROUP_WIDTHS))


def _permute_columns(a):
    parts = []
    for o, w, r in PROJ_RUNS:
        if o < 0:
            parts.append(jnp.zeros(a.shape[:-1] + (w,), a.dtype))
        else:
            parts.append(a[..., o:o + w] if r == 1 else jnp.repeat(a[..., o:o + w], r, axis=-1))
    return jnp.concatenate(parts, axis=-1)


def _t5_bucket_np(dist):
    n = np.maximum(dist, 0)
    max_exact = NUM_BUCKETS // 2
    nf = np.maximum(n, max_exact).astype(np.float32)
    large = max_exact + (np.log(nf / max_exact) / math.log(MAX_DISTANCE / max_exact)
                         * (NUM_BUCKETS - max_exact)).astype(np.int32)
    large = np.minimum(large, NUM_BUCKETS - 1)
    return np.where(n < max_exact, n, large).astype(np.int32)


def _nt_dot(a, b, precision=None):
    return lax.dot_general(a, b, (((1,), (1,)), ((), ())), precision=precision,
                           preferred_element_type=F32)


def _dot(a, b, precision=None):
    return jnp.dot(a, b, precision=precision, preferred_element_type=F32)


def _layer_norm_rows(y, g, b):
    mu = jnp.mean(y, axis=-1, keepdims=True)
    var = jnp.mean(jnp.square(y - mu), axis=-1, keepdims=True)
    return (y - mu) * lax.rsqrt(var + LN_EPS) * g + b


def _sigmoid(x):
    return 1.0 / (1.0 + jnp.exp(-x))


def _log_sigmoid(x):
    return -(jnp.maximum(-x, 0.0) + jnp.log1p(jnp.exp(-jnp.abs(x))))


def _sortable_key(x):
    bits = pltpu.bitcast(x, I32)
    return bits ^ ((bits >> 31) & jnp.int32(0x7FFFFFFF))


def _inproj_kernel(x_ref, w_ref, b_ref, *out_refs):
    xb = x_ref[...].astype(BF16)
    off = 0
    for o_ref, width in zip(out_refs, GROUP_WIDTHS):
        o_ref[...] = _dot(xb, w_ref[:, off:off + width]) + b_ref[:, off:off + width]
        off += width


def _inproj(x2d, w, b, tm=512):
    t = x2d.shape[0]
    return pl.pallas_call(
        _inproj_kernel,
        grid=(t // tm,),
        in_specs=[pl.BlockSpec((tm, D_MODEL), lambda i: (i, 0)),
                  pl.BlockSpec((D_MODEL, PROJ_WIDTH), lambda i: (0, 0)),
                  pl.BlockSpec((1, PROJ_WIDTH), lambda i: (0, 0))],
        out_specs=[pl.BlockSpec((tm, gw), lambda i: (i, 0)) for gw in GROUP_WIDTHS],
        out_shape=[jax.ShapeDtypeStruct((t, gw), F32) for gw in GROUP_WIDTHS],
        compiler_params=pltpu.CompilerParams(dimension_semantics=("arbitrary",),
                                             vmem_limit_bytes=VMEM_LIMIT_BYTES),
        name="inproj",
    )(x2d, w, b)


def _split_terms(x, n):
    terms, rest = [], x
    for _ in range(n):
        terms.append(rest.astype(BF16))
        rest = rest - terms[-1].astype(F32)
    return terms


def _iota(shape, dim):
    return lax.broadcasted_iota(I32, shape, dim)


def _mlstm_kernel(z_ref, gt_ref, cw_ref, ng_ref, o_ref, xpad_ref):
    seq = z_ref.shape[1]
    L, DK, DV, H = M_CHUNK, M_QK_DIM, HEAD_DIM, HEADS
    assert L == DV
    wq, wv = H * DK, H * DV
    lg_dk, lg_dv = int(math.log2(DK)), int(math.log2(DV))
    _, c_v, c_o, c_i64, c_i32, c_f64, c_f32 = CHUNK_STARTS[0]
    assert c_f32 == c_f64 + wv
    xpad_ref[0:8, :] = jnp.zeros((8, 2 * wq), F32)
    xpad_ref[8:, :] = z_ref[0, :, 0:2 * wq]

    one_if = lambda cond: jnp.where(cond, 1.0, 0.0).astype(BF16)
    tri_l = one_if(_iota((L, L), 0) >= _iota((L, L), 1))
    trow = _iota((L, wv), 0)
    tri_heads = trow >= (_iota((L, wv), 1) & (L - 1))
    r_vv, c_vv = _iota((wv, wv), 0), _iota((wv, wv), 1)
    same_head = (r_vv >> lg_dv) == (c_vv >> lg_dv)
    ones_bd = one_if(same_head)
    mean_bd = jnp.where(same_head, 1.0 / DV, 0.0).astype(BF16)
    tri_u_bd = one_if(same_head & ((r_vv & (L - 1)) <= (c_vv & (L - 1))))
    state_mask = (_iota((wq, wv), 0) >> lg_dk) == (_iota((wq, wv), 1) >> lg_dv)
    eye_q = one_if(_iota((wq, wq), 0) == _iota((wq, wq), 1))
    head_of_qlane = _iota((L, wq), 1) >> lg_dk
    head_of_vlane = _iota((L, wv), 1) >> lg_dv
    row8 = _iota((8, wv), 0)
    cw = cw_ref[...]
    ng = ng_ref[...]

    def head_mean(x):
        hi_lo = _split_terms(x, 2)
        r = _dot(jnp.concatenate(hi_lo, axis=0), mean_bd)
        return r[0:L] + r[L:2 * L]

    def chunk(c, carry):
        cbd, nbd, m64, m32 = carry
        s0 = pl.multiple_of(c * L, L)
        rows = pl.ds(s0, L)
        xw = xpad_ref[pl.ds(s0, L + 8), :]
        y = sum(cw[j:j + 1, :] * xw[5 + j:5 + j + L, :] for j in range(M_CONV))
        qk = y * _sigmoid(y)
        q = qk[:, 0:wq]
        k = qk[:, wq:] * (DK ** -0.5)
        qb = q.astype(BF16)
        v = z_ref[0, rows, c_v:c_v + wv]
        i64 = z_ref[0, rows, c_i64:c_i64 + wv]
        i32 = z_ref[0, rows, c_i32:c_i32 + wq]
        gr = gt_ref[0, c]

        flog = _log_sigmoid(z_ref[0, rows, c_f64:c_f64 + wv + wq])
        bsum = _dot(tri_l, jnp.concatenate(_split_terms(flog, 3), axis=1))
        w3 = wv + wq
        ball = bsum[:, 0:w3] + bsum[:, w3:2 * w3] + bsum[:, 2 * w3:3 * w3]
        b64, b32 = ball[:, 0:wv], ball[:, wv:w3]
        fterms = [t.astype(F32) for t in _split_terms(_log_sigmoid(gr[1:2, :]), 3)]
        frows = jnp.where(row8 == 0, fterms[0], jnp.where(row8 == 1, fterms[1], jnp.where(row8 == 2, fterms[2], 0.0)))
        bparts = _dot(frows.astype(BF16), tri_u_bd)
        brow = bparts[0:1, :] + bparts[1:2, :] + bparts[2:3, :]

        dall = jnp.where(tri_heads, b64 - brow + gr[0:1, :], NEG)
        cm = i64 - b64
        for sh in (1, 2, 4, 8, 16, 32):
            cm = jnp.where(trow >= sh, jnp.maximum(cm, pltpu.roll(cm, sh, 0)), cm)
        inter = b64 + m64
        m_t = jnp.maximum(inter, b64 + cm)
        kbd = jnp.concatenate([jnp.where(head_of_qlane == h, k, 0.0) for h in range(H)], axis=0).astype(BF16)
        sc = _nt_dot(qb, kbd) * jnp.exp(dall - m_t)
        wi = jnp.exp(inter - m_t)
        vb = v.astype(BF16)
        vbd = jnp.concatenate([jnp.where(head_of_vlane == h, v, 0.0) for h in range(H)], axis=0).astype(BF16)
        pv = _dot(sc.astype(BF16), jnp.concatenate([vbd, ones_bd], axis=1))
        qst = _dot(qb, jnp.concatenate([cbd, nbd], axis=1).astype(BF16))
        num = pv[:, 0:wv] + wi * qst[:, 0:wv]
        den = pv[:, wv:] + wi * qst[:, wv:]
        hh = num / jnp.maximum(jnp.abs(den), jnp.exp(-m_t))

        og = _sigmoid(z_ref[0, rows, c_o:c_o + wv]) * hh
        dev = og - head_mean(og)
        o_ref[0, rows, :] = dev * lax.rsqrt(head_mean(dev * dev) + LN_EPS) * ng

        bl64, bl32 = b64[L - 1:L, :], b32[L - 1:L, :]
        m64_new = jnp.maximum(bl64 + m64, jnp.max(bl64 - b64 + i64, axis=0, keepdims=True))
        g32 = bl32 - b32 + i32
        m32_new = jnp.maximum(bl32 + m32, jnp.max(g32, axis=0, keepdims=True))
        kw = k * jnp.exp(g32 - m32_new)
        wc = jnp.exp(bl64 + m64 - m64_new)
        kwt = _nt_dot(eye_q, kw.astype(BF16)).astype(BF16)
        upd = _dot(kwt, jnp.concatenate([vb, jnp.ones((L, wv), BF16)], axis=1))
        cbd = wc * cbd + jnp.where(state_mask, upd[:, 0:wv], 0.0)
        nbd = wc * nbd + jnp.where(state_mask, upd[:, wv:], 0.0)
        return cbd, nbd, m64_new, m32_new

    init = (jnp.zeros((wq, wv), F32), jnp.zeros((wq, wv), F32), jnp.zeros((1, wv), F32), jnp.zeros((1, wq), F32))
    lax.fori_loop(0, seq // L, chunk, init, unroll=4)


def _mlstm(za, gates_t, conv_w, norm_g):
    b, s, wa = za.shape
    nc = s // M_CHUNK
    return pl.pallas_call(
        _mlstm_kernel,
        grid=(b,),
        in_specs=[pl.BlockSpec((1, s, wa), lambda i: (i, 0, 0)),
                  pl.BlockSpec((1, nc) + gates_t.shape[2:], lambda i: (i, 0, 0, 0)),
                  pl.BlockSpec((M_CONV, 2 * HEADS * M_QK_DIM), lambda i: (0, 0)),
                  pl.BlockSpec((1, GROUP_WIDTH), lambda i: (0, 0))],
        out_specs=pl.BlockSpec((1, s, GROUP_WIDTH), lambda i: (i, 0, 0)),
        out_shape=jax.ShapeDtypeStruct((b, s, GROUP_WIDTH), F32),
        scratch_shapes=[pltpu.VMEM((s + 8, 2 * HEADS * M_QK_DIM), F32)],
        compiler_params=pltpu.CompilerParams(dimension_semantics=("arbitrary",),
                                             vmem_limit_bytes=VMEM_LIMIT_BYTES),
        name="mlstm",
    )(za, gates_t, conv_w, norm_g)


def _mlstm_gate_rows(zg):
    b, s, _ = zg.shape
    gates = zg[:, :, 0:2 * HEADS].reshape(b, s // M_CHUNK, M_CHUNK, 2, HEADS)
    return jnp.transpose(gates, (0, 1, 3, 4, 2)).reshape(b, s // M_CHUNK, 2, HEADS * M_CHUNK)


def _dilated_kernel(q0_ref, q1_ref, k0_ref, k1_ref, v0_ref, v1_ref, bias_ref, o_ref, acc_ref, mx_ref, den_ref):
    seq = q0_ref.shape[1]
    W = BLK
    hd = HEAD_DIM
    npair = HEADS // 2
    scale = hd ** -0.5
    assert math.log2(scale).is_integer()
    nk = 2 * W
    qi = _iota((W, 2 * nk), 0)
    ki = _iota((W, 2 * nk), 1) & (nk - 1)
    j = W + qi - ki
    band = (j >= 0) & (j <= W)
    head_of_lane = _iota((nk, 2 * hd), 1) >> int(math.log2(hd))
    ones_bd = jnp.concatenate([jnp.where(head_of_lane == hh, 1.0, 0.0) for hh in range(2)], axis=0).astype(BF16)
    q_refs, k_refs, v_refs = (q0_ref, q1_ref), (k0_ref, k1_ref), (v0_ref, v1_ref)

    def block_diag(x):
        return jnp.concatenate([jnp.where(head_of_lane == hh, x, 0.0) for hh in range(2)], axis=0).astype(BF16)

    for br, (window, dil) in enumerate(DIL_PATTERNS):
        assert window // dil == W
        nb = (seq // dil) // W

        def piece(idx, _, br=br, dil=dil, nb=nb):
            n = idx % nb
            if dil == 1:
                rows_q = pl.ds(pl.multiple_of(W * n, W), W)
                rows_p = pl.ds(pl.multiple_of(W * jnp.maximum(n - 1, 0), W), W)
            else:
                r = idx // nb
                rows_q = pl.ds(r + dil * W * n, W, stride=dil)
                rows_p = pl.ds(r + dil * W * jnp.maximum(n - 1, 0), W, stride=dil)
            mask = band & (ki >= jnp.where(n > 0, 0, W))
            for pr in range(npair):
                q = (q_refs[pr][0, rows_q, :] * scale).astype(BF16)
                k2 = jnp.concatenate([k_refs[pr][0, rows_p, :], k_refs[pr][0, rows_q, :]], axis=0)
                v2 = jnp.concatenate([v_refs[pr][0, rows_p, :], v_refs[pr][0, rows_q, :]], axis=0)
                lg = jnp.where(mask, _nt_dot(q, block_diag(k2)) + bias_ref[br, pr], NEG)
                ps, mxs = [], []
                for hh in range(2):
                    sl = slice(nk * hh, nk * (hh + 1))
                    m = jnp.max(lg[:, sl], axis=-1, keepdims=True)
                    ps.append(jnp.where(mask[:, sl], jnp.exp(lg[:, sl] - m), 0.0))
                    mxs.append(jnp.broadcast_to(m, (W, hd)))
                p = jnp.concatenate(ps, axis=-1).astype(BF16)
                pv = _dot(p, jnp.concatenate([block_diag(v2), ones_bd], axis=1))
                acc_ref[br, pr, rows_q, :] = pv[:, 0:2 * hd]
                den_ref[br, pr, rows_q, :] = pv[:, 2 * hd:]
                mx_ref[br, pr, rows_q, :] = jnp.concatenate(mxs, axis=-1)
            return 0

        lax.fori_loop(0, dil * nb, piece, 0, unroll=8)

    def combine(i, _):
        rows = pl.ds(pl.multiple_of(i * W, W), W)
        outs = []
        for pr in range(npair):
            ms = [mx_ref[b, pr, rows, :] for b in range(len(DIL_PATTERNS))]
            top = functools.reduce(jnp.maximum, ms)
            es = [jnp.exp(m - top) for m in ms]
            num = sum(e * acc_ref[b, pr, rows, :] for b, e in enumerate(es))
            den = sum(e * jnp.maximum(den_ref[b, pr, rows, :], 1e-30) for b, e in enumerate(es))
            outs.append(num / den)
        o_ref[0, rows, :] = jnp.concatenate(outs, axis=-1)
        return 0

    lax.fori_loop(0, seq // W, combine, 0)


def _dilated(zb, bias):
    b, s, wb = zb.shape
    nbr = len(DIL_PATTERNS)
    pair_spec = lambda c: pl.BlockSpec((1, s, LANES), lambda i: (i, 0, c))
    return pl.pallas_call(
        _dilated_kernel,
        grid=(b,),
        in_specs=[pair_spec(c) for c in range(wb // LANES)]
                 + [pl.BlockSpec(bias.shape, lambda i: (0, 0, 0, 0))],
        out_specs=pl.BlockSpec((1, s, GROUP_WIDTH), lambda i: (i, 0, 0)),
        out_shape=jax.ShapeDtypeStruct((b, s, GROUP_WIDTH), F32),
        scratch_shapes=[pltpu.VMEM((nbr, HEADS // 2, s, LANES), F32) for _ in range(3)],
        compiler_params=pltpu.CompilerParams(dimension_semantics=("arbitrary",),
                                             vmem_limit_bytes=VMEM_LIMIT_BYTES),
        name="dilated",
    )(*([zb] * (wb // LANES)), bias)


def _head_columns_t(q):
    qt = q.T
    return jnp.concatenate([qt[HEAD_DIM * h:HEAD_DIM * (h + 1), :] for h in range(HEADS)], axis=1)


LOG2_E = math.log2(math.e)


def _scaled_query_columns_t(q):
    return (_head_columns_t(q) * (HEAD_DIM ** -0.5 * LOG2_E)).astype(BF16)


def _unstack_heads_t(per_head_t):
    halves = []
    for h in range(0, HEADS, 2):
        halves.append(jnp.concatenate([per_head_t[h], per_head_t[h + 1]], axis=0).T)
    return jnp.concatenate(halves, axis=-1)


VT_ROWS = HEAD_DIM + 16


def _online_step(lg, mask, vt1, m, acc):
    lg = jnp.where(mask, lg, NEG)
    m_new = jnp.maximum(m, jnp.max(lg, axis=0, keepdims=True))
    p = jnp.where(mask, jnp.exp2(lg - m_new), 0.0)
    acc = acc * jnp.exp2(m - m_new) + _dot(vt1, p.astype(BF16))
    return m_new, acc


def _online_result(acc):
    return acc[0:HEAD_DIM, :] / jnp.maximum(acc[HEAD_DIM:HEAD_DIM + 1, :], 1e-30)


def _values_with_ones_t(z_rows):
    vt = z_rows.T[HEAD_DIM:2 * HEAD_DIM, :]
    return jnp.concatenate([vt, jnp.ones((VT_ROWS - HEAD_DIM, vt.shape[1]), F32)], axis=0).astype(BF16)


def _online_init(nq=BLK):
    return (tuple(jnp.full((1, nq), NEG, F32) for _ in range(HEADS)),
            tuple(jnp.zeros((VT_ROWS, nq), F32) for _ in range(HEADS)))


def _dsa_kernel(z_ref, btab_ref, o_ref, vt_ref, key_ref, hi_ref, lo_ref, lom_ref):
    i = pl.program_id(1)
    seq = z_ref.shape[1]
    nkb = seq // BLK
    nq = QRY
    hd = HEAD_DIM
    topk = min(DSA_TOPK, seq // 4)
    t0 = pl.multiple_of(i * nq, nq)

    assert nq == KEYS
    npair = i + 1
    c_q, c_iq, c_kv, c_ikw = CHUNK_STARTS[2]

    @pl.when(i == 0)
    def _():
        for kb in range(nkb):
            vt_ref[kb // 2, :, (kb % 2) * BLK:(kb % 2 + 1) * BLK] = _values_with_ones_t(
                z_ref[0, kb * BLK:(kb + 1) * BLK, c_kv:c_kv + 2 * hd])

    zq = z_ref[0, pl.ds(t0, nq), :]
    cq = zq[:, c_q:c_q + HEADS * hd]
    ciq = _head_columns_t(zq[:, c_iq:c_iq + IDX_HEADS * IDX_DIM]).astype(BF16)
    iw = (zq[:, c_ikw:c_ikw + LANES].T[IDX_DIM:IDX_DIM + IDX_HEADS, :]
          * ((IDX_HEADS * IDX_DIM) ** -0.5))
    s_loc = lax.broadcasted_iota(I32, (KEYS, nq), 0)
    t_glob = t0 + lax.broadcasted_iota(I32, (KEYS, nq), 1)

    def score_block(kp, _):
        r0 = pl.multiple_of(kp * KEYS, KEYS)
        ik = z_ref[0, pl.ds(r0, KEYS), c_ikw:c_ikw + IDX_DIM].astype(BF16)
        rel = _dot(ik, ciq)
        sc = jnp.zeros((KEYS, nq), F32)
        for h in range(IDX_HEADS):
            sc = sc + jnp.maximum(rel[:, nq * h:nq * (h + 1)], 0.0) * iw[h:h + 1, :]
        sc = jnp.where(r0 + s_loc <= t_glob, sc, NEG)
        key = _sortable_key(sc)
        key_ref[kp] = key
        hi_ref[kp] = (key >> 16).astype(I16)
        lo_ref[kp] = ((key & 0xFFFF) - 2 ** 15).astype(I16)
        return 0

    lax.fori_loop(0, npair, score_block, 0)

    def count(ref, pred, pairs):
        dt = ref.dtype
        rows = 8 * 4 // dt.itemsize
        def body(kp, acc):
            hit = jnp.where(pred(kp, ref[kp]), jnp.ones((), dt), jnp.zeros((), dt))
            hit = hit.reshape(KEYS // rows, rows, nq)
            parts = [hit[n] for n in range(KEYS // rows)]
            while len(parts) > 1:
                parts = [a + b for a, b in zip(parts[0::2], parts[1::2])]
            return acc + parts[0]
        acc = jnp.zeros((rows, nq), dt)
        for kp in range(pairs):
            acc = body(kp, acc)
        return jnp.sum(acc.astype(I32), axis=0, keepdims=True)

    def threshold(pairs):
        def half_search(ref, k):
            def bit(it, thr):
                cand = thr + lax.shift_left(jnp.int32(1), 15 - it)
                c = count(ref, lambda kp, half: half >= cand.astype(I16), pairs)
                return jnp.where(c >= k, cand, thr)
            return lax.fori_loop(0, 16, bit, jnp.full((1, nq), I16_MIN, I32))

        def run(_):
            thr_hi = half_search(hi_ref, topk)
            thr_hi16 = thr_hi.astype(I16)
            above_hi = count(hi_ref, lambda kp, half: half > thr_hi16, pairs)
            for kp in range(pairs):
                lom_ref[kp] = jnp.where(hi_ref[kp] == thr_hi16, lo_ref[kp], jnp.int16(I16_MIN))
            thr_lo = half_search(lom_ref, topk - above_hi)
            thr = (thr_hi << 16) | (thr_lo + 2 ** 15)
            thr_lo16 = thr_lo.astype(I16)
            n_gt = above_hi + count(lom_ref, lambda kp, half: half > thr_lo16, pairs)
            need = (topk - n_gt).astype(F32)
            earlier = jnp.zeros((1, nq), F32)
            for kp in range(pairs):
                tie = key_ref[kp] == thr
                tie01 = jnp.where(tie, 1.0, 0.0)
                rank = _dot(below, tie01.astype(BF16)) + earlier
                lom_ref[kp] = jnp.where(tie & (rank < need), 1, 0).astype(I16)
                earlier = earlier + jnp.sum(tie01, axis=0, keepdims=True)
            return thr
        return run

    below = jnp.where(_iota((KEYS, KEYS), 1) < _iota((KEYS, KEYS), 0), 1.0, 0.0).astype(BF16)
    thr = lax.switch(npair - 1, [threshold(p) for p in range(1, seq // KEYS + 1)], 0)

    qs = _scaled_query_columns_t(cq)

    def attend(kp, carry):
        ms, accs = carry
        r0 = pl.multiple_of(kp * KEYS, KEYS)
        kblk = z_ref[0, pl.ds(r0, KEYS), c_kv:c_kv + hd].astype(BF16)
        lg = _dot(kblk, qs) + btab_ref[jnp.minimum(i - kp, 2)]
        key = key_ref[kp]
        s_glob = r0 + s_loc
        mask = ((key > thr) | (lom_ref[kp].astype(I32) != 0)) & (s_glob <= t_glob)
        vt1 = vt_ref[kp]
        out = [_online_step(lg[:, nq * h:nq * (h + 1)], mask, vt1, ms[h], accs[h]) for h in range(HEADS)]
        return tuple(o[0] for o in out), tuple(o[1] for o in out)

    ms, accs = lax.fori_loop(0, npair, attend, _online_init(nq))
    o_ref[0] = _unstack_heads_t([_online_result(accs[h]) for h in range(HEADS)])


def _dsa(zc, btab):
    b, s, wc = zc.shape
    return pl.pallas_call(
        _dsa_kernel,
        grid=(b, s // QRY),
        in_specs=[pl.BlockSpec((1, s, wc), lambda bi, i: (bi, 0, 0)),
                  pl.BlockSpec(btab.shape, lambda bi, i: (0, 0, 0))],
        out_specs=pl.BlockSpec((1, QRY, GROUP_WIDTH), lambda bi, i: (bi, i, 0)),
        out_shape=jax.ShapeDtypeStruct((b, s, GROUP_WIDTH), F32),
        scratch_shapes=[pltpu.VMEM((s // KEYS, VT_ROWS, KEYS), BF16), pltpu.VMEM((s // KEYS, KEYS, QRY), I32)]
                       + [pltpu.VMEM((s // KEYS, KEYS, QRY), I16)] * 3,
        compiler_params=pltpu.CompilerParams(dimension_semantics=("arbitrary", "arbitrary"),
                                             vmem_limit_bytes=VMEM_LIMIT_BYTES),
        name="dsa",
    )(zc, btab)


def _nsa_kernel(z_ref, zc_ref, w1_ref, pos_ref, w2_ref, bsel_ref, bcmp_ref, ovt_ref, exp_ref, o_ref,
                vst_ref, vwt_ref, cmp_ref, cmpt_ref):
    i = pl.program_id(1)
    seq = z_ref.shape[1]
    nkb = seq // BLK
    hd = HEAD_DIM
    c_q, _, c_s, c_w, c_g = CHUNK_STARTS[3]
    n_cmp = (seq - NSA_CMP_LEN) // NSA_CMP_STRIDE + 1
    n_sel = seq // NSA_SEL_LEN
    topn = min(NSA_TOPN, n_sel)
    half = NSA_CMP_LEN // 2
    assert half == NSA_CMP_STRIDE and n_cmp + 1 == seq // NSA_CMP_STRIDE == BLK and n_sel <= BLK
    nq = QRY
    assert nq == KEYS
    t0 = pl.multiple_of(i * nq, nq)
    hi = lax.Precision.HIGHEST

    @pl.when(i == 0)
    def _():
        for kb in range(nkb):
            rows = slice(kb * BLK, (kb + 1) * BLK)
            cols = slice((kb % 2) * BLK, (kb % 2 + 1) * BLK)
            vst_ref[kb // 2, :, cols] = _values_with_ones_t(z_ref[0, rows, c_s:c_s + 2 * hd])
            vwt_ref[kb // 2, :, cols] = _values_with_ones_t(z_ref[0, rows, c_w:c_w + 2 * hd])
        first = jnp.zeros((BLK, 2 * NSA_CMP_HIDDEN), F32)
        second = jnp.zeros((BLK, 2 * NSA_CMP_HIDDEN), F32)
        for j in range(half):
            xj = zc_ref[0, pl.ds(j, BLK, stride=NSA_CMP_STRIDE), :]
            first = first + _dot((xj + pos_ref[j:j + 1, :]).astype(BF16), w1_ref[j])
            second = second + _dot((xj + pos_ref[half + j:half + j + 1, :]).astype(BF16), w1_ref[half + j])
        hid = first + pltpu.roll(second, BLK - 1, 0)
        hid = hid * _sigmoid(hid)
        cmp = _dot(hid.astype(BF16), w2_ref[...])
        cmp_ref[...] = cmp
        cmpt_ref[...] = cmp.T

    zq = z_ref[0, pl.ds(t0, nq), :]
    qs = _scaled_query_columns_t(zq[:, c_q:c_q + HEADS * hd])
    gates = _sigmoid(zq[:, c_g:c_g + LANES].T[0:16, :])
    row = lax.broadcasted_iota(I32, (BLK, nq), 0)
    t_glob = t0 + lax.broadcasted_iota(I32, (BLK, nq), 1)

    kcmp = cmp_ref[:, 0:hd].astype(BF16)
    vcmpt = cmpt_ref[hd:2 * hd, :].astype(BF16)
    lgc = _dot(kcmp, qs) + bcmp_ref[0]
    mask_c = (t_glob - (row * NSA_CMP_STRIDE + NSA_CMP_LEN - 1) >= 0) & (row < n_cmp)
    o_cmp, psum = [], jnp.zeros((BLK, nq), F32)
    for h in range(HEADS):
        lg = jnp.where(mask_c, lgc[:, nq * h:nq * (h + 1)], NEG)
        m = jnp.max(lg, axis=0, keepdims=True)
        p = jnp.where(mask_c, jnp.exp2(lg - m), 0.0)
        p = p / jnp.maximum(jnp.sum(p, axis=0, keepdims=True), 1e-30)
        o_cmp.append(_dot(vcmpt, p.astype(BF16)))
        psum = psum + p

    imp = _dot(ovt_ref[...], psum, precision=hi)
    cur = t_glob >> int(math.log2(NSA_SEL_LEN))
    forced = (row == 0) | (row == cur) | (row == cur - 1)
    imp = jnp.where(forced, NSA_FORCE, imp)
    imp = jnp.where(row * NSA_SEL_LEN <= t_glob, imp, NEG)
    imp = imp[0:n_sel, :]
    jrow = lax.broadcasted_iota(I32, (n_sel, nq), 0)
    rank = jnp.zeros((n_sel, nq), I32)
    for jp in range(n_sel):
        other = imp[jp:jp + 1, :]
        rank = rank + ((other > imp) | ((other == imp) & (jp < jrow))).astype(I32)
    chosen = jnp.where(rank < topn, 1.0, 0.0)
    chosen = jnp.concatenate([chosen, jnp.zeros((BLK - n_sel, nq), F32)], axis=0).astype(BF16)

    s_loc = lax.broadcasted_iota(I32, (KEYS, nq), 0)
    t_keys = t0 + lax.broadcasted_iota(I32, (KEYS, nq), 1)

    def attend(vt_ref, lanes, mask_fn):
        def body(kp, carry):
            ms, accs = carry
            r0 = pl.multiple_of(kp * KEYS, KEYS)
            kblk = z_ref[0, pl.ds(r0, KEYS), lanes].astype(BF16)
            lg = _dot(kblk, qs) + bsel_ref[jnp.minimum(i - kp, 2)]
            mask = mask_fn(r0)
            vt1 = vt_ref[kp]
            out = [_online_step(lg[:, nq * h:nq * (h + 1)], mask, vt1, ms[h], accs[h]) for h in range(HEADS)]
            return tuple(o[0] for o in out), tuple(o[1] for o in out)
        return body

    def mask_sel(r0):
        picked = _dot(exp_ref[pl.ds(r0, KEYS), :], chosen) > 0.5
        return picked & (r0 + s_loc <= t_keys)

    def mask_win(r0):
        dist = t_keys - (r0 + s_loc)
        return (dist >= 0) & (dist < NSA_WINDOW)

    sel_body = attend(vst_ref, slice(c_s, c_s + hd), mask_sel)
    win_body = attend(vwt_ref, slice(c_w, c_w + hd), mask_win)
    first_w = jnp.maximum(i - NSA_WINDOW // KEYS, 0)
    sel_carry = lax.fori_loop(0, first_w, sel_body, _online_init(nq))
    (_, acc_s), (_, acc_w) = lax.fori_loop(
        first_w, i + 1, lambda kp, c: (sel_body(kp, c[0]), win_body(kp, c[1])), (sel_carry, _online_init(nq)))

    outs = []
    for h in range(HEADS):
        outs.append(gates[3 * h:3 * h + 1, :] * o_cmp[h] + gates[3 * h + 1:3 * h + 2, :] * _online_result(acc_s[h])
                    + gates[3 * h + 2:3 * h + 3, :] * _online_result(acc_w[h]))
    o_ref[0] = _unstack_heads_t(outs)


def _nsa(zd, w1, pos, w2, bsel, bcmp, ovt, expand):
    b, s, wd = zd.shape
    full = lambda a: pl.BlockSpec(a.shape, lambda bi, i: (0,) * a.ndim)
    return pl.pallas_call(
        _nsa_kernel,
        grid=(b, s // QRY),
        in_specs=[pl.BlockSpec((1, s, wd), lambda bi, i: (bi, 0, 0)),
                  pl.BlockSpec((1, s, LANES), lambda bi, i: (bi, 0, CHUNK_STARTS[3][1] // LANES)),
                  full(w1), full(pos), full(w2), full(bsel),
                  pl.BlockSpec((1, BLK, HEADS * QRY), lambda bi, i: (i, 0, 0)),
                  full(ovt), full(expand)],
        out_specs=pl.BlockSpec((1, QRY, GROUP_WIDTH), lambda bi, i: (bi, i, 0)),
        out_shape=jax.ShapeDtypeStruct((b, s, GROUP_WIDTH), F32),
        scratch_shapes=[pltpu.VMEM((s // KEYS, VT_ROWS, KEYS), BF16), pltpu.VMEM((s // KEYS, VT_ROWS, KEYS), BF16),
                        pltpu.VMEM((BLK, BLK), F32), pltpu.VMEM((BLK, BLK), F32)],
        compiler_params=pltpu.CompilerParams(dimension_semantics=("arbitrary", "arbitrary"),
                                             vmem_limit_bytes=VMEM_LIMIT_BYTES),
        name="nsa",
    )(zd, zd, w1, pos, w2, bsel, bcmp, ovt, expand)


def _outproj_kernel(alpha, x_ref, a_ref, b_ref, c_ref, d_ref, w_ref, bo_ref, g_ref, beta_ref, o_ref):
    acc = bo_ref[...] + _dot(a_ref[...].astype(BF16), w_ref[0:GROUP_WIDTH, :])
    for n, m_ref in enumerate((b_ref, c_ref, d_ref), start=1):
        acc = acc + _dot(m_ref[...].astype(BF16), w_ref[n * GROUP_WIDTH:(n + 1) * GROUP_WIDTH, :])
    o_ref[...] = _layer_norm_rows(alpha * x_ref[...] + acc, g_ref[...], beta_ref[...])


def _outproj(alpha, x2d, mixed, w, bo, g, beta, tm=512):
    t = x2d.shape[0]
    row_spec = lambda width: pl.BlockSpec((tm, width), lambda i: (i, 0))
    const = lambda a: pl.BlockSpec(a.shape, lambda i: (0, 0))
    return pl.pallas_call(
        functools.partial(_outproj_kernel, alpha),
        grid=(t // tm,),
        in_specs=[row_spec(D_MODEL)] + [row_spec(GROUP_WIDTH)] * N_MIXERS
                 + [const(w), const(bo), const(g), const(beta)],
        out_specs=row_spec(D_MODEL),
        out_shape=jax.ShapeDtypeStruct((t, D_MODEL), F32),
        compiler_params=pltpu.CompilerParams(dimension_semantics=("arbitrary",),
                                             vmem_limit_bytes=VMEM_LIMIT_BYTES),
        name="outproj_ln",
    )(x2d, *mixed, w, bo, g, beta)


def _ffn_kernel(alpha, x_ref, w1_ref, b1_ref, w2_ref, b2_ref, g_ref, beta_ref, o_ref, xb_ref, acc_ref):
    j = pl.program_id(1)

    @pl.when(j == 0)
    def _():
        xb_ref[...] = x_ref[...].astype(BF16)
        acc_ref[...] = jnp.zeros_like(acc_ref)

    hdn = jnp.maximum(_dot(xb_ref[...], w1_ref[...]) + b1_ref[...], 0.0)
    acc_ref[...] += _dot(jnp.square(hdn).astype(BF16), w2_ref[...])

    @pl.when(j == pl.num_programs(1) - 1)
    def _():
        y = alpha * x_ref[...] + (acc_ref[...] + b2_ref[...])
        o_ref[...] = _layer_norm_rows(y, g_ref[...], beta_ref[...])


def _ffn(alpha, x2d, w1, b1, w2, b2, g, beta, tm=1024, tf=1024):
    t = x2d.shape[0]
    return pl.pallas_call(
        functools.partial(_ffn_kernel, alpha),
        grid=(t // tm, D_FF // tf),
        in_specs=[pl.BlockSpec((tm, D_MODEL), lambda i, j: (i, 0)),
                  pl.BlockSpec((D_MODEL, tf), lambda i, j: (0, j)),
                  pl.BlockSpec((1, tf), lambda i, j: (0, j)),
                  pl.BlockSpec((tf, D_MODEL), lambda i, j: (j, 0)),
                  pl.BlockSpec((1, D_MODEL), lambda i, j: (0, 0)),
                  pl.BlockSpec((1, D_MODEL), lambda i, j: (0, 0)),
                  pl.BlockSpec((1, D_MODEL), lambda i, j: (0, 0))],
        out_specs=pl.BlockSpec((tm, D_MODEL), lambda i, j: (i, 0)),
        out_shape=jax.ShapeDtypeStruct((t, D_MODEL), F32),
        scratch_shapes=[pltpu.VMEM((tm, D_MODEL), BF16), pltpu.VMEM((tm, D_MODEL), F32)],
        compiler_params=pltpu.CompilerParams(dimension_semantics=("arbitrary", "arbitrary"),
                                             vmem_limit_bytes=VMEM_LIMIT_BYTES),
        name="ffn_ln",
    )(x2d, w1, b1, w2, b2, g, beta)


def _bias_of_distance(rel_bias_heads, dist):
    onehot = np.eye(NUM_BUCKETS, dtype=np.float32)[_t5_bucket_np(np.asarray(dist))]
    return jnp.dot(rel_bias_heads.T, jnp.asarray(onehot.T), precision=lax.Precision.HIGHEST)


def _shifted_rows(v, n_rows, n_cols, step):
    period = v.shape[-1]
    assert n_cols <= period - step
    flat = jnp.tile(v, (1,) * (v.ndim - 1) + (n_rows,))[..., :n_rows * (period - step)]
    return flat.reshape(v.shape[:-1] + (n_rows, period - step))[..., :n_cols]


def _wrapped(period):
    idx = np.arange(period)
    return np.where(idx < period // 2, idx, idx - period)


def _dilated_bias(rel_bias):
    x = _wrapped(4 * BLK)
    tabs = [_shifted_rows(_bias_of_distance(rel_bias[:, 0:HEADS], (BLK - x) * dil), BLK, 2 * BLK, 1)
            for _, dil in DIL_PATTERNS]
    tabs = jnp.stack(tabs).reshape(len(DIL_PATTERNS), HEADS // 2, 2, BLK, 2 * BLK)
    return jnp.transpose(tabs, (0, 1, 3, 2, 4)).reshape(len(DIL_PATTERNS), HEADS // 2, BLK, 4 * BLK)


def _toeplitz_bias_t(rel_bias_heads, blk=BLK):
    assert (_t5_bucket_np(np.arange(blk + 1, 64 * blk)) == NUM_BUCKETS - 1).all()
    x = _wrapped(2 * blk)
    tabs = [_shifted_rows(_bias_of_distance(rel_bias_heads, blk * delta + x), blk, blk, 1)
            for delta in range(3)]
    return jnp.transpose(jnp.stack(tabs), (0, 2, 1, 3)).reshape(3, blk, HEADS * blk)


def _compressed_bias_t(rel_bias_heads, seq):
    nq = seq // QRY
    x = _wrapped(2 * seq + BLK)
    v = _bias_of_distance(rel_bias_heads, x - (NSA_CMP_LEN - 1))
    tab = _shifted_rows(v, BLK, seq, NSA_CMP_STRIDE)
    tab = tab.reshape(HEADS, BLK, nq, QRY)
    return jnp.transpose(tab, (2, 1, 0, 3)).reshape(nq, BLK, HEADS * QRY)


def _nsa_constants(seq):
    n_cmp = (seq - NSA_CMP_LEN) // NSA_CMP_STRIDE + 1
    n_sel = seq // NSA_SEL_LEN
    cs = np.arange(n_cmp)[:, None] * NSA_CMP_STRIDE
    ss = np.arange(n_sel)[None, :] * NSA_SEL_LEN
    ov = np.clip(np.minimum(cs + NSA_CMP_LEN, ss + NSA_SEL_LEN) - np.maximum(cs, ss), 0, None) / NSA_CMP_LEN
    ovt = np.zeros((BLK, BLK), np.float32)
    ovt[:n_sel, :n_cmp] = ov.T
    expand = np.zeros((seq, BLK), np.float32)
    expand[np.arange(seq), np.arange(seq) // NSA_SEL_LEN] = 1.0
    return jnp.asarray(ovt), jnp.asarray(expand, BF16)


def _nsa_weights(cmp_pos, cmp_w1, cmp_w2):
    hd, hid = HEAD_DIM, NSA_CMP_HIDDEN
    w1 = cmp_w1.reshape(2, NSA_CMP_LEN, hd, hid)
    zeros = jnp.zeros((NSA_CMP_LEN, hd, hid), F32)
    w1 = jnp.concatenate([jnp.concatenate([w1[0], zeros], axis=-1),
                          jnp.concatenate([zeros, w1[1]], axis=-1)], axis=1)
    pos = jnp.concatenate([cmp_pos[0], cmp_pos[1]], axis=-1)
    z2 = jnp.zeros((hid, hd), F32)
    w2 = jnp.concatenate([jnp.concatenate([cmp_w2[0], z2], axis=-1),
                          jnp.concatenate([z2, cmp_w2[1]], axis=-1)], axis=0)
    return w1.astype(BF16), pos, w2.astype(BF16)


def kernel(x, w_in, b_in, a_conv, a_norm, d_cmp_pos, d_cmp_w1, d_cmp_w2, w_out, b_out, ln1_g, ln1_b,
           w_ff1, b_ff1, w_ff2, b_ff2, ln2_g, ln2_b, rel_bias):
    bsz, seq, _ = x.shape
    depth = w_in.shape[0]
    alpha = (2 * depth) ** 0.25
    bias_dil = _dilated_bias(rel_bias)
    btab_dsa = _toeplitz_bias_t(rel_bias[:, HEADS:2 * HEADS], QRY) * LOG2_E
    btab_nsa = _toeplitz_bias_t(rel_bias[:, 2 * HEADS:3 * HEADS], QRY) * LOG2_E
    bcmp_nsa = _compressed_bias_t(rel_bias[:, 2 * HEADS:3 * HEADS], seq) * LOG2_E
    ovt, expand = _nsa_constants(seq)

    h = x.reshape(bsz * seq, D_MODEL)
    for l in range(depth):
        w_l = _permute_columns(w_in[l]).astype(BF16)
        b_l = _permute_columns(b_in[l])[None, :]
        za, zb, zc, zd, zg = (z.reshape(bsz, seq, -1) for z in _inproj(h, w_l, b_l))
        out_a = _mlstm(za, _mlstm_gate_rows(zg), a_conv[l], a_norm[l][None, :])
        out_b = _dilated(zb, bias_dil)
        out_c = _dsa(zc, btab_dsa)
        nsa_w1, nsa_pos, nsa_w2 = _nsa_weights(d_cmp_pos[l], d_cmp_w1[l], d_cmp_w2[l])
        out_d = _nsa(zd, nsa_w1, nsa_pos, nsa_w2, btab_nsa, bcmp_nsa, ovt, expand)
        mixed = [o.reshape(bsz * seq, GROUP_WIDTH) for o in (out_a, out_b, out_c, out_d)]
        h = _outproj(alpha, h, mixed, w_out[l].astype(BF16), b_out[l][None, :],
                     ln1_g[l][None, :], ln1_b[l][None, :])
        h = _ffn(alpha, h, w_ff1[l].astype(BF16), b_ff1[l][None, :], w_ff2[l].astype(BF16),
                 b_ff2[l][None, :], ln2_g[l][None, :], ln2_b[l][None, :])
    return h.reshape(bsz, seq, D_MODEL)
```

```python
import functools
import math

import numpy as np
import jax
import jax.numpy as jnp
from jax import lax
from jax.experimental import pallas as pl
from jax.experimental.pallas import tpu as pltpu

F32 = jnp.float32
BF16 = jnp.bfloat16
I32 = jnp.int32
I16 = jnp.int16
I16_MIN = -2 ** 15

D_MODEL = 1024
N_MIXERS = 4
HEADS = 4
HEAD_DIM = D_MODEL // (N_MIXERS * HEADS)
GROUP_WIDTH = HEADS * HEAD_DIM
D_FF = 4 * D_MODEL
LN_EPS = 1e-5
NEG = -1e30

M_QK_DIM = HEAD_DIM // 2
M_CHUNK = 64
M_CONV = 4
DIL_PATTERNS = ((128, 1), (512, 4), (2048, 16))
IDX_HEADS = 4
IDX_DIM = 64
DSA_TOPK = 256
NSA_CMP_LEN = 32
NSA_CMP_STRIDE = 16
NSA_SEL_LEN = 64
NSA_TOPN = 16
NSA_WINDOW = 512
NSA_CMP_HIDDEN = 256
NSA_FORCE = 1e9
NUM_BUCKETS = 32
MAX_DISTANCE = 128

LANES = 128
BLK = 128
KEYS = 2 * BLK
QRY = KEYS
VMEM_LIMIT_BYTES = 56 * 1024 * 1024

IN_SPLITS = (
    ('a_q', HEADS * M_QK_DIM), ('a_k', HEADS * M_QK_DIM), ('a_v', GROUP_WIDTH),
    ('a_i', HEADS), ('a_f', HEADS), ('a_o', GROUP_WIDTH),
    ('b_q', GROUP_WIDTH), ('b_k', GROUP_WIDTH), ('b_v', GROUP_WIDTH),
    ('c_q', GROUP_WIDTH), ('c_k', HEAD_DIM), ('c_v', HEAD_DIM),
    ('c_iq', IDX_HEADS * IDX_DIM), ('c_ik', IDX_DIM), ('c_iw', IDX_HEADS),
    ('d_q', GROUP_WIDTH), ('d_kc', HEAD_DIM), ('d_vc', HEAD_DIM),
    ('d_ks', HEAD_DIM), ('d_vs', HEAD_DIM), ('d_kw', HEAD_DIM), ('d_vw', HEAD_DIM),
    ('d_g', 3 * HEADS),
)

GROUP_LAYOUT = (
    (('a_q', 'a_k'), ('a_v',), ('a_o',), (('a_i', HEAD_DIM),), (('a_i', M_QK_DIM),),
     (('a_f', HEAD_DIM),), (('a_f', M_QK_DIM),)),
    (('b_q',), ('b_k',), ('b_v',)),
    (('c_q',), ('c_iq',), ('c_k', 'c_v'), ('c_ik', 'c_iw')),
    (('d_q',), ('d_kc', 'd_vc'), ('d_ks', 'd_vs'), ('d_kw', 'd_vw'), ('d_g',)),
    (('a_i', 'a_f'),),
)


def _round_up(n, m):
    return -(-n // m) * m


def _projection_layout():
    offs, off = {}, 0
    for name, width in IN_SPLITS:
        offs[name] = (off, width)
        off += width
    runs, group_widths, chunk_starts = [], [], []
    for group in GROUP_LAYOUT:
        gwidth = 0
        chunk_starts.append([])
        for chunk in group:
            chunk_starts[-1].append(gwidth)
            cwidth = 0
            for entry in chunk:
                name, rep = entry if isinstance(entry, tuple) else (entry, 1)
                o, w = offs[name]
                runs.append((o, w, rep))
                cwidth += w * rep
            pad = _round_up(cwidth, LANES) - cwidth
            if pad:
                runs.append((-1, pad, 1))
            gwidth += cwidth + pad
        group_widths.append(gwidth)
    return tuple(runs), tuple(group_widths), tuple(tuple(c) for c in chunk_starts)


PROJ_RUNS, GROUP_WIDTHS, CHUNK_STARTS = _projection_layout()
PROJ_WIDTH = int(sum(GROUP_WIDTHS))


def _permute_columns(a):
    parts = []
    for o, w, r in PROJ_RUNS:
        if o < 0:
            parts.append(jnp.zeros(a.shape[:-1] + (w,), a.dtype))
        else:
            parts.append(a[..., o:o + w] if r == 1 else jnp.repeat(a[..., o:o + w], r, axis=-1))
    return jnp.concatenate(parts, axis=-1)


def _t5_bucket_np(dist):
    n = np.maximum(dist, 0)
    max_exact = NUM_BUCKETS // 2
    nf = np.maximum(n, max_exact).astype(np.float32)
    large = max_exact + (np.log(nf / max_exact) / math.log(MAX_DISTANCE / max_exact)
                         * (NUM_BUCKETS - max_exact)).astype(np.int32)
    large = np.minimum(large, NUM_BUCKETS - 1)
    return np.where(n < max_exact, n, large).astype(np.int32)


def _nt_dot(a, b, precision=None):
    return lax.dot_general(a, b, (((1,), (1,)), ((), ())), precision=precision,
                           preferred_element_type=F32)


def _dot(a, b, precision=None):
    return jnp.dot(a, b, precision=precision, preferred_element_type=F32)


def _layer_norm_rows(y, g, b):
    mu = jnp.mean(y, axis=-1, keepdims=True)
    var = jnp.mean(jnp.square(y - mu), axis=-1, keepdims=True)
    return (y - mu) * lax.rsqrt(var + LN_EPS) * g + b


def _sigmoid(x):
    return 1.0 / (1.0 + jnp.exp(-x))


def _log_sigmoid(x):
    return -(jnp.maximum(-x, 0.0) + jnp.log1p(jnp.exp(-jnp.abs(x))))


def _sortable_key(x):
    bits = pltpu.bitcast(x, I32)
    return bits ^ ((bits >> 31) & jnp.int32(0x7FFFFFFF))


def _inproj_kernel(x_ref, w_ref, b_ref, *out_refs):
    xb = x_ref[...].astype(BF16)
    off = 0
    for o_ref, width in zip(out_refs, GROUP_WIDTHS):
        o_ref[...] = _dot(xb, w_ref[:, off:off + width]) + b_ref[:, off:off + width]
        off += width


def _inproj(x2d, w, b, tm=512):
    t = x2d.shape[0]
    return pl.pallas_call(
        _inproj_kernel,
        grid=(t // tm,),
        in_specs=[pl.BlockSpec((tm, D_MODEL), lambda i: (i, 0)),
                  pl.BlockSpec((D_MODEL, PROJ_WIDTH), lambda i: (0, 0)),
                  pl.BlockSpec((1, PROJ_WIDTH), lambda i: (0, 0))],
        out_specs=[pl.BlockSpec((tm, gw), lambda i: (i, 0)) for gw in GROUP_WIDTHS],
        out_shape=[jax.ShapeDtypeStruct((t, gw), F32) for gw in GROUP_WIDTHS],
        compiler_params=pltpu.CompilerParams(dimension_semantics=("arbitrary",),
                                             vmem_limit_bytes=VMEM_LIMIT_BYTES),
        name="inproj",
    )(x2d, w, b)


def _split_terms(x, n):
    terms, rest = [], x
    for _ in range(n):
        terms.append(rest.astype(BF16))
        rest = rest - terms[-1].astype(F32)
    return terms


def _iota(shape, dim):
    return lax.broadcasted_iota(I32, shape, dim)


def _mlstm_kernel(z_ref, gt_ref, cw_ref, ng_ref, o_ref, xpad_ref):
    seq = z_ref.shape[1]
    L, DK, DV, H = M_CHUNK, M_QK_DIM, HEAD_DIM, HEADS
    assert L == DV
    wq, wv = H * DK, H * DV
    lg_dk, lg_dv = int(math.log2(DK)), int(math.log2(DV))
    _, c_v, c_o, c_i64, c_i32, c_f64, c_f32 = CHUNK_STARTS[0]
    assert c_f32 == c_f64 + wv
    xpad_ref[0:8, :] = jnp.zeros((8, 2 * wq), F32)
    xpad_ref[8:, :] = z_ref[0, :, 0:2 * wq]

    one_if = lambda cond: jnp.where(cond, 1.0, 0.0).astype(BF16)
    tri_l = one_if(_iota((L, L), 0) >= _iota((L, L), 1))
    trow = _iota((L, wv), 0)
    tri_heads = trow >= (_iota((L, wv), 1) & (L - 1))
    r_vv, c_vv = _iota((wv, wv), 0), _iota((wv, wv), 1)
    same_head = (r_vv >> lg_dv) == (c_vv >> lg_dv)
    ones_bd = one_if(same_head)
    mean_bd = jnp.where(same_head, 1.0 / DV, 0.0).astype(BF16)
    tri_u_bd = one_if(same_head & ((r_vv & (L - 1)) <= (c_vv & (L - 1))))
    state_mask = (_iota((wq, wv), 0) >> lg_dk) == (_iota((wq, wv), 1) >> lg_dv)
    eye_q = one_if(_iota((wq, wq), 0) == _iota((wq, wq), 1))
    head_of_qlane = _iota((L, wq), 1) >> lg_dk
    head_of_vlane = _iota((L, wv), 1) >> lg_dv
    row8 = _iota((8, wv), 0)
    cw = cw_ref[...]
    ng = ng_ref[...]

    def head_mean(x):
        hi_lo = _split_terms(x, 2)
        r = _dot(jnp.concatenate(hi_lo, axis=0), mean_bd)
        return r[0:L] + r[L:2 * L]

    def chunk(c, carry):
        cbd, nbd, m64, m32 = carry
        s0 = pl.multiple_of(c * L, L)
        rows = pl.ds(s0, L)
        xw = xpad_ref[pl.ds(s0, L + 8), :]
        y = sum(cw[j:j + 1, :] * xw[5 + j:5 + j + L, :] for j in range(M_CONV))
        qk = y * _sigmoid(y)
        q = qk[:, 0:wq]
        k = qk[:, wq:] * (DK ** -0.5)
        qb = q.astype(BF16)
        v = z_ref[0, rows, c_v:c_v + wv]
        i64 = z_ref[0, rows, c_i64:c_i64 + wv]
        i32 = z_ref[0, rows, c_i32:c_i32 + wq]
        gr = gt_ref[0, c]

        flog = _log_sigmoid(z_ref[0, rows, c_f64:c_f64 + wv + wq])
        bsum = _dot(tri_l, jnp.concatenate(_split_terms(flog, 3), axis=1))
        w3 = wv + wq
        ball = bsum[:, 0:w3] + bsum[:, w3:2 * w3] + bsum[:, 2 * w3:3 * w3]
        b64, b32 = ball[:, 0:wv], ball[:, wv:w3]
        fterms = [t.astype(F32) for t in _split_terms(_log_sigmoid(gr[1:2, :]), 3)]
        frows = jnp.where(row8 == 0, fterms[0], jnp.where(row8 == 1, fterms[1], jnp.where(row8 == 2, fterms[2], 0.0)))
        bparts = _dot(frows.astype(BF16), tri_u_bd)
        brow = bparts[0:1, :] + bparts[1:2, :] + bparts[2:3, :]

        dall = jnp.where(tri_heads, b64 - brow + gr[0:1, :], NEG)
        cm = i64 - b64
        for sh in (1, 2, 4, 8, 16, 32):
            cm = jnp.where(trow >= sh, jnp.maximum(cm, pltpu.roll(cm, sh, 0)), cm)
        inter = b64 + m64
        m_t = jnp.maximum(inter, b64 + cm)
        kbd = jnp.concatenate([jnp.where(head_of_qlane == h, k, 0.0) for h in range(H)], axis=0).astype(BF16)
        sc = _nt_dot(qb, kbd) * jnp.exp(dall - m_t)
        wi = jnp.exp(inter - m_t)
        vb = v.astype(BF16)
        vbd = jnp.concatenate([jnp.where(head_of_vlane == h, v, 0.0) for h in range(H)], axis=0).astype(BF16)
        pv = _dot(sc.astype(BF16), jnp.concatenate([vbd, ones_bd], axis=1))
        qst = _dot(qb, jnp.concatenate([cbd, nbd], axis=1).astype(BF16))
        num = pv[:, 0:wv] + wi * qst[:, 0:wv]
        den = pv[:, wv:] + wi * qst[:, wv:]
        hh = num / jnp.maximum(jnp.abs(den), jnp.exp(-m_t))

        og = _sigmoid(z_ref[0, rows, c_o:c_o + wv]) * hh
        dev = og - head_mean(og)
        o_ref[0, rows, :] = dev * lax.rsqrt(head_mean(dev * dev) + LN_EPS) * ng

        bl64, bl32 = b64[L - 1:L, :], b32[L - 1:L, :]
        m64_new = jnp.maximum(bl64 + m64, jnp.max(bl64 - b64 + i64, axis=0, keepdims=True))
        g32 = bl32 - b32 + i32
        m32_new = jnp.maximum(bl32 + m32, jnp.max(g32, axis=0, keepdims=True))
        kw = k * jnp.exp(g32 - m32_new)
        wc = jnp.exp(bl64 + m64 - m64_new)
        kwt = _nt_dot(eye_q, kw.astype(BF16)).astype(BF16)
        upd = _dot(kwt, jnp.concatenate([vb, jnp.ones((L, wv), BF16)], axis=1))
        cbd = wc * cbd + jnp.where(state_mask, upd[:, 0:wv], 0.0)
        nbd = wc * nbd + jnp.where(state_mask, upd[:, wv:], 0.0)
        return cbd, nbd, m64_new, m32_new

    init = (jnp.zeros((wq, wv), F32), jnp.zeros((wq, wv), F32), jnp.zeros((1, wv), F32), jnp.zeros((1, wq), F32))
    lax.fori_loop(0, seq // L, chunk, init, unroll=4)


def _mlstm(za, gates_t, conv_w, norm_g):
    b, s, wa = za.shape
    nc = s // M_CHUNK
    return pl.pallas_call(
        _mlstm_kernel,
        grid=(b,),
        in_specs=[pl.BlockSpec((1, s, wa), lambda i: (i, 0, 0)),
                  pl.BlockSpec((1, nc) + gates_t.shape[2:], lambda i: (i, 0, 0, 0)),
                  pl.BlockSpec((M_CONV, 2 * HEADS * M_QK_DIM), lambda i: (0, 0)),
                  pl.BlockSpec((1, GROUP_WIDTH), lambda i: (0, 0))],
        out_specs=pl.BlockSpec((1, s, GROUP_WIDTH), lambda i: (i, 0, 0)),
        out_shape=jax.ShapeDtypeStruct((b, s, GROUP_WIDTH), F32),
        scratch_shapes=[pltpu.VMEM((s + 8, 2 * HEADS * M_QK_DIM), F32)],
        compiler_params=pltpu.CompilerParams(dimension_semantics=("arbitrary",),
                                             vmem_limit_bytes=VMEM_LIMIT_BYTES),
        name="mlstm",
    )(za, gates_t, conv_w, norm_g)


def _mlstm_gate_rows(zg):
    b, s, _ = zg.shape
    gates = zg[:, :, 0:2 * HEADS].reshape(b, s // M_CHUNK, M_CHUNK, 2, HEADS)
    return jnp.transpose(gates, (0, 1, 3, 4, 2)).reshape(b, s // M_CHUNK, 2, HEADS * M_CHUNK)


def _dilated_kernel(q0_ref, q1_ref, k0_ref, k1_ref, v0_ref, v1_ref, bias_ref, o_ref, acc_ref, mx_ref, den_ref):
    seq = q0_ref.shape[1]
    W = BLK
    hd = HEAD_DIM
    npair = HEADS // 2
    scale = hd ** -0.5
    assert math.log2(scale).is_integer()
    nk = 2 * W
    qi = _iota((W, 2 * nk), 0)
    ki = _iota((W, 2 * nk), 1) & (nk - 1)
    j = W + qi - ki
    band = (j >= 0) & (j <= W)
    head_of_lane = _iota((nk, 2 * hd), 1) >> int(math.log2(hd))
    ones_bd = jnp.concatenate([jnp.where(head_of_lane == hh, 1.0, 0.0) for hh in range(2)], axis=0).astype(BF16)
    q_refs, k_refs, v_refs = (q0_ref, q1_ref), (k0_ref, k1_ref), (v0_ref, v1_ref)

    def block_diag(x):
        return jnp.concatenate([jnp.where(head_of_lane == hh, x, 0.0) for hh in range(2)], axis=0).astype(BF16)

    for br, (window, dil) in enumerate(DIL_PATTERNS):
        assert window // dil == W
        nb = (seq // dil) // W

        def piece(idx, _, br=br, dil=dil, nb=nb):
            n = idx % nb
            if dil == 1:
                rows_q = pl.ds(pl.multiple_of(W * n, W), W)
                rows_p = pl.ds(pl.multiple_of(W * jnp.maximum(n - 1, 0), W), W)
            else:
                r = idx // nb
                rows_q = pl.ds(r + dil * W * n, W, stride=dil)
                rows_p = pl.ds(r + dil * W * jnp.maximum(n - 1, 0), W, stride=dil)
            mask = band & (ki >= jnp.where(n > 0, 0, W))
            for pr in range(npair):
                q = (q_refs[pr][0, rows_q, :] * scale).astype(BF16)
                k2 = jnp.concatenate([k_refs[pr][0, rows_p, :], k_refs[pr][0, rows_q, :]], axis=0)
                v2 = jnp.concatenate([v_refs[pr][0, rows_p, :], v_refs[pr][0, rows_q, :]], axis=0)
                lg = jnp.where(mask, _nt_dot(q, block_diag(k2)) + bias_ref[br, pr], NEG)
                ps, mxs = [], []
                for hh in range(2):
                    sl = slice(nk * hh, nk * (hh + 1))
                    m = jnp.max(lg[:, sl], axis=-1, keepdims=True)
                    ps.append(jnp.where(mask[:, sl], jnp.exp(lg[:, sl] - m), 0.0))
                    mxs.append(jnp.broadcast_to(m, (W, hd)))
                p = jnp.concatenate(ps, axis=-1).astype(BF16)
                pv = _dot(p, jnp.concatenate([block_diag(v2), ones_bd], axis=1))
                acc_ref[br, pr, rows_q, :] = pv[:, 0:2 * hd]
                den_ref[br, pr, rows_q, :] = pv[:, 2 * hd:]
                mx_ref[br, pr, rows_q, :] = jnp.concatenate(mxs, axis=-1)
            return 0

        lax.fori_loop(0, dil * nb, piece, 0, unroll=8)

    def combine(i, _):
        rows = pl.ds(pl.multiple_of(i * W, W), W)
        outs = []
        for pr in range(npair):
            ms = [mx_ref[b, pr, rows, :] for b in range(len(DIL_PATTERNS))]
            top = functools.reduce(jnp.maximum, ms)
            es = [jnp.exp(m - top) for m in ms]
            num = sum(e * acc_ref[b, pr, rows, :] for b, e in enumerate(es))
            den = sum(e * jnp.maximum(den_ref[b, pr, rows, :], 1e-30) for b, e in enumerate(es))
            outs.append(num / den)
        o_ref[0, rows, :] = jnp.concatenate(outs, axis=-1)
        return 0

    lax.fori_loop(0, seq // W, combine, 0)


def _dilated(zb, bias):
    b, s, wb = zb.shape
    nbr = len(DIL_PATTERNS)
    pair_spec = lambda c: pl.BlockSpec((1, s, LANES), lambda i: (i, 0, c))
    return pl.pallas_call(
        _dilated_kernel,
        grid=(b,),
        in_specs=[pair_spec(c) for c in range(wb // LANES)]
                 + [pl.BlockSpec(bias.shape, lambda i: (0, 0, 0, 0))],
        out_specs=pl.BlockSpec((1, s, GROUP_WIDTH), lambda i: (i, 0, 0)),
        out_shape=jax.ShapeDtypeStruct((b, s, GROUP_WIDTH), F32),
        scratch_shapes=[pltpu.VMEM((nbr, HEADS // 2, s, LANES), F32) for _ in range(3)],
        compiler_params=pltpu.CompilerParams(dimension_semantics=("arbitrary",),
                                             vmem_limit_bytes=VMEM_LIMIT_BYTES),
        name="dilated",
    )(*([zb] * (wb // LANES)), bias)


def _head_columns_t(q):
    qt = q.T
    return jnp.concatenate([qt[HEAD_DIM * h:HEAD_DIM * (h + 1), :] for h in range(HEADS)], axis=1)


LOG2_E = math.log2(math.e)


def _scaled_query_columns_t(q):
    return (_head_columns_t(q) * (HEAD_DIM ** -0.5 * LOG2_E)).astype(BF16)


def _unstack_heads_t(per_head_t):
    halves = []
    for h in range(0, HEADS, 2):
        halves.append(jnp.concatenate([per_head_t[h], per_head_t[h + 1]], axis=0).T)
    return jnp.concatenate(halves, axis=-1)


VT_ROWS = HEAD_DIM + 16


def _online_step(lg, mask, vt1, m, acc):
    lg = jnp.where(mask, lg, NEG)
    m_new = jnp.maximum(m, jnp.max(lg, axis=0, keepdims=True))
    p = jnp.where(mask, jnp.exp2(lg - m_new), 0.0)
    acc = acc * jnp.exp2(m - m_new) + _dot(vt1, p.astype(BF16))
    return m_new, acc


def _online_result(acc):
    return acc[0:HEAD_DIM, :] / jnp.maximum(acc[HEAD_DIM:HEAD_DIM + 1, :], 1e-30)


def _values_with_ones_t(z_rows):
    vt = z_rows.T[HEAD_DIM:2 * HEAD_DIM, :]
    return jnp.concatenate([vt, jnp.ones((VT_ROWS - HEAD_DIM, vt.shape[1]), F32)], axis=0).astype(BF16)


def _online_init(nq=BLK):
    return (tuple(jnp.full((1, nq), NEG, F32) for _ in range(HEADS)),
            tuple(jnp.zeros((VT_ROWS, nq), F32) for _ in range(HEADS)))


def _dsa_kernel(z_ref, btab_ref, o_ref, vt_ref, key_ref, hi_ref, lo_ref, lom_ref):
    i = pl.program_id(1)
    seq = z_ref.shape[1]
    nkb = seq // BLK
    nq = QRY
    hd = HEAD_DIM
    topk = min(DSA_TOPK, seq // 4)
    t0 = pl.multiple_of(i * nq, nq)

    assert nq == KEYS
    npair = i + 1
    c_q, c_iq, c_kv, c_ikw = CHUNK_STARTS[2]

    @pl.when(i == 0)
    def _():
        for kb in range(nkb):
            vt_ref[kb // 2, :, (kb % 2) * BLK:(kb % 2 + 1) * BLK] = _values_with_ones_t(
                z_ref[0, kb * BLK:(kb + 1) * BLK, c_kv:c_kv + 2 * hd])

    zq = z_ref[0, pl.ds(t0, nq), :]
    cq = zq[:, c_q:c_q + HEADS * hd]
    ciq = _head_columns_t(zq[:, c_iq:c_iq + IDX_HEADS * IDX_DIM]).astype(BF16)
    iw = (zq[:, c_ikw:c_ikw + LANES].T[IDX_DIM:IDX_DIM + IDX_HEADS, :]
          * ((IDX_HEADS * IDX_DIM) ** -0.5))
    s_loc = lax.broadcasted_iota(I32, (KEYS, nq), 0)
    t_glob = t0 + lax.broadcasted_iota(I32, (KEYS, nq), 1)

    def score_block(kp, _):
        r0 = pl.multiple_of(kp * KEYS, KEYS)
        ik = z_ref[0, pl.ds(r0, KEYS), c_ikw:c_ikw + IDX_DIM].astype(BF16)
        rel = _dot(ik, ciq)
        sc = jnp.zeros((KEYS, nq), F32)
        for h in range(IDX_HEADS):
            sc = sc + jnp.maximum(rel[:, nq * h:nq * (h + 1)], 0.0) * iw[h:h + 1, :]
        sc = jnp.where(r0 + s_loc <= t_glob, sc, NEG)
        key = _sortable_key(sc)
        key_ref[kp] = key
        hi_ref[kp] = (key >> 16).astype(I16)
        lo_ref[kp] = ((key & 0xFFFF) - 2 ** 15).astype(I16)
        return 0

    lax.fori_loop(0, npair // 2, lambda t, c: score_block(2 * t + 1, score_block(2 * t, c)), 0)
    lax.cond(npair % 2 == 1, lambda: score_block(npair - 1, 0), lambda: 0)

    def count(ref, pred, pairs):
        dt = ref.dtype
        rows = 8 * 4 // dt.itemsize
        def body(kp, acc):
            hit = jnp.where(pred(kp, ref[kp]), jnp.ones((), dt), jnp.zeros((), dt))
            hit = hit.reshape(KEYS // rows, rows, nq)
            parts = [hit[n] for n in range(KEYS // rows)]
            while len(parts) > 1:
                parts = [a + b for a, b in zip(parts[0::2], parts[1::2])]
            return acc + parts[0]
        acc = jnp.zeros((rows, nq), dt)
        for kp in range(pairs):
            acc = body(kp, acc)
        return jnp.sum(acc.astype(I32), axis=0, keepdims=True)

    def threshold(pairs):
        def half_search(ref, k):
            def bit(it, thr):
                cand = thr + lax.shift_left(jnp.int32(1), 15 - it)
                c = count(ref, lambda kp, half: half >= cand.astype(I16), pairs)
                return jnp.where(c >= k, cand, thr)
            return lax.fori_loop(0, 16, bit, jnp.full((1, nq), I16_MIN, I32))

        def run(_):
            thr_hi = half_search(hi_ref, topk)
            thr_hi16 = thr_hi.astype(I16)
            above_hi = count(hi_ref, lambda kp, half: half > thr_hi16, pairs)
            for kp in range(pairs):
                lom_ref[kp] = jnp.where(hi_ref[kp] == thr_hi16, lo_ref[kp], jnp.int16(I16_MIN))
            thr_lo = half_search(lom_ref, topk - above_hi)
            thr = (thr_hi << 16) | (thr_lo + 2 ** 15)
            thr_lo16 = thr_lo.astype(I16)
            n_gt = above_hi + count(lom_ref, lambda kp, half: half > thr_lo16, pairs)
            need = (topk - n_gt).astype(F32)
            earlier = jnp.zeros((1, nq), F32)
            for kp in range(pairs):
                tie = key_ref[kp] == thr
                tie01 = jnp.where(tie, 1.0, 0.0)
                rank = _dot(below, tie01.astype(BF16)) + earlier
                lom_ref[kp] = jnp.where(tie & (rank < need), 1, 0).astype(I16)
                earlier = earlier + jnp.sum(tie01, axis=0, keepdims=True)
            return thr
        return run

    below = jnp.where(_iota((KEYS, KEYS), 1) < _iota((KEYS, KEYS), 0), 1.0, 0.0).astype(BF16)
    thr = lax.switch(npair - 1, [threshold(p) for p in range(1, seq // KEYS + 1)], 0)

    qs = _scaled_query_columns_t(cq)

    def attend(kp, carry):
        ms, accs = carry
        r0 = pl.multiple_of(kp * KEYS, KEYS)
        kblk = z_ref[0, pl.ds(r0, KEYS), c_kv:c_kv + hd].astype(BF16)
        lg = _dot(kblk, qs) + btab_ref[jnp.minimum(i - kp, 2)]
        key = key_ref[kp]
        s_glob = r0 + s_loc
        mask = ((key > thr) | (lom_ref[kp].astype(I32) != 0)) & (s_glob <= t_glob)
        vt1 = vt_ref[kp]
        out = [_online_step(lg[:, nq * h:nq * (h + 1)], mask, vt1, ms[h], accs[h]) for h in range(HEADS)]
        return tuple(o[0] for o in out), tuple(o[1] for o in out)

    ms, accs = lax.fori_loop(0, npair, attend, _online_init(nq))
    o_ref[0] = _unstack_heads_t([_online_result(accs[h]) for h in range(HEADS)])


def _dsa(zc, btab):
    b, s, wc = zc.shape
    return pl.pallas_call(
        _dsa_kernel,
        grid=(b, s // QRY),
        in_specs=[pl.BlockSpec((1, s, wc), lambda bi, i: (bi, 0, 0)),
                  pl.BlockSpec(btab.shape, lambda bi, i: (0, 0, 0))],
        out_specs=pl.BlockSpec((1, QRY, GROUP_WIDTH), lambda bi, i: (bi, i, 0)),
        out_shape=jax.ShapeDtypeStruct((b, s, GROUP_WIDTH), F32),
        scratch_shapes=[pltpu.VMEM((s // KEYS, VT_ROWS, KEYS), BF16), pltpu.VMEM((s // KEYS, KEYS, QRY), I32)]
                       + [pltpu.VMEM((s // KEYS, KEYS, QRY), I16)] * 3,
        compiler_params=pltpu.CompilerParams(dimension_semantics=("arbitrary", "arbitrary"),
                                             vmem_limit_bytes=VMEM_LIMIT_BYTES),
        name="dsa",
    )(zc, btab)


def _nsa_kernel(z_ref, zc_ref, w1_ref, pos_ref, w2_ref, bsel_ref, bcmp_ref, ovt_ref, exp_ref, o_ref,
                vst_ref, vwt_ref, cmp_ref, cmpt_ref):
    i = pl.program_id(1)
    seq = z_ref.shape[1]
    nkb = seq // BLK
    hd = HEAD_DIM
    c_q, _, c_s, c_w, c_g = CHUNK_STARTS[3]
    n_cmp = (seq - NSA_CMP_LEN) // NSA_CMP_STRIDE + 1
    n_sel = seq // NSA_SEL_LEN
    topn = min(NSA_TOPN, n_sel)
    half = NSA_CMP_LEN // 2
    assert half == NSA_CMP_STRIDE and n_cmp + 1 == seq // NSA_CMP_STRIDE == BLK and n_sel <= BLK
    nq = QRY
    assert nq == KEYS
    t0 = pl.multiple_of(i * nq, nq)
    hi = lax.Precision.HIGHEST

    @pl.when(i == 0)
    def _():
        for kb in range(nkb):
            rows = slice(kb * BLK, (kb + 1) * BLK)
            cols = slice((kb % 2) * BLK, (kb % 2 + 1) * BLK)
            vst_ref[kb // 2, :, cols] = _values_with_ones_t(z_ref[0, rows, c_s:c_s + 2 * hd])
            vwt_ref[kb // 2, :, cols] = _values_with_ones_t(z_ref[0, rows, c_w:c_w + 2 * hd])
        first = jnp.zeros((BLK, 2 * NSA_CMP_HIDDEN), F32)
        second = jnp.zeros((BLK, 2 * NSA_CMP_HIDDEN), F32)
        for j in range(half):
            xj = zc_ref[0, pl.ds(j, BLK, stride=NSA_CMP_STRIDE), :]
            first = first + _dot((xj + pos_ref[j:j + 1, :]).astype(BF16), w1_ref[j])
            second = second + _dot((xj + pos_ref[half + j:half + j + 1, :]).astype(BF16), w1_ref[half + j])
        hid = first + pltpu.roll(second, BLK - 1, 0)
        hid = hid * _sigmoid(hid)
        cmp = _dot(hid.astype(BF16), w2_ref[...])
        cmp_ref[...] = cmp
        cmpt_ref[...] = cmp.T

    zq = z_ref[0, pl.ds(t0, nq), :]
    qs = _scaled_query_columns_t(zq[:, c_q:c_q + HEADS * hd])
    gates = _sigmoid(zq[:, c_g:c_g + LANES].T[0:16, :])
    row = lax.broadcasted_iota(I32, (BLK, nq), 0)
    t_glob = t0 + lax.broadcasted_iota(I32, (BLK, nq), 1)

    kcmp = cmp_ref[:, 0:hd].astype(BF16)
    vcmpt = cmpt_ref[hd:2 * hd, :].astype(BF16)
    lgc = _dot(kcmp, qs) + bcmp_ref[0]
    mask_c = (t_glob - (row * NSA_CMP_STRIDE + NSA_CMP_LEN - 1) >= 0) & (row < n_cmp)
    o_cmp, psum = [], jnp.zeros((BLK, nq), F32)
    for h in range(HEADS):
        lg = jnp.where(mask_c, lgc[:, nq * h:nq * (h + 1)], NEG)
        m = jnp.max(lg, axis=0, keepdims=True)
        p = jnp.where(mask_c, jnp.exp2(lg - m), 0.0)
        p = p / jnp.maximum(jnp.sum(p, axis=0, keepdims=True), 1e-30)
        o_cmp.append(_dot(vcmpt, p.astype(BF16)))
        psum = psum + p

    imp = _dot(ovt_ref[...], psum, precision=hi)
    cur = t_glob >> int(math.log2(NSA_SEL_LEN))
    forced = (row == 0) | (row == cur) | (row == cur - 1)
    imp = jnp.where(forced, NSA_FORCE, imp)
    imp = jnp.where(row * NSA_SEL_LEN <= t_glob, imp, NEG)
    imp = imp[0:n_sel, :]
    jrow = lax.broadcasted_iota(I32, (n_sel, nq), 0)
    rank = jnp.zeros((n_sel, nq), I32)
    for jp in range(n_sel):
        other = imp[jp:jp + 1, :]
        rank = rank + ((other > imp) | ((other == imp) & (jp < jrow))).astype(I32)
    chosen = jnp.where(rank < topn, 1.0, 0.0)
    chosen = jnp.concatenate([chosen, jnp.zeros((BLK - n_sel, nq), F32)], axis=0).astype(BF16)

    s_loc = lax.broadcasted_iota(I32, (KEYS, nq), 0)
    t_keys = t0 + lax.broadcasted_iota(I32, (KEYS, nq), 1)

    def attend(vt_ref, lanes, mask_fn):
        def body(kp, carry):
            ms, accs = carry
            r0 = pl.multiple_of(kp * KEYS, KEYS)
            kblk = z_ref[0, pl.ds(r0, KEYS), lanes].astype(BF16)
            lg = _dot(kblk, qs) + bsel_ref[jnp.minimum(i - kp, 2)]
            mask = mask_fn(r0)
            vt1 = vt_ref[kp]
            out = [_online_step(lg[:, nq * h:nq * (h + 1)], mask, vt1, ms[h], accs[h]) for h in range(HEADS)]
            return tuple(o[0] for o in out), tuple(o[1] for o in out)
        return body

    def mask_sel(r0):
        picked = _dot(exp_ref[pl.ds(r0, KEYS), :], chosen) > 0.5
        return picked & (r0 + s_loc <= t_keys)

    def mask_win(r0):
        dist = t_keys - (r0 + s_loc)
        return (dist >= 0) & (dist < NSA_WINDOW)

    sel_body = attend(vst_ref, slice(c_s, c_s + hd), mask_sel)
    win_body = attend(vwt_ref, slice(c_w, c_w + hd), mask_win)
    first_w = jnp.maximum(i - NSA_WINDOW // KEYS, 0)
    sel_carry = lax.fori_loop(0, first_w, sel_body, _online_init(nq))
    (_, acc_s), (_, acc_w) = lax.fori_loop(
        first_w, i + 1, lambda kp, c: (sel_body(kp, c[0]), win_body(kp, c[1])), (sel_carry, _online_init(nq)))

    outs = []
    for h in range(HEADS):
        outs.append(gates[3 * h:3 * h + 1, :] * o_cmp[h] + gates[3 * h + 1:3 * h + 2, :] * _online_result(acc_s[h])
                    + gates[3 * h + 2:3 * h + 3, :] * _online_result(acc_w[h]))
    o_ref[0] = _unstack_heads_t(outs)


def _nsa(zd, w1, pos, w2, bsel, bcmp, ovt, expand):
    b, s, wd = zd.shape
    full = lambda a: pl.BlockSpec(a.shape, lambda bi, i: (0,) * a.ndim)
    return pl.pallas_call(
        _nsa_kernel,
        grid=(b, s // QRY),
        in_specs=[pl.BlockSpec((1, s, wd), lambda bi, i: (bi, 0, 0)),
                  pl.BlockSpec((1, s, LANES), lambda bi, i: (bi, 0, CHUNK_STARTS[3][1] // LANES)),
                  full(w1), full(pos), full(w2), full(bsel),
                  pl.BlockSpec((1, BLK, HEADS * QRY), lambda bi, i: (i, 0, 0)),
                  full(ovt), full(expand)],
        out_specs=pl.BlockSpec((1, QRY, GROUP_WIDTH), lambda bi, i: (bi, i, 0)),
        out_shape=jax.ShapeDtypeStruct((b, s, GROUP_WIDTH), F32),
        scratch_shapes=[pltpu.VMEM((s // KEYS, VT_ROWS, KEYS), BF16), pltpu.VMEM((s // KEYS, VT_ROWS, KEYS), BF16),
                        pltpu.VMEM((BLK, BLK), F32), pltpu.VMEM((BLK, BLK), F32)],
        compiler_params=pltpu.CompilerParams(dimension_semantics=("arbitrary", "arbitrary"),
                                             vmem_limit_bytes=VMEM_LIMIT_BYTES),
        name="nsa",
    )(zd, zd, w1, pos, w2, bsel, bcmp, ovt, expand)


def _outproj_kernel(alpha, x_ref, a_ref, b_ref, c_ref, d_ref, w_ref, bo_ref, g_ref, beta_ref, o_ref):
    acc = bo_ref[...] + _dot(a_ref[...].astype(BF16), w_ref[0:GROUP_WIDTH, :])
    for n, m_ref in enumerate((b_ref, c_ref, d_ref), start=1):
        acc = acc + _dot(m_ref[...].astype(BF16), w_ref[n * GROUP_WIDTH:(n + 1) * GROUP_WIDTH, :])
    o_ref[...] = _layer_norm_rows(alpha * x_ref[...] + acc, g_ref[...], beta_ref[...])


def _outproj(alpha, x2d, mixed, w, bo, g, beta, tm=512):
    t = x2d.shape[0]
    row_spec = lambda width: pl.BlockSpec((tm, width), lambda i: (i, 0))
    const = lambda a: pl.BlockSpec(a.shape, lambda i: (0, 0))
    return pl.pallas_call(
        functools.partial(_outproj_kernel, alpha),
        grid=(t // tm,),
        in_specs=[row_spec(D_MODEL)] + [row_spec(GROUP_WIDTH)] * N_MIXERS
                 + [const(w), const(bo), const(g), const(beta)],
        out_specs=row_spec(D_MODEL),
        out_shape=jax.ShapeDtypeStruct((t, D_MODEL), F32),
        compiler_params=pltpu.CompilerParams(dimension_semantics=("arbitrary",),
                                             vmem_limit_bytes=VMEM_LIMIT_BYTES),
        name="outproj_ln",
    )(x2d, *mixed, w, bo, g, beta)


def _ffn_kernel(alpha, x_ref, w1_ref, b1_ref, w2_ref, b2_ref, g_ref, beta_ref, o_ref, xb_ref, acc_ref):
    j = pl.program_id(1)

    @pl.when(j == 0)
    def _():
        xb_ref[...] = x_ref[...].astype(BF16)
        acc_ref[...] = jnp.zeros_like(acc_ref)

    hdn = jnp.maximum(_dot(xb_ref[...], w1_ref[...]) + b1_ref[...], 0.0)
    acc_ref[...] += _dot(jnp.square(hdn).astype(BF16), w2_ref[...])

    @pl.when(j == pl.num_programs(1) - 1)
    def _():
        y = alpha * x_ref[...] + (acc_ref[...] + b2_ref[...])
        o_ref[...] = _layer_norm_rows(y, g_ref[...], beta_ref[...])


def _ffn(alpha, x2d, w1, b1, w2, b2, g, beta, tm=1024, tf=1024):
    t = x2d.shape[0]
    return pl.pallas_call(
        functools.partial(_ffn_kernel, alpha),
        grid=(t // tm, D_FF // tf),
        in_specs=[pl.BlockSpec((tm, D_MODEL), lambda i, j: (i, 0)),
                  pl.BlockSpec((D_MODEL, tf), lambda i, j: (0, j)),
                  pl.BlockSpec((1, tf), lambda i, j: (0, j)),
                  pl.BlockSpec((tf, D_MODEL), lambda i, j: (j, 0)),
                  pl.BlockSpec((1, D_MODEL), lambda i, j: (0, 0)),
                  pl.BlockSpec((1, D_MODEL), lambda i, j: (0, 0)),
                  pl.BlockSpec((1, D_MODEL), lambda i, j: (0, 0))],
        out_specs=pl.BlockSpec((tm, D_MODEL), lambda i, j: (i, 0)),
        out_shape=jax.ShapeDtypeStruct((t, D_MODEL), F32),
        scratch_shapes=[pltpu.VMEM((tm, D_MODEL), BF16), pltpu.VMEM((tm, D_MODEL), F32)],
        compiler_params=pltpu.CompilerParams(dimension_semantics=("arbitrary", "arbitrary"),
                                             vmem_limit_bytes=VMEM_LIMIT_BYTES),
        name="ffn_ln",
    )(x2d, w1, b1, w2, b2, g, beta)


def _bias_of_distance(rel_bias_heads, dist):
    onehot = np.eye(NUM_BUCKETS, dtype=np.float32)[_t5_bucket_np(np.asarray(dist))]
    return jnp.dot(rel_bias_heads.T, jnp.asarray(onehot.T), precision=lax.Precision.HIGHEST)


def _shifted_rows(v, n_rows, n_cols, step):
    period = v.shape[-1]
    assert n_cols <= period - step
    flat = jnp.tile(v, (1,) * (v.ndim - 1) + (n_rows,))[..., :n_rows * (period - step)]
    return flat.reshape(v.shape[:-1] + (n_rows, period - step))[..., :n_cols]


def _wrapped(period):
    idx = np.arange(period)
    return np.where(idx < period // 2, idx, idx - period)


def _dilated_bias(rel_bias):
    x = _wrapped(4 * BLK)
    tabs = [_shifted_rows(_bias_of_distance(rel_bias[:, 0:HEADS], (BLK - x) * dil), BLK, 2 * BLK, 1)
            for _, dil in DIL_PATTERNS]
    tabs = jnp.stack(tabs).reshape(len(DIL_PATTERNS), HEADS // 2, 2, BLK, 2 * BLK)
    return jnp.transpose(tabs, (0, 1, 3, 2, 4)).reshape(len(DIL_PATTERNS), HEADS // 2, BLK, 4 * BLK)


def _toeplitz_bias_t(rel_bias_heads, blk=BLK):
    assert (_t5_bucket_np(np.arange(blk + 1, 64 * blk)) == NUM_BUCKETS - 1).all()
    x = _wrapped(2 * blk)
    tabs = [_shifted_rows(_bias_of_distance(rel_bias_heads, blk * delta + x), blk, blk, 1)
            for delta in range(3)]
    return jnp.transpose(jnp.stack(tabs), (0, 2, 1, 3)).reshape(3, blk, HEADS * blk)


def _compressed_bias_t(rel_bias_heads, seq):
    nq = seq // QRY
    x = _wrapped(2 * seq + BLK)
    v = _bias_of_distance(rel_bias_heads, x - (NSA_CMP_LEN - 1))
    tab = _shifted_rows(v, BLK, seq, NSA_CMP_STRIDE)
    tab = tab.reshape(HEADS, BLK, nq, QRY)
    return jnp.transpose(tab, (2, 1, 0, 3)).reshape(nq, BLK, HEADS * QRY)


def _nsa_constants(seq):
    n_cmp = (seq - NSA_CMP_LEN) // NSA_CMP_STRIDE + 1
    n_sel = seq // NSA_SEL_LEN
    cs = np.arange(n_cmp)[:, None] * NSA_CMP_STRIDE
    ss = np.arange(n_sel)[None, :] * NSA_SEL_LEN
    ov = np.clip(np.minimum(cs + NSA_CMP_LEN, ss + NSA_SEL_LEN) - np.maximum(cs, ss), 0, None) / NSA_CMP_LEN
    ovt = np.zeros((BLK, BLK), np.float32)
    ovt[:n_sel, :n_cmp] = ov.T
    expand = np.zeros((seq, BLK), np.float32)
    expand[np.arange(seq), np.arange(seq) // NSA_SEL_LEN] = 1.0
    return jnp.asarray(ovt), jnp.asarray(expand, BF16)


def _nsa_weights(cmp_pos, cmp_w1, cmp_w2):
    hd, hid = HEAD_DIM, NSA_CMP_HIDDEN
    w1 = cmp_w1.reshape(2, NSA_CMP_LEN, hd, hid)
    zeros = jnp.zeros((NSA_CMP_LEN, hd, hid), F32)
    w1 = jnp.concatenate([jnp.concatenate([w1[0], zeros], axis=-1),
                          jnp.concatenate([zeros, w1[1]], axis=-1)], axis=1)
    pos = jnp.concatenate([cmp_pos[0], cmp_pos[1]], axis=-1)
    z2 = jnp.zeros((hid, hd), F32)
    w2 = jnp.concatenate([jnp.concatenate([cmp_w2[0], z2], axis=-1),
                          jnp.concatenate([z2, cmp_w2[1]], axis=-1)], axis=0)
    return w1.astype(BF16), pos, w2.astype(BF16)


def kernel(x, w_in, b_in, a_conv, a_norm, d_cmp_pos, d_cmp_w1, d_cmp_w2, w_out, b_out, ln1_g, ln1_b,
           w_ff1, b_ff1, w_ff2, b_ff2, ln2_g, ln2_b, rel_bias):
    bsz, seq, _ = x.shape
    depth = w_in.shape[0]
    alpha = (2 * depth) ** 0.25
    bias_dil = _dilated_bias(rel_bias)
    btab_dsa = _toeplitz_bias_t(rel_bias[:, HEADS:2 * HEADS], QRY) * LOG2_E
    btab_nsa = _toeplitz_bias_t(rel_bias[:, 2 * HEADS:3 * HEADS], QRY) * LOG2_E
    bcmp_nsa = _compressed_bias_t(rel_bias[:, 2 * HEADS:3 * HEADS], seq) * LOG2_E
    ovt, expand = _nsa_constants(seq)

    h = x.reshape(bsz * seq, D_MODEL)
    for l in range(depth):
        w_l = _permute_columns(w_in[l]).astype(BF16)
        b_l = _permute_columns(b_in[l])[None, :]
        za, zb, zc, zd, zg = (z.reshape(bsz, seq, -1) for z in _inproj(h, w_l, b_l))
        out_a = _mlstm(za, _mlstm_gate_rows(zg), a_conv[l], a_norm[l][None, :])
        out_b = _dilated(zb, bias_dil)
        out_c = _dsa(zc, btab_dsa)
        nsa_w1, nsa_pos, nsa_w2 = _nsa_weights(d_cmp_pos[l], d_cmp_w1[l], d_cmp_w2[l])
        out_d = _nsa(zd, nsa_w1, nsa_pos, nsa_w2, btab_nsa, bcmp_nsa, ovt, expand)
        mixed = [o.reshape(bsz * seq, GROUP_WIDTH) for o in (out_a, out_b, out_c, out_d)]
        h = _outproj(alpha, h, mixed, w_out[l].astype(BF16), b_out[l][None, :],
                     ln1_g[l][None, :], ln1_b[l][None, :])
        h = _ffn(alpha, h, w_ff1[l].astype(BF16), b_ff1[l][None, :], w_ff2[l].astype(BF16),
                 b_ff2[l][None, :], ln2_g[l][None, :], ln2_b[l][None, :])
    return h.reshape(bsz, seq, D_MODEL)
```

```python
import functools
import math

import numpy as np
import jax
import jax.numpy as jnp
from jax import lax
from jax.experimental import pallas as pl
from jax.experimental.pallas import tpu as pltpu

F32 = jnp.float32
BF16 = jnp.bfloat16
I32 = jnp.int32
I16 = jnp.int16
I16_MIN = -2 ** 15

D_MODEL = 1024
N_MIXERS = 4
HEADS = 4
HEAD_DIM = D_MODEL // (N_MIXERS * HEADS)
GROUP_WIDTH = HEADS * HEAD_DIM
D_FF = 4 * D_MODEL
LN_EPS = 1e-5
NEG = -1e30

M_QK_DIM = HEAD_DIM // 2
M_CHUNK = 64
M_CONV = 4
DIL_PATTERNS = ((128, 1), (512, 4), (2048, 16))
IDX_HEADS = 4
IDX_DIM = 64
DSA_TOPK = 256
NSA_CMP_LEN = 32
NSA_CMP_STRIDE = 16
NSA_SEL_LEN = 64
NSA_TOPN = 16
NSA_WINDOW = 512
NSA_CMP_HIDDEN = 256
NSA_FORCE = 1e9
NUM_BUCKETS = 32
MAX_DISTANCE = 128

LANES = 128
BLK = 128
KEYS = 2 * BLK
QRY = KEYS
VMEM_LIMIT_BYTES = 56 * 1024 * 1024

IN_SPLITS = (
    ('a_q', HEADS * M_QK_DIM), ('a_k', HEADS * M_QK_DIM), ('a_v', GROUP_WIDTH),
    ('a_i', HEADS), ('a_f', HEADS), ('a_o', GROUP_WIDTH),
    ('b_q', GROUP_WIDTH), ('b_k', GROUP_WIDTH), ('b_v', GROUP_WIDTH),
    ('c_q', GROUP_WIDTH), ('c_k', HEAD_DIM), ('c_v', HEAD_DIM),
    ('c_iq', IDX_HEADS * IDX_DIM), ('c_ik', IDX_DIM), ('c_iw', IDX_HEADS),
    ('d_q', GROUP_WIDTH), ('d_kc', HEAD_DIM), ('d_vc', HEAD_DIM),
    ('d_ks', HEAD_DIM), ('d_vs', HEAD_DIM), ('d_kw', HEAD_DIM), ('d_vw', HEAD_DIM),
    ('d_g', 3 * HEADS),
)

GROUP_LAYOUT = (
    (('a_q', 'a_k'), ('a_v',), ('a_o',), (('a_i', HEAD_DIM),), (('a_i', M_QK_DIM),),
     (('a_f', HEAD_DIM),), (('a_f', M_QK_DIM),)),
    (('b_q',), ('b_k',), ('b_v',)),
    (('c_q',), ('c_iq',), ('c_k', 'c_v'), ('c_ik', 'c_iw')),
    (('d_q',), ('d_kc', 'd_vc'), ('d_ks', 'd_vs'), ('d_kw', 'd_vw'), ('d_g',)),
    (('a_i', 'a_f'),),
)


def _round_up(n, m):
    return -(-n // m) * m


def _projection_layout():
    offs, off = {}, 0
    for name, width in IN_SPLITS:
        offs[name] = (off, width)
        off += width
    runs, group_widths, chunk_starts = [], [], []
    for group in GROUP_LAYOUT:
        gwidth = 0
        chunk_starts.append([])
        for chunk in group:
            chunk_starts[-1].append(gwidth)
            cwidth = 0
            for entry in chunk:
                name, rep = entry if isinstance(entry, tuple) else (entry, 1)
                o, w = offs[name]
                runs.append((o, w, rep))
                cwidth += w * rep
            pad = _round_up(cwidth, LANES) - cwidth
            if pad:
                runs.append((-1, pad, 1))
            gwidth += cwidth + pad
        group_widths.append(gwidth)
    return tuple(runs), tuple(group_widths), tuple(tuple(c) for c in chunk_starts)


PROJ_RUNS, GROUP_WIDTHS, CHUNK_STARTS = _projection_layout()
PROJ_WIDTH = int(sum(GROUP_WIDTHS))


def _permute_columns(a):
    parts = []
    for o, w, r in PROJ_RUNS:
        if o < 0:
            parts.append(jnp.zeros(a.shape[:-1] + (w,), a.dtype))
        else:
            parts.append(a[..., o:o + w] if r == 1 else jnp.repeat(a[..., o:o + w], r, axis=-1))
    return jnp.concatenate(parts, axis=-1)


def _t5_bucket_np(dist):
    n = np.maximum(dist, 0)
    max_exact = NUM_BUCKETS // 2
    nf = np.maximum(n, max_exact).astype(np.float32)
    large = max_exact + (np.log(nf / max_exact) / math.log(MAX_DISTANCE / max_exact)
                         * (NUM_BUCKETS - max_exact)).astype(np.int32)
    large = np.minimum(large, NUM_BUCKETS - 1)
    return np.where(n < max_exact, n, large).astype(np.int32)


def _nt_dot(a, b, precision=None):
    return lax.dot_general(a, b, (((1,), (1,)), ((), ())), precision=precision,
                           preferred_element_type=F32)


def _dot(a, b, precision=None):
    return jnp.dot(a, b, precision=precision, preferred_element_type=F32)


def _layer_norm_rows(y, g, b):
    mu = jnp.mean(y, axis=-1, keepdims=True)
    var = jnp.mean(jnp.square(y - mu), axis=-1, keepdims=True)
    return (y - mu) * lax.rsqrt(var + LN_EPS) * g + b


def _sigmoid(x):
    return 1.0 / (1.0 + jnp.exp(-x))


def _log_sigmoid(x):
    return -(jnp.maximum(-x, 0.0) + jnp.log1p(jnp.exp(-jnp.abs(x))))


def _sortable_key(x):
    bits = pltpu.bitcast(x, I32)
    return bits ^ ((bits >> 31) & jnp.int32(0x7FFFFFFF))


def _inproj_kernel(x_ref, w_ref, b_ref, *out_refs):
    xb = x_ref[...].astype(BF16)
    off = 0
    for o_ref, width in zip(out_refs, GROUP_WIDTHS):
        o_ref[...] = _dot(xb, w_ref[:, off:off + width]) + b_ref[:, off:off + width]
        off += width


def _inproj(x2d, w, b, tm=512):
    t = x2d.shape[0]
    return pl.pallas_call(
        _inproj_kernel,
        grid=(t // tm,),
        in_specs=[pl.BlockSpec((tm, D_MODEL), lambda i: (i, 0)),
                  pl.BlockSpec((D_MODEL, PROJ_WIDTH), lambda i: (0, 0)),
                  pl.BlockSpec((1, PROJ_WIDTH), lambda i: (0, 0))],
        out_specs=[pl.BlockSpec((tm, gw), lambda i: (i, 0)) for gw in GROUP_WIDTHS],
        out_shape=[jax.ShapeDtypeStruct((t, gw), F32) for gw in GROUP_WIDTHS],
        compiler_params=pltpu.CompilerParams(dimension_semantics=("arbitrary",),
                                             vmem_limit_bytes=VMEM_LIMIT_BYTES),
        name="inproj",
    )(x2d, w, b)


def _split_terms(x, n):
    terms, rest = [], x
    for _ in range(n):
        terms.append(rest.astype(BF16))
        rest = rest - terms[-1].astype(F32)
    return terms


def _iota(shape, dim):
    return lax.broadcasted_iota(I32, shape, dim)


def _mlstm_kernel(z_ref, gt_ref, cw_ref, ng_ref, o_ref, xpad_ref):
    seq = z_ref.shape[1]
    L, DK, DV, H = M_CHUNK, M_QK_DIM, HEAD_DIM, HEADS
    assert L == DV
    wq, wv = H * DK, H * DV
    lg_dk, lg_dv = int(math.log2(DK)), int(math.log2(DV))
    _, c_v, c_o, c_i64, c_i32, c_f64, c_f32 = CHUNK_STARTS[0]
    assert c_f32 == c_f64 + wv
    xpad_ref[0:8, :] = jnp.zeros((8, 2 * wq), F32)
    xpad_ref[8:, :] = z_ref[0, :, 0:2 * wq]

    one_if = lambda cond: jnp.where(cond, 1.0, 0.0).astype(BF16)
    tri_l = one_if(_iota((L, L), 0) >= _iota((L, L), 1))
    trow = _iota((L, wv), 0)
    tri_heads = trow >= (_iota((L, wv), 1) & (L - 1))
    r_vv, c_vv = _iota((wv, wv), 0), _iota((wv, wv), 1)
    same_head = (r_vv >> lg_dv) == (c_vv >> lg_dv)
    ones_bd = one_if(same_head)
    mean_bd = jnp.where(same_head, 1.0 / DV, 0.0).astype(BF16)
    tri_u_bd = one_if(same_head & ((r_vv & (L - 1)) <= (c_vv & (L - 1))))
    state_mask = (_iota((wq, wv), 0) >> lg_dk) == (_iota((wq, wv), 1) >> lg_dv)
    eye_q = one_if(_iota((wq, wq), 0) == _iota((wq, wq), 1))
    head_of_qlane = _iota((L, wq), 1) >> lg_dk
    head_of_vlane = _iota((L, wv), 1) >> lg_dv
    row8 = _iota((8, wv), 0)
    cw = cw_ref[...]
    ng = ng_ref[...]

    def head_mean(x):
        hi_lo = _split_terms(x, 2)
        r = _dot(jnp.concatenate(hi_lo, axis=0), mean_bd)
        return r[0:L] + r[L:2 * L]

    def chunk(c, carry):
        cbd, nbd, m64, m32 = carry
        s0 = pl.multiple_of(c * L, L)
        rows = pl.ds(s0, L)
        xw = xpad_ref[pl.ds(s0, L + 8), :]
        y = sum(cw[j:j + 1, :] * xw[5 + j:5 + j + L, :] for j in range(M_CONV))
        qk = y * _sigmoid(y)
        q = qk[:, 0:wq]
        k = qk[:, wq:] * (DK ** -0.5)
        qb = q.astype(BF16)
        v = z_ref[0, rows, c_v:c_v + wv]
        i64 = z_ref[0, rows, c_i64:c_i64 + wv]
        i32 = z_ref[0, rows, c_i32:c_i32 + wq]
        gr = gt_ref[0, c]

        flog = _log_sigmoid(z_ref[0, rows, c_f64:c_f64 + wv + wq])
        bsum = _dot(tri_l, jnp.concatenate(_split_terms(flog, 3), axis=1))
        w3 = wv + wq
        ball = bsum[:, 0:w3] + bsum[:, w3:2 * w3] + bsum[:, 2 * w3:3 * w3]
        b64, b32 = ball[:, 0:wv], ball[:, wv:w3]
        fterms = [t.astype(F32) for t in _split_terms(_log_sigmoid(gr[1:2, :]), 3)]
        frows = jnp.where(row8 == 0, fterms[0], jnp.where(row8 == 1, fterms[1], jnp.where(row8 == 2, fterms[2], 0.0)))
        bparts = _dot(frows.astype(BF16), tri_u_bd)
        brow = bparts[0:1, :] + bparts[1:2, :] + bparts[2:3, :]

        dall = jnp.where(tri_heads, b64 - brow + gr[0:1, :], NEG)
        cm = i64 - b64
        for sh in (1, 2, 4, 8, 16, 32):
            cm = jnp.where(trow >= sh, jnp.maximum(cm, pltpu.roll(cm, sh, 0)), cm)
        inter = b64 + m64
        m_t = jnp.maximum(inter, b64 + cm)
        kbd = jnp.concatenate([jnp.where(head_of_qlane == h, k, 0.0) for h in range(H)], axis=0).astype(BF16)
        sc = _nt_dot(qb, kbd) * jnp.exp(dall - m_t)
        wi = jnp.exp(inter - m_t)
        vb = v.astype(BF16)
        vbd = jnp.concatenate([jnp.where(head_of_vlane == h, v, 0.0) for h in range(H)], axis=0).astype(BF16)
        pv = _dot(sc.astype(BF16), jnp.concatenate([vbd, ones_bd], axis=1))
        qst = _dot(qb, jnp.concatenate([cbd, nbd], axis=1).astype(BF16))
        num = pv[:, 0:wv] + wi * qst[:, 0:wv]
        den = pv[:, wv:] + wi * qst[:, wv:]
        hh = num / jnp.maximum(jnp.abs(den), jnp.exp(-m_t))

        og = _sigmoid(z_ref[0, rows, c_o:c_o + wv]) * hh
        dev = og - head_mean(og)
        o_ref[0, rows, :] = dev * lax.rsqrt(head_mean(dev * dev) + LN_EPS) * ng

        bl64, bl32 = b64[L - 1:L, :], b32[L - 1:L, :]
        m64_new = jnp.maximum(bl64 + m64, jnp.max(bl64 - b64 + i64, axis=0, keepdims=True))
        g32 = bl32 - b32 + i32
        m32_new = jnp.maximum(bl32 + m32, jnp.max(g32, axis=0, keepdims=True))
        kw = k * jnp.exp(g32 - m32_new)
        wc = jnp.exp(bl64 + m64 - m64_new)
        kwt = _nt_dot(eye_q, kw.astype(BF16)).astype(BF16)
        upd = _dot(kwt, jnp.concatenate([vb, jnp.ones((L, wv), BF16)], axis=1))
        cbd = wc * cbd + jnp.where(state_mask, upd[:, 0:wv], 0.0)
        nbd = wc * nbd + jnp.where(state_mask, upd[:, wv:], 0.0)
        return cbd, nbd, m64_new, m32_new

    init = (jnp.zeros((wq, wv), F32), jnp.zeros((wq, wv), F32), jnp.zeros((1, wv), F32), jnp.zeros((1, wq), F32))
    lax.fori_loop(0, seq // L, chunk, init, unroll=4)


def _mlstm(za, gates_t, conv_w, norm_g):
    b, s, wa = za.shape
    nc = s // M_CHUNK
    return pl.pallas_call(
        _mlstm_kernel,
        grid=(b,),
        in_specs=[pl.BlockSpec((1, s, wa), lambda i: (i, 0, 0)),
                  pl.BlockSpec((1, nc) + gates_t.shape[2:], lambda i: (i, 0, 0, 0)),
                  pl.BlockSpec((M_CONV, 2 * HEADS * M_QK_DIM), lambda i: (0, 0)),
                  pl.BlockSpec((1, GROUP_WIDTH), lambda i: (0, 0))],
        out_specs=pl.BlockSpec((1, s, GROUP_WIDTH), lambda i: (i, 0, 0)),
        out_shape=jax.ShapeDtypeStruct((b, s, GROUP_WIDTH), F32),
        scratch_shapes=[pltpu.VMEM((s + 8, 2 * HEADS * M_QK_DIM), F32)],
        compiler_params=pltpu.CompilerParams(dimension_semantics=("arbitrary",),
                                             vmem_limit_bytes=VMEM_LIMIT_BYTES),
        name="mlstm",
    )(za, gates_t, conv_w, norm_g)


def _mlstm_gate_rows(zg):
    b, s, _ = zg.shape
    gates = zg[:, :, 0:2 * HEADS].reshape(b, s // M_CHUNK, M_CHUNK, 2, HEADS)
    return jnp.transpose(gates, (0, 1, 3, 4, 2)).reshape(b, s // M_CHUNK, 2, HEADS * M_CHUNK)


def _dilated_kernel(q0_ref, q1_ref, k0_ref, k1_ref, v0_ref, v1_ref, bias_ref, o_ref, acc_ref, mx_ref, den_ref):
    seq = q0_ref.shape[1]
    W = BLK
    hd = HEAD_DIM
    npair = HEADS // 2
    scale = hd ** -0.5
    assert math.log2(scale).is_integer()
    nk = 2 * W
    qi = _iota((W, 2 * nk), 0)
    ki = _iota((W, 2 * nk), 1) & (nk - 1)
    j = W + qi - ki
    band = (j >= 0) & (j <= W)
    head_of_lane = _iota((nk, 2 * hd), 1) >> int(math.log2(hd))
    ones_bd = jnp.concatenate([jnp.where(head_of_lane == hh, 1.0, 0.0) for hh in range(2)], axis=0).astype(BF16)
    q_refs, k_refs, v_refs = (q0_ref, q1_ref), (k0_ref, k1_ref), (v0_ref, v1_ref)

    def block_diag(x):
        return jnp.concatenate([jnp.where(head_of_lane == hh, x, 0.0) for hh in range(2)], axis=0).astype(BF16)

    for br, (window, dil) in enumerate(DIL_PATTERNS):
        assert window // dil == W
        nb = (seq // dil) // W

        def piece(idx, _, br=br, dil=dil, nb=nb):
            n = idx % nb
            if dil == 1:
                rows_q = pl.ds(pl.multiple_of(W * n, W), W)
                rows_p = pl.ds(pl.multiple_of(W * jnp.maximum(n - 1, 0), W), W)
            else:
                r = idx // nb
                rows_q = pl.ds(r + dil * W * n, W, stride=dil)
                rows_p = pl.ds(r + dil * W * jnp.maximum(n - 1, 0), W, stride=dil)
            mask = band & (ki >= jnp.where(n > 0, 0, W))
            for pr in range(npair):
                q = (q_refs[pr][0, rows_q, :] * scale).astype(BF16)
                k2 = jnp.concatenate([k_refs[pr][0, rows_p, :], k_refs[pr][0, rows_q, :]], axis=0)
                v2 = jnp.concatenate([v_refs[pr][0, rows_p, :], v_refs[pr][0, rows_q, :]], axis=0)
                lg = jnp.where(mask, _nt_dot(q, block_diag(k2)) + bias_ref[br, pr], NEG)
                ps, mxs = [], []
                for hh in range(2):
                    sl = slice(nk * hh, nk * (hh + 1))
                    m = jnp.max(lg[:, sl], axis=-1, keepdims=True)
                    ps.append(jnp.where(mask[:, sl], jnp.exp(lg[:, sl] - m), 0.0))
                    mxs.append(jnp.broadcast_to(m, (W, hd)))
                p = jnp.concatenate(ps, axis=-1).astype(BF16)
                pv = _dot(p, jnp.concatenate([block_diag(v2), ones_bd], axis=1))
                acc_ref[br, pr, rows_q, :] = pv[:, 0:2 * hd]
                den_ref[br, pr, rows_q, :] = pv[:, 2 * hd:]
                mx_ref[br, pr, rows_q, :] = jnp.concatenate(mxs, axis=-1)
            return 0

        lax.fori_loop(0, dil * nb, piece, 0, unroll=8)

    def combine(i, _):
        rows = pl.ds(pl.multiple_of(i * W, W), W)
        outs = []
        for pr in range(npair):
            ms = [mx_ref[b, pr, rows, :] for b in range(len(DIL_PATTERNS))]
            top = functools.reduce(jnp.maximum, ms)
            es = [jnp.exp(m - top) for m in ms]
            num = sum(e * acc_ref[b, pr, rows, :] for b, e in enumerate(es))
            den = sum(e * jnp.maximum(den_ref[b, pr, rows, :], 1e-30) for b, e in enumerate(es))
            outs.append(num / den)
        o_ref[0, rows, :] = jnp.concatenate(outs, axis=-1)
        return 0

    lax.fori_loop(0, seq // W, combine, 0)


def _dilated(zb, bias):
    b, s, wb = zb.shape
    nbr = len(DIL_PATTERNS)
    pair_spec = lambda c: pl.BlockSpec((1, s, LANES), lambda i: (i, 0, c))
    return pl.pallas_call(
        _dilated_kernel,
        grid=(b,),
        in_specs=[pair_spec(c) for c in range(wb // LANES)]
                 + [pl.BlockSpec(bias.shape, lambda i: (0, 0, 0, 0))],
        out_specs=pl.BlockSpec((1, s, GROUP_WIDTH), lambda i: (i, 0, 0)),
        out_shape=jax.ShapeDtypeStruct((b, s, GROUP_WIDTH), F32),
        scratch_shapes=[pltpu.VMEM((nbr, HEADS // 2, s, LANES), F32) for _ in range(3)],
        compiler_params=pltpu.CompilerParams(dimension_semantics=("arbitrary",),
                                             vmem_limit_bytes=VMEM_LIMIT_BYTES),
        name="dilated",
    )(*([zb] * (wb // LANES)), bias)


def _head_columns_t(q):
    qt = q.T
    return jnp.concatenate([qt[HEAD_DIM * h:HEAD_DIM * (h + 1), :] for h in range(HEADS)], axis=1)


LOG2_E = math.log2(math.e)


def _scaled_query_columns_t(q):
    return (_head_columns_t(q) * (HEAD_DIM ** -0.5 * LOG2_E)).astype(BF16)


def _unstack_heads_t(per_head_t):
    halves = []
    for h in range(0, HEADS, 2):
        halves.append(jnp.concatenate([per_head_t[h], per_head_t[h + 1]], axis=0).T)
    return jnp.concatenate(halves, axis=-1)


VT_ROWS = HEAD_DIM + 16


def _online_step(lg, mask, vt1, m, acc):
    lg = jnp.where(mask, lg, NEG)
    m_new = jnp.maximum(m, jnp.max(lg, axis=0, keepdims=True))
    p = jnp.where(mask, jnp.exp2(lg - m_new), 0.0)
    acc = acc * jnp.exp2(m - m_new) + _dot(vt1, p.astype(BF16))
    return m_new, acc


def _online_result(acc):
    return acc[0:HEAD_DIM, :] / jnp.maximum(acc[HEAD_DIM:HEAD_DIM + 1, :], 1e-30)


def _values_with_ones_t(z_rows):
    vt = z_rows.T[HEAD_DIM:2 * HEAD_DIM, :]
    return jnp.concatenate([vt, jnp.ones((VT_ROWS - HEAD_DIM, vt.shape[1]), F32)], axis=0).astype(BF16)


def _online_init(nq=BLK):
    return (tuple(jnp.full((1, nq), NEG, F32) for _ in range(HEADS)),
            tuple(jnp.zeros((VT_ROWS, nq), F32) for _ in range(HEADS)))


def _dsa_kernel(z_ref, btab_ref, o_ref, vt_ref, key_ref, hi_ref, lo_ref, lom_ref):
    i = pl.program_id(1)
    seq = z_ref.shape[1]
    nkb = seq // BLK
    nq = QRY
    hd = HEAD_DIM
    topk = min(DSA_TOPK, seq // 4)
    t0 = pl.multiple_of(i * nq, nq)

    assert nq == KEYS
    npair = i + 1
    c_q, c_iq, c_kv, c_ikw = CHUNK_STARTS[2]

    @pl.when(i == 0)
    def _():
        for kb in range(nkb):
            vt_ref[kb // 2, :, (kb % 2) * BLK:(kb % 2 + 1) * BLK] = _values_with_ones_t(
                z_ref[0, kb * BLK:(kb + 1) * BLK, c_kv:c_kv + 2 * hd])

    zq = z_ref[0, pl.ds(t0, nq), :]
    cq = zq[:, c_q:c_q + HEADS * hd]
    ciq = _head_columns_t(zq[:, c_iq:c_iq + IDX_HEADS * IDX_DIM]).astype(BF16)
    iw = (zq[:, c_ikw:c_ikw + LANES].T[IDX_DIM:IDX_DIM + IDX_HEADS, :]
          * ((IDX_HEADS * IDX_DIM) ** -0.5))
    s_loc = lax.broadcasted_iota(I32, (KEYS, nq), 0)
    t_glob = t0 + lax.broadcasted_iota(I32, (KEYS, nq), 1)

    def score_block(kp, _):
        r0 = pl.multiple_of(kp * KEYS, KEYS)
        ik = z_ref[0, pl.ds(r0, KEYS), c_ikw:c_ikw + IDX_DIM].astype(BF16)
        rel = _dot(ik, ciq)
        sc = jnp.zeros((KEYS, nq), F32)
        for h in range(IDX_HEADS):
            sc = sc + jnp.maximum(rel[:, nq * h:nq * (h + 1)], 0.0) * iw[h:h + 1, :]
        sc = jnp.where(r0 + s_loc <= t_glob, sc, NEG)
        key = _sortable_key(sc)
        key_ref[kp] = key
        hi_ref[kp] = (key >> 16).astype(I16)
        lo_ref[kp] = ((key & 0xFFFF) - 2 ** 15).astype(I16)
        return 0

    lax.fori_loop(0, npair // 2, lambda t, c: score_block(2 * t + 1, score_block(2 * t, c)), 0)
    lax.cond(npair % 2 == 1, lambda: score_block(npair - 1, 0), lambda: 0)

    def count(ref, pred, pairs):
        dt = ref.dtype
        rows = 8 * 4 // dt.itemsize
        def body(kp, acc):
            hit = jnp.where(pred(kp, ref[kp]), jnp.ones((), dt), jnp.zeros((), dt))
            hit = hit.reshape(KEYS // rows, rows, nq)
            parts = [hit[n] for n in range(KEYS // rows)]
            while len(parts) > 1:
                parts = [a + b for a, b in zip(parts[0::2], parts[1::2])]
            return acc + parts[0]
        acc = jnp.zeros((rows, nq), dt)
        for kp in range(pairs):
            acc = body(kp, acc)
        return jnp.sum(acc.astype(I32), axis=0, keepdims=True)

    def threshold(pairs):
        def half_search(ref, k):
            def bit(it, thr):
                cand = thr + lax.shift_left(jnp.int32(1), 15 - it)
                c = count(ref, lambda kp, half: half >= cand.astype(I16), pairs)
                return jnp.where(c >= k, cand, thr)
            return lax.fori_loop(0, 16, bit, jnp.full((1, nq), I16_MIN, I32))

        def run(_):
            thr_hi = half_search(hi_ref, topk)
            thr_hi16 = thr_hi.astype(I16)
            above_hi = count(hi_ref, lambda kp, half: half > thr_hi16, pairs)
            for kp in range(pairs):
                lom_ref[kp] = jnp.where(hi_ref[kp] == thr_hi16, lo_ref[kp], jnp.int16(I16_MIN))
            thr_lo = half_search(lom_ref, topk - above_hi)
            thr = (thr_hi << 16) | (thr_lo + 2 ** 15)
            thr_lo16 = thr_lo.astype(I16)
            n_gt = above_hi + count(lom_ref, lambda kp, half: half > thr_lo16, pairs)
            need = (topk - n_gt).astype(F32)
            earlier = jnp.zeros((1, nq), F32)
            for kp in range(pairs):
                tie = key_ref[kp] == thr
                tie01 = jnp.where(tie, 1.0, 0.0)
                rank = _dot(below, tie01.astype(BF16)) + earlier
                lom_ref[kp] = jnp.where(tie & (rank < need), 1, 0).astype(I16)
                earlier = earlier + jnp.sum(tie01, axis=0, keepdims=True)
            return thr
        return run

    below = jnp.where(_iota((KEYS, KEYS), 1) < _iota((KEYS, KEYS), 0), 1.0, 0.0).astype(BF16)
    thr = lax.switch(npair - 1, [threshold(p) for p in range(1, seq // KEYS + 1)], 0)

    qs = _scaled_query_columns_t(cq)

    def attend(kp, carry):
        ms, accs = carry
        r0 = pl.multiple_of(kp * KEYS, KEYS)
        kblk = z_ref[0, pl.ds(r0, KEYS), c_kv:c_kv + hd].astype(BF16)
        lg = _dot(kblk, qs) + btab_ref[jnp.minimum(i - kp, 2)]
        key = key_ref[kp]
        s_glob = r0 + s_loc
        mask = ((key > thr) | (lom_ref[kp].astype(I32) != 0)) & (s_glob <= t_glob)
        vt1 = vt_ref[kp]
        out = [_online_step(lg[:, nq * h:nq * (h + 1)], mask, vt1, ms[h], accs[h]) for h in range(HEADS)]
        return tuple(o[0] for o in out), tuple(o[1] for o in out)

    carry = lax.fori_loop(0, npair // 2, lambda t, c: attend(2 * t + 1, attend(2 * t, c)), _online_init(nq))
    ms, accs = lax.cond(npair % 2 == 1, lambda c: attend(npair - 1, c), lambda c: c, carry)
    o_ref[0] = _unstack_heads_t([_online_result(accs[h]) for h in range(HEADS)])


def _dsa(zc, btab):
    b, s, wc = zc.shape
    return pl.pallas_call(
        _dsa_kernel,
        grid=(b, s // QRY),
        in_specs=[pl.BlockSpec((1, s, wc), lambda bi, i: (bi, 0, 0)),
                  pl.BlockSpec(btab.shape, lambda bi, i: (0, 0, 0))],
        out_specs=pl.BlockSpec((1, QRY, GROUP_WIDTH), lambda bi, i: (bi, i, 0)),
        out_shape=jax.ShapeDtypeStruct((b, s, GROUP_WIDTH), F32),
        scratch_shapes=[pltpu.VMEM((s // KEYS, VT_ROWS, KEYS), BF16), pltpu.VMEM((s // KEYS, KEYS, QRY), I32)]
                       + [pltpu.VMEM((s // KEYS, KEYS, QRY), I16)] * 3,
        compiler_params=pltpu.CompilerParams(dimension_semantics=("arbitrary", "arbitrary"),
                                             vmem_limit_bytes=VMEM_LIMIT_BYTES),
        name="dsa",
    )(zc, btab)


def _nsa_kernel(z_ref, zc_ref, w1_ref, pos_ref, w2_ref, bsel_ref, bcmp_ref, ovt_ref, exp_ref, o_ref,
                vst_ref, vwt_ref, cmp_ref, cmpt_ref):
    i = pl.program_id(1)
    seq = z_ref.shape[1]
    nkb = seq // BLK
    hd = HEAD_DIM
    c_q, _, c_s, c_w, c_g = CHUNK_STARTS[3]
    n_cmp = (seq - NSA_CMP_LEN) // NSA_CMP_STRIDE + 1
    n_sel = seq // NSA_SEL_LEN
    topn = min(NSA_TOPN, n_sel)
    half = NSA_CMP_LEN // 2
    assert half == NSA_CMP_STRIDE and n_cmp + 1 == seq // NSA_CMP_STRIDE == BLK and n_sel <= BLK
    nq = QRY
    assert nq == KEYS
    t0 = pl.multiple_of(i * nq, nq)
    hi = lax.Precision.HIGHEST

    @pl.when(i == 0)
    def _():
        for kb in range(nkb):
            rows = slice(kb * BLK, (kb + 1) * BLK)
            cols = slice((kb % 2) * BLK, (kb % 2 + 1) * BLK)
            vst_ref[kb // 2, :, cols] = _values_with_ones_t(z_ref[0, rows, c_s:c_s + 2 * hd])
            vwt_ref[kb // 2, :, cols] = _values_with_ones_t(z_ref[0, rows, c_w:c_w + 2 * hd])
        first = jnp.zeros((BLK, 2 * NSA_CMP_HIDDEN), F32)
        second = jnp.zeros((BLK, 2 * NSA_CMP_HIDDEN), F32)
        for j in range(half):
            xj = zc_ref[0, pl.ds(j, BLK, stride=NSA_CMP_STRIDE), :]
            first = first + _dot((xj + pos_ref[j:j + 1, :]).astype(BF16), w1_ref[j])
            second = second + _dot((xj + pos_ref[half + j:half + j + 1, :]).astype(BF16), w1_ref[half + j])
        hid = first + pltpu.roll(second, BLK - 1, 0)
        hid = hid * _sigmoid(hid)
        cmp = _dot(hid.astype(BF16), w2_ref[...])
        cmp_ref[...] = cmp
        cmpt_ref[...] = cmp.T

    zq = z_ref[0, pl.ds(t0, nq), :]
    qs = _scaled_query_columns_t(zq[:, c_q:c_q + HEADS * hd])
    gates = _sigmoid(zq[:, c_g:c_g + LANES].T[0:16, :])
    row = lax.broadcasted_iota(I32, (BLK, nq), 0)
    t_glob = t0 + lax.broadcasted_iota(I32, (BLK, nq), 1)

    kcmp = cmp_ref[:, 0:hd].astype(BF16)
    vcmpt = cmpt_ref[hd:2 * hd, :].astype(BF16)
    lgc = _dot(kcmp, qs) + bcmp_ref[0]
    mask_c = (t_glob - (row * NSA_CMP_STRIDE + NSA_CMP_LEN - 1) >= 0) & (row < n_cmp)
    o_cmp, psum = [], jnp.zeros((BLK, nq), F32)
    for h in range(HEADS):
        lg = jnp.where(mask_c, lgc[:, nq * h:nq * (h + 1)], NEG)
        m = jnp.max(lg, axis=0, keepdims=True)
        p = jnp.where(mask_c, jnp.exp2(lg - m), 0.0)
        p = p / jnp.maximum(jnp.sum(p, axis=0, keepdims=True), 1e-30)
        o_cmp.append(_dot(vcmpt, p.astype(BF16)))
        psum = psum + p

    imp = _dot(ovt_ref[...], psum, precision=hi)
    cur = t_glob >> int(math.log2(NSA_SEL_LEN))
    forced = (row == 0) | (row == cur) | (row == cur - 1)
    imp = jnp.where(forced, NSA_FORCE, imp)
    imp = jnp.where(row * NSA_SEL_LEN <= t_glob, imp, NEG)
    imp = imp[0:n_sel, :]
    jrow = lax.broadcasted_iota(I32, (n_sel, nq), 0)
    rank = jnp.zeros((n_sel, nq), I32)
    for jp in range(n_sel):
        other = imp[jp:jp + 1, :]
        rank = rank + ((other > imp) | ((other == imp) & (jp < jrow))).astype(I32)
    chosen = jnp.where(rank < topn, 1.0, 0.0)
    chosen = jnp.concatenate([chosen, jnp.zeros((BLK - n_sel, nq), F32)], axis=0).astype(BF16)

    s_loc = lax.broadcasted_iota(I32, (KEYS, nq), 0)
    t_keys = t0 + lax.broadcasted_iota(I32, (KEYS, nq), 1)

    def attend(vt_ref, lanes, mask_fn):
        def body(kp, carry):
            ms, accs = carry
            r0 = pl.multiple_of(kp * KEYS, KEYS)
            kblk = z_ref[0, pl.ds(r0, KEYS), lanes].astype(BF16)
            lg = _dot(kblk, qs) + bsel_ref[jnp.minimum(i - kp, 2)]
            mask = mask_fn(r0)
            vt1 = vt_ref[kp]
            out = [_online_step(lg[:, nq * h:nq * (h + 1)], mask, vt1, ms[h], accs[h]) for h in range(HEADS)]
            return tuple(o[0] for o in out), tuple(o[1] for o in out)
        return body

    def mask_sel(r0):
        picked = _dot(exp_ref[pl.ds(r0, KEYS), :], chosen) > 0.5
        return picked & (r0 + s_loc <= t_keys)

    def mask_win(r0):
        dist = t_keys - (r0 + s_loc)
        return (dist >= 0) & (dist < NSA_WINDOW)

    sel_body = attend(vst_ref, slice(c_s, c_s + hd), mask_sel)
    win_body = attend(vwt_ref, slice(c_w, c_w + hd), mask_win)
    first_w = jnp.maximum(i - NSA_WINDOW // KEYS, 0)
    sel_carry = lax.fori_loop(0, first_w, sel_body, _online_init(nq))
    (_, acc_s), (_, acc_w) = lax.fori_loop(
        first_w, i + 1, lambda kp, c: (sel_body(kp, c[0]), win_body(kp, c[1])), (sel_carry, _online_init(nq)))

    outs = []
    for h in range(HEADS):
        outs.append(gates[3 * h:3 * h + 1, :] * o_cmp[h] + gates[3 * h + 1:3 * h + 2, :] * _online_result(acc_s[h])
                    + gates[3 * h + 2:3 * h + 3, :] * _online_result(acc_w[h]))
    o_ref[0] = _unstack_heads_t(outs)


def _nsa(zd, w1, pos, w2, bsel, bcmp, ovt, expand):
    b, s, wd = zd.shape
    full = lambda a: pl.BlockSpec(a.shape, lambda bi, i: (0,) * a.ndim)
    return pl.pallas_call(
        _nsa_kernel,
        grid=(b, s // QRY),
        in_specs=[pl.BlockSpec((1, s, wd), lambda bi, i: (bi, 0, 0)),
                  pl.BlockSpec((1, s, LANES), lambda bi, i: (bi, 0, CHUNK_STARTS[3][1] // LANES)),
                  full(w1), full(pos), full(w2), full(bsel),
                  pl.BlockSpec((1, BLK, HEADS * QRY), lambda bi, i: (i, 0, 0)),
                  full(ovt), full(expand)],
        out_specs=pl.BlockSpec((1, QRY, GROUP_WIDTH), lambda bi, i: (bi, i, 0)),
        out_shape=jax.ShapeDtypeStruct((b, s, GROUP_WIDTH), F32),
        scratch_shapes=[pltpu.VMEM((s // KEYS, VT_ROWS, KEYS), BF16), pltpu.VMEM((s // KEYS, VT_ROWS, KEYS), BF16),
                        pltpu.VMEM((BLK, BLK), F32), pltpu.VMEM((BLK, BLK), F32)],
        compiler_params=pltpu.CompilerParams(dimension_semantics=("arbitrary", "arbitrary"),
                                             vmem_limit_bytes=VMEM_LIMIT_BYTES),
        name="nsa",
    )(zd, zd, w1, pos, w2, bsel, bcmp, ovt, expand)


def _outproj_kernel(alpha, x_ref, a_ref, b_ref, c_ref, d_ref, w_ref, bo_ref, g_ref, beta_ref, o_ref):
    acc = bo_ref[...] + _dot(a_ref[...].astype(BF16), w_ref[0:GROUP_WIDTH, :])
    for n, m_ref in enumerate((b_ref, c_ref, d_ref), start=1):
        acc = acc + _dot(m_ref[...].astype(BF16), w_ref[n * GROUP_WIDTH:(n + 1) * GROUP_WIDTH, :])
    o_ref[...] = _layer_norm_rows(alpha * x_ref[...] + acc, g_ref[...], beta_ref[...])


def _outproj(alpha, x2d, mixed, w, bo, g, beta, tm=512):
    t = x2d.shape[0]
    row_spec = lambda width: pl.BlockSpec((tm, width), lambda i: (i, 0))
    const = lambda a: pl.BlockSpec(a.shape, lambda i: (0, 0))
    return pl.pallas_call(
        functools.partial(_outproj_kernel, alpha),
        grid=(t // tm,),
        in_specs=[row_spec(D_MODEL)] + [row_spec(GROUP_WIDTH)] * N_MIXERS
                 + [const(w), const(bo), const(g), const(beta)],
        out_specs=row_spec(D_MODEL),
        out_shape=jax.ShapeDtypeStruct((t, D_MODEL), F32),
        compiler_params=pltpu.CompilerParams(dimension_semantics=("arbitrary",),
                                             vmem_limit_bytes=VMEM_LIMIT_BYTES),
        name="outproj_ln",
    )(x2d, *mixed, w, bo, g, beta)


def _ffn_kernel(alpha, x_ref, w1_ref, b1_ref, w2_ref, b2_ref, g_ref, beta_ref, o_ref, xb_ref, acc_ref):
    j = pl.program_id(1)

    @pl.when(j == 0)
    def _():
        xb_ref[...] = x_ref[...].astype(BF16)
        acc_ref[...] = jnp.zeros_like(acc_ref)

    hdn = jnp.maximum(_dot(xb_ref[...], w1_ref[...]) + b1_ref[...], 0.0)
    acc_ref[...] += _dot(jnp.square(hdn).astype(BF16), w2_ref[...])

    @pl.when(j == pl.num_programs(1) - 1)
    def _():
        y = alpha * x_ref[...] + (acc_ref[...] + b2_ref[...])
        o_ref[...] = _layer_norm_rows(y, g_ref[...], beta_ref[...])


def _ffn(alpha, x2d, w1, b1, w2, b2, g, beta, tm=1024, tf=1024):
    t = x2d.shape[0]
    return pl.pallas_call(
        functools.partial(_ffn_kernel, alpha),
        grid=(t // tm, D_FF // tf),
        in_specs=[pl.BlockSpec((tm, D_MODEL), lambda i, j: (i, 0)),
                  pl.BlockSpec((D_MODEL, tf), lambda i, j: (0, j)),
                  pl.BlockSpec((1, tf), lambda i, j: (0, j)),
                  pl.BlockSpec((tf, D_MODEL), lambda i, j: (j, 0)),
                  pl.BlockSpec((1, D_MODEL), lambda i, j: (0, 0)),
                  pl.BlockSpec((1, D_MODEL), lambda i, j: (0, 0)),
                  pl.BlockSpec((1, D_MODEL), lambda i, j: (0, 0))],
        out_specs=pl.BlockSpec((tm, D_MODEL), lambda i, j: (i, 0)),
        out_shape=jax.ShapeDtypeStruct((t, D_MODEL), F32),
        scratch_shapes=[pltpu.VMEM((tm, D_MODEL), BF16), pltpu.VMEM((tm, D_MODEL), F32)],
        compiler_params=pltpu.CompilerParams(dimension_semantics=("arbitrary", "arbitrary"),
                                             vmem_limit_bytes=VMEM_LIMIT_BYTES),
        name="ffn_ln",
    )(x2d, w1, b1, w2, b2, g, beta)


def _bias_of_distance(rel_bias_heads, dist):
    onehot = np.eye(NUM_BUCKETS, dtype=np.float32)[_t5_bucket_np(np.asarray(dist))]
    return jnp.dot(rel_bias_heads.T, jnp.asarray(onehot.T), precision=lax.Precision.HIGHEST)


def _shifted_rows(v, n_rows, n_cols, step):
    period = v.shape[-1]
    assert n_cols <= period - step
    flat = jnp.tile(v, (1,) * (v.ndim - 1) + (n_rows,))[..., :n_rows * (period - step)]
    return flat.reshape(v.shape[:-1] + (n_rows, period - step))[..., :n_cols]


def _wrapped(period):
    idx = np.arange(period)
    return np.where(idx < period // 2, idx, idx - period)


def _dilated_bias(rel_bias):
    x = _wrapped(4 * BLK)
    tabs = [_shifted_rows(_bias_of_distance(rel_bias[:, 0:HEADS], (BLK - x) * dil), BLK, 2 * BLK, 1)
            for _, dil in DIL_PATTERNS]
    tabs = jnp.stack(tabs).reshape(len(DIL_PATTERNS), HEADS // 2, 2, BLK, 2 * BLK)
    return jnp.transpose(tabs, (0, 1, 3, 2, 4)).reshape(len(DIL_PATTERNS), HEADS // 2, BLK, 4 * BLK)


def _toeplitz_bias_t(rel_bias_heads, blk=BLK):
    assert (_t5_bucket_np(np.arange(blk + 1, 64 * blk)) == NUM_BUCKETS - 1).all()
    x = _wrapped(2 * blk)
    tabs = [_shifted_rows(_bias_of_distance(rel_bias_heads, blk * delta + x), blk, blk, 1)
            for delta in range(3)]
    return jnp.transpose(jnp.stack(tabs), (0, 2, 1, 3)).reshape(3, blk, HEADS * blk)


def _compressed_bias_t(rel_bias_heads, seq):
    nq = seq // QRY
    x = _wrapped(2 * seq + BLK)
    v = _bias_of_distance(rel_bias_heads, x - (NSA_CMP_LEN - 1))
    tab = _shifted_rows(v, BLK, seq, NSA_CMP_STRIDE)
    tab = tab.reshape(HEADS, BLK, nq, QRY)
    return jnp.transpose(tab, (2, 1, 0, 3)).reshape(nq, BLK, HEADS * QRY)


def _nsa_constants(seq):
    n_cmp = (seq - NSA_CMP_LEN) // NSA_CMP_STRIDE + 1
    n_sel = seq // NSA_SEL_LEN
    cs = np.arange(n_cmp)[:, None] * NSA_CMP_STRIDE
    ss = np.arange(n_sel)[None, :] * NSA_SEL_LEN
    ov = np.clip(np.minimum(cs + NSA_CMP_LEN, ss + NSA_SEL_LEN) - np.maximum(cs, ss), 0, None) / NSA_CMP_LEN
    ovt = np.zeros((BLK, BLK), np.float32)
    ovt[:n_sel, :n_cmp] = ov.T
    expand = np.zeros((seq, BLK), np.float32)
    expand[np.arange(seq), np.arange(seq) // NSA_SEL_LEN] = 1.0
    return jnp.asarray(ovt), jnp.asarray(expand, BF16)


def _nsa_weights(cmp_pos, cmp_w1, cmp_w2):
    hd, hid = HEAD_DIM, NSA_CMP_HIDDEN
    w1 = cmp_w1.reshape(2, NSA_CMP_LEN, hd, hid)
    zeros = jnp.zeros((NSA_CMP_LEN, hd, hid), F32)
    w1 = jnp.concatenate([jnp.concatenate([w1[0], zeros], axis=-1),
                          jnp.concatenate([zeros, w1[1]], axis=-1)], axis=1)
    pos = jnp.concatenate([cmp_pos[0], cmp_pos[1]], axis=-1)
    z2 = jnp.zeros((hid, hd), F32)
    w2 = jnp.concatenate([jnp.concatenate([cmp_w2[0], z2], axis=-1),
                          jnp.concatenate([z2, cmp_w2[1]], axis=-1)], axis=0)
    return w1.astype(BF16), pos, w2.astype(BF16)


def kernel(x, w_in, b_in, a_conv, a_norm, d_cmp_pos, d_cmp_w1, d_cmp_w2, w_out, b_out, ln1_g, ln1_b,
           w_ff1, b_ff1, w_ff2, b_ff2, ln2_g, ln2_b, rel_bias):
    bsz, seq, _ = x.shape
    depth = w_in.shape[0]
    alpha = (2 * depth) ** 0.25
    bias_dil = _dilated_bias(rel_bias)
    btab_dsa = _toeplitz_bias_t(rel_bias[:, HEADS:2 * HEADS], QRY) * LOG2_E
    btab_nsa = _toeplitz_bias_t(rel_bias[:, 2 * HEADS:3 * HEADS], QRY) * LOG2_E
    bcmp_nsa = _compressed_bias_t(rel_bias[:, 2 * HEADS:3 * HEADS], seq) * LOG2_E
    ovt, expand = _nsa_constants(seq)

    h = x.reshape(bsz * seq, D_MODEL)
    for l in range(depth):
        w_l = _permute_columns(w_in[l]).astype(BF16)
        b_l = _permute_columns(b_in[l])[None, :]
        za, zb, zc, zd, zg = (z.reshape(bsz, seq, -1) for z in _inproj(h, w_l, b_l))
        out_a = _mlstm(za, _mlstm_gate_rows(zg), a_conv[l], a_norm[l][None, :])
        out_b = _dilated(zb, bias_dil)
        out_c = _dsa(zc, btab_dsa)
        nsa_w1, nsa_pos, nsa_w2 = _nsa_weights(d_cmp_pos[l], d_cmp_w1[l], d_cmp_w2[l])
        out_d = _nsa(zd, nsa_w1, nsa_pos, nsa_w2, btab_nsa, bcmp_nsa, ovt, expand)
        mixed = [o.reshape(bsz * seq, GROUP_WIDTH) for o in (out_a, out_b, out_c, out_d)]
        h = _outproj(alpha, h, mixed, w_out[l].astype(BF16), b_out[l][None, :],
                     ln1_g[l][None, :], ln1_b[l][None, :])
        h = _ffn(alpha, h, w_ff1[l].astype(BF16), b_ff1[l][None, :], w_ff2[l].astype(BF16),
                 b_ff2[l][None, :], ln2_g[l][None, :], ln2_b[l][None, :])
    return h.reshape(bsz, seq, D_MODEL)
```

```python
import functools
import math

import numpy as np
import jax
import jax.numpy as jnp
from jax import lax
from jax.experimental import pallas as pl
from jax.experimental.pallas import tpu as pltpu

F32 = jnp.float32
BF16 = jnp.bfloat16
I32 = jnp.int32
I16 = jnp.int16
I16_MIN = -2 ** 15

D_MODEL = 1024
N_MIXERS = 4
HEADS = 4
HEAD_DIM = D_MODEL // (N_MIXERS * HEADS)
GROUP_WIDTH = HEADS * HEAD_DIM
D_FF = 4 * D_MODEL
LN_EPS = 1e-5
NEG = -1e30

M_QK_DIM = HEAD_DIM // 2
M_CHUNK = 64
M_CONV = 4
DIL_PATTERNS = ((128, 1), (512, 4), (2048, 16))
IDX_HEADS = 4
IDX_DIM = 64
DSA_TOPK = 256
NSA_CMP_LEN = 32
NSA_CMP_STRIDE = 16
NSA_SEL_LEN = 64
NSA_TOPN = 16
NSA_WINDOW = 512
NSA_CMP_HIDDEN = 256
NSA_FORCE = 1e9
NUM_BUCKETS = 32
MAX_DISTANCE = 128

LANES = 128
BLK = 128
KEYS = 2 * BLK
QRY = KEYS
VMEM_LIMIT_BYTES = 56 * 1024 * 1024

IN_SPLITS = (
    ('a_q', HEADS * M_QK_DIM), ('a_k', HEADS * M_QK_DIM), ('a_v', GROUP_WIDTH),
    ('a_i', HEADS), ('a_f', HEADS), ('a_o', GROUP_WIDTH),
    ('b_q', GROUP_WIDTH), ('b_k', GROUP_WIDTH), ('b_v', GROUP_WIDTH),
    ('c_q', GROUP_WIDTH), ('c_k', HEAD_DIM), ('c_v', HEAD_DIM),
    ('c_iq', IDX_HEADS * IDX_DIM), ('c_ik', IDX_DIM), ('c_iw', IDX_HEADS),
    ('d_q', GROUP_WIDTH), ('d_kc', HEAD_DIM), ('d_vc', HEAD_DIM),
    ('d_ks', HEAD_DIM), ('d_vs', HEAD_DIM), ('d_kw', HEAD_DIM), ('d_vw', HEAD_DIM),
    ('d_g', 3 * HEADS),
)

GROUP_LAYOUT = (
    (('a_q', 'a_k'), ('a_v',), ('a_o',), (('a_i', HEAD_DIM),), (('a_i', M_QK_DIM),),
     (('a_f', HEAD_DIM),), (('a_f', M_QK_DIM),)),
    (('b_q',), ('b_k',), ('b_v',)),
    (('c_q',), ('c_iq',), ('c_k', 'c_v'), ('c_ik', 'c_iw')),
    (('d_q',), ('d_kc', 'd_vc'), ('d_ks', 'd_vs'), ('d_kw', 'd_vw'), ('d_g',)),
    (('a_i', 'a_f'),),
)


def _round_up(n, m):
    return -(-n // m) * m


def _projection_layout():
    offs, off = {}, 0
    for name, width in IN_SPLITS:
        offs[name] = (off, width)
        off += width
    runs, group_widths, chunk_starts = [], [], []
    for group in GROUP_LAYOUT:
        gwidth = 0
        chunk_starts.append([])
        for chunk in group:
            chunk_starts[-1].append(gwidth)
            cwidth = 0
            for entry in chunk:
                name, rep = entry if isinstance(entry, tuple) else (entry, 1)
                o, w = offs[name]
                runs.append((o, w, rep))
                cwidth += w * rep
            pad = _round_up(cwidth, LANES) - cwidth
            if pad:
                runs.append((-1, pad, 1))
            gwidth += cwidth + pad
        group_widths.append(gwidth)
    return tuple(runs), tuple(group_widths), tuple(tuple(c) for c in chunk_starts)


PROJ_RUNS, GROUP_WIDTHS, CHUNK_STARTS = _projection_layout()
PROJ_WIDTH = int(sum(GROUP_WIDTHS))


def _permute_columns(a):
    parts = []
    for o, w, r in PROJ_RUNS:
        if o < 0:
            parts.append(jnp.zeros(a.shape[:-1] + (w,), a.dtype))
        else:
            parts.append(a[..., o:o + w] if r == 1 else jnp.repeat(a[..., o:o + w], r, axis=-1))
    return jnp.concatenate(parts, axis=-1)


def _t5_bucket_np(dist):
    n = np.maximum(dist, 0)
    max_exact = NUM_BUCKETS // 2
    nf = np.maximum(n, max_exact).astype(np.float32)
    large = max_exact + (np.log(nf / max_exact) / math.log(MAX_DISTANCE / max_exact)
                         * (NUM_BUCKETS - max_exact)).astype(np.int32)
    large = np.minimum(large, NUM_BUCKETS - 1)
    return np.where(n < max_exact, n, large).astype(np.int32)


def _nt_dot(a, b, precision=None):
    return lax.dot_general(a, b, (((1,), (1,)), ((), ())), precision=precision,
                           preferred_element_type=F32)


def _dot(a, b, precision=None):
    return jnp.dot(a, b, precision=precision, preferred_element_type=F32)


def _layer_norm_rows(y, g, b):
    mu = jnp.mean(y, axis=-1, keepdims=True)
    var = jnp.mean(jnp.square(y - mu), axis=-1, keepdims=True)
    return (y - mu) * lax.rsqrt(var + LN_EPS) * g + b


def _sigmoid(x):
    return 1.0 / (1.0 + jnp.exp(-x))


def _log_sigmoid(x):
    return -(jnp.maximum(-x, 0.0) + jnp.log1p(jnp.exp(-jnp.abs(x))))


def _sortable_key(x):
    bits = pltpu.bitcast(x, I32)
    return bits ^ ((bits >> 31) & jnp.int32(0x7FFFFFFF))


def _inproj_kernel(x_ref, w_ref, b_ref, *out_refs):
    xb = x_ref[...].astype(BF16)
    off = 0
    for o_ref, width in zip(out_refs, GROUP_WIDTHS):
        o_ref[...] = _dot(xb, w_ref[:, off:off + width]) + b_ref[:, off:off + width]
        off += width


def _inproj(x2d, w, b, tm=512):
    t = x2d.shape[0]
    return pl.pallas_call(
        _inproj_kernel,
        grid=(t // tm,),
        in_specs=[pl.BlockSpec((tm, D_MODEL), lambda i: (i, 0)),
                  pl.BlockSpec((D_MODEL, PROJ_WIDTH), lambda i: (0, 0)),
                  pl.BlockSpec((1, PROJ_WIDTH), lambda i: (0, 0))],
        out_specs=[pl.BlockSpec((tm, gw), lambda i: (i, 0)) for gw in GROUP_WIDTHS],
        out_shape=[jax.ShapeDtypeStruct((t, gw), F32) for gw in GROUP_WIDTHS],
        compiler_params=pltpu.CompilerParams(dimension_semantics=("arbitrary",),
                                             vmem_limit_bytes=VMEM_LIMIT_BYTES),
        name="inproj",
    )(x2d, w, b)


def _split_terms(x, n):
    terms, rest = [], x
    for _ in range(n):
        terms.append(rest.astype(BF16))
        rest = rest - terms[-1].astype(F32)
    return terms


def _iota(shape, dim):
    return lax.broadcasted_iota(I32, shape, dim)


def _mlstm_kernel(z_ref, gt_ref, cw_ref, ng_ref, o_ref, xpad_ref):
    seq = z_ref.shape[1]
    L, DK, DV, H = M_CHUNK, M_QK_DIM, HEAD_DIM, HEADS
    assert L == DV
    wq, wv = H * DK, H * DV
    lg_dk, lg_dv = int(math.log2(DK)), int(math.log2(DV))
    _, c_v, c_o, c_i64, c_i32, c_f64, c_f32 = CHUNK_STARTS[0]
    assert c_f32 == c_f64 + wv
    xpad_ref[0:8, :] = jnp.zeros((8, 2 * wq), F32)
    xpad_ref[8:, :] = z_ref[0, :, 0:2 * wq]

    one_if = lambda cond: jnp.where(cond, 1.0, 0.0).astype(BF16)
    tri_l = one_if(_iota((L, L), 0) >= _iota((L, L), 1))
    trow = _iota((L, wv), 0)
    tri_heads = trow >= (_iota((L, wv), 1) & (L - 1))
    r_vv, c_vv = _iota((wv, wv), 0), _iota((wv, wv), 1)
    same_head = (r_vv >> lg_dv) == (c_vv >> lg_dv)
    ones_bd = one_if(same_head)
    mean_bd = jnp.where(same_head, 1.0 / DV, 0.0).astype(BF16)
    tri_u_bd = one_if(same_head & ((r_vv & (L - 1)) <= (c_vv & (L - 1))))
    state_mask = (_iota((wq, wv), 0) >> lg_dk) == (_iota((wq, wv), 1) >> lg_dv)
    eye_q = one_if(_iota((wq, wq), 0) == _iota((wq, wq), 1))
    head_of_qlane = _iota((L, wq), 1) >> lg_dk
    head_of_vlane = _iota((L, wv), 1) >> lg_dv
    row8 = _iota((8, wv), 0)
    cw = cw_ref[...]
    ng = ng_ref[...]

    def head_mean(x):
        hi_lo = _split_terms(x, 2)
        r = _dot(jnp.concatenate(hi_lo, axis=0), mean_bd)
        return r[0:L] + r[L:2 * L]

    def chunk(c, carry):
        cbd, nbd, m64, m32 = carry
        s0 = pl.multiple_of(c * L, L)
        rows = pl.ds(s0, L)
        xw = xpad_ref[pl.ds(s0, L + 8), :]
        y = sum(cw[j:j + 1, :] * xw[5 + j:5 + j + L, :] for j in range(M_CONV))
        qk = y * _sigmoid(y)
        q = qk[:, 0:wq]
        k = qk[:, wq:] * (DK ** -0.5)
        qb = q.astype(BF16)
        v = z_ref[0, rows, c_v:c_v + wv]
        i64 = z_ref[0, rows, c_i64:c_i64 + wv]
        i32 = z_ref[0, rows, c_i32:c_i32 + wq]
        gr = gt_ref[0, c]

        flog = _log_sigmoid(z_ref[0, rows, c_f64:c_f64 + wv + wq])
        bsum = _dot(tri_l, jnp.concatenate(_split_terms(flog, 3), axis=1))
        w3 = wv + wq
        ball = bsum[:, 0:w3] + bsum[:, w3:2 * w3] + bsum[:, 2 * w3:3 * w3]
        b64, b32 = ball[:, 0:wv], ball[:, wv:w3]
        fterms = [t.astype(F32) for t in _split_terms(_log_sigmoid(gr[1:2, :]), 3)]
        frows = jnp.where(row8 == 0, fterms[0], jnp.where(row8 == 1, fterms[1], jnp.where(row8 == 2, fterms[2], 0.0)))
        bparts = _dot(frows.astype(BF16), tri_u_bd)
        brow = bparts[0:1, :] + bparts[1:2, :] + bparts[2:3, :]

        dall = jnp.where(tri_heads, b64 - brow + gr[0:1, :], NEG)
        cm = i64 - b64
        for sh in (1, 2, 4, 8, 16, 32):
            cm = jnp.where(trow >= sh, jnp.maximum(cm, pltpu.roll(cm, sh, 0)), cm)
        inter = b64 + m64
        m_t = jnp.maximum(inter, b64 + cm)
        kbd = jnp.concatenate([jnp.where(head_of_qlane == h, k, 0.0) for h in range(H)], axis=0).astype(BF16)
        sc = _nt_dot(qb, kbd) * jnp.exp(dall - m_t)
        wi = jnp.exp(inter - m_t)
        vb = v.astype(BF16)
        vbd = jnp.concatenate([jnp.where(head_of_vlane == h, v, 0.0) for h in range(H)], axis=0).astype(BF16)
        pv = _dot(sc.astype(BF16), jnp.concatenate([vbd, ones_bd], axis=1))
        qst = _dot(qb, jnp.concatenate([cbd, nbd], axis=1).astype(BF16))
        num = pv[:, 0:wv] + wi * qst[:, 0:wv]
        den = pv[:, wv:] + wi * qst[:, wv:]
        hh = num / jnp.maximum(jnp.abs(den), jnp.exp(-m_t))

        og = _sigmoid(z_ref[0, rows, c_o:c_o + wv]) * hh
        dev = og - head_mean(og)
        o_ref[0, rows, :] = dev * lax.rsqrt(head_mean(dev * dev) + LN_EPS) * ng

        bl64, bl32 = b64[L - 1:L, :], b32[L - 1:L, :]
        m64_new = jnp.maximum(bl64 + m64, jnp.max(bl64 - b64 + i64, axis=0, keepdims=True))
        g32 = bl32 - b32 + i32
        m32_new = jnp.maximum(bl32 + m32, jnp.max(g32, axis=0, keepdims=True))
        kw = k * jnp.exp(g32 - m32_new)
        wc = jnp.exp(bl64 + m64 - m64_new)
        kwt = _nt_dot(eye_q, kw.astype(BF16)).astype(BF16)
        upd = _dot(kwt, jnp.concatenate([vb, jnp.ones((L, wv), BF16)], axis=1))
        cbd = wc * cbd + jnp.where(state_mask, upd[:, 0:wv], 0.0)
        nbd = wc * nbd + jnp.where(state_mask, upd[:, wv:], 0.0)
        return cbd, nbd, m64_new, m32_new

    init = (jnp.zeros((wq, wv), F32), jnp.zeros((wq, wv), F32), jnp.zeros((1, wv), F32), jnp.zeros((1, wq), F32))
    lax.fori_loop(0, seq // L, chunk, init, unroll=4)


def _mlstm(za, gates_t, conv_w, norm_g):
    b, s, wa = za.shape
    nc = s // M_CHUNK
    return pl.pallas_call(
        _mlstm_kernel,
        grid=(b,),
        in_specs=[pl.BlockSpec((1, s, wa), lambda i: (i, 0, 0)),
                  pl.BlockSpec((1, nc) + gates_t.shape[2:], lambda i: (i, 0, 0, 0)),
                  pl.BlockSpec((M_CONV, 2 * HEADS * M_QK_DIM), lambda i: (0, 0)),
                  pl.BlockSpec((1, GROUP_WIDTH), lambda i: (0, 0))],
        out_specs=pl.BlockSpec((1, s, GROUP_WIDTH), lambda i: (i, 0, 0)),
        out_shape=jax.ShapeDtypeStruct((b, s, GROUP_WIDTH), F32),
        scratch_shapes=[pltpu.VMEM((s + 8, 2 * HEADS * M_QK_DIM), F32)],
        compiler_params=pltpu.CompilerParams(dimension_semantics=("arbitrary",),
                                             vmem_limit_bytes=VMEM_LIMIT_BYTES),
        name="mlstm",
    )(za, gates_t, conv_w, norm_g)


def _mlstm_gate_rows(zg):
    b, s, _ = zg.shape
    gates = zg[:, :, 0:2 * HEADS].reshape(b, s // M_CHUNK, M_CHUNK, 2, HEADS)
    return jnp.transpose(gates, (0, 1, 3, 4, 2)).reshape(b, s // M_CHUNK, 2, HEADS * M_CHUNK)


def _dilated_kernel(q0_ref, q1_ref, k0_ref, k1_ref, v0_ref, v1_ref, bias_ref, o_ref, acc_ref, mx_ref, den_ref):
    seq = q0_ref.shape[1]
    W = BLK
    hd = HEAD_DIM
    npair = HEADS // 2
    scale = hd ** -0.5
    assert math.log2(scale).is_integer()
    nk = 2 * W
    qi = _iota((W, 2 * nk), 0)
    ki = _iota((W, 2 * nk), 1) & (nk - 1)
    j = W + qi - ki
    band = (j >= 0) & (j <= W)
    head_of_lane = _iota((nk, 2 * hd), 1) >> int(math.log2(hd))
    ones_bd = jnp.concatenate([jnp.where(head_of_lane == hh, 1.0, 0.0) for hh in range(2)], axis=0).astype(BF16)
    q_refs, k_refs, v_refs = (q0_ref, q1_ref), (k0_ref, k1_ref), (v0_ref, v1_ref)

    def block_diag(x):
        return jnp.concatenate([jnp.where(head_of_lane == hh, x, 0.0) for hh in range(2)], axis=0).astype(BF16)

    for br, (window, dil) in enumerate(DIL_PATTERNS):
        assert window // dil == W
        nb = (seq // dil) // W

        def piece(idx, _, br=br, dil=dil, nb=nb):
            n = idx % nb
            if dil == 1:
                rows_q = pl.ds(pl.multiple_of(W * n, W), W)
                rows_p = pl.ds(pl.multiple_of(W * jnp.maximum(n - 1, 0), W), W)
            else:
                r = idx // nb
                rows_q = pl.ds(r + dil * W * n, W, stride=dil)
                rows_p = pl.ds(r + dil * W * jnp.maximum(n - 1, 0), W, stride=dil)
            mask = band & (ki >= jnp.where(n > 0, 0, W))
            for pr in range(npair):
                q = (q_refs[pr][0, rows_q, :] * scale).astype(BF16)
                k2 = jnp.concatenate([k_refs[pr][0, rows_p, :], k_refs[pr][0, rows_q, :]], axis=0)
                v2 = jnp.concatenate([v_refs[pr][0, rows_p, :], v_refs[pr][0, rows_q, :]], axis=0)
                lg = jnp.where(mask, _nt_dot(q, block_diag(k2)) + bias_ref[br, pr], NEG)
                ps, mxs = [], []
                for hh in range(2):
                    sl = slice(nk * hh, nk * (hh + 1))
                    m = jnp.max(lg[:, sl], axis=-1, keepdims=True)
                    ps.append(jnp.where(mask[:, sl], jnp.exp(lg[:, sl] - m), 0.0))
                    mxs.append(jnp.broadcast_to(m, (W, hd)))
                p = jnp.concatenate(ps, axis=-1).astype(BF16)
                pv = _dot(p, jnp.concatenate([block_diag(v2), ones_bd], axis=1))
                acc_ref[br, pr, rows_q, :] = pv[:, 0:2 * hd]
                den_ref[br, pr, rows_q, :] = pv[:, 2 * hd:]
                mx_ref[br, pr, rows_q, :] = jnp.concatenate(mxs, axis=-1)
            return 0

        lax.fori_loop(0, dil * nb, piece, 0, unroll=8)

    def combine(i, _):
        rows = pl.ds(pl.multiple_of(i * W, W), W)
        outs = []
        for pr in range(npair):
            ms = [mx_ref[b, pr, rows, :] for b in range(len(DIL_PATTERNS))]
            top = functools.reduce(jnp.maximum, ms)
            es = [jnp.exp(m - top) for m in ms]
            num = sum(e * acc_ref[b, pr, rows, :] for b, e in enumerate(es))
            den = sum(e * jnp.maximum(den_ref[b, pr, rows, :], 1e-30) for b, e in enumerate(es))
            outs.append(num / den)
        o_ref[0, rows, :] = jnp.concatenate(outs, axis=-1)
        return 0

    lax.fori_loop(0, seq // W, combine, 0)


def _dilated(zb, bias):
    b, s, wb = zb.shape
    nbr = len(DIL_PATTERNS)
    pair_spec = lambda c: pl.BlockSpec((1, s, LANES), lambda i: (i, 0, c))
    return pl.pallas_call(
        _dilated_kernel,
        grid=(b,),
        in_specs=[pair_spec(c) for c in range(wb // LANES)]
                 + [pl.BlockSpec(bias.shape, lambda i: (0, 0, 0, 0))],
        out_specs=pl.BlockSpec((1, s, GROUP_WIDTH), lambda i: (i, 0, 0)),
        out_shape=jax.ShapeDtypeStruct((b, s, GROUP_WIDTH), F32),
        scratch_shapes=[pltpu.VMEM((nbr, HEADS // 2, s, LANES), F32) for _ in range(3)],
        compiler_params=pltpu.CompilerParams(dimension_semantics=("arbitrary",),
                                             vmem_limit_bytes=VMEM_LIMIT_BYTES),
        name="dilated",
    )(*([zb] * (wb // LANES)), bias)


def _head_columns_t(q):
    qt = q.T
    return jnp.concatenate([qt[HEAD_DIM * h:HEAD_DIM * (h + 1), :] for h in range(HEADS)], axis=1)


LOG2_E = math.log2(math.e)


def _scaled_query_columns_t(q):
    return (_head_columns_t(q) * (HEAD_DIM ** -0.5 * LOG2_E)).astype(BF16)


def _unstack_heads_t(per_head_t):
    halves = []
    for h in range(0, HEADS, 2):
        halves.append(jnp.concatenate([per_head_t[h], per_head_t[h + 1]], axis=0).T)
    return jnp.concatenate(halves, axis=-1)


VT_ROWS = HEAD_DIM + 16


def _online_step(lg, mask, vt1, m, acc):
    lg = jnp.where(mask, lg, NEG)
    m_new = jnp.maximum(m, jnp.max(lg, axis=0, keepdims=True))
    p = jnp.where(mask, jnp.exp2(lg - m_new), 0.0)
    acc = acc * jnp.exp2(m - m_new) + _dot(vt1, p.astype(BF16))
    return m_new, acc


def _online_result(acc):
    return acc[0:HEAD_DIM, :] / jnp.maximum(acc[HEAD_DIM:HEAD_DIM + 1, :], 1e-30)


def _values_with_ones_t(z_rows):
    vt = z_rows.T[HEAD_DIM:2 * HEAD_DIM, :]
    return jnp.concatenate([vt, jnp.ones((VT_ROWS - HEAD_DIM, vt.shape[1]), F32)], axis=0).astype(BF16)


def _online_init(nq=BLK):
    return (tuple(jnp.full((1, nq), NEG, F32) for _ in range(HEADS)),
            tuple(jnp.zeros((VT_ROWS, nq), F32) for _ in range(HEADS)))


def _dsa_kernel(z_ref, btab_ref, o_ref, vt_ref, key_ref, hi_ref, lo_ref, lom_ref):
    i = pl.program_id(1)
    seq = z_ref.shape[1]
    nkb = seq // BLK
    nq = QRY
    hd = HEAD_DIM
    topk = min(DSA_TOPK, seq // 4)
    t0 = pl.multiple_of(i * nq, nq)

    assert nq == KEYS
    npair = i + 1
    c_q, c_iq, c_kv, c_ikw = CHUNK_STARTS[2]

    @pl.when(i == 0)
    def _():
        for kb in range(nkb):
            vt_ref[kb // 2, :, (kb % 2) * BLK:(kb % 2 + 1) * BLK] = _values_with_ones_t(
                z_ref[0, kb * BLK:(kb + 1) * BLK, c_kv:c_kv + 2 * hd])

    zq = z_ref[0, pl.ds(t0, nq), :]
    cq = zq[:, c_q:c_q + HEADS * hd]
    ciq = _head_columns_t(zq[:, c_iq:c_iq + IDX_HEADS * IDX_DIM]).astype(BF16)
    iw = (zq[:, c_ikw:c_ikw + LANES].T[IDX_DIM:IDX_DIM + IDX_HEADS, :]
          * ((IDX_HEADS * IDX_DIM) ** -0.5))
    s_loc = lax.broadcasted_iota(I32, (KEYS, nq), 0)
    t_glob = t0 + lax.broadcasted_iota(I32, (KEYS, nq), 1)

    def score_block(kp, _):
        r0 = pl.multiple_of(kp * KEYS, KEYS)
        ik = z_ref[0, pl.ds(r0, KEYS), c_ikw:c_ikw + IDX_DIM].astype(BF16)
        rel = _dot(ik, ciq)
        sc = jnp.zeros((KEYS, nq), F32)
        for h in range(IDX_HEADS):
            sc = sc + jnp.maximum(rel[:, nq * h:nq * (h + 1)], 0.0) * iw[h:h + 1, :]
        sc = jnp.where(r0 + s_loc <= t_glob, sc, NEG)
        key = _sortable_key(sc)
        key_ref[kp] = key
        hi_ref[kp] = (key >> 16).astype(I16)
        lo_ref[kp] = ((key & 0xFFFF) - 2 ** 15).astype(I16)
        return 0

    lax.fori_loop(0, npair // 2, lambda t, c: score_block(2 * t + 1, score_block(2 * t, c)), 0)
    lax.cond(npair % 2 == 1, lambda: score_block(npair - 1, 0), lambda: 0)

    def count(ref, pred, pairs):
        dt = ref.dtype
        rows = 8 * 4 // dt.itemsize
        def body(kp, acc):
            hit = jnp.where(pred(kp, ref[kp]), jnp.ones((), dt), jnp.zeros((), dt))
            hit = hit.reshape(KEYS // rows, rows, nq)
            parts = [hit[n] for n in range(KEYS // rows)]
            while len(parts) > 1:
                parts = [a + b for a, b in zip(parts[0::2], parts[1::2])]
            return acc + parts[0]
        acc = jnp.zeros((rows, nq), dt)
        for kp in range(pairs):
            acc = body(kp, acc)
        return jnp.sum(acc.astype(I32), axis=0, keepdims=True)

    def threshold(pairs):
        def half_search(ref, k):
            def bit(it, thr):
                cand = thr + lax.shift_left(jnp.int32(1), 15 - it)
                c = count(ref, lambda kp, half: half >= cand.astype(I16), pairs)
                return jnp.where(c >= k, cand, thr)
            return lax.fori_loop(0, 16, bit, jnp.full((1, nq), I16_MIN, I32))

        def run(_):
            thr_hi = half_search(hi_ref, topk)
            thr_hi16 = thr_hi.astype(I16)
            above_hi = count(hi_ref, lambda kp, half: half > thr_hi16, pairs)
            for kp in range(pairs):
                lom_ref[kp] = jnp.where(hi_ref[kp] == thr_hi16, lo_ref[kp], jnp.int16(I16_MIN))
            thr_lo = half_search(lom_ref, topk - above_hi)
            thr = (thr_hi << 16) | (thr_lo + 2 ** 15)
            thr_lo16 = thr_lo.astype(I16)
            n_gt = above_hi + count(lom_ref, lambda kp, half: half > thr_lo16, pairs)
            need = (topk - n_gt).astype(F32)
            earlier = jnp.zeros((1, nq), F32)
            for kp in range(pairs):
                tie = key_ref[kp] == thr
                tie01 = jnp.where(tie, 1.0, 0.0)
                rank = _dot(below, tie01.astype(BF16)) + earlier
                lom_ref[kp] = jnp.where(tie & (rank < need), 1, 0).astype(I16)
                earlier = earlier + jnp.sum(tie01, axis=0, keepdims=True)
            return thr
        return run

    below = jnp.where(_iota((KEYS, KEYS), 1) < _iota((KEYS, KEYS), 0), 1.0, 0.0).astype(BF16)
    thr = lax.switch(npair - 1, [threshold(p) for p in range(1, seq // KEYS + 1)], 0)

    qs = _scaled_query_columns_t(cq)

    def attend(kp, carry):
        ms, accs = carry
        r0 = pl.multiple_of(kp * KEYS, KEYS)
        kblk = z_ref[0, pl.ds(r0, KEYS), c_kv:c_kv + hd].astype(BF16)
        lg = _dot(kblk, qs) + btab_ref[jnp.minimum(i - kp, 2)]
        key = key_ref[kp]
        s_glob = r0 + s_loc
        mask = ((key > thr) | (lom_ref[kp].astype(I32) != 0)) & (s_glob <= t_glob)
        vt1 = vt_ref[kp]
        out = [_online_step(lg[:, nq * h:nq * (h + 1)], mask, vt1, ms[h], accs[h]) for h in range(HEADS)]
        return tuple(o[0] for o in out), tuple(o[1] for o in out)

    carry = lax.fori_loop(0, npair // 2, lambda t, c: attend(2 * t + 1, attend(2 * t, c)), _online_init(nq))
    ms, accs = lax.cond(npair % 2 == 1, lambda c: attend(npair - 1, c), lambda c: c, carry)
    o_ref[0] = _unstack_heads_t([_online_result(accs[h]) for h in range(HEADS)])


def _dsa(zc, btab):
    b, s, wc = zc.shape
    return pl.pallas_call(
        _dsa_kernel,
        grid=(b, s // QRY),
        in_specs=[pl.BlockSpec((1, s, wc), lambda bi, i: (bi, 0, 0)),
                  pl.BlockSpec(btab.shape, lambda bi, i: (0, 0, 0))],
        out_specs=pl.BlockSpec((1, QRY, GROUP_WIDTH), lambda bi, i: (bi, i, 0)),
        out_shape=jax.ShapeDtypeStruct((b, s, GROUP_WIDTH), F32),
        scratch_shapes=[pltpu.VMEM((s // KEYS, VT_ROWS, KEYS), BF16), pltpu.VMEM((s // KEYS, KEYS, QRY), I32)]
                       + [pltpu.VMEM((s // KEYS, KEYS, QRY), I16)] * 3,
        compiler_params=pltpu.CompilerParams(dimension_semantics=("arbitrary", "arbitrary"),
                                             vmem_limit_bytes=VMEM_LIMIT_BYTES),
        name="dsa",
    )(zc, btab)


def _nsa_kernel(z_ref, zc_ref, w1_ref, pos_ref, w2_ref, bsel_ref, bcmp_ref, ovt_ref, exp_ref, o_ref,
                vst_ref, vwt_ref, cmp_ref, cmpt_ref):
    i = pl.program_id(1)
    seq = z_ref.shape[1]
    nkb = seq // BLK
    hd = HEAD_DIM
    c_q, _, c_s, c_w, c_g = CHUNK_STARTS[3]
    n_cmp = (seq - NSA_CMP_LEN) // NSA_CMP_STRIDE + 1
    n_sel = seq // NSA_SEL_LEN
    topn = min(NSA_TOPN, n_sel)
    half = NSA_CMP_LEN // 2
    assert half == NSA_CMP_STRIDE and n_cmp + 1 == seq // NSA_CMP_STRIDE == BLK and n_sel <= BLK
    nq = QRY
    assert nq == KEYS
    t0 = pl.multiple_of(i * nq, nq)
    hi = lax.Precision.HIGHEST

    @pl.when(i == 0)
    def _():
        for kb in range(nkb):
            rows = slice(kb * BLK, (kb + 1) * BLK)
            cols = slice((kb % 2) * BLK, (kb % 2 + 1) * BLK)
            vst_ref[kb // 2, :, cols] = _values_with_ones_t(z_ref[0, rows, c_s:c_s + 2 * hd])
            vwt_ref[kb // 2, :, cols] = _values_with_ones_t(z_ref[0, rows, c_w:c_w + 2 * hd])
        first = jnp.zeros((BLK, 2 * NSA_CMP_HIDDEN), F32)
        second = jnp.zeros((BLK, 2 * NSA_CMP_HIDDEN), F32)
        for j in range(half):
            xj = zc_ref[0, pl.ds(j, BLK, stride=NSA_CMP_STRIDE), :]
            first = first + _dot((xj + pos_ref[j:j + 1, :]).astype(BF16), w1_ref[j])
            second = second + _dot((xj + pos_ref[half + j:half + j + 1, :]).astype(BF16), w1_ref[half + j])
        hid = first + pltpu.roll(second, BLK - 1, 0)
        hid = hid * _sigmoid(hid)
        cmp = _dot(hid.astype(BF16), w2_ref[...])
        cmp_ref[...] = cmp
        cmpt_ref[...] = cmp.T

    zq = z_ref[0, pl.ds(t0, nq), :]
    qs = _scaled_query_columns_t(zq[:, c_q:c_q + HEADS * hd])
    gates = _sigmoid(zq[:, c_g:c_g + LANES].T[0:16, :])
    row = lax.broadcasted_iota(I32, (BLK, nq), 0)
    t_glob = t0 + lax.broadcasted_iota(I32, (BLK, nq), 1)

    kcmp = cmp_ref[:, 0:hd].astype(BF16)
    vcmpt = cmpt_ref[hd:2 * hd, :].astype(BF16)
    lgc = _dot(kcmp, qs) + bcmp_ref[0]
    mask_c = (t_glob - (row * NSA_CMP_STRIDE + NSA_CMP_LEN - 1) >= 0) & (row < n_cmp)
    o_cmp, psum = [], jnp.zeros((BLK, nq), F32)
    for h in range(HEADS):
        lg = jnp.where(mask_c, lgc[:, nq * h:nq * (h + 1)], NEG)
        m = jnp.max(lg, axis=0, keepdims=True)
        p = jnp.where(mask_c, jnp.exp2(lg - m), 0.0)
        p = p / jnp.maximum(jnp.sum(p, axis=0, keepdims=True), 1e-30)
        o_cmp.append(_dot(vcmpt, p.astype(BF16)))
        psum = psum + p

    imp = _dot(ovt_ref[...], psum, precision=hi)
    cur = t_glob >> int(math.log2(NSA_SEL_LEN))
    forced = (row == 0) | (row == cur) | (row == cur - 1)
    imp = jnp.where(forced, NSA_FORCE, imp)
    imp = jnp.where(row * NSA_SEL_LEN <= t_glob, imp, NEG)
    imp = imp[0:n_sel, :]
    jrow = lax.broadcasted_iota(I32, (n_sel, nq), 0)
    rank = jnp.zeros((n_sel, nq), I32)
    for jp in range(n_sel):
        other = imp[jp:jp + 1, :]
        rank = rank + ((other > imp) | ((other == imp) & (jp < jrow))).astype(I32)
    chosen = jnp.where(rank < topn, 1.0, 0.0)
    chosen = jnp.concatenate([chosen, jnp.zeros((BLK - n_sel, nq), F32)], axis=0).astype(BF16)

    s_loc = lax.broadcasted_iota(I32, (KEYS, nq), 0)
    t_keys = t0 + lax.broadcasted_iota(I32, (KEYS, nq), 1)

    def attend(vt_ref, lanes, mask_fn):
        def body(kp, carry):
            ms, accs = carry
            r0 = pl.multiple_of(kp * KEYS, KEYS)
            kblk = z_ref[0, pl.ds(r0, KEYS), lanes].astype(BF16)
            lg = _dot(kblk, qs) + bsel_ref[jnp.minimum(i - kp, 2)]
            mask = mask_fn(r0)
            vt1 = vt_ref[kp]
            out = [_online_step(lg[:, nq * h:nq * (h + 1)], mask, vt1, ms[h], accs[h]) for h in range(HEADS)]
            return tuple(o[0] for o in out), tuple(o[1] for o in out)
        return body

    def mask_sel(r0):
        picked = _dot(exp_ref[pl.ds(r0, KEYS), :], chosen) > 0.5
        return picked & (r0 + s_loc <= t_keys)

    def mask_win(r0):
        dist = t_keys - (r0 + s_loc)
        return (dist >= 0) & (dist < NSA_WINDOW)

    sel_body = attend(vst_ref, slice(c_s, c_s + hd), mask_sel)
    win_body = attend(vwt_ref, slice(c_w, c_w + hd), mask_win)
    first_w = jnp.maximum(i - NSA_WINDOW // KEYS, 0)
    sel_carry = lax.fori_loop(0, first_w // 2, lambda t, c: sel_body(2 * t + 1, sel_body(2 * t, c)), _online_init(nq))
    sel_carry = lax.cond(first_w % 2 == 1, lambda c: sel_body(first_w - 1, c), lambda c: c, sel_carry)
    (_, acc_s), (_, acc_w) = lax.fori_loop(
        first_w, i + 1, lambda kp, c: (sel_body(kp, c[0]), win_body(kp, c[1])), (sel_carry, _online_init(nq)))

    outs = []
    for h in range(HEADS):
        outs.append(gates[3 * h:3 * h + 1, :] * o_cmp[h] + gates[3 * h + 1:3 * h + 2, :] * _online_result(acc_s[h])
                    + gates[3 * h + 2:3 * h + 3, :] * _online_result(acc_w[h]))
    o_ref[0] = _unstack_heads_t(outs)


def _nsa(zd, w1, pos, w2, bsel, bcmp, ovt, expand):
    b, s, wd = zd.shape
    full = lambda a: pl.BlockSpec(a.shape, lambda bi, i: (0,) * a.ndim)
    return pl.pallas_call(
        _nsa_kernel,
        grid=(b, s // QRY),
        in_specs=[pl.BlockSpec((1, s, wd), lambda bi, i: (bi, 0, 0)),
                  pl.BlockSpec((1, s, LANES), lambda bi, i: (bi, 0, CHUNK_STARTS[3][1] // LANES)),
                  full(w1), full(pos), full(w2), full(bsel),
                  pl.BlockSpec((1, BLK, HEADS * QRY), lambda bi, i: (i, 0, 0)),
                  full(ovt), full(expand)],
        out_specs=pl.BlockSpec((1, QRY, GROUP_WIDTH), lambda bi, i: (bi, i, 0)),
        out_shape=jax.ShapeDtypeStruct((b, s, GROUP_WIDTH), F32),
        scratch_shapes=[pltpu.VMEM((s // KEYS, VT_ROWS, KEYS), BF16), pltpu.VMEM((s // KEYS, VT_ROWS, KEYS), BF16),
                        pltpu.VMEM((BLK, BLK), F32), pltpu.VMEM((BLK, BLK), F32)],
        compiler_params=pltpu.CompilerParams(dimension_semantics=("arbitrary", "arbitrary"),
                                             vmem_limit_bytes=VMEM_LIMIT_BYTES),
        name="nsa",
    )(zd, zd, w1, pos, w2, bsel, bcmp, ovt, expand)


def _outproj_kernel(alpha, x_ref, a_ref, b_ref, c_ref, d_ref, w_ref, bo_ref, g_ref, beta_ref, o_ref):
    acc = bo_ref[...] + _dot(a_ref[...].astype(BF16), w_ref[0:GROUP_WIDTH, :])
    for n, m_ref in enumerate((b_ref, c_ref, d_ref), start=1):
        acc = acc + _dot(m_ref[...].astype(BF16), w_ref[n * GROUP_WIDTH:(n + 1) * GROUP_WIDTH, :])
    o_ref[...] = _layer_norm_rows(alpha * x_ref[...] + acc, g_ref[...], beta_ref[...])


def _outproj(alpha, x2d, mixed, w, bo, g, beta, tm=512):
    t = x2d.shape[0]
    row_spec = lambda width: pl.BlockSpec((tm, width), lambda i: (i, 0))
    const = lambda a: pl.BlockSpec(a.shape, lambda i: (0, 0))
    return pl.pallas_call(
        functools.partial(_outproj_kernel, alpha),
        grid=(t // tm,),
        in_specs=[row_spec(D_MODEL)] + [row_spec(GROUP_WIDTH)] * N_MIXERS
                 + [const(w), const(bo), const(g), const(beta)],
        out_specs=row_spec(D_MODEL),
        out_shape=jax.ShapeDtypeStruct((t, D_MODEL), F32),
        compiler_params=pltpu.CompilerParams(dimension_semantics=("arbitrary",),
                                             vmem_limit_bytes=VMEM_LIMIT_BYTES),
        name="outproj_ln",
    )(x2d, *mixed, w, bo, g, beta)


def _ffn_kernel(alpha, x_ref, w1_ref, b1_ref, w2_ref, b2_ref, g_ref, beta_ref, o_ref, xb_ref, acc_ref):
    j = pl.program_id(1)

    @pl.when(j == 0)
    def _():
        xb_ref[...] = x_ref[...].astype(BF16)
        acc_ref[...] = jnp.zeros_like(acc_ref)

    hdn = jnp.maximum(_dot(xb_ref[...], w1_ref[...]) + b1_ref[...], 0.0)
    acc_ref[...] += _dot(jnp.square(hdn).astype(BF16), w2_ref[...])

    @pl.when(j == pl.num_programs(1) - 1)
    def _():
        y = alpha * x_ref[...] + (acc_ref[...] + b2_ref[...])
        o_ref[...] = _layer_norm_rows(y, g_ref[...], beta_ref[...])


def _ffn(alpha, x2d, w1, b1, w2, b2, g, beta, tm=1024, tf=1024):
    t = x2d.shape[0]
    return pl.pallas_call(
        functools.partial(_ffn_kernel, alpha),
        grid=(t // tm, D_FF // tf),
        in_specs=[pl.BlockSpec((tm, D_MODEL), lambda i, j: (i, 0)),
                  pl.BlockSpec((D_MODEL, tf), lambda i, j: (0, j)),
                  pl.BlockSpec((1, tf), lambda i, j: (0, j)),
                  pl.BlockSpec((tf, D_MODEL), lambda i, j: (j, 0)),
                  pl.BlockSpec((1, D_MODEL), lambda i, j: (0, 0)),
                  pl.BlockSpec((1, D_MODEL), lambda i, j: (0, 0)),
                  pl.BlockSpec((1, D_MODEL), lambda i, j: (0, 0))],
        out_specs=pl.BlockSpec((tm, D_MODEL), lambda i, j: (i, 0)),
        out_shape=jax.ShapeDtypeStruct((t, D_MODEL), F32),
        scratch_shapes=[pltpu.VMEM((tm, D_MODEL), BF16), pltpu.VMEM((tm, D_MODEL), F32)],
        compiler_params=pltpu.CompilerParams(dimension_semantics=("arbitrary", "arbitrary"),
                                             vmem_limit_bytes=VMEM_LIMIT_BYTES),
        name="ffn_ln",
    )(x2d, w1, b1, w2, b2, g, beta)


def _bias_of_distance(rel_bias_heads, dist):
    onehot = np.eye(NUM_BUCKETS, dtype=np.float32)[_t5_bucket_np(np.asarray(dist))]
    return jnp.dot(rel_bias_heads.T, jnp.asarray(onehot.T), precision=lax.Precision.HIGHEST)


def _shifted_rows(v, n_rows, n_cols, step):
    period = v.shape[-1]
    assert n_cols <= period - step
    flat = jnp.tile(v, (1,) * (v.ndim - 1) + (n_rows,))[..., :n_rows * (period - step)]
    return flat.reshape(v.shape[:-1] + (n_rows, period - step))[..., :n_cols]


def _wrapped(period):
    idx = np.arange(period)
    return np.where(idx < period // 2, idx, idx - period)


def _dilated_bias(rel_bias):
    x = _wrapped(4 * BLK)
    tabs = [_shifted_rows(_bias_of_distance(rel_bias[:, 0:HEADS], (BLK - x) * dil), BLK, 2 * BLK, 1)
            for _, dil in DIL_PATTERNS]
    tabs = jnp.stack(tabs).reshape(len(DIL_PATTERNS), HEADS // 2, 2, BLK, 2 * BLK)
    return jnp.transpose(tabs, (0, 1, 3, 2, 4)).reshape(len(DIL_PATTERNS), HEADS // 2, BLK, 4 * BLK)


def _toeplitz_bias_t(rel_bias_heads, blk=BLK):
    assert (_t5_bucket_np(np.arange(blk + 1, 64 * blk)) == NUM_BUCKETS - 1).all()
    x = _wrapped(2 * blk)
    tabs = [_shifted_rows(_bias_of_distance(rel_bias_heads, blk * delta + x), blk, blk, 1)
            for delta in range(3)]
    return jnp.transpose(jnp.stack(tabs), (0, 2, 1, 3)).reshape(3, blk, HEADS * blk)


def _compressed_bias_t(rel_bias_heads, seq):
    nq = seq // QRY
    x = _wrapped(2 * seq + BLK)
    v = _bias_of_distance(rel_bias_heads, x - (NSA_CMP_LEN - 1))
    tab = _shifted_rows(v, BLK, seq, NSA_CMP_STRIDE)
    tab = tab.reshape(HEADS, BLK, nq, QRY)
    return jnp.transpose(tab, (2, 1, 0, 3)).reshape(nq, BLK, HEADS * QRY)


def _nsa_constants(seq):
    n_cmp = (seq - NSA_CMP_LEN) // NSA_CMP_STRIDE + 1
    n_sel = seq // NSA_SEL_LEN
    cs = np.arange(n_cmp)[:, None] * NSA_CMP_STRIDE
    ss = np.arange(n_sel)[None, :] * NSA_SEL_LEN
    ov = np.clip(np.minimum(cs + NSA_CMP_LEN, ss + NSA_SEL_LEN) - np.maximum(cs, ss), 0, None) / NSA_CMP_LEN
    ovt = np.zeros((BLK, BLK), np.float32)
    ovt[:n_sel, :n_cmp] = ov.T
    expand = np.zeros((seq, BLK), np.float32)
    expand[np.arange(seq), np.arange(seq) // NSA_SEL_LEN] = 1.0
    return jnp.asarray(ovt), jnp.asarray(expand, BF16)


def _nsa_weights(cmp_pos, cmp_w1, cmp_w2):
    hd, hid = HEAD_DIM, NSA_CMP_HIDDEN
    w1 = cmp_w1.reshape(2, NSA_CMP_LEN, hd, hid)
    zeros = jnp.zeros((NSA_CMP_LEN, hd, hid), F32)
    w1 = jnp.concatenate([jnp.concatenate([w1[0], zeros], axis=-1),
                          jnp.concatenate([zeros, w1[1]], axis=-1)], axis=1)
    pos = jnp.concatenate([cmp_pos[0], cmp_pos[1]], axis=-1)
    z2 = jnp.zeros((hid, hd), F32)
    w2 = jnp.concatenate([jnp.concatenate([cmp_w2[0], z2], axis=-1),
                          jnp.concatenate([z2, cmp_w2[1]], axis=-1)], axis=0)
    return w1.astype(BF16), pos, w2.astype(BF16)


def kernel(x, w_in, b_in, a_conv, a_norm, d_cmp_pos, d_cmp_w1, d_cmp_w2, w_out, b_out, ln1_g, ln1_b,
           w_ff1, b_ff1, w_ff2, b_ff2, ln2_g, ln2_b, rel_bias):
    bsz, seq, _ = x.shape
    depth = w_in.shape[0]
    alpha = (2 * depth) ** 0.25
    bias_dil = _dilated_bias(rel_bias)
    btab_dsa = _toeplitz_bias_t(rel_bias[:, HEADS:2 * HEADS], QRY) * LOG2_E
    btab_nsa = _toeplitz_bias_t(rel_bias[:, 2 * HEADS:3 * HEADS], QRY) * LOG2_E
    bcmp_nsa = _compressed_bias_t(rel_bias[:, 2 * HEADS:3 * HEADS], seq) * LOG2_E
    ovt, expand = _nsa_constants(seq)

    h = x.reshape(bsz * seq, D_MODEL)
    for l in range(depth):
        w_l = _permute_columns(w_in[l]).astype(BF16)
        b_l = _permute_columns(b_in[l])[None, :]
        za, zb, zc, zd, zg = (z.reshape(bsz, seq, -1) for z in _inproj(h, w_l, b_l))
        out_a = _mlstm(za, _mlstm_gate_rows(zg), a_conv[l], a_norm[l][None, :])
        out_b = _dilated(zb, bias_dil)
        out_c = _dsa(zc, btab_dsa)
        nsa_w1, nsa_pos, nsa_w2 = _nsa_weights(d_cmp_pos[l], d_cmp_w1[l], d_cmp_w2[l])
        out_d = _nsa(zd, nsa_w1, nsa_pos, nsa_w2, btab_nsa, bcmp_nsa, ovt, expand)
        mixed = [o.reshape(bsz * seq, GROUP_WIDTH) for o in (out_a, out_b, out_c, out_d)]
        h = _outproj(alpha, h, mixed, w_out[l].astype(BF16), b_out[l][None, :],
                     ln1_g[l][None, :], ln1_b[l][None, :])
        h = _ffn(alpha, h, w_ff1[l].astype(BF16), b_ff1[l][None, :], w_ff2[l].astype(BF16),
                 b_ff2[l][None, :], ln2_g[l][None, :], ln2_b[l][None, :])
    return h.reshape(bsz, seq, D_MODEL)
```

```python
import functools
import math

import numpy as np
import jax
import jax.numpy as jnp
from jax import lax
from jax.experimental import pallas as pl
from jax.experimental.pallas import tpu as pltpu

F32 = jnp.float32
BF16 = jnp.bfloat16
I32 = jnp.int32
I16 = jnp.int16
I16_MIN = -2 ** 15

D_MODEL = 1024
N_MIXERS = 4
HEADS = 4
HEAD_DIM = D_MODEL // (N_MIXERS * HEADS)
GROUP_WIDTH = HEADS * HEAD_DIM
D_FF = 4 * D_MODEL
LN_EPS = 1e-5
NEG = -1e30

M_QK_DIM = HEAD_DIM // 2
M_CHUNK = 64
M_CONV = 4
DIL_PATTERNS = ((128, 1), (512, 4), (2048, 16))
IDX_HEADS = 4
IDX_DIM = 64
DSA_TOPK = 256
NSA_CMP_LEN = 32
NSA_CMP_STRIDE = 16
NSA_SEL_LEN = 64
NSA_TOPN = 16
NSA_WINDOW = 512
NSA_CMP_HIDDEN = 256
NSA_FORCE = 1e9
NUM_BUCKETS = 32
MAX_DISTANCE = 128

LANES = 128
BLK = 128
KEYS = 2 * BLK
QRY = KEYS
VMEM_LIMIT_BYTES = 56 * 1024 * 1024

IN_SPLITS = (
    ('a_q', HEADS * M_QK_DIM), ('a_k', HEADS * M_QK_DIM), ('a_v', GROUP_WIDTH),
    ('a_i', HEADS), ('a_f', HEADS), ('a_o', GROUP_WIDTH),
    ('b_q', GROUP_WIDTH), ('b_k', GROUP_WIDTH), ('b_v', GROUP_WIDTH),
    ('c_q', GROUP_WIDTH), ('c_k', HEAD_DIM), ('c_v', HEAD_DIM),
    ('c_iq', IDX_HEADS * IDX_DIM), ('c_ik', IDX_DIM), ('c_iw', IDX_HEADS),
    ('d_q', GROUP_WIDTH), ('d_kc', HEAD_DIM), ('d_vc', HEAD_DIM),
    ('d_ks', HEAD_DIM), ('d_vs', HEAD_DIM), ('d_kw', HEAD_DIM), ('d_vw', HEAD_DIM),
    ('d_g', 3 * HEADS),
)

GROUP_LAYOUT = (
    (('a_q', 'a_k'), ('a_v',), ('a_o',), (('a_i', HEAD_DIM),), (('a_i', M_QK_DIM),),
     (('a_f', HEAD_DIM),), (('a_f', M_QK_DIM),)),
    (('b_q',), ('b_k',), ('b_v',)),
    (('c_q',), ('c_iq',), ('c_k', 'c_v'), ('c_ik', 'c_iw')),
    (('d_q',), ('d_kc', 'd_vc'), ('d_ks', 'd_vs'), ('d_kw', 'd_vw'), ('d_g',)),
    (('a_i', 'a_f'),),
)


def _round_up(n, m):
    return -(-n // m) * m


def _projection_layout():
    offs, off = {}, 0
    for name, width in IN_SPLITS:
        offs[name] = (off, width)
        off += width
    runs, group_widths, chunk_starts = [], [], []
    for group in GROUP_LAYOUT:
        gwidth = 0
        chunk_starts.append([])
        for chunk in group:
            chunk_starts[-1].append(gwidth)
            cwidth = 0
            for entry in chunk:
                name, rep = entry if isinstance(entry, tuple) else (entry, 1)
                o, w = offs[name]
                runs.append((o, w, rep))
                cwidth += w * rep
            pad = _round_up(cwidth, LANES) - cwidth
            if pad:
                runs.append((-1, pad, 1))
            gwidth += cwidth + pad
        group_widths.append(gwidth)
    return tuple(runs), tuple(group_widths), tuple(tuple(c) for c in chunk_starts)


PROJ_RUNS, GROUP_WIDTHS, CHUNK_STARTS = _projection_layout()
PROJ_WIDTH = int(sum(GROUP_WIDTHS))


def _permute_columns(a):
    parts = []
    for o, w, r in PROJ_RUNS:
        if o < 0:
            parts.append(jnp.zeros(a.shape[:-1] + (w,), a.dtype))
        else:
            parts.append(a[..., o:o + w] if r == 1 else jnp.repeat(a[..., o:o + w], r, axis=-1))
    return jnp.concatenate(parts, axis=-1)


def _t5_bucket_np(dist):
    n = np.maximum(dist, 0)
    max_exact = NUM_BUCKETS // 2
    nf = np.maximum(n, max_exact).astype(np.float32)
    large = max_exact + (np.log(nf / max_exact) / math.log(MAX_DISTANCE / max_exact)
                         * (NUM_BUCKETS - max_exact)).astype(np.int32)
    large = np.minimum(large, NUM_BUCKETS - 1)
    return np.where(n < max_exact, n, large).astype(np.int32)


def _nt_dot(a, b, precision=None):
    return lax.dot_general(a, b, (((1,), (1,)), ((), ())), precision=precision,
                           preferred_element_type=F32)


def _dot(a, b, precision=None):
    return jnp.dot(a, b, precision=precision, preferred_element_type=F32)


def _layer_norm_rows(y, g, b):
    mu = jnp.mean(y, axis=-1, keepdims=True)
    var = jnp.mean(jnp.square(y - mu), axis=-1, keepdims=True)
    return (y - mu) * lax.rsqrt(var + LN_EPS) * g + b


def _sigmoid(x):
    return 1.0 / (1.0 + jnp.exp(-x))


def _log_sigmoid(x):
    return -(jnp.maximum(-x, 0.0) + jnp.log1p(jnp.exp(-jnp.abs(x))))


def _sortable_key(x):
    bits = pltpu.bitcast(x, I32)
    return bits ^ ((bits >> 31) & jnp.int32(0x7FFFFFFF))


def _inproj_kernel(x_ref, w_ref, b_ref, *out_refs):
    xb = x_ref[...].astype(BF16)
    off = 0
    for o_ref, width in zip(out_refs, GROUP_WIDTHS):
        o_ref[...] = _dot(xb, w_ref[:, off:off + width]) + b_ref[:, off:off + width]
        off += width


def _inproj(x2d, w, b, tm=512):
    t = x2d.shape[0]
    return pl.pallas_call(
        _inproj_kernel,
        grid=(t // tm,),
        in_specs=[pl.BlockSpec((tm, D_MODEL), lambda i: (i, 0)),
                  pl.BlockSpec((D_MODEL, PROJ_WIDTH), lambda i: (0, 0)),
                  pl.BlockSpec((1, PROJ_WIDTH), lambda i: (0, 0))],
        out_specs=[pl.BlockSpec((tm, gw), lambda i: (i, 0)) for gw in GROUP_WIDTHS],
        out_shape=[jax.ShapeDtypeStruct((t, gw), F32) for gw in GROUP_WIDTHS],
        compiler_params=pltpu.CompilerParams(dimension_semantics=("arbitrary",),
                                             vmem_limit_bytes=VMEM_LIMIT_BYTES),
        name="inproj",
    )(x2d, w, b)


def _split_terms(x, n):
    terms, rest = [], x
    for _ in range(n):
        terms.append(rest.astype(BF16))
        rest = rest - terms[-1].astype(F32)
    return terms


def _iota(shape, dim):
    return lax.broadcasted_iota(I32, shape, dim)


def _mlstm_kernel(z_ref, gt_ref, cw_ref, ng_ref, o_ref, xpad_ref):
    seq = z_ref.shape[1]
    L, DK, DV, H = M_CHUNK, M_QK_DIM, HEAD_DIM, HEADS
    assert L == DV
    wq, wv = H * DK, H * DV
    lg_dk, lg_dv = int(math.log2(DK)), int(math.log2(DV))
    _, c_v, c_o, c_i64, c_i32, c_f64, c_f32 = CHUNK_STARTS[0]
    assert c_f32 == c_f64 + wv
    xpad_ref[0:8, :] = jnp.zeros((8, 2 * wq), F32)
    xpad_ref[8:, :] = z_ref[0, :, 0:2 * wq]

    one_if = lambda cond: jnp.where(cond, 1.0, 0.0).astype(BF16)
    tri_l = one_if(_iota((L, L), 0) >= _iota((L, L), 1))
    trow = _iota((L, wv), 0)
    tri_heads = trow >= (_iota((L, wv), 1) & (L - 1))
    r_vv, c_vv = _iota((wv, wv), 0), _iota((wv, wv), 1)
    same_head = (r_vv >> lg_dv) == (c_vv >> lg_dv)
    ones_bd = one_if(same_head)
    mean_bd = jnp.where(same_head, 1.0 / DV, 0.0).astype(BF16)
    tri_u_bd = one_if(same_head & ((r_vv & (L - 1)) <= (c_vv & (L - 1))))
    state_mask = (_iota((wq, wv), 0) >> lg_dk) == (_iota((wq, wv), 1) >> lg_dv)
    eye_q = one_if(_iota((wq, wq), 0) == _iota((wq, wq), 1))
    head_of_qlane = _iota((L, wq), 1) >> lg_dk
    head_of_vlane = _iota((L, wv), 1) >> lg_dv
    row8 = _iota((8, wv), 0)
    cw = cw_ref[...]
    ng = ng_ref[...]

    def head_mean(x):
        hi_lo = _split_terms(x, 2)
        r = _dot(jnp.concatenate(hi_lo, axis=0), mean_bd)
        return r[0:L] + r[L:2 * L]

    def chunk(c, carry):
        cbd, nbd, m64, m32 = carry
        s0 = pl.multiple_of(c * L, L)
        rows = pl.ds(s0, L)
        xw = xpad_ref[pl.ds(s0, L + 8), :]
        y = sum(cw[j:j + 1, :] * xw[5 + j:5 + j + L, :] for j in range(M_CONV))
        qk = y * _sigmoid(y)
        q = qk[:, 0:wq]
        k = qk[:, wq:] * (DK ** -0.5)
        qb = q.astype(BF16)
        v = z_ref[0, rows, c_v:c_v + wv]
        i64 = z_ref[0, rows, c_i64:c_i64 + wv]
        i32 = z_ref[0, rows, c_i32:c_i32 + wq]
        gr = gt_ref[0, c]

        flog = _log_sigmoid(z_ref[0, rows, c_f64:c_f64 + wv + wq])
        bsum = _dot(tri_l, jnp.concatenate(_split_terms(flog, 3), axis=1))
        w3 = wv + wq
        ball = bsum[:, 0:w3] + bsum[:, w3:2 * w3] + bsum[:, 2 * w3:3 * w3]
        b64, b32 = ball[:, 0:wv], ball[:, wv:w3]
        fterms = [t.astype(F32) for t in _split_terms(_log_sigmoid(gr[1:2, :]), 3)]
        frows = jnp.where(row8 == 0, fterms[0], jnp.where(row8 == 1, fterms[1], jnp.where(row8 == 2, fterms[2], 0.0)))
        bparts = _dot(frows.astype(BF16), tri_u_bd)
        brow = bparts[0:1, :] + bparts[1:2, :] + bparts[2:3, :]

        dall = jnp.where(tri_heads, b64 - brow + gr[0:1, :], NEG)
        cm = i64 - b64
        for sh in (1, 2, 4, 8, 16, 32):
            cm = jnp.where(trow >= sh, jnp.maximum(cm, pltpu.roll(cm, sh, 0)), cm)
        inter = b64 + m64
        m_t = jnp.maximum(inter, b64 + cm)
        kbd = jnp.concatenate([jnp.where(head_of_qlane == h, k, 0.0) for h in range(H)], axis=0).astype(BF16)
        sc = _nt_dot(qb, kbd) * jnp.exp(dall - m_t)
        wi = jnp.exp(inter - m_t)
        vb = v.astype(BF16)
        vbd = jnp.concatenate([jnp.where(head_of_vlane == h, v, 0.0) for h in range(H)], axis=0).astype(BF16)
        pv = _dot(sc.astype(BF16), jnp.concatenate([vbd, ones_bd], axis=1))
        qst = _dot(qb, jnp.concatenate([cbd, nbd], axis=1).astype(BF16))
        num = pv[:, 0:wv] + wi * qst[:, 0:wv]
        den = pv[:, wv:] + wi * qst[:, wv:]
        hh = num / jnp.maximum(jnp.abs(den), jnp.exp(-m_t))

        og = _sigmoid(z_ref[0, rows, c_o:c_o + wv]) * hh
        dev = og - head_mean(og)
        o_ref[0, rows, :] = dev * lax.rsqrt(head_mean(dev * dev) + LN_EPS) * ng

        bl64, bl32 = b64[L - 1:L, :], b32[L - 1:L, :]
        m64_new = jnp.maximum(bl64 + m64, jnp.max(bl64 - b64 + i64, axis=0, keepdims=True))
        g32 = bl32 - b32 + i32
        m32_new = jnp.maximum(bl32 + m32, jnp.max(g32, axis=0, keepdims=True))
        kw = k * jnp.exp(g32 - m32_new)
        wc = jnp.exp(bl64 + m64 - m64_new)
        kwt = _nt_dot(eye_q, kw.astype(BF16)).astype(BF16)
        upd = _dot(kwt, jnp.concatenate([vb, jnp.ones((L, wv), BF16)], axis=1))
        cbd = wc * cbd + jnp.where(state_mask, upd[:, 0:wv], 0.0)
        nbd = wc * nbd + jnp.where(state_mask, upd[:, wv:], 0.0)
        return cbd, nbd, m64_new, m32_new

    init = (jnp.zeros((wq, wv), F32), jnp.zeros((wq, wv), F32), jnp.zeros((1, wv), F32), jnp.zeros((1, wq), F32))
    lax.fori_loop(0, seq // L, chunk, init, unroll=4)


def _mlstm(za, gates_t, conv_w, norm_g):
    b, s, wa = za.shape
    nc = s // M_CHUNK
    return pl.pallas_call(
        _mlstm_kernel,
        grid=(b,),
        in_specs=[pl.BlockSpec((1, s, wa), lambda i: (i, 0, 0)),
                  pl.BlockSpec((1, nc) + gates_t.shape[2:], lambda i: (i, 0, 0, 0)),
                  pl.BlockSpec((M_CONV, 2 * HEADS * M_QK_DIM), lambda i: (0, 0)),
                  pl.BlockSpec((1, GROUP_WIDTH), lambda i: (0, 0))],
        out_specs=pl.BlockSpec((1, s, GROUP_WIDTH), lambda i: (i, 0, 0)),
        out_shape=jax.ShapeDtypeStruct((b, s, GROUP_WIDTH), F32),
        scratch_shapes=[pltpu.VMEM((s + 8, 2 * HEADS * M_QK_DIM), F32)],
        compiler_params=pltpu.CompilerParams(dimension_semantics=("arbitrary",),
                                             vmem_limit_bytes=VMEM_LIMIT_BYTES),
        name="mlstm",
    )(za, gates_t, conv_w, norm_g)


def _mlstm_gate_rows(zg):
    b, s, _ = zg.shape
    gates = zg[:, :, 0:2 * HEADS].reshape(b, s // M_CHUNK, M_CHUNK, 2, HEADS)
    return jnp.transpose(gates, (0, 1, 3, 4, 2)).reshape(b, s // M_CHUNK, 2, HEADS * M_CHUNK)


def _dilated_kernel(q0_ref, q1_ref, k0_ref, k1_ref, v0_ref, v1_ref, bias_ref, o_ref, acc_ref, mx_ref, den_ref):
    seq = q0_ref.shape[1]
    W = BLK
    hd = HEAD_DIM
    npair = HEADS // 2
    scale = hd ** -0.5
    assert math.log2(scale).is_integer()
    nk = 2 * W
    qi = _iota((W, 2 * nk), 0)
    ki = _iota((W, 2 * nk), 1) & (nk - 1)
    j = W + qi - ki
    band = (j >= 0) & (j <= W)
    head_of_lane = _iota((nk, 2 * hd), 1) >> int(math.log2(hd))
    ones_bd = jnp.concatenate([jnp.where(head_of_lane == hh, 1.0, 0.0) for hh in range(2)], axis=0).astype(BF16)
    q_refs, k_refs, v_refs = (q0_ref, q1_ref), (k0_ref, k1_ref), (v0_ref, v1_ref)

    def block_diag(x):
        return jnp.concatenate([jnp.where(head_of_lane == hh, x, 0.0) for hh in range(2)], axis=0).astype(BF16)

    for br, (window, dil) in enumerate(DIL_PATTERNS):
        assert window // dil == W
        nb = (seq // dil) // W

        def piece(idx, _, br=br, dil=dil, nb=nb):
            n = idx % nb
            if dil == 1:
                rows_q = pl.ds(pl.multiple_of(W * n, W), W)
                rows_p = pl.ds(pl.multiple_of(W * jnp.maximum(n - 1, 0), W), W)
            else:
                r = idx // nb
                rows_q = pl.ds(r + dil * W * n, W, stride=dil)
                rows_p = pl.ds(r + dil * W * jnp.maximum(n - 1, 0), W, stride=dil)
            mask = band & (ki >= jnp.where(n > 0, 0, W))
            for pr in range(npair):
                q = (q_refs[pr][0, rows_q, :] * scale).astype(BF16)
                k2 = jnp.concatenate([k_refs[pr][0, rows_p, :], k_refs[pr][0, rows_q, :]], axis=0)
                v2 = jnp.concatenate([v_refs[pr][0, rows_p, :], v_refs[pr][0, rows_q, :]], axis=0)
                lg = jnp.where(mask, _nt_dot(q, block_diag(k2)) + bias_ref[br, pr], NEG)
                ps, mxs = [], []
                for hh in range(2):
                    sl = slice(nk * hh, nk * (hh + 1))
                    m = jnp.max(lg[:, sl], axis=-1, keepdims=True)
                    ps.append(jnp.where(mask[:, sl], jnp.exp(lg[:, sl] - m), 0.0))
                    mxs.append(jnp.broadcast_to(m, (W, hd)))
                p = jnp.concatenate(ps, axis=-1).astype(BF16)
                pv = _dot(p, jnp.concatenate([block_diag(v2), ones_bd], axis=1))
                acc_ref[br, pr, rows_q, :] = pv[:, 0:2 * hd]
                den_ref[br, pr, rows_q, :] = pv[:, 2 * hd:]
                mx_ref[br, pr, rows_q, :] = jnp.concatenate(mxs, axis=-1)
            return 0

        lax.fori_loop(0, dil * nb, piece, 0, unroll=8)

    def combine(i, _):
        rows = pl.ds(pl.multiple_of(i * W, W), W)
        outs = []
        for pr in range(npair):
            ms = [mx_ref[b, pr, rows, :] for b in range(len(DIL_PATTERNS))]
            top = functools.reduce(jnp.maximum, ms)
            es = [jnp.exp(m - top) for m in ms]
            num = sum(e * acc_ref[b, pr, rows, :] for b, e in enumerate(es))
            den = sum(e * jnp.maximum(den_ref[b, pr, rows, :], 1e-30) for b, e in enumerate(es))
            outs.append(num / den)
        o_ref[0, rows, :] = jnp.concatenate(outs, axis=-1)
        return 0

    lax.fori_loop(0, seq // W, combine, 0)


def _dilated(zb, bias):
    b, s, wb = zb.shape
    nbr = len(DIL_PATTERNS)
    pair_spec = lambda c: pl.BlockSpec((1, s, LANES), lambda i: (i, 0, c))
    return pl.pallas_call(
        _dilated_kernel,
        grid=(b,),
        in_specs=[pair_spec(c) for c in range(wb // LANES)]
                 + [pl.BlockSpec(bias.shape, lambda i: (0, 0, 0, 0))],
        out_specs=pl.BlockSpec((1, s, GROUP_WIDTH), lambda i: (i, 0, 0)),
        out_shape=jax.ShapeDtypeStruct((b, s, GROUP_WIDTH), F32),
        scratch_shapes=[pltpu.VMEM((nbr, HEADS // 2, s, LANES), F32) for _ in range(3)],
        compiler_params=pltpu.CompilerParams(dimension_semantics=("arbitrary",),
                                             vmem_limit_bytes=VMEM_LIMIT_BYTES),
        name="dilated",
    )(*([zb] * (wb // LANES)), bias)


def _head_columns_t(q):
    qt = q.T
    return jnp.concatenate([qt[HEAD_DIM * h:HEAD_DIM * (h + 1), :] for h in range(HEADS)], axis=1)


LOG2_E = math.log2(math.e)


def _scaled_query_columns_t(q):
    return (_head_columns_t(q) * (HEAD_DIM ** -0.5 * LOG2_E)).astype(BF16)


def _unstack_heads_t(per_head_t):
    halves = []
    for h in range(0, HEADS, 2):
        halves.append(jnp.concatenate([per_head_t[h], per_head_t[h + 1]], axis=0).T)
    return jnp.concatenate(halves, axis=-1)


VT_ROWS = HEAD_DIM + 16


def _online_step(lg, mask, vt1, m, acc):
    half = lg.shape[0] // 2
    for rows in (slice(0, half), slice(half, 2 * half)):
        lgh = jnp.where(mask[rows], lg[rows], NEG)
        m_new = jnp.maximum(m, jnp.max(lgh, axis=0, keepdims=True))
        p = jnp.where(mask[rows], jnp.exp2(lgh - m_new), 0.0)
        acc = acc * jnp.exp2(m - m_new) + _dot(vt1[:, rows], p.astype(BF16))
        m = m_new
    return m, acc


def _online_result(acc):
    return acc[0:HEAD_DIM, :] / jnp.maximum(acc[HEAD_DIM:HEAD_DIM + 1, :], 1e-30)


def _values_with_ones_t(z_rows):
    vt = z_rows.T[HEAD_DIM:2 * HEAD_DIM, :]
    return jnp.concatenate([vt, jnp.ones((VT_ROWS - HEAD_DIM, vt.shape[1]), F32)], axis=0).astype(BF16)


def _online_init(nq=BLK):
    return (tuple(jnp.full((1, nq), NEG, F32) for _ in range(HEADS)),
            tuple(jnp.zeros((VT_ROWS, nq), F32) for _ in range(HEADS)))


def _dsa_kernel(z_ref, btab_ref, o_ref, vt_ref, key_ref, hi_ref, lo_ref, lom_ref):
    i = pl.program_id(1)
    seq = z_ref.shape[1]
    nkb = seq // BLK
    nq = QRY
    hd = HEAD_DIM
    topk = min(DSA_TOPK, seq // 4)
    t0 = pl.multiple_of(i * nq, nq)

    assert nq == KEYS
    npair = i + 1
    c_q, c_iq, c_kv, c_ikw = CHUNK_STARTS[2]

    @pl.when(i == 0)
    def _():
        for kb in range(nkb):
            vt_ref[kb // 2, :, (kb % 2) * BLK:(kb % 2 + 1) * BLK] = _values_with_ones_t(
                z_ref[0, kb * BLK:(kb + 1) * BLK, c_kv:c_kv + 2 * hd])

    zq = z_ref[0, pl.ds(t0, nq), :]
    cq = zq[:, c_q:c_q + HEADS * hd]
    ciq = _head_columns_t(zq[:, c_iq:c_iq + IDX_HEADS * IDX_DIM]).astype(BF16)
    iw = (zq[:, c_ikw:c_ikw + LANES].T[IDX_DIM:IDX_DIM + IDX_HEADS, :]
          * ((IDX_HEADS * IDX_DIM) ** -0.5))
    s_loc = lax.broadcasted_iota(I32, (KEYS, nq), 0)
    t_glob = t0 + lax.broadcasted_iota(I32, (KEYS, nq), 1)

    def score_block(kp, _):
        r0 = pl.multiple_of(kp * KEYS, KEYS)
        ik = z_ref[0, pl.ds(r0, KEYS), c_ikw:c_ikw + IDX_DIM].astype(BF16)
        rel = _dot(ik, ciq)
        sc = jnp.zeros((KEYS, nq), F32)
        for h in range(IDX_HEADS):
            sc = sc + jnp.maximum(rel[:, nq * h:nq * (h + 1)], 0.0) * iw[h:h + 1, :]
        sc = jnp.where(r0 + s_loc <= t_glob, sc, NEG)
        key = _sortable_key(sc)
        key_ref[kp] = key
        hi_ref[kp] = (key >> 16).astype(I16)
        lo_ref[kp] = ((key & 0xFFFF) - 2 ** 15).astype(I16)
        return 0

    lax.fori_loop(0, npair // 2, lambda t, c: score_block(2 * t + 1, score_block(2 * t, c)), 0)
    lax.cond(npair % 2 == 1, lambda: score_block(npair - 1, 0), lambda: 0)

    def count(ref, pred, pairs):
        dt = ref.dtype
        rows = 8 * 4 // dt.itemsize
        def body(kp, acc):
            hit = jnp.where(pred(kp, ref[kp]), jnp.ones((), dt), jnp.zeros((), dt))
            hit = hit.reshape(KEYS // rows, rows, nq)
            parts = [hit[n] for n in range(KEYS // rows)]
            while len(parts) > 1:
                parts = [a + b for a, b in zip(parts[0::2], parts[1::2])]
            return acc + parts[0]
        acc = jnp.zeros((rows, nq), dt)
        for kp in range(pairs):
            acc = body(kp, acc)
        return jnp.sum(acc.astype(I32), axis=0, keepdims=True)

    def threshold(pairs):
        def half_search(ref, k):
            def bit(it, thr):
                cand = thr + lax.shift_left(jnp.int32(1), 15 - it)
                c = count(ref, lambda kp, half: half >= cand.astype(I16), pairs)
                return jnp.where(c >= k, cand, thr)
            return lax.fori_loop(0, 16, bit, jnp.full((1, nq), I16_MIN, I32))

        def run(_):
            thr_hi = half_search(hi_ref, topk)
            thr_hi16 = thr_hi.astype(I16)
            above_hi = count(hi_ref, lambda kp, half: half > thr_hi16, pairs)
            for kp in range(pairs):
                lom_ref[kp] = jnp.where(hi_ref[kp] == thr_hi16, lo_ref[kp], jnp.int16(I16_MIN))
            thr_lo = half_search(lom_ref, topk - above_hi)
            thr = (thr_hi << 16) | (thr_lo + 2 ** 15)
            thr_lo16 = thr_lo.astype(I16)
            n_gt = above_hi + count(lom_ref, lambda kp, half: half > thr_lo16, pairs)
            need = (topk - n_gt).astype(F32)
            earlier = jnp.zeros((1, nq), F32)
            for kp in range(pairs):
                tie = key_ref[kp] == thr
                tie01 = jnp.where(tie, 1.0, 0.0)
                rank = _dot(below, tie01.astype(BF16)) + earlier
                lom_ref[kp] = jnp.where(tie & (rank < need), 1, 0).astype(I16)
                earlier = earlier + jnp.sum(tie01, axis=0, keepdims=True)
            return thr
        return run

    below = jnp.where(_iota((KEYS, KEYS), 1) < _iota((KEYS, KEYS), 0), 1.0, 0.0).astype(BF16)
    thr = lax.switch(npair - 1, [threshold(p) for p in range(1, seq // KEYS + 1)], 0)

    qs = _scaled_query_columns_t(cq)

    def attend(kp, carry):
        ms, accs = carry
        r0 = pl.multiple_of(kp * KEYS, KEYS)
        kblk = z_ref[0, pl.ds(r0, KEYS), c_kv:c_kv + hd].astype(BF16)
        lg = _dot(kblk, qs) + btab_ref[jnp.minimum(i - kp, 2)]
        key = key_ref[kp]
        s_glob = r0 + s_loc
        mask = ((key > thr) | (lom_ref[kp].astype(I32) != 0)) & (s_glob <= t_glob)
        vt1 = vt_ref[kp]
        out = [_online_step(lg[:, nq * h:nq * (h + 1)], mask, vt1, ms[h], accs[h]) for h in range(HEADS)]
        return tuple(o[0] for o in out), tuple(o[1] for o in out)

    carry = lax.fori_loop(0, npair // 2, lambda t, c: attend(2 * t + 1, attend(2 * t, c)), _online_init(nq))
    ms, accs = lax.cond(npair % 2 == 1, lambda c: attend(npair - 1, c), lambda c: c, carry)
    o_ref[0] = _unstack_heads_t([_online_result(accs[h]) for h in range(HEADS)])


def _dsa(zc, btab):
    b, s, wc = zc.shape
    return pl.pallas_call(
        _dsa_kernel,
        grid=(b, s // QRY),
        in_specs=[pl.BlockSpec((1, s, wc), lambda bi, i: (bi, 0, 0)),
                  pl.BlockSpec(btab.shape, lambda bi, i: (0, 0, 0))],
        out_specs=pl.BlockSpec((1, QRY, GROUP_WIDTH), lambda bi, i: (bi, i, 0)),
        out_shape=jax.ShapeDtypeStruct((b, s, GROUP_WIDTH), F32),
        scratch_shapes=[pltpu.VMEM((s // KEYS, VT_ROWS, KEYS), BF16), pltpu.VMEM((s // KEYS, KEYS, QRY), I32)]
                       + [pltpu.VMEM((s // KEYS, KEYS, QRY), I16)] * 3,
        compiler_params=pltpu.CompilerParams(dimension_semantics=("arbitrary", "arbitrary"),
                                             vmem_limit_bytes=VMEM_LIMIT_BYTES),
        name="dsa",
    )(zc, btab)


def _nsa_kernel(z_ref, zc_ref, w1_ref, pos_ref, w2_ref, bsel_ref, bcmp_ref, ovt_ref, exp_ref, o_ref,
                vst_ref, vwt_ref, cmp_ref, cmpt_ref):
    i = pl.program_id(1)
    seq = z_ref.shape[1]
    nkb = seq // BLK
    hd = HEAD_DIM
    c_q, _, c_s, c_w, c_g = CHUNK_STARTS[3]
    n_cmp = (seq - NSA_CMP_LEN) // NSA_CMP_STRIDE + 1
    n_sel = seq // NSA_SEL_LEN
    topn = min(NSA_TOPN, n_sel)
    half = NSA_CMP_LEN // 2
    assert half == NSA_CMP_STRIDE and n_cmp + 1 == seq // NSA_CMP_STRIDE == BLK and n_sel <= BLK
    nq = QRY
    assert nq == KEYS
    t0 = pl.multiple_of(i * nq, nq)
    hi = lax.Precision.HIGHEST

    @pl.when(i == 0)
    def _():
        for kb in range(nkb):
            rows = slice(kb * BLK, (kb + 1) * BLK)
            cols = slice((kb % 2) * BLK, (kb % 2 + 1) * BLK)
            vst_ref[kb // 2, :, cols] = _values_with_ones_t(z_ref[0, rows, c_s:c_s + 2 * hd])
            vwt_ref[kb // 2, :, cols] = _values_with_ones_t(z_ref[0, rows, c_w:c_w + 2 * hd])
        first = jnp.zeros((BLK, 2 * NSA_CMP_HIDDEN), F32)
        second = jnp.zeros((BLK, 2 * NSA_CMP_HIDDEN), F32)
        for j in range(half):
            xj = zc_ref[0, pl.ds(j, BLK, stride=NSA_CMP_STRIDE), :]
            first = first + _dot((xj + pos_ref[j:j + 1, :]).astype(BF16), w1_ref[j])
            second = second + _dot((xj + pos_ref[half + j:half + j + 1, :]).astype(BF16), w1_ref[half + j])
        hid = first + pltpu.roll(second, BLK - 1, 0)
        hid = hid * _sigmoid(hid)
        cmp = _dot(hid.astype(BF16), w2_ref[...])
        cmp_ref[...] = cmp
        cmpt_ref[...] = cmp.T

    zq = z_ref[0, pl.ds(t0, nq), :]
    qs = _scaled_query_columns_t(zq[:, c_q:c_q + HEADS * hd])
    gates = _sigmoid(zq[:, c_g:c_g + LANES].T[0:16, :])
    row = lax.broadcasted_iota(I32, (BLK, nq), 0)
    t_glob = t0 + lax.broadcasted_iota(I32, (BLK, nq), 1)

    kcmp = cmp_ref[:, 0:hd].astype(BF16)
    vcmpt = cmpt_ref[hd:2 * hd, :].astype(BF16)
    lgc = _dot(kcmp, qs) + bcmp_ref[0]
    mask_c = (t_glob - (row * NSA_CMP_STRIDE + NSA_CMP_LEN - 1) >= 0) & (row < n_cmp)
    o_cmp, psum = [], jnp.zeros((BLK, nq), F32)
    for h in range(HEADS):
        lg = jnp.where(mask_c, lgc[:, nq * h:nq * (h + 1)], NEG)
        m = jnp.max(lg, axis=0, keepdims=True)
        p = jnp.where(mask_c, jnp.exp2(lg - m), 0.0)
        p = p / jnp.maximum(jnp.sum(p, axis=0, keepdims=True), 1e-30)
        o_cmp.append(_dot(vcmpt, p.astype(BF16)))
        psum = psum + p

    imp = _dot(ovt_ref[...], psum, precision=hi)
    cur = t_glob >> int(math.log2(NSA_SEL_LEN))
    forced = (row == 0) | (row == cur) | (row == cur - 1)
    imp = jnp.where(forced, NSA_FORCE, imp)
    imp = jnp.where(row * NSA_SEL_LEN <= t_glob, imp, NEG)
    imp = imp[0:n_sel, :]
    jrow = lax.broadcasted_iota(I32, (n_sel, nq), 0)
    rank = jnp.zeros((n_sel, nq), I32)
    for jp in range(n_sel):
        other = imp[jp:jp + 1, :]
        rank = rank + ((other > imp) | ((other == imp) & (jp < jrow))).astype(I32)
    chosen = jnp.where(rank < topn, 1.0, 0.0)
    chosen = jnp.concatenate([chosen, jnp.zeros((BLK - n_sel, nq), F32)], axis=0).astype(BF16)

    s_loc = lax.broadcasted_iota(I32, (KEYS, nq), 0)
    t_keys = t0 + lax.broadcasted_iota(I32, (KEYS, nq), 1)

    def attend(vt_ref, lanes, mask_fn):
        def body(kp, carry):
            ms, accs = carry
            r0 = pl.multiple_of(kp * KEYS, KEYS)
            kblk = z_ref[0, pl.ds(r0, KEYS), lanes].astype(BF16)
            lg = _dot(kblk, qs) + bsel_ref[jnp.minimum(i - kp, 2)]
            mask = mask_fn(r0)
            vt1 = vt_ref[kp]
            out = [_online_step(lg[:, nq * h:nq * (h + 1)], mask, vt1, ms[h], accs[h]) for h in range(HEADS)]
            return tuple(o[0] for o in out), tuple(o[1] for o in out)
        return body

    def mask_sel(r0):
        picked = _dot(exp_ref[pl.ds(r0, KEYS), :], chosen) > 0.5
        return picked & (r0 + s_loc <= t_keys)

    def mask_win(r0):
        dist = t_keys - (r0 + s_loc)
        return (dist >= 0) & (dist < NSA_WINDOW)

    sel_body = attend(vst_ref, slice(c_s, c_s + hd), mask_sel)
    win_body = attend(vwt_ref, slice(c_w, c_w + hd), mask_win)
    first_w = jnp.maximum(i - NSA_WINDOW // KEYS, 0)
    sel_carry = lax.fori_loop(0, first_w // 2, lambda t, c: sel_body(2 * t + 1, sel_body(2 * t, c)), _online_init(nq))
    sel_carry = lax.cond(first_w % 2 == 1, lambda c: sel_body(first_w - 1, c), lambda c: c, sel_carry)
    (_, acc_s), (_, acc_w) = lax.fori_loop(
        first_w, i + 1, lambda kp, c: (sel_body(kp, c[0]), win_body(kp, c[1])), (sel_carry, _online_init(nq)))

    outs = []
    for h in range(HEADS):
        outs.append(gates[3 * h:3 * h + 1, :] * o_cmp[h] + gates[3 * h + 1:3 * h + 2, :] * _online_result(acc_s[h])
                    + gates[3 * h + 2:3 * h + 3, :] * _online_result(acc_w[h]))
    o_ref[0] = _unstack_heads_t(outs)


def _nsa(zd, w1, pos, w2, bsel, bcmp, ovt, expand):
    b, s, wd = zd.shape
    full = lambda a: pl.BlockSpec(a.shape, lambda bi, i: (0,) * a.ndim)
    return pl.pallas_call(
        _nsa_kernel,
        grid=(b, s // QRY),
        in_specs=[pl.BlockSpec((1, s, wd), lambda bi, i: (bi, 0, 0)),
                  pl.BlockSpec((1, s, LANES), lambda bi, i: (bi, 0, CHUNK_STARTS[3][1] // LANES)),
                  full(w1), full(pos), full(w2), full(bsel),
                  pl.BlockSpec((1, BLK, HEADS * QRY), lambda bi, i: (i, 0, 0)),
                  full(ovt), full(expand)],
        out_specs=pl.BlockSpec((1, QRY, GROUP_WIDTH), lambda bi, i: (bi, i, 0)),
        out_shape=jax.ShapeDtypeStruct((b, s, GROUP_WIDTH), F32),
        scratch_shapes=[pltpu.VMEM((s // KEYS, VT_ROWS, KEYS), BF16), pltpu.VMEM((s // KEYS, VT_ROWS, KEYS), BF16),
                        pltpu.VMEM((BLK, BLK), F32), pltpu.VMEM((BLK, BLK), F32)],
        compiler_params=pltpu.CompilerParams(dimension_semantics=("arbitrary", "arbitrary"),
                                             vmem_limit_bytes=VMEM_LIMIT_BYTES),
        name="nsa",
    )(zd, zd, w1, pos, w2, bsel, bcmp, ovt, expand)


def _outproj_kernel(alpha, x_ref, a_ref, b_ref, c_ref, d_ref, w_ref, bo_ref, g_ref, beta_ref, o_ref):
    acc = bo_ref[...] + _dot(a_ref[...].astype(BF16), w_ref[0:GROUP_WIDTH, :])
    for n, m_ref in enumerate((b_ref, c_ref, d_ref), start=1):
        acc = acc + _dot(m_ref[...].astype(BF16), w_ref[n * GROUP_WIDTH:(n + 1) * GROUP_WIDTH, :])
    o_ref[...] = _layer_norm_rows(alpha * x_ref[...] + acc, g_ref[...], beta_ref[...])


def _outproj(alpha, x2d, mixed, w, bo, g, beta, tm=512):
    t = x2d.shape[0]
    row_spec = lambda width: pl.BlockSpec((tm, width), lambda i: (i, 0))
    const = lambda a: pl.BlockSpec(a.shape, lambda i: (0, 0))
    return pl.pallas_call(
        functools.partial(_outproj_kernel, alpha),
        grid=(t // tm,),
        in_specs=[row_spec(D_MODEL)] + [row_spec(GROUP_WIDTH)] * N_MIXERS
                 + [const(w), const(bo), const(g), const(beta)],
        out_specs=row_spec(D_MODEL),
        out_shape=jax.ShapeDtypeStruct((t, D_MODEL), F32),
        compiler_params=pltpu.CompilerParams(dimension_semantics=("arbitrary",),
                                             vmem_limit_bytes=VMEM_LIMIT_BYTES),
        name="outproj_ln",
    )(x2d, *mixed, w, bo, g, beta)


def _ffn_kernel(alpha, x_ref, w1_ref, b1_ref, w2_ref, b2_ref, g_ref, beta_ref, o_ref, xb_ref, acc_ref):
    j = pl.program_id(1)

    @pl.when(j == 0)
    def _():
        xb_ref[...] = x_ref[...].astype(BF16)
        acc_ref[...] = jnp.zeros_like(acc_ref)

    hdn = jnp.maximum(_dot(xb_ref[...], w1_ref[...]) + b1_ref[...], 0.0)
    acc_ref[...] += _dot(jnp.square(hdn).astype(BF16), w2_ref[...])

    @pl.when(j == pl.num_programs(1) - 1)
    def _():
        y = alpha * x_ref[...] + (acc_ref[...] + b2_ref[...])
        o_ref[...] = _layer_norm_rows(y, g_ref[...], beta_ref[...])


def _ffn(alpha, x2d, w1, b1, w2, b2, g, beta, tm=1024, tf=1024):
    t = x2d.shape[0]
    return pl.pallas_call(
        functools.partial(_ffn_kernel, alpha),
        grid=(t // tm, D_FF // tf),
        in_specs=[pl.BlockSpec((tm, D_MODEL), lambda i, j: (i, 0)),
                  pl.BlockSpec((D_MODEL, tf), lambda i, j: (0, j)),
                  pl.BlockSpec((1, tf), lambda i, j: (0, j)),
                  pl.BlockSpec((tf, D_MODEL), lambda i, j: (j, 0)),
                  pl.BlockSpec((1, D_MODEL), lambda i, j: (0, 0)),
                  pl.BlockSpec((1, D_MODEL), lambda i, j: (0, 0)),
                  pl.BlockSpec((1, D_MODEL), lambda i, j: (0, 0))],
        out_specs=pl.BlockSpec((tm, D_MODEL), lambda i, j: (i, 0)),
        out_shape=jax.ShapeDtypeStruct((t, D_MODEL), F32),
        scratch_shapes=[pltpu.VMEM((tm, D_MODEL), BF16), pltpu.VMEM((tm, D_MODEL), F32)],
        compiler_params=pltpu.CompilerParams(dimension_semantics=("arbitrary", "arbitrary"),
                                             vmem_limit_bytes=VMEM_LIMIT_BYTES),
        name="ffn_ln",
    )(x2d, w1, b1, w2, b2, g, beta)


def _bias_of_distance(rel_bias_heads, dist):
    onehot = np.eye(NUM_BUCKETS, dtype=np.float32)[_t5_bucket_np(np.asarray(dist))]
    return jnp.dot(rel_bias_heads.T, jnp.asarray(onehot.T), precision=lax.Precision.HIGHEST)


def _shifted_rows(v, n_rows, n_cols, step):
    period = v.shape[-1]
    assert n_cols <= period - step
    flat = jnp.tile(v, (1,) * (v.ndim - 1) + (n_rows,))[..., :n_rows * (period - step)]
    return flat.reshape(v.shape[:-1] + (n_rows, period - step))[..., :n_cols]


def _wrapped(period):
    idx = np.arange(period)
    return np.where(idx < period // 2, idx, idx - period)


def _dilated_bias(rel_bias):
    x = _wrapped(4 * BLK)
    tabs = [_shifted_rows(_bias_of_distance(rel_bias[:, 0:HEADS], (BLK - x) * dil), BLK, 2 * BLK, 1)
            for _, dil in DIL_PATTERNS]
    tabs = jnp.stack(tabs).reshape(len(DIL_PATTERNS), HEADS // 2, 2, BLK, 2 * BLK)
    return jnp.transpose(tabs, (0, 1, 3, 2, 4)).reshape(len(DIL_PATTERNS), HEADS // 2, BLK, 4 * BLK)


def _toeplitz_bias_t(rel_bias_heads, blk=BLK):
    assert (_t5_bucket_np(np.arange(blk + 1, 64 * blk)) == NUM_BUCKETS - 1).all()
    x = _wrapped(2 * blk)
    tabs = [_shifted_rows(_bias_of_distance(rel_bias_heads, blk * delta + x), blk, blk, 1)
            for delta in range(3)]
    return jnp.transpose(jnp.stack(tabs), (0, 2, 1, 3)).reshape(3, blk, HEADS * blk)


def _compressed_bias_t(rel_bias_heads, seq):
    nq = seq // QRY
    x = _wrapped(2 * seq + BLK)
    v = _bias_of_distance(rel_bias_heads, x - (NSA_CMP_LEN - 1))
    tab = _shifted_rows(v, BLK, seq, NSA_CMP_STRIDE)
    tab = tab.reshape(HEADS, BLK, nq, QRY)
    return jnp.transpose(tab, (2, 1, 0, 3)).reshape(nq, BLK, HEADS * QRY)


def _nsa_constants(seq):
    n_cmp = (seq - NSA_CMP_LEN) // NSA_CMP_STRIDE + 1
    n_sel = seq // NSA_SEL_LEN
    cs = np.arange(n_cmp)[:, None] * NSA_CMP_STRIDE
    ss = np.arange(n_sel)[None, :] * NSA_SEL_LEN
    ov = np.clip(np.minimum(cs + NSA_CMP_LEN, ss + NSA_SEL_LEN) - np.maximum(cs, ss), 0, None) / NSA_CMP_LEN
    ovt = np.zeros((BLK, BLK), np.float32)
    ovt[:n_sel, :n_cmp] = ov.T
    expand = np.zeros((seq, BLK), np.float32)
    expand[np.arange(seq), np.arange(seq) // NSA_SEL_LEN] = 1.0
    return jnp.asarray(ovt), jnp.asarray(expand, BF16)


def _nsa_weights(cmp_pos, cmp_w1, cmp_w2):
    hd, hid = HEAD_DIM, NSA_CMP_HIDDEN
    w1 = cmp_w1.reshape(2, NSA_CMP_LEN, hd, hid)
    zeros = jnp.zeros((NSA_CMP_LEN, hd, hid), F32)
    w1 = jnp.concatenate([jnp.concatenate([w1[0], zeros], axis=-1),
                          jnp.concatenate([zeros, w1[1]], axis=-1)], axis=1)
    pos = jnp.concatenate([cmp_pos[0], cmp_pos[1]], axis=-1)
    z2 = jnp.zeros((hid, hd), F32)
    w2 = jnp.concatenate([jnp.concatenate([cmp_w2[0], z2], axis=-1),
                          jnp.concatenate([z2, cmp_w2[1]], axis=-1)], axis=0)
    return w1.astype(BF16), pos, w2.astype(BF16)


def kernel(x, w_in, b_in, a_conv, a_norm, d_cmp_pos, d_cmp_w1, d_cmp_w2, w_out, b_out, ln1_g, ln1_b,
           w_ff1, b_ff1, w_ff2, b_ff2, ln2_g, ln2_b, rel_bias):
    bsz, seq, _ = x.shape
    depth = w_in.shape[0]
    alpha = (2 * depth) ** 0.25
    bias_dil = _dilated_bias(rel_bias)
    btab_dsa = _toeplitz_bias_t(rel_bias[:, HEADS:2 * HEADS], QRY) * LOG2_E
    btab_nsa = _toeplitz_bias_t(rel_bias[:, 2 * HEADS:3 * HEADS], QRY) * LOG2_E
    bcmp_nsa = _compressed_bias_t(rel_bias[:, 2 * HEADS:3 * HEADS], seq) * LOG2_E
    ovt, expand = _nsa_constants(seq)

    h = x.reshape(bsz * seq, D_MODEL)
    for l in range(depth):
        w_l = _permute_columns(w_in[l]).astype(BF16)
        b_l = _permute_columns(b_in[l])[None, :]
        za, zb, zc, zd, zg = (z.reshape(bsz, seq, -1) for z in _inproj(h, w_l, b_l))
        out_a = _mlstm(za, _mlstm_gate_rows(zg), a_conv[l], a_norm[l][None, :])
        out_b = _dilated(zb, bias_dil)
        out_c = _dsa(zc, btab_dsa)
        nsa_w1, nsa_pos, nsa_w2 = _nsa_weights(d_cmp_pos[l], d_cmp_w1[l], d_cmp_w2[l])
        out_d = _nsa(zd, nsa_w1, nsa_pos, nsa_w2, btab_nsa, bcmp_nsa, ovt, expand)
        mixed = [o.reshape(bsz * seq, GROUP_WIDTH) for o in (out_a, out_b, out_c, out_d)]
        h = _outproj(alpha, h, mixed, w_out[l].astype(BF16), b_out[l][None, :],
                     ln1_g[l][None, :], ln1_b[l][None, :])
        h = _ffn(alpha, h, w_ff1[l].astype(BF16), b_ff1[l][None, :], w_ff2[l].astype(BF16),
                 b_ff2[l][None, :], ln2_g[l][None, :], ln2_b[l][None, :])
    return h.reshape(bsz, seq, D_MODEL)
```
